```python
import jax
import jax.numpy as jnp
from jax import lax
import numpy as np


D_MODEL = 1024
BATCH = 4
SEQ = 4096
DEPTH = 1

ML_HEADS = 4
ML_HEAD_DIM = 128
ML_WIDTH = ML_HEADS * ML_HEAD_DIM
ML_CONV = 4
ML_CHUNK = 64
NSA_HEADS = 8
NSA_KV_GROUPS = 2
NSA_HEAD_DIM = 64
NSA_WIDTH = NSA_HEADS * NSA_HEAD_DIM
NSA_KV_WIDTH = NSA_KV_GROUPS * NSA_HEAD_DIM
NSA_N_BRANCH = 3
CMP_BLOCK = 32
CMP_STRIDE = 16
CMP_HIDDEN = 128
SEL_BLOCK = 64
SEL_COUNT = 16
SEL_FORCE = 1e4
WINDOW = 512
NSA_QBLOCK = 64
N_MIXERS = 2
MEM_TOKENS = 256
XA_HEADS = 4
XA_HEAD_DIM = D_MODEL // XA_HEADS
XA_WIDTH = XA_HEADS * XA_HEAD_DIM
MOE_GROUPS = 4
MOE_EXPERTS_PER_GROUP = 4
MOE_EXPERTS = MOE_GROUPS * MOE_EXPERTS_PER_GROUP
MOE_TOPK = 2
MOE_HIDDEN = 512

RMS_EPS = 1e-6
NEG_INF = -1e30
IN_SPLITS = (ML_WIDTH,) * 4 + (ML_HEADS,) * 2 + (NSA_WIDTH,) + (NSA_KV_WIDTH,) * 6 + (NSA_N_BRANCH * NSA_HEADS, N_MIXERS * D_MODEL)
D_IN = 4 * ML_WIDTH + 2 * ML_HEADS + NSA_WIDTH + 6 * NSA_KV_WIDTH + NSA_N_BRANCH * NSA_HEADS + N_MIXERS * D_MODEL

kernel_name = 'hybrid_mlstm_nsa_hmoe_block'


def rms_norm(x, g):
    xf = x.astype(jnp.float32)
    y = xf * lax.rsqrt(jnp.mean(xf * xf, axis=-1, keepdims=True) + RMS_EPS)
    return (y * g.astype(jnp.float32)).astype(x.dtype)


def alibi_slopes(n_heads):
    return jnp.exp2(-8.0 * jnp.arange(1, n_heads + 1, dtype=jnp.float32) / n_heads)


def causal_depthwise_conv(x, w):
    width, channels = w.shape
    return lax.conv_general_dilated(x, w[:, None, :].astype(x.dtype), window_strides=(1,), padding=[(width - 1, 0)], dimension_numbers=('NWC', 'WIO', 'NWC'), feature_group_count=channels)


def mlstm_chunkwise(q, k, v, i_pre, log_f):
    B, H, S, d = q.shape
    L = ML_CHUNK
    nc = S // L

    def to_chunks(a):
        return jnp.moveaxis(a.reshape(B, H, nc, L, *a.shape[3:]), 2, 0)

    tri = jnp.tril(jnp.ones((L, L), dtype=bool))

    def step(carry, xs):
        C, n, m = carry
        qc, kc, vc, ic, fc = xs
        b = jnp.cumsum(fc, axis=-1)
        dlog = jnp.where(tri, b[..., :, None] - b[..., None, :] + ic[..., None, :], -jnp.inf)
        inter = b + m[..., None]
        mt = jnp.maximum(jnp.max(dlog, axis=-1), inter)
        w_intra = jnp.exp(dlog - mt[..., None])
        w_inter = jnp.exp(inter - mt)
        s = jnp.einsum('bhtd,bhsd->bhts', qc, kc) * w_intra
        num = jnp.einsum('bhts,bhsd->bhtd', s, vc) + w_inter[..., None] * jnp.einsum('bhvk,bhtk->bhtv', C, qc)
        den = jnp.sum(s, axis=-1) + w_inter * jnp.einsum('bhk,bhtk->bht', n, qc)
        h = num / jnp.maximum(jnp.abs(den), jnp.exp(-mt))[..., None]
        bl = b[..., -1]
        logw = bl[..., None] - b + ic
        m_new = jnp.maximum(bl + m, jnp.max(logw, axis=-1))
        decay = jnp.exp(bl + m - m_new)
        ws = jnp.exp(logw - m_new[..., None])
        C_new = decay[..., None, None] * C + jnp.einsum('bhsv,bhsk->bhvk', vc * ws[..., None], kc)
        n_new = decay[..., None] * n + jnp.einsum('bhs,bhsk->bhk', ws, kc)
        return (C_new, n_new, m_new), h

    init = (jnp.zeros((B, H, d, d), jnp.float32), jnp.zeros((B, H, d), jnp.float32), jnp.zeros((B, H), jnp.float32))
    _, h = lax.scan(step, init, (to_chunks(q), to_chunks(k), to_chunks(v), to_chunks(i_pre), to_chunks(log_f)))
    return jnp.moveaxis(h, 0, 2).reshape(B, H, S, d)


def nsa_attention(q, k_cmp, v_cmp, k_slc, v_slc, k_win, v_win, gates, cmp_pos_k, cmp_pos_v, cmp_k_w1, cmp_k_w2, cmp_v_w1, cmp_v_w2):
    B, S, _ = q.shape
    H, G, dh = NSA_HEADS, NSA_KV_GROUPS, NSA_HEAD_DIM
    HG = H // G
    QB = NSA_QBLOCK
    nqb = S // QB
    n_cmp = S // CMP_STRIDE - 1
    n_slc = S // SEL_BLOCK
    n_sel = min(SEL_COUNT, n_slc)
    f32 = jnp.float32
    slopes = alibi_slopes(H).reshape(G, HG)

    qh = (q.reshape(B, S, H, dh) * dh ** -0.5).transpose(0, 2, 1, 3)
    q_blocks = jnp.moveaxis(qh.reshape(B, H, nqb, QB, dh), 2, 0)
    g_blocks = jnp.moveaxis(jax.nn.sigmoid(gates.astype(f32)).reshape(B, nqb, QB, H, NSA_N_BRANCH), 1, 0)

    def kv(a):
        return a.reshape(B, S, G, dh)

    def compress(a, pos, w1, w2):
        ch = kv(a).reshape(B, S // CMP_STRIDE, CMP_STRIDE, G, dh)
        blk = jnp.concatenate([ch[:, :-1], ch[:, 1:]], axis=2) + pos[None, None, :, None, :]
        blk = blk.transpose(0, 3, 1, 2, 4).reshape(B, G, n_cmp, CMP_BLOCK * dh)
        return jax.nn.gelu(blk @ w1) @ w2

    Kc = compress(k_cmp, cmp_pos_k, cmp_k_w1, cmp_k_w2)
    Vc = compress(v_cmp, cmp_pos_v, cmp_v_w1, cmp_v_w2)
    cmp_start = jnp.arange(n_cmp) * CMP_STRIDE
    cmp_end = cmp_start + CMP_BLOCK - 1
    cmp_center = cmp_start.astype(f32) + (CMP_BLOCK - 1) / 2
    cs = cmp_start[:, None]
    ss = jnp.arange(n_slc)[None, :] * SEL_BLOCK
    overlap = jnp.clip(jnp.minimum(cs + CMP_BLOCK, ss + SEL_BLOCK) - jnp.maximum(cs, ss), 0, None).astype(f32) / CMP_BLOCK

    Ks = kv(k_slc).reshape(B, n_slc, SEL_BLOCK, G, dh).transpose(0, 3, 1, 2, 4)
    Vs = kv(v_slc).reshape(B, n_slc, SEL_BLOCK, G, dh).transpose(0, 3, 1, 2, 4)
    Kw = jnp.pad(kv(k_win).transpose(0, 2, 1, 3), ((0, 0), (0, 0), (WINDOW, 0), (0, 0)))
    Vw = jnp.pad(kv(v_win).transpose(0, 2, 1, 3), ((0, 0), (0, 0), (WINDOW, 0), (0, 0)))
    bi = jnp.arange(B)[:, None, None, None]
    gi = jnp.arange(G)[None, :, None, None]
    sel_off = jnp.arange(SEL_BLOCK)
    s_idx = jnp.arange(n_slc)

    def query_block(args):
        i, qb, gb = args
        t = i * QB + jnp.arange(QB)
        tf = t.astype(f32)
        qg = qb.reshape(B, G, HG, QB, dh)
        sc = jnp.einsum('bgxqd,bgjd->bgxqj', qg, Kc).astype(f32)
        valid_c = cmp_end[None, :] <= t[:, None]
        sc = jnp.where(valid_c, sc - slopes[None, :, :, None, None] * (tf[:, None] - cmp_center[None, :]), NEG_INF)
        p_cmp = jax.nn.softmax(sc, axis=-1) * valid_c
        o_cmp = jnp.einsum('bgxqj,bgjd->bgxqd', p_cmp.astype(Vc.dtype), Vc)
        p_slc = jnp.einsum('bgxqj,js->bgqs', p_cmp, overlap)
        cur = t // SEL_BLOCK
        forced = (s_idx[None, :] == 0) | (s_idx[None, :] == cur[:, None]) | (s_idx[None, :] == cur[:, None] - 1)
        future = s_idx[None, :] > cur[:, None]
        score = jnp.where(forced, SEL_FORCE, jnp.where(future, -SEL_FORCE, p_slc))
        _, idx = lax.top_k(score, n_sel)
        Kg = Ks[bi, gi, idx]
        Vg = Vs[bi, gi, idx]
        dist = t[None, None, :, None, None] - (idx[..., None] * SEL_BLOCK + sel_off)
        ssc = jnp.einsum('bgxqd,bgqnsd->bgxqns', qg, Kg).astype(f32)
        ssc = jnp.where((dist >= 0)[:, :, None], ssc - slopes[None, :, :, None, None, None] * dist[:, :, None].astype(f32), NEG_INF)
        p_s = jax.nn.softmax(ssc.reshape(B, G, HG, QB, n_sel * SEL_BLOCK), axis=-1).reshape(ssc.shape)
        o_slc = jnp.einsum('bgxqns,bgqnsd->bgxqd', p_s.astype(Vg.dtype), Vg)
        Kwb = lax.dynamic_slice_in_dim(Kw, i * QB, QB + WINDOW, axis=2)
        Vwb = lax.dynamic_slice_in_dim(Vw, i * QB, QB + WINDOW, axis=2)
        kp = i * QB - WINDOW + jnp.arange(QB + WINDOW)
        dw = t[:, None] - kp[None, :]
        valid_w = (dw >= 0) & (dw < WINDOW) & (kp[None, :] >= 0)
        sw = jnp.einsum('bgxqd,bgkd->bgxqk', qg, Kwb).astype(f32)
        sw = jnp.where(valid_w, sw - slopes[None, :, :, None, None] * dw.astype(f32), NEG_INF)
        o_win = jnp.einsum('bgxqk,bgkd->bgxqd', jax.nn.softmax(sw, axis=-1).astype(Vwb.dtype), Vwb)
        gb = gb.transpose(0, 2, 1, 3).reshape(B, G, HG, QB, NSA_N_BRANCH)
        o = gb[..., 0:1] * o_cmp + gb[..., 1:2] * o_slc + gb[..., 2:3] * o_win
        return o.reshape(B, H, QB, dh).astype(qb.dtype)

    out = lax.map(query_block, (jnp.arange(nqb), q_blocks, g_blocks))
    return out.transpose(1, 0, 3, 2, 4).reshape(B, S, H * dh)


def hybrid_mixer(h, w_in, conv_qk, b_igate, b_fgate, mlstm_norm, cmp_pos_k, cmp_pos_v, cmp_k_w1, cmp_k_w2, cmp_v_w1, cmp_v_w2, w_br_mlstm, w_br_nsa, w_mix_out):
    B, S, _ = h.shape
    f32 = jnp.float32
    offs = [int(o) for o in np.cumsum(IN_SPLITS)[:-1]]
    (ml_q, ml_k, ml_v, ml_o, ml_i, ml_f, ns_q, k_cmp, v_cmp, k_slc, v_slc, k_win, v_win, ns_gates, merge_pre) = jnp.split(h @ w_in, offs, axis=-1)
    qk = jax.nn.silu(causal_depthwise_conv(jnp.concatenate([ml_q, ml_k], axis=-1), conv_qk))
    ml_q, ml_k = jnp.split(qk, 2, axis=-1)

    def heads(a):
        return a.reshape(B, S, ML_HEADS, ML_HEAD_DIM).transpose(0, 2, 1, 3).astype(f32)

    i_pre = (ml_i.astype(f32) + b_igate.astype(f32)).transpose(0, 2, 1)
    log_f = jax.nn.log_sigmoid(ml_f.astype(f32) + b_fgate.astype(f32)).transpose(0, 2, 1)
    h_cell = mlstm_chunkwise(heads(ml_q), heads(ml_k) * ML_HEAD_DIM ** -0.5, heads(ml_v), i_pre, log_f)
    h_cell = rms_norm(h_cell.transpose(0, 2, 1, 3), mlstm_norm.reshape(ML_HEADS, ML_HEAD_DIM)).reshape(B, S, ML_WIDTH)
    y_ml = (jax.nn.sigmoid(ml_o.astype(f32)) * h_cell).astype(h.dtype) @ w_br_mlstm
    y_ns = nsa_attention(ns_q, k_cmp, v_cmp, k_slc, v_slc, k_win, v_win, ns_gates, cmp_pos_k, cmp_pos_v, cmp_k_w1, cmp_k_w2, cmp_v_w1, cmp_v_w2) @ w_br_nsa
    g_ml, g_ns = jnp.split(jax.nn.sigmoid(merge_pre), 2, axis=-1)
    return (g_ml * y_ml + g_ns * y_ns) @ w_mix_out


def memory_cross_attention(h, m, wq, wkv, wo):
    B, S, _ = h.shape
    q = (h @ wq).reshape(B, S, XA_HEADS, XA_HEAD_DIM)
    k, v = jnp.split(m @ wkv, 2, axis=-1)
    k = k.reshape(B, -1, XA_HEADS, XA_HEAD_DIM)
    v = v.reshape(B, -1, XA_HEADS, XA_HEAD_DIM)
    s = jnp.einsum('bshd,bmhd->bhsm', q, k).astype(jnp.float32) * XA_HEAD_DIM ** -0.5
    p = jax.nn.softmax(s, axis=-1).astype(v.dtype)
    o = jnp.einsum('bhsm,bmhd->bshd', p, v).reshape(B, S, XA_WIDTH)
    return o @ wo


def hierarchical_moe(h, wg, bg, we, be, w1, w3, w2):
    B, S, D = h.shape
    f32 = jnp.float32
    hf = h.reshape(B * S, D)
    n_tok = B * S
    p_group = jax.nn.softmax((hf @ wg).astype(f32) + bg.astype(f32), axis=-1)
    g_val, g_idx = lax.top_k(p_group, 1)
    e_logits = ((hf @ we).astype(f32) + be.astype(f32)).reshape(n_tok, MOE_GROUPS, MOE_EXPERTS_PER_GROUP)
    e_in_group = e_logits[jnp.arange(n_tok), g_idx[:, 0]]
    e_val, e_idx = lax.top_k(e_in_group, MOE_TOPK)
    combine = g_val * jax.nn.softmax(e_val, axis=-1)
    expert_id = g_idx * MOE_EXPERTS_PER_GROUP + e_idx
    weights = jnp.sum(jax.nn.one_hot(expert_id, MOE_EXPERTS, dtype=f32) * combine[..., None], axis=1)
    y = jnp.zeros((n_tok, D), f32)
    for e in range(MOE_EXPERTS):
        a = jax.nn.silu(hf @ w1[e]) * (hf @ w3[e])
        y = y + weights[:, e:e + 1] * (a @ w2[e])
    return y.astype(h.dtype).reshape(B, S, D)


def setup_inputs(seed: int = 0) -> dict:
    key = jax.random.key(seed)
    ks = jax.random.split(key, 32)
    f32 = jnp.float32
    L = DEPTH
    dh = NSA_HEAD_DIM

    def nrm(k, shape, scale):
        return scale * jax.random.normal(k, shape, f32)

    def gain(k, shape):
        return 1.0 + 0.02 * jax.random.normal(k, shape, f32)

    return {
        'x': nrm(ks[0], (BATCH, SEQ, D_MODEL), 1.0),
        'mem': nrm(ks[1], (BATCH, MEM_TOKENS, D_MODEL), 1.0),
        'norm_mix': gain(ks[2], (L, D_MODEL)),
        'w_in': nrm(ks[3], (L, D_MODEL, D_IN), D_MODEL ** -0.5),
        'conv_qk': nrm(ks[4], (L, ML_CONV, 2 * ML_WIDTH), ML_CONV ** -0.5),
        'b_igate': nrm(ks[5], (L, ML_HEADS), 0.1),
        'b_fgate': jnp.linspace(3.0, 6.0, ML_HEADS, dtype=f32)[None, :] + nrm(ks[6], (L, ML_HEADS), 0.1),
        'mlstm_norm': gain(ks[7], (L, ML_WIDTH)),
        'cmp_pos_k': nrm(ks[8], (L, CMP_BLOCK, dh), 0.1),
        'cmp_pos_v': nrm(ks[9], (L, CMP_BLOCK, dh), 0.1),
        'cmp_k_w1': nrm(ks[10], (L, CMP_BLOCK * dh, CMP_HIDDEN), (CMP_BLOCK * dh) ** -0.5),
        'cmp_k_w2': nrm(ks[11], (L, CMP_HIDDEN, dh), CMP_HIDDEN ** -0.5),
        'cmp_v_w1': nrm(ks[12], (L, CMP_BLOCK * dh, CMP_HIDDEN), (CMP_BLOCK * dh) ** -0.5),
        'cmp_v_w2': nrm(ks[13], (L, CMP_HIDDEN, dh), CMP_HIDDEN ** -0.5),
        'w_br_mlstm': nrm(ks[14], (L, ML_WIDTH, D_MODEL), ML_WIDTH ** -0.5),
        'w_br_nsa': nrm(ks[15], (L, NSA_WIDTH, D_MODEL), NSA_WIDTH ** -0.5),
        'w_mix_out': nrm(ks[16], (L, D_MODEL, D_MODEL), D_MODEL ** -0.5),
        'norm_xattn': gain(ks[17], (L, D_MODEL)),
        'norm_mem': gain(ks[18], (L, D_MODEL)),
        'xa_wq': nrm(ks[19], (L, D_MODEL, XA_WIDTH), D_MODEL ** -0.5),
        'xa_wkv': nrm(ks[20], (L, D_MODEL, 2 * XA_WIDTH), D_MODEL ** -0.5),
        'xa_wo': nrm(ks[21], (L, XA_WIDTH, D_MODEL), XA_WIDTH ** -0.5),
        'norm_ffn': gain(ks[22], (L, D_MODEL)),
        'router_group_w': nrm(ks[23], (L, D_MODEL, MOE_GROUPS), D_MODEL ** -0.5),
        'router_group_b': nrm(ks[24], (L, MOE_GROUPS), 0.01),
        'router_expert_w': nrm(ks[25], (L, D_MODEL, MOE_EXPERTS), D_MODEL ** -0.5),
        'router_expert_b': nrm(ks[26], (L, MOE_EXPERTS), 0.01),
        'moe_w1': nrm(ks[27], (L, MOE_EXPERTS, D_MODEL, MOE_HIDDEN), D_MODEL ** -0.5),
        'moe_w3': nrm(ks[28], (L, MOE_EXPERTS, D_MODEL, MOE_HIDDEN), D_MODEL ** -0.5),
        'moe_w2': nrm(ks[29], (L, MOE_EXPERTS, MOE_HIDDEN, D_MODEL), MOE_HIDDEN ** -0.5),
        'norm_final': gain(ks[30], (D_MODEL,)),
    }


def reference(x, mem, norm_mix, w_in, conv_qk, b_igate, b_fgate, mlstm_norm, cmp_pos_k, cmp_pos_v, cmp_k_w1, cmp_k_w2, cmp_v_w1, cmp_v_w2, w_br_mlstm, w_br_nsa, w_mix_out, norm_xattn, norm_mem, xa_wq, xa_wkv, xa_wo, norm_ffn, router_group_w, router_group_b, router_expert_w, router_expert_b, moe_w1, moe_w3, moe_w2, norm_final):
    for layer in range(DEPTH):
        x = x + hybrid_mixer(rms_norm(x, norm_mix[layer]), w_in[layer], conv_qk[layer], b_igate[layer], b_fgate[layer], mlstm_norm[layer], cmp_pos_k[layer], cmp_pos_v[layer], cmp_k_w1[layer], cmp_k_w2[layer], cmp_v_w1[layer], cmp_v_w2[layer], w_br_mlstm[layer], w_br_nsa[layer], w_mix_out[layer])
        x = x + memory_cross_attention(rms_norm(x, norm_xattn[layer]), rms_norm(mem, norm_mem[layer]), xa_wq[layer], xa_wkv[layer], xa_wo[layer])
        x = x + hierarchical_moe(rms_norm(x, norm_ffn[layer]), router_group_w[layer], router_group_b[layer], router_expert_w[layer], router_expert_b[layer], moe_w1[layer], moe_w3[layer], moe_w2[layer])
    return rms_norm(x, norm_final)
```

```python
import functools
import math

import numpy as np
import jax
import jax.numpy as jnp
from jax import lax
from jax.experimental import pallas as pl
from jax.experimental.pallas import tpu as pltpu

F32 = jnp.float32
BF16 = jnp.bfloat16

ML_HEADS = 4
ML_HEAD_DIM = 128
ML_WIDTH = ML_HEADS * ML_HEAD_DIM
ML_CONV = 4
NSA_HEADS = 8
NSA_KV_GROUPS = 2
NSA_HEAD_DIM = 64
NSA_GROUP_HEADS = NSA_HEADS // NSA_KV_GROUPS
NSA_WIDTH = NSA_HEADS * NSA_HEAD_DIM
NSA_KV_WIDTH = NSA_KV_GROUPS * NSA_HEAD_DIM
NSA_N_BRANCH = 3
CMP_BLOCK = 32
CMP_STRIDE = 16
SEL_BLOCK = 64
SEL_COUNT = 16
SEL_FORCE = 1e4
WINDOW = 512
XA_HEADS = 4
MOE_GROUPS = 4
MOE_EXPERTS_PER_GROUP = 4
MOE_EXPERTS = MOE_GROUPS * MOE_EXPERTS_PER_GROUP
RMS_EPS = 1e-6
NEG_INF = -1e30
SEL_MASK = -float(2 ** 30)

VMEM_LIMIT_BYTES = 56 * 1024 * 1024
ROUTER_ROWS = 32


def _cparams(*sem):
    return pltpu.CompilerParams(dimension_semantics=sem, vmem_limit_bytes=VMEM_LIMIT_BYTES)


def _rms(x, g):
    return x * lax.rsqrt(jnp.mean(x * x, axis=-1, keepdims=True) + RMS_EPS) * g


def _dot(a, b):
    return jnp.dot(a, b, preferred_element_type=F32)


def _dot_nt(a, b):
    return lax.dot_general(a, b, (((1,), (1,)), ((), ())), preferred_element_type=F32)


def _dot_tn(a, b):
    return lax.dot_general(a, b, (((0,), (0,)), ((), ())), preferred_element_type=F32)


def _norm_matmul_kernel(x_ref, g_ref, w_ref, o_ref, h_ref):
    @pl.when(pl.program_id(1) == 0)
    def _():
        h_ref[...] = _rms(x_ref[...], g_ref[...]).astype(BF16)

    o_ref[...] = _dot(h_ref[...], w_ref[...]).astype(o_ref.dtype)


def _norm_matmul(x2d, g, w, out_dtype, tm, tn):
    m, d = x2d.shape
    n = w.shape[1]
    return pl.pallas_call(
        _norm_matmul_kernel,
        grid=(m // tm, n // tn),
        in_specs=[
            pl.BlockSpec((tm, d), lambda i, j: (i, 0)),
            pl.BlockSpec((1, d), lambda i, j: (0, 0)),
            pl.BlockSpec((d, tn), lambda i, j: (0, j)),
        ],
        out_specs=pl.BlockSpec((tm, tn), lambda i, j: (i, j)),
        out_shape=jax.ShapeDtypeStruct((m, n), out_dtype),
        scratch_shapes=[pltpu.VMEM((tm, d), BF16)],
        compiler_params=_cparams("parallel", "arbitrary"),
        name="norm_matmul",
    )(x2d, g.reshape(1, d), w)


def _mlstm_kernel(q_ref, k_ref, v_ref, o_ref, qt_ref, kt_ref, gcol_ref, grow_ref, bcol_ref, brow_ref,
                  cw_ref, ng_ref, out_ref, c_scr, n_scr, m_scr, *, chunk):
    L = chunk
    c = pl.program_id(2)

    @pl.when(c == 0)
    def _():
        c_scr[...] = jnp.zeros_like(c_scr)
        n_scr[...] = jnp.zeros_like(n_scr)
        m_scr[...] = jnp.zeros_like(m_scr)

    row8 = lax.broadcasted_iota(jnp.int32, (8, ML_HEAD_DIM), 0)

    def conv_silu(x_ref, tail_ref, w):
        x = x_ref[0].astype(F32)
        tail = jnp.where(c > 0, tail_ref[0].astype(F32), 0.0)
        y = x * w[ML_CONV - 1:ML_CONV, :]
        for j in range(ML_CONV - 1):
            s = ML_CONV - 1 - j
            r = pltpu.roll(x, s, 0)
            head = jnp.where(row8 < s, pltpu.roll(tail, s, 0), r[:8])
            y = y + jnp.concatenate([head, r[8:]], axis=0) * w[j:j + 1, :]
        return y * jax.nn.sigmoid(y)

    cw = cw_ref[0]
    q = conv_silu(q_ref, qt_ref, cw[:, :ML_HEAD_DIM])
    k = conv_silu(k_ref, kt_ref, cw[:, ML_HEAD_DIM:]) * (ML_HEAD_DIM ** -0.5)
    v = v_ref[0]
    qb = q.astype(BF16)
    kb = k.astype(BF16)

    gcol = gcol_ref[0, 0] + bcol_ref[0]
    grow = grow_ref[0, 0] + brow_ref[0]
    i_col = gcol[:, 0:1]
    f_col = jax.nn.log_sigmoid(gcol[:, 1:2])
    i_row = grow[0:1, :]
    f_row = jax.nn.log_sigmoid(grow[1:2, :])

    t_idx = lax.broadcasted_iota(jnp.int32, (L, L), 0)
    s_idx = lax.broadcasted_iota(jnp.int32, (L, L), 1)
    tri = s_idx <= t_idx
    b_col = jnp.sum(jnp.where(tri, f_row, 0.0), axis=1, keepdims=True)
    b_row = jnp.sum(jnp.where(t_idx <= s_idx, f_col, 0.0), axis=0, keepdims=True)
    dlog = jnp.where(tri, b_col + (i_row - b_row), NEG_INF)
    m_prev = m_scr[...]
    inter = b_col + m_prev
    mt = jnp.maximum(jnp.max(dlog, axis=1, keepdims=True), inter)
    w_intra = jnp.exp(dlog - mt)
    w_inter = jnp.exp(inter - mt)

    s = _dot_nt(qb, kb) * w_intra
    cmat = c_scr[...]
    nvec = n_scr[...]
    num = _dot(s.astype(BF16), v) + w_inter * _dot(qb, cmat.astype(BF16))
    den = jnp.sum(s, axis=1, keepdims=True) + w_inter * jnp.sum(q * nvec, axis=1, keepdims=True)
    h = num / jnp.maximum(jnp.abs(den), jnp.exp(-mt))

    bl = jnp.sum(f_row, axis=1, keepdims=True)
    logw = bl - b_col + i_col
    m_new = jnp.maximum(bl + m_prev, jnp.max(logw, axis=0, keepdims=True))
    decay = jnp.exp(bl + m_prev - m_new)
    kw = k * jnp.exp(logw - m_new)
    c_scr[...] = decay * cmat + _dot_tn(kw.astype(BF16), v)
    n_scr[...] = decay * nvec + jnp.sum(kw, axis=0, keepdims=True)
    m_scr[...] = m_new

    hn = _rms(h, ng_ref[0])
    out_ref[0] = (jax.nn.sigmoid(o_ref[0].astype(F32)) * hn).astype(out_ref.dtype)


def _mlstm(act, gates_col, gates_row, b_col, b_row, conv_w, norm_g, *, col0, chunk):
    B, S, _ = act.shape
    H, d, L = ML_HEADS, ML_HEAD_DIM, chunk
    tail_blocks = L // 8

    def blk(off):
        return pl.BlockSpec((1, L, d), lambda b, h, c: (b, c, col0 + off + h))

    def tail(off):
        return pl.BlockSpec((1, 8, d), lambda b, h, c: (b, jnp.maximum(c * tail_blocks - 1, 0), col0 + off + h))

    return pl.pallas_call(
        functools.partial(_mlstm_kernel, chunk=L),
        grid=(B, H, S // L),
        in_specs=[
            blk(0), blk(H), blk(2 * H), blk(3 * H), tail(0), tail(H),
            pl.BlockSpec((1, 1, L, 2), lambda b, h, c: (b, h, c, 0)),
            pl.BlockSpec((1, 1, 2, L), lambda b, h, c: (b, h, 0, c)),
            pl.BlockSpec((1, 1, 2), lambda b, h, c: (h, 0, 0)),
            pl.BlockSpec((1, 2, 1), lambda b, h, c: (h, 0, 0)),
            pl.BlockSpec((1, ML_CONV, 2 * d), lambda b, h, c: (h, 0, 0)),
            pl.BlockSpec((1, 1, d), lambda b, h, c: (h, 0, 0)),
        ],
        out_specs=pl.BlockSpec((1, L, d), lambda b, h, c: (b, c, h)),
        out_shape=jax.ShapeDtypeStruct((B, S, H * d), BF16),
        scratch_shapes=[pltpu.VMEM((d, d), F32), pltpu.VMEM((1, d), F32), pltpu.VMEM((1, 1), F32)],
        compiler_params=_cparams("parallel", "parallel", "arbitrary"),
        name="mlstm",
    )(act, act, act, act, act, act, gates_col, gates_row, b_col, b_row, conv_w, norm_g)


def _compress_kernel(x_ref, pos_ref, w1_ref, w2_ref, o_ref):
    x = x_ref[0, 0, 0].astype(F32)
    n = x.shape[0]
    top = _dot((x + pos_ref[0, 0]).astype(BF16), w1_ref[0, 0])
    bot = _dot((x + pos_ref[0, 1]).astype(BF16), w1_ref[0, 1])
    pre = top + pltpu.roll(bot, n - 1, 0)
    hid = jax.nn.gelu(pre, approximate=True)
    out = _dot(hid.astype(BF16), w2_ref[0])
    row = lax.broadcasted_iota(jnp.int32, out.shape, 0)
    o_ref[0, 0, 0] = jnp.where(row < n - 1, out, 0.0).astype(o_ref.dtype)


def _compress(chunks, pos, w1, w2):
    _, B, G, n, width = chunks.shape
    hidden = w1.shape[-1]
    dh = w2.shape[-1]
    return pl.pallas_call(
        _compress_kernel,
        grid=(2, B, G),
        in_specs=[
            pl.BlockSpec((1, 1, 1, n, width), lambda a, b, g: (a, b, g, 0, 0)),
            pl.BlockSpec((1, 2, 1, width), lambda a, b, g: (a, 0, 0, 0)),
            pl.BlockSpec((1, 2, width, hidden), lambda a, b, g: (a, 0, 0, 0)),
            pl.BlockSpec((1, hidden, dh), lambda a, b, g: (a, 0, 0)),
        ],
        out_specs=pl.BlockSpec((1, 1, 1, n, dh), lambda a, b, g: (a, b, g, 0, 0)),
        out_shape=jax.ShapeDtypeStruct((2, B, G, n, dh), BF16),
        compiler_params=_cparams("parallel", "parallel", "parallel"),
        name="nsa_compress",
    )(chunks, pos, w1, w2)


def _nsa_kernel(q_ref, gate_ref, kc_ref, vc_ref, ks_ref, vs_ref, kw_ref, vw_ref, ovl_ref, out_ref,
                m_scr, l_scr, acc_scr, *, tq, seq):
    HG, dh = NSA_GROUP_HEADS, NSA_HEAD_DIM
    g = pl.program_id(1)
    i = pl.program_id(2)
    t0 = i * tq
    rows = HG * tq
    n_cmp_pad = kc_ref.shape[2]
    n_slc = seq // SEL_BLOCK

    q_all = q_ref[0]
    q_heads = [q_all[:, x * dh:(x + 1) * dh] for x in range(HG)]
    qs = (jnp.concatenate(q_heads, axis=0).astype(F32) * (dh ** -0.5)).astype(BF16)
    r_idx = lax.broadcasted_iota(jnp.int32, (rows, 1), 0)
    head = g * HG + r_idx // tq
    slope = jnp.zeros((rows, 1), F32)
    for hh in range(NSA_HEADS):
        slope = jnp.where(head == hh, 2.0 ** (-8.0 * (hh + 1) / NSA_HEADS), slope)
    t_col = t0 + r_idx % tq
    t_col_f = t_col.astype(F32)

    sc = _dot_nt(qs, kc_ref[0, 0])
    j_idx = lax.broadcasted_iota(jnp.int32, (1, n_cmp_pad), 1)
    cmp_end = j_idx * CMP_STRIDE + (CMP_BLOCK - 1)
    cmp_center = (j_idx * CMP_STRIDE).astype(F32) + (CMP_BLOCK - 1) / 2
    valid_c = cmp_end <= t_col
    sc = jnp.where(valid_c, sc - slope * (t_col_f - cmp_center), NEG_INF)
    e = jnp.exp(sc - jnp.max(sc, axis=1, keepdims=True))
    p_cmp = jnp.where(valid_c, e / jnp.sum(e, axis=1, keepdims=True), 0.0)
    o_cmp = _dot(p_cmp.astype(BF16), vc_ref[0, 0])

    p_sum = p_cmp[0:tq]
    for x in range(1, HG):
        p_sum = p_sum + p_cmp[x * tq:(x + 1) * tq]
    p_hi = p_sum.astype(BF16)
    p_lo = (p_sum - p_hi.astype(F32)).astype(BF16)
    ovl = ovl_ref[...]
    p_slc = _dot_nt(ovl, p_hi) + _dot_nt(ovl, p_lo)
    blk = lax.broadcasted_iota(jnp.int32, (n_slc, tq), 0)
    cur = (t0 + lax.broadcasted_iota(jnp.int32, (n_slc, tq), 1)) // SEL_BLOCK
    forced = (blk == 0) | (blk == cur) | (blk == cur - 1)
    score = jnp.where(forced, SEL_FORCE, jnp.where(blk > cur, -SEL_FORCE, p_slc))
    rank = jnp.zeros((n_slc, tq), F32)
    for kk in range(n_slc):
        sk = score[kk:kk + 1, :]
        ge = jnp.where(sk >= score, 1.0, 0.0)
        gt = jnp.where(sk > score, 1.0, 0.0)
        rank = rank + jnp.where(blk > kk, ge, gt)
    sel_bias = jnp.where(rank < min(SEL_COUNT, n_slc), 0.0, SEL_MASK)
    sel_bias = sel_bias.T.astype(BF16)
    q_aug = jnp.concatenate(
        [jnp.concatenate([qs[x * tq:(x + 1) * tq], sel_bias], axis=1) for x in range(HG)], axis=0)

    m_scr[...] = jnp.full(m_scr.shape, NEG_INF, F32)
    l_scr[...] = jnp.zeros(l_scr.shape, F32)
    acc_scr[...] = jnp.zeros(acc_scr.shape, F32)
    kpos_base = lax.broadcasted_iota(jnp.int32, (1, tq), 1)

    def slc_chunk(c, diagonal):
        start = pl.multiple_of(c * tq, tq)
        kk = ks_ref[0, 0, pl.ds(start, tq), :]
        vv = vs_ref[0, 0, pl.ds(start, tq), :]
        kpos = start + kpos_base
        s = _dot_nt(q_aug, kk) - slope * (t_col_f - kpos.astype(F32))
        if diagonal:
            s = jnp.where(kpos <= t_col, s, NEG_INF)
        m_old = m_scr[...]
        m_new = jnp.maximum(m_old, jnp.max(s, axis=1, keepdims=True))
        alpha = jnp.exp(m_old - m_new)
        p = jnp.exp(s - m_new)
        l_scr[...] = alpha * l_scr[...] + jnp.sum(p, axis=1, keepdims=True)
        acc_scr[...] = alpha * acc_scr[...] + _dot(p.astype(BF16), vv)
        m_scr[...] = m_new

    def body(c, carry):
        slc_chunk(c, False)
        return carry

    lax.fori_loop(0, i, body, 0)
    slc_chunk(i, True)
    o_slc = acc_scr[...] / l_scr[...]

    n_win = WINDOW + tq
    start = pl.multiple_of(jnp.maximum(t0 - WINDOW, 0), tq)
    kwin = kw_ref[0, 0, pl.ds(start, n_win), :]
    vwin = vw_ref[0, 0, pl.ds(start, n_win), :]
    dw = t_col - (start + lax.broadcasted_iota(jnp.int32, (1, n_win), 1))
    valid_w = (dw >= 0) & (dw < WINDOW)
    sw = jnp.where(valid_w, _dot_nt(qs, kwin) - slope * dw.astype(F32), NEG_INF)
    ew = jnp.exp(sw - jnp.max(sw, axis=1, keepdims=True))
    o_win = _dot(ew.astype(BF16), vwin) / jnp.sum(ew, axis=1, keepdims=True)

    gates = jax.nn.sigmoid(gate_ref[0, 0])

    def gate_col(r):
        return jnp.concatenate([gates[:, x * NSA_N_BRANCH + r:x * NSA_N_BRANCH + r + 1] for x in range(HG)], axis=0)

    o = gate_col(0) * o_cmp + gate_col(1) * o_slc + gate_col(2) * o_win
    out_ref[0] = jnp.concatenate([o[x * tq:(x + 1) * tq] for x in range(HG)], axis=1).astype(out_ref.dtype)


def _nsa_attention(act, gates, kc, vc, ks_aug, vs, kw, vw, overlap_t, *, q_col0, tq):
    B, S, _ = act.shape
    G, HG, dh = NSA_KV_GROUPS, NSA_GROUP_HEADS, NSA_HEAD_DIM
    n_slc = S // SEL_BLOCK
    rows = HG * tq

    def full(a):
        return pl.BlockSpec((1, 1) + a.shape[2:], lambda b, g, i: (b, g, 0, 0))

    return pl.pallas_call(
        functools.partial(_nsa_kernel, tq=tq, seq=S),
        grid=(B, G, S // tq),
        in_specs=[
            pl.BlockSpec((1, tq, HG * dh), lambda b, g, i: (b, i, q_col0 + g)),
            pl.BlockSpec((1, 1, tq, HG * NSA_N_BRANCH), lambda b, g, i: (b, g, i, 0)),
            full(kc), full(vc), full(ks_aug), full(vs), full(kw), full(vw),
            pl.BlockSpec(overlap_t.shape, lambda b, g, i: (0, 0)),
        ],
        out_specs=pl.BlockSpec((1, tq, HG * dh), lambda b, g, i: (b, i, g)),
        out_shape=jax.ShapeDtypeStruct((B, S, NSA_WIDTH), BF16),
        scratch_shapes=[pltpu.VMEM((rows, 1), F32), pltpu.VMEM((rows, 1), F32), pltpu.VMEM((rows, dh), F32)],
        compiler_params=_cparams("parallel", "parallel", "arbitrary"),
        name="nsa_attn",
    )(act, gates, kc, vc, ks_aug, vs, kw, vw, overlap_t)


def _mix_out_kernel(x_ref, hm_ref, hn_ref, gm_ref, gn_ref, wm_ref, wn_ref, wo_ref, o_ref):
    y_ml = _dot(hm_ref[...], wm_ref[...])
    y_ns = _dot(hn_ref[...], wn_ref[...])
    mix = jax.nn.sigmoid(gm_ref[...].astype(F32)) * y_ml + jax.nn.sigmoid(gn_ref[...].astype(F32)) * y_ns
    o_ref[...] = x_ref[...] + _dot(mix.astype(BF16), wo_ref[...])


def _mix_out(x2d, h_ml, h_ns, act2d, wm, wn, wo, *, gate_col0, tm):
    n, d = x2d.shape

    def res(w):
        return pl.BlockSpec(w.shape, lambda i: (0, 0))

    return pl.pallas_call(
        _mix_out_kernel,
        grid=(n // tm,),
        in_specs=[
            pl.BlockSpec((tm, d), lambda i: (i, 0)),
            pl.BlockSpec((tm, h_ml.shape[1]), lambda i: (i, 0)),
            pl.BlockSpec((tm, h_ns.shape[1]), lambda i: (i, 0)),
            pl.BlockSpec((tm, d), lambda i: (i, gate_col0)),
            pl.BlockSpec((tm, d), lambda i: (i, gate_col0 + 1)),
            res(wm), res(wn), res(wo),
        ],
        out_specs=pl.BlockSpec((tm, d), lambda i: (i, 0)),
        out_shape=jax.ShapeDtypeStruct((n, d), F32),
        compiler_params=_cparams("parallel"),
        name="mix_out",
    )(x2d, h_ml, h_ns, act2d, act2d, wm, wn, wo)


def _xattn_kernel(x_ref, g_ref, wq_ref, k_ref, v_ref, wo_ref, o_ref):
    x = x_ref[...]
    d = x.shape[1]
    dh = d // XA_HEADS
    h = _rms(x, g_ref[...]).astype(BF16)
    q = (_dot(h, wq_ref[...]) * (dh ** -0.5)).astype(BF16)
    k = k_ref[0]
    v = v_ref[0]
    outs = []
    for hd in range(XA_HEADS):
        sl = slice(hd * dh, (hd + 1) * dh)
        s = _dot_nt(q[:, sl], k[:, sl])
        e = jnp.exp(s - jnp.max(s, axis=1, keepdims=True))
        p = e / jnp.sum(e, axis=1, keepdims=True)
        outs.append(_dot(p.astype(BF16), v[:, sl]))
    o = jnp.concatenate(outs, axis=1).astype(BF16)
    o_ref[...] = x + _dot(o, wo_ref[...])


def _xattn(x2d, g, wq, kv, wo, *, seq, tm):
    n, d = x2d.shape
    n_mem = kv.shape[1]
    tiles_per_batch = seq // tm
    return pl.pallas_call(
        _xattn_kernel,
        grid=(n // tm,),
        in_specs=[
            pl.BlockSpec((tm, d), lambda i: (i, 0)),
            pl.BlockSpec((1, d), lambda i: (0, 0)),
            pl.BlockSpec(wq.shape, lambda i: (0, 0)),
            pl.BlockSpec((1, n_mem, d), lambda i: (i // tiles_per_batch, 0, 0)),
            pl.BlockSpec((1, n_mem, d), lambda i: (i // tiles_per_batch, 0, 1)),
            pl.BlockSpec(wo.shape, lambda i: (0, 0)),
        ],
        out_specs=pl.BlockSpec((tm, d), lambda i: (i, 0)),
        out_shape=jax.ShapeDtypeStruct((n, d), F32),
        compiler_params=_cparams("parallel"),
        name="xattn",
    )(x2d, g.reshape(1, d), wq, kv, kv, wo)


def _route(logits, bg, be):
    G, EG = MOE_GROUPS, MOE_EXPERTS_PER_GROUP
    tm = logits.shape[1]
    lg = logits[0:G] + bg
    eg = jnp.exp(lg - jnp.max(lg, axis=0, keepdims=True))
    pg = eg / jnp.sum(eg, axis=0, keepdims=True)
    g_val = jnp.max(pg, axis=0, keepdims=True)
    g_row = lax.broadcasted_iota(jnp.int32, (G, tm), 0)
    g_idx = jnp.min(jnp.where(pg == g_val, g_row, G), axis=0, keepdims=True)
    el = logits[8:8 + G * EG] + be
    e_in = jnp.zeros((EG, tm), F32)
    for gg in range(G):
        e_in = jnp.where(g_idx == gg, el[gg * EG:(gg + 1) * EG], e_in)
    e_row = lax.broadcasted_iota(jnp.int32, (EG, tm), 0)
    v1 = jnp.max(e_in, axis=0, keepdims=True)
    i1 = jnp.min(jnp.where(e_in == v1, e_row, EG), axis=0, keepdims=True)
    rest = jnp.where(e_row == i1, -jnp.inf, e_in)
    v2 = jnp.max(rest, axis=0, keepdims=True)
    i2 = jnp.min(jnp.where(rest == v2, e_row, EG), axis=0, keepdims=True)
    e2 = jnp.exp(v2 - v1)
    c1 = g_val / (1.0 + e2)
    c2 = g_val * e2 / (1.0 + e2)
    ex = lax.broadcasted_iota(jnp.int32, (G * EG, tm), 0)
    base = g_idx * EG
    return jnp.where(ex == base + i1, c1, 0.0) + jnp.where(ex == base + i2, c2, 0.0)


def _moe_kernel(x_ref, g_ref, wr_ref, bg_ref, be_ref, w1_ref, w3_ref, w2_ref, gf_ref, o_ref,
                h_scr, wt_scr, acc_scr):
    e = pl.program_id(1)

    @pl.when(e == 0)
    def _():
        h = _rms(x_ref[...], g_ref[...])
        h_scr[...] = h.astype(BF16)
        logits = lax.dot_general(wr_ref[...], h, (((1,), (1,)), ((), ())), precision=lax.Precision.HIGHEST,
                                 preferred_element_type=F32)
        wt = _route(logits, bg_ref[...], be_ref[...])
        pad = jnp.zeros((wt_scr.shape[1] - wt.shape[0], wt.shape[1]), F32)
        wt_scr[...] = jnp.concatenate([wt, pad], axis=0).T
        acc_scr[...] = jnp.zeros_like(acc_scr)

    h = h_scr[...]
    lane = lax.broadcasted_iota(jnp.int32, wt_scr.shape, 1)
    w_col = jnp.sum(jnp.where(lane == e, wt_scr[...], 0.0), axis=1, keepdims=True)
    a = _dot(h, w1_ref[0])
    a = a * jax.nn.sigmoid(a) * _dot(h, w3_ref[0]) * w_col
    acc_scr[...] += _dot(a.astype(BF16), w2_ref[0])

    @pl.when(e == pl.num_programs(1) - 1)
    def _():
        o_ref[...] = _rms(x_ref[...] + acc_scr[...], gf_ref[...])


def _moe(x2d, g, wr, bg, be, w1, w3, w2, g_final, *, tm):
    n, d = x2d.shape
    E, _, hid = w1.shape
    return pl.pallas_call(
        _moe_kernel,
        grid=(n // tm, E),
        in_specs=[
            pl.BlockSpec((tm, d), lambda i, e: (i, 0)),
            pl.BlockSpec((1, d), lambda i, e: (0, 0)),
            pl.BlockSpec(wr.shape, lambda i, e: (0, 0)),
            pl.BlockSpec(bg.shape, lambda i, e: (0, 0)),
            pl.BlockSpec(be.shape, lambda i, e: (0, 0)),
            pl.BlockSpec((1, d, hid), lambda i, e: (e, 0, 0)),
            pl.BlockSpec((1, d, hid), lambda i, e: (e, 0, 0)),
            pl.BlockSpec((1, hid, d), lambda i, e: (e, 0, 0)),
            pl.BlockSpec((1, d), lambda i, e: (0, 0)),
        ],
        out_specs=pl.BlockSpec((tm, d), lambda i, e: (i, 0)),
        out_shape=jax.ShapeDtypeStruct((n, d), F32),
        scratch_shapes=[pltpu.VMEM((tm, d), BF16), pltpu.VMEM((tm, 128), F32), pltpu.VMEM((tm, d), F32)],
        compiler_params=_cparams("parallel", "arbitrary"),
        name="moe",
    )(x2d, g.reshape(1, d), wr, bg, be, w1, w3, w2, g_final.reshape(1, d))


def _overlap_t(seq):
    n_cmp_pad = seq // CMP_STRIDE
    cs = np.arange(n_cmp_pad)[None, :] * CMP_STRIDE
    ss = np.arange(seq // SEL_BLOCK)[:, None] * SEL_BLOCK
    ov = np.clip(np.minimum(cs + CMP_BLOCK, ss + SEL_BLOCK) - np.maximum(cs, ss), 0, None) / CMP_BLOCK
    ov[:, n_cmp_pad - 1] = 0.0
    return jnp.asarray(ov, BF16)


def _layer(x, mem, norm_mix, w_in, conv_qk, b_igate, b_fgate, mlstm_norm, cmp_pos_k, cmp_pos_v, cmp_k_w1, cmp_k_w2,
           cmp_v_w1, cmp_v_w2, w_br_mlstm, w_br_nsa, w_mix_out, norm_xattn, norm_mem, xa_wq, xa_wkv, xa_wo, norm_ffn,
           router_group_w, router_group_b, router_expert_w, router_expert_b, moe_w1, moe_w3, moe_w2, norm_final):
    B, S, D = x.shape
    N = B * S
    H, d = ML_HEADS, ML_HEAD_DIM
    G, HG, dh = NSA_KV_GROUPS, NSA_GROUP_HEADS, NSA_HEAD_DIM
    x2d = x.reshape(N, D)

    o_mlqkvo = 0
    o_mlif = 4 * ML_WIDTH
    o_nsq = o_mlif + 2 * H
    o_kv = o_nsq + NSA_WIDTH
    o_nsg = o_kv + 6 * NSA_KV_WIDTH
    o_merge = o_nsg + NSA_N_BRANCH * NSA_HEADS
    w_act = jnp.concatenate([w_in[:, o_merge:o_merge + 2 * D], w_in[:, o_mlqkvo:o_mlif], w_in[:, o_nsq:o_kv],
                             w_in[:, o_kv:o_nsg]], axis=1).astype(BF16)
    n_small = 2 * H + NSA_N_BRANCH * NSA_HEADS
    w_small = jnp.concatenate([w_in[:, o_mlif:o_nsq], w_in[:, o_nsg:o_merge], jnp.zeros((D, 128 - n_small), F32)],
                              axis=1).astype(BF16)
    act = _norm_matmul(x2d, norm_mix, w_act, BF16, tm=1024, tn=768)
    small = _norm_matmul(x2d, norm_mix, w_small, F32, tm=1024, tn=128)
    act3 = act.reshape(B, S, act.shape[1])
    c_ml = 2 * D
    c_nsq = c_ml + 4 * ML_WIDTH
    c_kv = c_nsq + NSA_WIDTH

    gi = small[:, 0:H].reshape(B, S, H)
    gf = small[:, H:2 * H].reshape(B, S, H)
    g_if = jnp.stack([gi, gf], axis=-1)
    gates_col = g_if.transpose(0, 2, 1, 3)
    gates_row = g_if.transpose(0, 2, 3, 1)
    bias = jnp.stack([b_igate, b_fgate], axis=-1)
    conv_w = jnp.concatenate([conv_qk[:, :ML_WIDTH].reshape(ML_CONV, H, d), conv_qk[:, ML_WIDTH:].reshape(ML_CONV, H, d)],
                             axis=-1).transpose(1, 0, 2)
    h_ml = _mlstm(act3, gates_col, gates_row, bias.reshape(H, 1, 2), bias.reshape(H, 2, 1), conv_w,
                  mlstm_norm.reshape(H, 1, d), col0=c_ml // d, chunk=256)

    kv = act3[:, :, c_kv:c_kv + 6 * NSA_KV_WIDTH].reshape(B, S, 6, G, dh).transpose(2, 0, 3, 1, 4)
    chunks = kv[0:2].reshape(2, B, G, S // CMP_STRIDE, CMP_STRIDE * dh)
    pos = jnp.stack([cmp_pos_k, cmp_pos_v]).reshape(2, 2, 1, CMP_STRIDE * dh)
    w1c = jnp.stack([cmp_k_w1, cmp_v_w1]).reshape(2, 2, CMP_STRIDE * dh, -1).astype(BF16)
    w2c = jnp.stack([cmp_k_w2, cmp_v_w2]).astype(BF16)
    kvc = _compress(chunks, pos, w1c, w2c)
    n_slc = S // SEL_BLOCK
    onehot = (jnp.arange(S)[:, None] // SEL_BLOCK == jnp.arange(n_slc)[None, :]).astype(BF16)
    ks_aug = jnp.concatenate([kv[2], jnp.broadcast_to(onehot, (B, G, S, n_slc))], axis=-1)
    ns_gates = small[:, 2 * H:n_small].reshape(B, S, G, HG * NSA_N_BRANCH).transpose(0, 2, 1, 3)
    h_ns = _nsa_attention(act3, ns_gates, kvc[0], kvc[1], ks_aug, kv[3], kv[4], kv[5], _overlap_t(S),
                          q_col0=c_nsq // (HG * dh), tq=256)

    x1 = _mix_out(x2d, h_ml.reshape(N, ML_WIDTH), h_ns.reshape(N, NSA_WIDTH), act, w_br_mlstm.astype(BF16),
                  w_br_nsa.astype(BF16), w_mix_out.astype(BF16), gate_col0=0, tm=512)

    n_mem = mem.shape[1]
    kv_mem = _norm_matmul(mem.reshape(B * n_mem, D), norm_mem, xa_wkv.astype(BF16), BF16, tm=B * n_mem, tn=512)
    x2 = _xattn(x1, norm_xattn, xa_wq.astype(BF16), kv_mem.reshape(B, n_mem, 2 * D), xa_wo.astype(BF16), seq=S, tm=512)

    wr = jnp.zeros((ROUTER_ROWS, D), F32)
    wr = wr.at[0:MOE_GROUPS].set(router_group_w.T).at[8:8 + MOE_EXPERTS].set(router_expert_w.T)
    return _moe(x2, norm_ffn, wr, router_group_b.reshape(MOE_GROUPS, 1), router_expert_b.reshape(MOE_EXPERTS, 1),
                moe_w1.astype(BF16), moe_w3.astype(BF16), moe_w2.astype(BF16), norm_final, tm=1024).reshape(B, S, D)


def kernel(x, mem, norm_mix, w_in, conv_qk, b_igate, b_fgate, mlstm_norm, cmp_pos_k, cmp_pos_v, cmp_k_w1, cmp_k_w2, cmp_v_w1, cmp_v_w2, w_br_mlstm, w_br_nsa, w_mix_out, norm_xattn, norm_mem, xa_wq, xa_wkv, xa_wo, norm_ffn, router_group_w, router_group_b, router_expert_w, router_expert_b, moe_w1, moe_w3, moe_w2, norm_final):
    depth = w_in.shape[0]
    assert depth == 1, "the fused final norm assumes a single layer"
    layer = 0
    return _layer(x, mem, norm_mix[layer], w_in[layer], conv_qk[layer], b_igate[layer], b_fgate[layer], mlstm_norm[layer],
                  cmp_pos_k[layer], cmp_pos_v[layer], cmp_k_w1[layer], cmp_k_w2[layer], cmp_v_w1[layer], cmp_v_w2[layer],
                  w_br_mlstm[layer], w_br_nsa[layer], w_mix_out[layer], norm_xattn[layer], norm_mem[layer], xa_wq[layer],
                  xa_wkv[layer], xa_wo[layer], norm_ffn[layer], router_group_w[layer], router_group_b[layer],
                  router_expert_w[layer], router_expert_b[layer], moe_w1[layer], moe_w3[layer], moe_w2[layer], norm_final)
```

```python
import functools
import math

import numpy as np
import jax
import jax.numpy as jnp
from jax import lax
from jax.experimental import pallas as pl
from jax.experimental.pallas import tpu as pltpu

F32 = jnp.float32
BF16 = jnp.bfloat16

ML_HEADS = 4
ML_HEAD_DIM = 128
ML_WIDTH = ML_HEADS * ML_HEAD_DIM
ML_CONV = 4
NSA_HEADS = 8
NSA_KV_GROUPS = 2
NSA_HEAD_DIM = 64
NSA_GROUP_HEADS = NSA_HEADS // NSA_KV_GROUPS
NSA_WIDTH = NSA_HEADS * NSA_HEAD_DIM
NSA_KV_WIDTH = NSA_KV_GROUPS * NSA_HEAD_DIM
NSA_N_BRANCH = 3
CMP_BLOCK = 32
CMP_STRIDE = 16
SEL_BLOCK = 64
SEL_COUNT = 16
SEL_FORCE = 1e4
WINDOW = 512
XA_HEADS = 4
MOE_GROUPS = 4
MOE_EXPERTS_PER_GROUP = 4
MOE_EXPERTS = MOE_GROUPS * MOE_EXPERTS_PER_GROUP
RMS_EPS = 1e-6
NEG_INF = -1e30
SEL_MASK = -float(2 ** 30)

VMEM_LIMIT_BYTES = 56 * 1024 * 1024
ROUTER_ROWS = 32


def _cparams(*sem):
    return pltpu.CompilerParams(dimension_semantics=sem, vmem_limit_bytes=VMEM_LIMIT_BYTES)


def _rms(x, g):
    return x * lax.rsqrt(jnp.mean(x * x, axis=-1, keepdims=True) + RMS_EPS) * g


def _dot(a, b):
    return jnp.dot(a, b, preferred_element_type=F32)


def _dot_nt(a, b):
    return lax.dot_general(a, b, (((1,), (1,)), ((), ())), preferred_element_type=F32)


def _dot_tn(a, b):
    return lax.dot_general(a, b, (((0,), (0,)), ((), ())), preferred_element_type=F32)


def _norm_matmul_kernel(x_ref, g_ref, w_ref, o_ref, h_ref):
    @pl.when(pl.program_id(1) == 0)
    def _():
        h_ref[...] = _rms(x_ref[...], g_ref[...]).astype(BF16)

    o_ref[...] = _dot(h_ref[...], w_ref[...]).astype(o_ref.dtype)


def _norm_matmul(x2d, g, w, out_dtype, tm, tn):
    m, d = x2d.shape
    n = w.shape[1]
    return pl.pallas_call(
        _norm_matmul_kernel,
        grid=(m // tm, n // tn),
        in_specs=[
            pl.BlockSpec((tm, d), lambda i, j: (i, 0)),
            pl.BlockSpec((1, d), lambda i, j: (0, 0)),
            pl.BlockSpec((d, tn), lambda i, j: (0, j)),
        ],
        out_specs=pl.BlockSpec((tm, tn), lambda i, j: (i, j)),
        out_shape=jax.ShapeDtypeStruct((m, n), out_dtype),
        scratch_shapes=[pltpu.VMEM((tm, d), BF16)],
        compiler_params=_cparams("parallel", "arbitrary"),
        name="norm_matmul",
    )(x2d, g.reshape(1, d), w)


def _mlstm_kernel(q_ref, k_ref, v_ref, o_ref, qt_ref, kt_ref, gcol_ref, grow_ref, bcol_ref, brow_ref,
                  cw_ref, ng_ref, out_ref, c_scr, n_scr, m_scr, *, chunk):
    L = chunk
    c = pl.program_id(2)

    @pl.when(c == 0)
    def _():
        c_scr[...] = jnp.zeros_like(c_scr)
        n_scr[...] = jnp.zeros_like(n_scr)
        m_scr[...] = jnp.zeros_like(m_scr)

    row8 = lax.broadcasted_iota(jnp.int32, (8, ML_HEAD_DIM), 0)

    def conv_silu(x_ref, tail_ref, w):
        x = x_ref[0].astype(F32)
        tail = jnp.where(c > 0, tail_ref[0].astype(F32), 0.0)
        y = x * w[ML_CONV - 1:ML_CONV, :]
        for j in range(ML_CONV - 1):
            s = ML_CONV - 1 - j
            r = pltpu.roll(x, s, 0)
            head = jnp.where(row8 < s, pltpu.roll(tail, s, 0), r[:8])
            y = y + jnp.concatenate([head, r[8:]], axis=0) * w[j:j + 1, :]
        return y * jax.nn.sigmoid(y)

    cw = cw_ref[0]
    q = conv_silu(q_ref, qt_ref, cw[:, :ML_HEAD_DIM])
    k = conv_silu(k_ref, kt_ref, cw[:, ML_HEAD_DIM:]) * (ML_HEAD_DIM ** -0.5)
    v = v_ref[0]
    qb = q.astype(BF16)
    kb = k.astype(BF16)

    gcol = gcol_ref[0, 0] + bcol_ref[0]
    grow = grow_ref[0, 0] + brow_ref[0]
    i_col = gcol[:, 0:1]
    f_col = jax.nn.log_sigmoid(gcol[:, 1:2])
    i_row = grow[0:1, :]
    f_row = jax.nn.log_sigmoid(grow[1:2, :])

    t_idx = lax.broadcasted_iota(jnp.int32, (L, L), 0)
    s_idx = lax.broadcasted_iota(jnp.int32, (L, L), 1)
    tri = s_idx <= t_idx
    b_col = jnp.sum(jnp.where(tri, f_row, 0.0), axis=1, keepdims=True)
    b_row = jnp.sum(jnp.where(t_idx <= s_idx, f_col, 0.0), axis=0, keepdims=True)
    dlog = jnp.where(tri, b_col + (i_row - b_row), NEG_INF)
    m_prev = m_scr[...]
    inter = b_col + m_prev
    mt = jnp.maximum(jnp.max(dlog, axis=1, keepdims=True), inter)
    w_intra = jnp.exp(dlog - mt)
    w_inter = jnp.exp(inter - mt)

    s = _dot_nt(qb, kb) * w_intra
    cmat = c_scr[...]
    nvec = n_scr[...]
    num = _dot(s.astype(BF16), v) + w_inter * _dot(qb, cmat.astype(BF16))
    den = jnp.sum(s, axis=1, keepdims=True) + w_inter * jnp.sum(q * nvec, axis=1, keepdims=True)
    h = num / jnp.maximum(jnp.abs(den), jnp.exp(-mt))

    bl = jnp.sum(f_row, axis=1, keepdims=True)
    logw = bl - b_col + i_col
    m_new = jnp.maximum(bl + m_prev, jnp.max(logw, axis=0, keepdims=True))
    decay = jnp.exp(bl + m_prev - m_new)
    kw = k * jnp.exp(logw - m_new)
    c_scr[...] = decay * cmat + _dot_tn(kw.astype(BF16), v)
    n_scr[...] = decay * nvec + jnp.sum(kw, axis=0, keepdims=True)
    m_scr[...] = m_new

    hn = _rms(h, ng_ref[0])
    out_ref[0] = (jax.nn.sigmoid(o_ref[0].astype(F32)) * hn).astype(out_ref.dtype)


def _mlstm(act, gates_col, gates_row, b_col, b_row, conv_w, norm_g, *, col0, chunk):
    B, S, _ = act.shape
    H, d, L = ML_HEADS, ML_HEAD_DIM, chunk
    tail_blocks = L // 8

    def blk(off):
        return pl.BlockSpec((1, L, d), lambda b, h, c: (b, c, col0 + off + h))

    def tail(off):
        return pl.BlockSpec((1, 8, d), lambda b, h, c: (b, jnp.maximum(c * tail_blocks - 1, 0), col0 + off + h))

    return pl.pallas_call(
        functools.partial(_mlstm_kernel, chunk=L),
        grid=(B, H, S // L),
        in_specs=[
            blk(0), blk(H), blk(2 * H), blk(3 * H), tail(0), tail(H),
            pl.BlockSpec((1, 1, L, 2), lambda b, h, c: (b, h, c, 0)),
            pl.BlockSpec((1, 1, 2, L), lambda b, h, c: (b, h, 0, c)),
            pl.BlockSpec((1, 1, 2), lambda b, h, c: (h, 0, 0)),
            pl.BlockSpec((1, 2, 1), lambda b, h, c: (h, 0, 0)),
            pl.BlockSpec((1, ML_CONV, 2 * d), lambda b, h, c: (h, 0, 0)),
            pl.BlockSpec((1, 1, d), lambda b, h, c: (h, 0, 0)),
        ],
        out_specs=pl.BlockSpec((1, L, d), lambda b, h, c: (b, c, h)),
        out_shape=jax.ShapeDtypeStruct((B, S, H * d), BF16),
        scratch_shapes=[pltpu.VMEM((d, d), F32), pltpu.VMEM((1, d), F32), pltpu.VMEM((1, 1), F32)],
        compiler_params=_cparams("parallel", "parallel", "arbitrary"),
        name="mlstm",
    )(act, act, act, act, act, act, gates_col, gates_row, b_col, b_row, conv_w, norm_g)


def _compress_kernel(x_ref, pos_ref, w1_ref, w2_ref, o_ref):
    x = x_ref[0, 0, 0].astype(F32)
    n = x.shape[0]
    top = _dot((x + pos_ref[0, 0]).astype(BF16), w1_ref[0, 0])
    bot = _dot((x + pos_ref[0, 1]).astype(BF16), w1_ref[0, 1])
    pre = top + pltpu.roll(bot, n - 1, 0)
    hid = jax.nn.gelu(pre, approximate=True)
    out = _dot(hid.astype(BF16), w2_ref[0])
    row = lax.broadcasted_iota(jnp.int32, out.shape, 0)
    o_ref[0, 0, 0] = jnp.where(row < n - 1, out, 0.0).astype(o_ref.dtype)


def _compress(chunks, pos, w1, w2):
    _, B, G, n, width = chunks.shape
    hidden = w1.shape[-1]
    dh = w2.shape[-1]
    return pl.pallas_call(
        _compress_kernel,
        grid=(2, B, G),
        in_specs=[
            pl.BlockSpec((1, 1, 1, n, width), lambda a, b, g: (a, b, g, 0, 0)),
            pl.BlockSpec((1, 2, 1, width), lambda a, b, g: (a, 0, 0, 0)),
            pl.BlockSpec((1, 2, width, hidden), lambda a, b, g: (a, 0, 0, 0)),
            pl.BlockSpec((1, hidden, dh), lambda a, b, g: (a, 0, 0)),
        ],
        out_specs=pl.BlockSpec((1, 1, 1, n, dh), lambda a, b, g: (a, b, g, 0, 0)),
        out_shape=jax.ShapeDtypeStruct((2, B, G, n, dh), BF16),
        compiler_params=_cparams("parallel", "parallel", "parallel"),
        name="nsa_compress",
    )(chunks, pos, w1, w2)


def _nsa_kernel(q_ref, gate_ref, kc_ref, vc_ref, ks_ref, vs_ref, kw_ref, vw_ref, ovl_ref, out_ref,
                s_scr, mrun_scr, lrun_scr, acc_scr, *, tq, seq):
    HG, dh = NSA_GROUP_HEADS, NSA_HEAD_DIM
    g = pl.program_id(1)
    i = pl.program_id(2)
    t0 = i * tq
    rows = HG * tq
    n_cmp_pad = kc_ref.shape[2]
    n_slc = seq // SEL_BLOCK
    half = tq // 2

    q_all = q_ref[0]
    q_heads = [(q_all[:, x * dh:(x + 1) * dh].astype(F32) * (dh ** -0.5)).astype(BF16) for x in range(HG)]
    qs = jnp.concatenate(q_heads, axis=0)

    def head_slope(x):
        sl = jnp.float32(0.0)
        for hh in range(NSA_HEADS):
            sl = jnp.where(g * HG + x == hh, 2.0 ** (-8.0 * (hh + 1) / NSA_HEADS), sl)
        return sl

    slopes = [head_slope(x) for x in range(HG)]
    slope = jnp.concatenate([jnp.full((tq, 1), slopes[x], F32) for x in range(HG)], axis=0)
    r_idx = lax.broadcasted_iota(jnp.int32, (rows, 1), 0)
    t_col = t0 + r_idx % tq

    def alibi_cols(x, width):
        lane = lax.broadcasted_iota(jnp.int32, (tq, width), 1)
        return jnp.where(lane == 0, slopes[x] * SEL_BLOCK, jnp.where(lane == 1, slopes[x], 0.0)).astype(BF16)

    sc = _dot_nt(qs, kc_ref[0, 0])
    j_idx = lax.broadcasted_iota(jnp.int32, (1, n_cmp_pad), 1)
    cmp_end = j_idx * CMP_STRIDE + (CMP_BLOCK - 1)
    cmp_center = (j_idx * CMP_STRIDE).astype(F32) + (CMP_BLOCK - 1) / 2
    valid_c = cmp_end <= t_col
    sc = jnp.where(valid_c, sc + slope * cmp_center, NEG_INF)
    e = jnp.exp(sc - jnp.max(sc, axis=1, keepdims=True))
    p_cmp = jnp.where(valid_c, e / jnp.sum(e, axis=1, keepdims=True), 0.0)
    o_cmp = _dot(p_cmp.astype(BF16), vc_ref[0, 0])

    p_sum = p_cmp[0:tq]
    for x in range(1, HG):
        p_sum = p_sum + p_cmp[x * tq:(x + 1) * tq]
    p_hi = p_sum.astype(BF16)
    p_lo = (p_sum - p_hi.astype(F32)).astype(BF16)
    ovl = ovl_ref[...]
    p_slc = _dot_nt(ovl, p_hi) + _dot_nt(ovl, p_lo)
    blk = lax.broadcasted_iota(jnp.int32, (n_slc, tq), 0)
    cur = (t0 + lax.broadcasted_iota(jnp.int32, (n_slc, tq), 1)) // SEL_BLOCK
    forced = (blk == 0) | (blk == cur) | (blk == cur - 1)
    score = jnp.where(forced, SEL_FORCE, jnp.where(blk > cur, -SEL_FORCE, p_slc))
    rank = jnp.zeros((n_slc, tq), F32)
    for kk in range(n_slc):
        sk = score[kk:kk + 1, :]
        ge = jnp.where(sk >= score, 1.0, 0.0)
        gt = jnp.where(sk > score, 1.0, 0.0)
        rank = rank + jnp.where(blk > kk, ge, gt)
    sel_bias = jnp.where(rank < min(SEL_COUNT, n_slc), 0.0, SEL_MASK)
    sel_bias = sel_bias.T.astype(BF16)
    q_aug = jnp.concatenate(
        [jnp.concatenate([q_heads[x], sel_bias, alibi_cols(x, 2 * dh)], axis=1) for x in range(HG)], axis=0)

    rr = lax.broadcasted_iota(jnp.int32, (rows, tq), 0) % tq
    ll = lax.broadcasted_iota(jnp.int32, (rows, tq), 1)
    causal = ll <= rr

    mrun_scr[...] = jnp.full(mrun_scr.shape, NEG_INF, F32)
    lrun_scr[...] = jnp.zeros(lrun_scr.shape, F32)
    acc_scr[...] = jnp.zeros(acc_scr.shape, F32)

    def scores(c, diagonal):
        start = pl.multiple_of(c * tq, tq)
        s = _dot_nt(q_aug, ks_ref[0, 0, pl.ds(start, tq), :])
        if diagonal:
            s = jnp.where(causal, s, NEG_INF)
        s_scr[c] = s
        mrun_scr[...] = jnp.maximum(mrun_scr[...], jnp.maximum(s[:, :half], s[:, half:]))

    def pass1(c, carry):
        scores(c, False)
        return carry

    lax.fori_loop(0, i, pass1, 0)
    scores(i, True)
    m_row = jnp.max(mrun_scr[...], axis=1, keepdims=True)
    mrun_scr[...] = jnp.broadcast_to(m_row, mrun_scr.shape)

    def pass2(c, carry):
        start = pl.multiple_of(c * tq, tq)
        s = s_scr[c]
        m_b = mrun_scr[...]
        p0 = jnp.exp(s[:, :half] - m_b)
        p1 = jnp.exp(s[:, half:] - m_b)
        lrun_scr[...] += p0 + p1
        p = jnp.concatenate([p0, p1], axis=1).astype(BF16)
        acc_scr[...] += _dot(p, vs_ref[0, 0, pl.ds(start, tq), :])
        return carry

    lax.fori_loop(0, i + 1, pass2, 0)
    o_slc = acc_scr[...] / jnp.sum(lrun_scr[...], axis=1, keepdims=True)

    qw_aug = jnp.concatenate([jnp.concatenate([q_heads[x], alibi_cols(x, dh)], axis=1) for x in range(HG)], axis=0)
    start_a = pl.multiple_of(jnp.maximum(t0 - 2 * tq, 0), tq)
    start_b = pl.multiple_of(jnp.maximum(t0 - tq, 0), tq)
    start_c = pl.multiple_of(t0, tq)
    s_a = jnp.where((ll > rr) & (i >= 2), _dot_nt(qw_aug, kw_ref[0, 0, pl.ds(start_a, tq), :]), NEG_INF)
    s_b = jnp.where(i >= 1, _dot_nt(qw_aug, kw_ref[0, 0, pl.ds(start_b, tq), :]), NEG_INF)
    s_c = jnp.where(causal, _dot_nt(qw_aug, kw_ref[0, 0, pl.ds(start_c, tq), :]), NEG_INF)
    m_w = jnp.maximum(jnp.maximum(jnp.max(s_a, axis=1, keepdims=True), jnp.max(s_b, axis=1, keepdims=True)),
                      jnp.max(s_c, axis=1, keepdims=True))
    e_a = jnp.exp(s_a - m_w)
    e_b = jnp.exp(s_b - m_w)
    e_c = jnp.exp(s_c - m_w)
    l_w = jnp.sum(e_a, axis=1, keepdims=True) + jnp.sum(e_b, axis=1, keepdims=True) + jnp.sum(e_c, axis=1, keepdims=True)
    o_win = (_dot(e_a.astype(BF16), vw_ref[0, 0, pl.ds(start_a, tq), :])
             + _dot(e_b.astype(BF16), vw_ref[0, 0, pl.ds(start_b, tq), :])
             + _dot(e_c.astype(BF16), vw_ref[0, 0, pl.ds(start_c, tq), :])) / l_w

    gates = jax.nn.sigmoid(gate_ref[0, 0])

    def gate_col(r):
        return jnp.concatenate([gates[:, x * NSA_N_BRANCH + r:x * NSA_N_BRANCH + r + 1] for x in range(HG)], axis=0)

    o = gate_col(0) * o_cmp + gate_col(1) * o_slc + gate_col(2) * o_win
    out_ref[0] = jnp.concatenate([o[x * tq:(x + 1) * tq] for x in range(HG)], axis=1).astype(out_ref.dtype)


def _nsa_attention(act, gates, kc, vc, ks_aug, vs, kw_aug, vw, overlap_t, *, q_col0, tq):
    B, S, _ = act.shape
    G, HG, dh = NSA_KV_GROUPS, NSA_GROUP_HEADS, NSA_HEAD_DIM
    rows = HG * tq
    assert WINDOW == 2 * tq, "the window branch reads exactly the two chunks before the diagonal one"

    def full(a):
        return pl.BlockSpec((1, 1) + a.shape[2:], lambda b, g, i: (b, g, 0, 0))

    return pl.pallas_call(
        functools.partial(_nsa_kernel, tq=tq, seq=S),
        grid=(B, G, S // tq),
        in_specs=[
            pl.BlockSpec((1, tq, HG * dh), lambda b, g, i: (b, i, q_col0 + g)),
            pl.BlockSpec((1, 1, tq, HG * NSA_N_BRANCH), lambda b, g, i: (b, g, i, 0)),
            full(kc), full(vc), full(ks_aug), full(vs), full(kw_aug), full(vw),
            pl.BlockSpec(overlap_t.shape, lambda b, g, i: (0, 0)),
        ],
        out_specs=pl.BlockSpec((1, tq, HG * dh), lambda b, g, i: (b, i, g)),
        out_shape=jax.ShapeDtypeStruct((B, S, NSA_WIDTH), BF16),
        scratch_shapes=[pltpu.VMEM((S // tq, rows, tq), F32), pltpu.VMEM((rows, tq // 2), F32),
                        pltpu.VMEM((rows, tq // 2), F32), pltpu.VMEM((rows, dh), F32)],
        compiler_params=_cparams("parallel", "parallel", "arbitrary"),
        name="nsa_attn",
    )(act, gates, kc, vc, ks_aug, vs, kw_aug, vw, overlap_t)


def _mix_out_kernel(x_ref, hm_ref, hn_ref, gm_ref, gn_ref, wm_ref, wn_ref, wo_ref, o_ref):
    y_ml = _dot(hm_ref[...], wm_ref[...])
    y_ns = _dot(hn_ref[...], wn_ref[...])
    mix = jax.nn.sigmoid(gm_ref[...].astype(F32)) * y_ml + jax.nn.sigmoid(gn_ref[...].astype(F32)) * y_ns
    o_ref[...] = x_ref[...] + _dot(mix.astype(BF16), wo_ref[...])


def _mix_out(x2d, h_ml, h_ns, act2d, wm, wn, wo, *, gate_col0, tm):
    n, d = x2d.shape

    def res(w):
        return pl.BlockSpec(w.shape, lambda i: (0, 0))

    return pl.pallas_call(
        _mix_out_kernel,
        grid=(n // tm,),
        in_specs=[
            pl.BlockSpec((tm, d), lambda i: (i, 0)),
            pl.BlockSpec((tm, h_ml.shape[1]), lambda i: (i, 0)),
            pl.BlockSpec((tm, h_ns.shape[1]), lambda i: (i, 0)),
            pl.BlockSpec((tm, d), lambda i: (i, gate_col0)),
            pl.BlockSpec((tm, d), lambda i: (i, gate_col0 + 1)),
            res(wm), res(wn), res(wo),
        ],
        out_specs=pl.BlockSpec((tm, d), lambda i: (i, 0)),
        out_shape=jax.ShapeDtypeStruct((n, d), F32),
        compiler_params=_cparams("parallel"),
        name="mix_out",
    )(x2d, h_ml, h_ns, act2d, act2d, wm, wn, wo)


def _xattn_kernel(x_ref, g_ref, wq_ref, k_ref, v_ref, wo_ref, o_ref):
    x = x_ref[...]
    d = x.shape[1]
    dh = d // XA_HEADS
    h = _rms(x, g_ref[...]).astype(BF16)
    q = (_dot(h, wq_ref[...]) * (dh ** -0.5)).astype(BF16)
    k = k_ref[0]
    v = v_ref[0]
    outs = []
    for hd in range(XA_HEADS):
        sl = slice(hd * dh, (hd + 1) * dh)
        s = _dot_nt(q[:, sl], k[:, sl])
        e = jnp.exp(s - jnp.max(s, axis=1, keepdims=True))
        p = e / jnp.sum(e, axis=1, keepdims=True)
        outs.append(_dot(p.astype(BF16), v[:, sl]))
    o = jnp.concatenate(outs, axis=1).astype(BF16)
    o_ref[...] = x + _dot(o, wo_ref[...])


def _xattn(x2d, g, wq, kv, wo, *, seq, tm):
    n, d = x2d.shape
    n_mem = kv.shape[1]
    tiles_per_batch = seq // tm
    return pl.pallas_call(
        _xattn_kernel,
        grid=(n // tm,),
        in_specs=[
            pl.BlockSpec((tm, d), lambda i: (i, 0)),
            pl.BlockSpec((1, d), lambda i: (0, 0)),
            pl.BlockSpec(wq.shape, lambda i: (0, 0)),
            pl.BlockSpec((1, n_mem, d), lambda i: (i // tiles_per_batch, 0, 0)),
            pl.BlockSpec((1, n_mem, d), lambda i: (i // tiles_per_batch, 0, 1)),
            pl.BlockSpec(wo.shape, lambda i: (0, 0)),
        ],
        out_specs=pl.BlockSpec((tm, d), lambda i: (i, 0)),
        out_shape=jax.ShapeDtypeStruct((n, d), F32),
        compiler_params=_cparams("parallel"),
        name="xattn",
    )(x2d, g.reshape(1, d), wq, kv, kv, wo)


def _route(logits, bg, be):
    G, EG = MOE_GROUPS, MOE_EXPERTS_PER_GROUP
    tm = logits.shape[1]
    lg = logits[0:G] + bg
    eg = jnp.exp(lg - jnp.max(lg, axis=0, keepdims=True))
    pg = eg / jnp.sum(eg, axis=0, keepdims=True)
    g_val = jnp.max(pg, axis=0, keepdims=True)
    g_row = lax.broadcasted_iota(jnp.int32, (G, tm), 0)
    g_idx = jnp.min(jnp.where(pg == g_val, g_row, G), axis=0, keepdims=True)
    el = logits[8:8 + G * EG] + be
    e_in = jnp.zeros((EG, tm), F32)
    for gg in range(G):
        e_in = jnp.where(g_idx == gg, el[gg * EG:(gg + 1) * EG], e_in)
    e_row = lax.broadcasted_iota(jnp.int32, (EG, tm), 0)
    v1 = jnp.max(e_in, axis=0, keepdims=True)
    i1 = jnp.min(jnp.where(e_in == v1, e_row, EG), axis=0, keepdims=True)
    rest = jnp.where(e_row == i1, -jnp.inf, e_in)
    v2 = jnp.max(rest, axis=0, keepdims=True)
    i2 = jnp.min(jnp.where(rest == v2, e_row, EG), axis=0, keepdims=True)
    e2 = jnp.exp(v2 - v1)
    c1 = g_val / (1.0 + e2)
    c2 = g_val * e2 / (1.0 + e2)
    ex = lax.broadcasted_iota(jnp.int32, (G * EG, tm), 0)
    base = g_idx * EG
    return jnp.where(ex == base + i1, c1, 0.0) + jnp.where(ex == base + i2, c2, 0.0)


def _moe_kernel(x_ref, g_ref, wr_ref, bg_ref, be_ref, w1_ref, w3_ref, w2_ref, gf_ref, o_ref,
                h_scr, wt_scr, acc_scr):
    e = pl.program_id(1)

    @pl.when(e == 0)
    def _():
        h = _rms(x_ref[...], g_ref[...])
        h_scr[...] = h.astype(BF16)
        logits = lax.dot_general(wr_ref[...], h, (((1,), (1,)), ((), ())), precision=lax.Precision.HIGHEST,
                                 preferred_element_type=F32)
        wt = _route(logits, bg_ref[...], be_ref[...])
        pad = jnp.zeros((wt_scr.shape[1] - wt.shape[0], wt.shape[1]), F32)
        wt_scr[...] = jnp.concatenate([wt, pad], axis=0).T
        acc_scr[...] = jnp.zeros_like(acc_scr)

    h = h_scr[...]
    lane = lax.broadcasted_iota(jnp.int32, wt_scr.shape, 1)
    w_col = jnp.sum(jnp.where(lane == e, wt_scr[...], 0.0), axis=1, keepdims=True)
    a = _dot(h, w1_ref[0])
    a = a * jax.nn.sigmoid(a) * _dot(h, w3_ref[0]) * w_col
    acc_scr[...] += _dot(a.astype(BF16), w2_ref[0])

    @pl.when(e == pl.num_programs(1) - 1)
    def _():
        o_ref[...] = _rms(x_ref[...] + acc_scr[...], gf_ref[...])


def _moe(x2d, g, wr, bg, be, w1, w3, w2, g_final, *, tm):
    n, d = x2d.shape
    E, _, hid = w1.shape
    return pl.pallas_call(
        _moe_kernel,
        grid=(n // tm, E),
        in_specs=[
            pl.BlockSpec((tm, d), lambda i, e: (i, 0)),
            pl.BlockSpec((1, d), lambda i, e: (0, 0)),
            pl.BlockSpec(wr.shape, lambda i, e: (0, 0)),
            pl.BlockSpec(bg.shape, lambda i, e: (0, 0)),
            pl.BlockSpec(be.shape, lambda i, e: (0, 0)),
            pl.BlockSpec((1, d, hid), lambda i, e: (e, 0, 0)),
            pl.BlockSpec((1, d, hid), lambda i, e: (e, 0, 0)),
            pl.BlockSpec((1, hid, d), lambda i, e: (e, 0, 0)),
            pl.BlockSpec((1, d), lambda i, e: (0, 0)),
        ],
        out_specs=pl.BlockSpec((tm, d), lambda i, e: (i, 0)),
        out_shape=jax.ShapeDtypeStruct((n, d), F32),
        scratch_shapes=[pltpu.VMEM((tm, d), BF16), pltpu.VMEM((tm, 128), F32), pltpu.VMEM((tm, d), F32)],
        compiler_params=_cparams("parallel", "arbitrary"),
        name="moe",
    )(x2d, g.reshape(1, d), wr, bg, be, w1, w3, w2, g_final.reshape(1, d))


def _overlap_t(seq):
    n_cmp_pad = seq // CMP_STRIDE
    cs = np.arange(n_cmp_pad)[None, :] * CMP_STRIDE
    ss = np.arange(seq // SEL_BLOCK)[:, None] * SEL_BLOCK
    ov = np.clip(np.minimum(cs + CMP_BLOCK, ss + SEL_BLOCK) - np.maximum(cs, ss), 0, None) / CMP_BLOCK
    ov[:, n_cmp_pad - 1] = 0.0
    return jnp.asarray(ov, BF16)


def _layer(x, mem, norm_mix, w_in, conv_qk, b_igate, b_fgate, mlstm_norm, cmp_pos_k, cmp_pos_v, cmp_k_w1, cmp_k_w2,
           cmp_v_w1, cmp_v_w2, w_br_mlstm, w_br_nsa, w_mix_out, norm_xattn, norm_mem, xa_wq, xa_wkv, xa_wo, norm_ffn,
           router_group_w, router_group_b, router_expert_w, router_expert_b, moe_w1, moe_w3, moe_w2, norm_final):
    B, S, D = x.shape
    N = B * S
    H, d = ML_HEADS, ML_HEAD_DIM
    G, HG, dh = NSA_KV_GROUPS, NSA_GROUP_HEADS, NSA_HEAD_DIM
    x2d = x.reshape(N, D)

    o_mlqkvo = 0
    o_mlif = 4 * ML_WIDTH
    o_nsq = o_mlif + 2 * H
    o_kv = o_nsq + NSA_WIDTH
    o_nsg = o_kv + 6 * NSA_KV_WIDTH
    o_merge = o_nsg + NSA_N_BRANCH * NSA_HEADS
    w_act = jnp.concatenate([w_in[:, o_merge:o_merge + 2 * D], w_in[:, o_mlqkvo:o_mlif], w_in[:, o_nsq:o_kv],
                             w_in[:, o_kv:o_nsg]], axis=1).astype(BF16)
    n_small = 2 * H + NSA_N_BRANCH * NSA_HEADS
    w_small = jnp.concatenate([w_in[:, o_mlif:o_nsq], w_in[:, o_nsg:o_merge], jnp.zeros((D, 128 - n_small), F32)],
                              axis=1).astype(BF16)
    act = _norm_matmul(x2d, norm_mix, w_act, BF16, tm=1024, tn=768)
    small = _norm_matmul(x2d, norm_mix, w_small, F32, tm=1024, tn=128)
    act3 = act.reshape(B, S, act.shape[1])
    c_ml = 2 * D
    c_nsq = c_ml + 4 * ML_WIDTH
    c_kv = c_nsq + NSA_WIDTH

    gi = small[:, 0:H].reshape(B, S, H)
    gf = small[:, H:2 * H].reshape(B, S, H)
    g_if = jnp.stack([gi, gf], axis=-1)
    gates_col = g_if.transpose(0, 2, 1, 3)
    gates_row = g_if.transpose(0, 2, 3, 1)
    bias = jnp.stack([b_igate, b_fgate], axis=-1)
    conv_w = jnp.concatenate([conv_qk[:, :ML_WIDTH].reshape(ML_CONV, H, d), conv_qk[:, ML_WIDTH:].reshape(ML_CONV, H, d)],
                             axis=-1).transpose(1, 0, 2)
    h_ml = _mlstm(act3, gates_col, gates_row, bias.reshape(H, 1, 2), bias.reshape(H, 2, 1), conv_w,
                  mlstm_norm.reshape(H, 1, d), col0=c_ml // d, chunk=256)

    kv = act3[:, :, c_kv:c_kv + 6 * NSA_KV_WIDTH].reshape(B, S, 6, G, dh).transpose(2, 0, 3, 1, 4)
    chunks = kv[0:2].reshape(2, B, G, S // CMP_STRIDE, CMP_STRIDE * dh)
    pos = jnp.stack([cmp_pos_k, cmp_pos_v]).reshape(2, 2, 1, CMP_STRIDE * dh)
    w1c = jnp.stack([cmp_k_w1, cmp_v_w1]).reshape(2, 2, CMP_STRIDE * dh, -1).astype(BF16)
    w2c = jnp.stack([cmp_k_w2, cmp_v_w2]).astype(BF16)
    kvc = _compress(chunks, pos, w1c, w2c)
    n_slc = S // SEL_BLOCK
    onehot = (jnp.arange(S)[:, None] // SEL_BLOCK == jnp.arange(n_slc)[None, :]).astype(BF16)
    tok = jnp.arange(S)
    kpos = jnp.zeros((S, 2 * dh), BF16).at[:, 0].set((tok // SEL_BLOCK).astype(BF16)).at[:, 1].set((tok % SEL_BLOCK).astype(BF16))
    ks_aug = jnp.concatenate([kv[2], jnp.broadcast_to(onehot, (B, G, S, n_slc)), jnp.broadcast_to(kpos, (B, G, S, 2 * dh))],
                             axis=-1)
    kw_aug = jnp.concatenate([kv[4], jnp.broadcast_to(kpos[:, :dh], (B, G, S, dh))], axis=-1)
    ns_gates = small[:, 2 * H:n_small].reshape(B, S, G, HG * NSA_N_BRANCH).transpose(0, 2, 1, 3)
    h_ns = _nsa_attention(act3, ns_gates, kvc[0], kvc[1], ks_aug, kv[3], kw_aug, kv[5], _overlap_t(S),
                          q_col0=c_nsq // (HG * dh), tq=256)

    x1 = _mix_out(x2d, h_ml.reshape(N, ML_WIDTH), h_ns.reshape(N, NSA_WIDTH), act, w_br_mlstm.astype(BF16),
                  w_br_nsa.astype(BF16), w_mix_out.astype(BF16), gate_col0=0, tm=512)

    n_mem = mem.shape[1]
    kv_mem = _norm_matmul(mem.reshape(B * n_mem, D), norm_mem, xa_wkv.astype(BF16), BF16, tm=B * n_mem, tn=512)
    x2 = _xattn(x1, norm_xattn, xa_wq.astype(BF16), kv_mem.reshape(B, n_mem, 2 * D), xa_wo.astype(BF16), seq=S, tm=512)

    wr = jnp.zeros((ROUTER_ROWS, D), F32)
    wr = wr.at[0:MOE_GROUPS].set(router_group_w.T).at[8:8 + MOE_EXPERTS].set(router_expert_w.T)
    return _moe(x2, norm_ffn, wr, router_group_b.reshape(MOE_GROUPS, 1), router_expert_b.reshape(MOE_EXPERTS, 1),
                moe_w1.astype(BF16), moe_w3.astype(BF16), moe_w2.astype(BF16), norm_final, tm=1024).reshape(B, S, D)


def kernel(x, mem, norm_mix, w_in, conv_qk, b_igate, b_fgate, mlstm_norm, cmp_pos_k, cmp_pos_v, cmp_k_w1, cmp_k_w2, cmp_v_w1, cmp_v_w2, w_br_mlstm, w_br_nsa, w_mix_out, norm_xattn, norm_mem, xa_wq, xa_wkv, xa_wo, norm_ffn, router_group_w, router_group_b, router_expert_w, router_expert_b, moe_w1, moe_w3, moe_w2, norm_final):
    depth = w_in.shape[0]
    assert depth == 1, "the fused final norm assumes a single layer"
    layer = 0
    return _layer(x, mem, norm_mix[layer], w_in[layer], conv_qk[layer], b_igate[layer], b_fgate[layer], mlstm_norm[layer],
                  cmp_pos_k[layer], cmp_pos_v[layer], cmp_k_w1[layer], cmp_k_w2[layer], cmp_v_w1[layer], cmp_v_w2[layer],
                  w_br_mlstm[layer], w_br_nsa[layer], w_mix_out[layer], norm_xattn[layer], norm_mem[layer], xa_wq[layer],
                  xa_wkv[layer], xa_wo[layer], norm_ffn[layer], router_group_w[layer], router_group_b[layer],
                  router_expert_w[layer], router_expert_b[layer], moe_w1[layer], moe_w3[layer], moe_w2[layer], norm_final)
```

```python
import functools
import math

import numpy as np
import jax
import jax.numpy as jnp
from jax import lax
from jax.experimental import pallas as pl
from jax.experimental.pallas import tpu as pltpu

F32 = jnp.float32
BF16 = jnp.bfloat16

ML_HEADS = 4
ML_HEAD_DIM = 128
ML_WIDTH = ML_HEADS * ML_HEAD_DIM
ML_CONV = 4
NSA_HEADS = 8
NSA_KV_GROUPS = 2
NSA_HEAD_DIM = 64
NSA_GROUP_HEADS = NSA_HEADS // NSA_KV_GROUPS
NSA_WIDTH = NSA_HEADS * NSA_HEAD_DIM
NSA_KV_WIDTH = NSA_KV_GROUPS * NSA_HEAD_DIM
NSA_N_BRANCH = 3
CMP_BLOCK = 32
CMP_STRIDE = 16
SEL_BLOCK = 64
SEL_COUNT = 16
SEL_FORCE = 1e4
WINDOW = 512
XA_HEADS = 4
MOE_GROUPS = 4
MOE_EXPERTS_PER_GROUP = 4
MOE_EXPERTS = MOE_GROUPS * MOE_EXPERTS_PER_GROUP
RMS_EPS = 1e-6
NEG_INF = -1e30
SEL_MASK = -float(2 ** 30)

VMEM_LIMIT_BYTES = 56 * 1024 * 1024
ROUTER_ROWS = 32


def _cparams(*sem):
    return pltpu.CompilerParams(dimension_semantics=sem, vmem_limit_bytes=VMEM_LIMIT_BYTES)


def _rms(x, g):
    return x * lax.rsqrt(jnp.mean(x * x, axis=-1, keepdims=True) + RMS_EPS) * g


def _dot(a, b):
    return jnp.dot(a, b, preferred_element_type=F32)


def _dot_nt(a, b):
    return lax.dot_general(a, b, (((1,), (1,)), ((), ())), preferred_element_type=F32)


def _dot_tn(a, b):
    return lax.dot_general(a, b, (((0,), (0,)), ((), ())), preferred_element_type=F32)


def _norm_matmul_kernel(x_ref, g_ref, w_ref, o_ref, h_ref):
    @pl.when(pl.program_id(1) == 0)
    def _():
        h_ref[...] = _rms(x_ref[...], g_ref[...]).astype(BF16)

    o_ref[...] = _dot(h_ref[...], w_ref[...]).astype(o_ref.dtype)


def _norm_matmul(x2d, g, w, out_dtype, tm, tn):
    m, d = x2d.shape
    n = w.shape[1]
    return pl.pallas_call(
        _norm_matmul_kernel,
        grid=(m // tm, n // tn),
        in_specs=[
            pl.BlockSpec((tm, d), lambda i, j: (i, 0)),
            pl.BlockSpec((1, d), lambda i, j: (0, 0)),
            pl.BlockSpec((d, tn), lambda i, j: (0, j)),
        ],
        out_specs=pl.BlockSpec((tm, tn), lambda i, j: (i, j)),
        out_shape=jax.ShapeDtypeStruct((m, n), out_dtype),
        scratch_shapes=[pltpu.VMEM((tm, d), BF16)],
        compiler_params=_cparams("parallel", "arbitrary"),
        name="norm_matmul",
    )(x2d, g.reshape(1, d), w)


def _mlstm_kernel(q_ref, k_ref, v_ref, o_ref, qt_ref, kt_ref, gcol_ref, grow_ref, bcol_ref, brow_ref,
                  cw_ref, ng_ref, out_ref, c_scr, n_scr, m_scr, *, chunk):
    L = chunk
    c = pl.program_id(2)

    @pl.when(c == 0)
    def _():
        c_scr[...] = jnp.zeros_like(c_scr)
        n_scr[...] = jnp.zeros_like(n_scr)
        m_scr[...] = jnp.zeros_like(m_scr)

    row8 = lax.broadcasted_iota(jnp.int32, (8, ML_HEAD_DIM), 0)

    def conv_silu(x_ref, tail_ref, w):
        x = x_ref[0].astype(F32)
        tail = jnp.where(c > 0, tail_ref[0].astype(F32), 0.0)
        y = x * w[ML_CONV - 1:ML_CONV, :]
        for j in range(ML_CONV - 1):
            s = ML_CONV - 1 - j
            r = pltpu.roll(x, s, 0)
            head = jnp.where(row8 < s, pltpu.roll(tail, s, 0), r[:8])
            y = y + jnp.concatenate([head, r[8:]], axis=0) * w[j:j + 1, :]
        return y * jax.nn.sigmoid(y)

    cw = cw_ref[0]
    q = conv_silu(q_ref, qt_ref, cw[:, :ML_HEAD_DIM])
    k = conv_silu(k_ref, kt_ref, cw[:, ML_HEAD_DIM:]) * (ML_HEAD_DIM ** -0.5)
    v = v_ref[0]
    qb = q.astype(BF16)
    kb = k.astype(BF16)

    gcol = gcol_ref[0, 0] + bcol_ref[0]
    grow = grow_ref[0, 0] + brow_ref[0]
    i_col = gcol[:, 0:1]
    f_col = jax.nn.log_sigmoid(gcol[:, 1:2])
    i_row = grow[0:1, :]
    f_row = jax.nn.log_sigmoid(grow[1:2, :])

    t_idx = lax.broadcasted_iota(jnp.int32, (L, L), 0)
    s_idx = lax.broadcasted_iota(jnp.int32, (L, L), 1)
    tri = s_idx <= t_idx
    b_col = jnp.sum(jnp.where(tri, f_row, 0.0), axis=1, keepdims=True)
    b_row = jnp.sum(jnp.where(t_idx <= s_idx, f_col, 0.0), axis=0, keepdims=True)
    dlog = jnp.where(tri, b_col + (i_row - b_row), NEG_INF)
    m_prev = m_scr[...]
    inter = b_col + m_prev
    mt = jnp.maximum(jnp.max(dlog, axis=1, keepdims=True), inter)
    w_intra = jnp.exp(dlog - mt)
    w_inter = jnp.exp(inter - mt)

    s = _dot_nt(qb, kb) * w_intra
    cmat = c_scr[...]
    nvec = n_scr[...]
    num = _dot(s.astype(BF16), v) + w_inter * _dot(qb, cmat.astype(BF16))
    den = jnp.sum(s, axis=1, keepdims=True) + w_inter * jnp.sum(q * nvec, axis=1, keepdims=True)
    h = num / jnp.maximum(jnp.abs(den), jnp.exp(-mt))

    bl = jnp.sum(f_row, axis=1, keepdims=True)
    logw = bl - b_col + i_col
    m_new = jnp.maximum(bl + m_prev, jnp.max(logw, axis=0, keepdims=True))
    decay = jnp.exp(bl + m_prev - m_new)
    kw = k * jnp.exp(logw - m_new)
    c_scr[...] = decay * cmat + _dot_tn(kw.astype(BF16), v)
    n_scr[...] = decay * nvec + jnp.sum(kw, axis=0, keepdims=True)
    m_scr[...] = m_new

    hn = _rms(h, ng_ref[0])
    out_ref[0] = (jax.nn.sigmoid(o_ref[0].astype(F32)) * hn).astype(out_ref.dtype)


def _mlstm(act, gates_col, gates_row, b_col, b_row, conv_w, norm_g, *, col0, chunk):
    B, S, _ = act.shape
    H, d, L = ML_HEADS, ML_HEAD_DIM, chunk
    tail_blocks = L // 8

    def blk(off):
        return pl.BlockSpec((1, L, d), lambda b, h, c: (b, c, col0 + off + h))

    def tail(off):
        return pl.BlockSpec((1, 8, d), lambda b, h, c: (b, jnp.maximum(c * tail_blocks - 1, 0), col0 + off + h))

    return pl.pallas_call(
        functools.partial(_mlstm_kernel, chunk=L),
        grid=(B, H, S // L),
        in_specs=[
            blk(0), blk(H), blk(2 * H), blk(3 * H), tail(0), tail(H),
            pl.BlockSpec((1, 1, L, 2), lambda b, h, c: (b, h, c, 0)),
            pl.BlockSpec((1, 1, 2, L), lambda b, h, c: (b, h, 0, c)),
            pl.BlockSpec((1, 1, 2), lambda b, h, c: (h, 0, 0)),
            pl.BlockSpec((1, 2, 1), lambda b, h, c: (h, 0, 0)),
            pl.BlockSpec((1, ML_CONV, 2 * d), lambda b, h, c: (h, 0, 0)),
            pl.BlockSpec((1, 1, d), lambda b, h, c: (h, 0, 0)),
        ],
        out_specs=pl.BlockSpec((1, L, d), lambda b, h, c: (b, c, h)),
        out_shape=jax.ShapeDtypeStruct((B, S, H * d), BF16),
        scratch_shapes=[pltpu.VMEM((d, d), F32), pltpu.VMEM((1, d), F32), pltpu.VMEM((1, 1), F32)],
        compiler_params=_cparams("parallel", "parallel", "arbitrary"),
        name="mlstm",
    )(act, act, act, act, act, act, gates_col, gates_row, b_col, b_row, conv_w, norm_g)


def _compress_kernel(x_ref, pos_ref, w1_ref, w2_ref, o_ref):
    x = x_ref[0, 0, 0].astype(F32)
    n = x.shape[0]
    top = _dot((x + pos_ref[0, 0]).astype(BF16), w1_ref[0, 0])
    bot = _dot((x + pos_ref[0, 1]).astype(BF16), w1_ref[0, 1])
    pre = top + pltpu.roll(bot, n - 1, 0)
    hid = jax.nn.gelu(pre, approximate=True)
    out = _dot(hid.astype(BF16), w2_ref[0])
    row = lax.broadcasted_iota(jnp.int32, out.shape, 0)
    o_ref[0, 0, 0] = jnp.where(row < n - 1, out, 0.0).astype(o_ref.dtype)


def _compress(chunks, pos, w1, w2):
    _, B, G, n, width = chunks.shape
    hidden = w1.shape[-1]
    dh = w2.shape[-1]
    return pl.pallas_call(
        _compress_kernel,
        grid=(2, B, G),
        in_specs=[
            pl.BlockSpec((1, 1, 1, n, width), lambda a, b, g: (a, b, g, 0, 0)),
            pl.BlockSpec((1, 2, 1, width), lambda a, b, g: (a, 0, 0, 0)),
            pl.BlockSpec((1, 2, width, hidden), lambda a, b, g: (a, 0, 0, 0)),
            pl.BlockSpec((1, hidden, dh), lambda a, b, g: (a, 0, 0)),
        ],
        out_specs=pl.BlockSpec((1, 1, 1, n, dh), lambda a, b, g: (a, b, g, 0, 0)),
        out_shape=jax.ShapeDtypeStruct((2, B, G, n, dh), BF16),
        compiler_params=_cparams("parallel", "parallel", "parallel"),
        name="nsa_compress",
    )(chunks, pos, w1, w2)


def _nsa_kernel(q_ref, gate_ref, kc_ref, vc_ref, ks_ref, vs_ref, kw_ref, vw_ref, ovl_ref, out_ref,
                s_scr, mrun_scr, lrun_scr, acc_scr, *, tq, seq):
    HG, dh = NSA_GROUP_HEADS, NSA_HEAD_DIM
    g = pl.program_id(1)
    i = pl.program_id(2)
    t0 = i * tq
    rows = HG * tq
    n_cmp_pad = kc_ref.shape[2]
    n_slc = seq // SEL_BLOCK
    half = tq // 2

    q_all = q_ref[0]
    q_heads = [(q_all[:, x * dh:(x + 1) * dh].astype(F32) * (dh ** -0.5)).astype(BF16) for x in range(HG)]
    qs = jnp.concatenate(q_heads, axis=0)

    def head_slope(x):
        sl = jnp.float32(0.0)
        for hh in range(NSA_HEADS):
            sl = jnp.where(g * HG + x == hh, 2.0 ** (-8.0 * (hh + 1) / NSA_HEADS), sl)
        return sl

    slopes = [head_slope(x) for x in range(HG)]
    slope = jnp.concatenate([jnp.full((tq, 1), slopes[x], F32) for x in range(HG)], axis=0)
    r_idx = lax.broadcasted_iota(jnp.int32, (rows, 1), 0)
    t_col = t0 + r_idx % tq

    def alibi_cols(x, width):
        lane = lax.broadcasted_iota(jnp.int32, (tq, width), 1)
        return jnp.where(lane == 0, slopes[x] * SEL_BLOCK, jnp.where(lane == 1, slopes[x], 0.0)).astype(BF16)

    sc = _dot_nt(qs, kc_ref[0, 0])
    j_idx = lax.broadcasted_iota(jnp.int32, (1, n_cmp_pad), 1)
    cmp_end = j_idx * CMP_STRIDE + (CMP_BLOCK - 1)
    cmp_center = (j_idx * CMP_STRIDE).astype(F32) + (CMP_BLOCK - 1) / 2
    valid_c = cmp_end <= t_col
    sc = jnp.where(valid_c, sc + slope * cmp_center, NEG_INF)
    e = jnp.exp(sc - jnp.max(sc, axis=1, keepdims=True))
    p_cmp = jnp.where(valid_c, e / jnp.sum(e, axis=1, keepdims=True), 0.0)
    o_cmp = _dot(p_cmp.astype(BF16), vc_ref[0, 0])

    p_sum = p_cmp[0:tq]
    for x in range(1, HG):
        p_sum = p_sum + p_cmp[x * tq:(x + 1) * tq]
    p_hi = p_sum.astype(BF16)
    p_lo = (p_sum - p_hi.astype(F32)).astype(BF16)
    ovl = ovl_ref[...]
    p_slc = _dot_nt(ovl, p_hi) + _dot_nt(ovl, p_lo)
    blk = lax.broadcasted_iota(jnp.int32, (n_slc, tq), 0)
    cur = (t0 + lax.broadcasted_iota(jnp.int32, (n_slc, tq), 1)) // SEL_BLOCK
    forced = (blk == 0) | (blk == cur) | (blk == cur - 1)
    score = jnp.where(forced, SEL_FORCE, jnp.where(blk > cur, -SEL_FORCE, p_slc))
    rank = jnp.zeros((n_slc, tq), F32)
    for kk in range(n_slc):
        sk = score[kk:kk + 1, :]
        ge = jnp.where(sk >= score, 1.0, 0.0)
        gt = jnp.where(sk > score, 1.0, 0.0)
        rank = rank + jnp.where(blk > kk, ge, gt)
    sel_bias = jnp.where(rank < min(SEL_COUNT, n_slc), 0.0, SEL_MASK)
    sel_bias = sel_bias.T.astype(BF16)
    q_aug = jnp.concatenate(
        [jnp.concatenate([q_heads[x], sel_bias, alibi_cols(x, 2 * dh)], axis=1) for x in range(HG)], axis=0)

    rr = lax.broadcasted_iota(jnp.int32, (rows, tq), 0) % tq
    ll = lax.broadcasted_iota(jnp.int32, (rows, tq), 1)
    causal = ll <= rr

    mrun_scr[...] = jnp.full(mrun_scr.shape, NEG_INF, F32)
    lrun_scr[...] = jnp.zeros(lrun_scr.shape, F32)
    acc_scr[...] = jnp.zeros(acc_scr.shape, F32)

    def scores(c, diagonal):
        start = pl.multiple_of(c * tq, tq)
        s = _dot_nt(q_aug, ks_ref[0, 0, pl.ds(start, tq), :])
        if diagonal:
            s = jnp.where(causal, s, NEG_INF)
        s_scr[c] = s
        mrun_scr[...] = jnp.maximum(mrun_scr[...], jnp.maximum(s[:, :half], s[:, half:]))

    def pass1(c, carry):
        scores(c, False)
        return carry

    lax.fori_loop(0, i, pass1, 0)
    scores(i, True)
    m_row = jnp.max(mrun_scr[...], axis=1, keepdims=True)
    mrun_scr[...] = jnp.broadcast_to(m_row, mrun_scr.shape)

    def pass2(c, carry):
        start = pl.multiple_of(c * tq, tq)
        s = s_scr[c]
        m_b = mrun_scr[...]
        p0 = jnp.exp(s[:, :half] - m_b)
        p1 = jnp.exp(s[:, half:] - m_b)
        lrun_scr[...] += p0 + p1
        p = jnp.concatenate([p0, p1], axis=1).astype(BF16)
        acc_scr[...] += _dot(p, vs_ref[0, 0, pl.ds(start, tq), :])
        return carry

    lax.fori_loop(0, i + 1, pass2, 0)
    o_slc = acc_scr[...] / jnp.sum(lrun_scr[...], axis=1, keepdims=True)

    qw_aug = jnp.concatenate([jnp.concatenate([q_heads[x], alibi_cols(x, dh)], axis=1) for x in range(HG)], axis=0)
    start_a = pl.multiple_of(jnp.maximum(t0 - 2 * tq, 0), tq)
    start_b = pl.multiple_of(jnp.maximum(t0 - tq, 0), tq)
    start_c = pl.multiple_of(t0, tq)
    s_a = jnp.where((ll > rr) & (i >= 2), _dot_nt(qw_aug, kw_ref[0, 0, pl.ds(start_a, tq), :]), NEG_INF)
    s_b = jnp.where(i >= 1, _dot_nt(qw_aug, kw_ref[0, 0, pl.ds(start_b, tq), :]), NEG_INF)
    s_c = jnp.where(causal, _dot_nt(qw_aug, kw_ref[0, 0, pl.ds(start_c, tq), :]), NEG_INF)
    m_w = jnp.maximum(jnp.maximum(jnp.max(s_a, axis=1, keepdims=True), jnp.max(s_b, axis=1, keepdims=True)),
                      jnp.max(s_c, axis=1, keepdims=True))
    e_a = jnp.exp(s_a - m_w)
    e_b = jnp.exp(s_b - m_w)
    e_c = jnp.exp(s_c - m_w)
    l_w = jnp.sum(e_a, axis=1, keepdims=True) + jnp.sum(e_b, axis=1, keepdims=True) + jnp.sum(e_c, axis=1, keepdims=True)
    o_win = (_dot(e_a.astype(BF16), vw_ref[0, 0, pl.ds(start_a, tq), :])
             + _dot(e_b.astype(BF16), vw_ref[0, 0, pl.ds(start_b, tq), :])
             + _dot(e_c.astype(BF16), vw_ref[0, 0, pl.ds(start_c, tq), :])) / l_w

    gates = jax.nn.sigmoid(gate_ref[0, 0])

    def gate_col(r):
        return jnp.concatenate([gates[:, x * NSA_N_BRANCH + r:x * NSA_N_BRANCH + r + 1] for x in range(HG)], axis=0)

    o = gate_col(0) * o_cmp + gate_col(1) * o_slc + gate_col(2) * o_win
    out_ref[0] = jnp.concatenate([o[x * tq:(x + 1) * tq] for x in range(HG)], axis=1).astype(out_ref.dtype)


def _nsa_attention(act, gates, kc, vc, ks_aug, vs, kw_aug, vw, overlap_t, *, q_col0, tq):
    B, S, _ = act.shape
    G, HG, dh = NSA_KV_GROUPS, NSA_GROUP_HEADS, NSA_HEAD_DIM
    rows = HG * tq
    assert WINDOW == 2 * tq, "the window branch reads exactly the two chunks before the diagonal one"

    def full(a):
        return pl.BlockSpec((1, 1) + a.shape[2:], lambda b, g, i: (b, g, 0, 0))

    return pl.pallas_call(
        functools.partial(_nsa_kernel, tq=tq, seq=S),
        grid=(B, G, S // tq),
        in_specs=[
            pl.BlockSpec((1, tq, HG * dh), lambda b, g, i: (b, i, q_col0 + g)),
            pl.BlockSpec((1, 1, tq, HG * NSA_N_BRANCH), lambda b, g, i: (b, g, i, 0)),
            full(kc), full(vc), full(ks_aug), full(vs), full(kw_aug), full(vw),
            pl.BlockSpec(overlap_t.shape, lambda b, g, i: (0, 0)),
        ],
        out_specs=pl.BlockSpec((1, tq, HG * dh), lambda b, g, i: (b, i, g)),
        out_shape=jax.ShapeDtypeStruct((B, S, NSA_WIDTH), BF16),
        scratch_shapes=[pltpu.VMEM((S // tq, rows, tq), F32), pltpu.VMEM((rows, tq // 2), F32),
                        pltpu.VMEM((rows, tq // 2), F32), pltpu.VMEM((rows, dh), F32)],
        compiler_params=_cparams("parallel", "parallel", "arbitrary"),
        name="nsa_attn",
    )(act, gates, kc, vc, ks_aug, vs, kw_aug, vw, overlap_t)


def _mix_out_kernel(x_ref, hm_ref, hn_ref, gm_ref, gn_ref, wm_ref, wn_ref, wo_ref, o_ref):
    y_ml = _dot(hm_ref[...], wm_ref[...])
    y_ns = _dot(hn_ref[...], wn_ref[...])
    mix = jax.nn.sigmoid(gm_ref[...].astype(F32)) * y_ml + jax.nn.sigmoid(gn_ref[...].astype(F32)) * y_ns
    o_ref[...] = x_ref[...] + _dot(mix.astype(BF16), wo_ref[...])


def _mix_out(x2d, h_ml, h_ns, act2d, wm, wn, wo, *, gate_col0, tm):
    n, d = x2d.shape

    def res(w):
        return pl.BlockSpec(w.shape, lambda i: (0, 0))

    return pl.pallas_call(
        _mix_out_kernel,
        grid=(n // tm,),
        in_specs=[
            pl.BlockSpec((tm, d), lambda i: (i, 0)),
            pl.BlockSpec((tm, h_ml.shape[1]), lambda i: (i, 0)),
            pl.BlockSpec((tm, h_ns.shape[1]), lambda i: (i, 0)),
            pl.BlockSpec((tm, d), lambda i: (i, gate_col0)),
            pl.BlockSpec((tm, d), lambda i: (i, gate_col0 + 1)),
            res(wm), res(wn), res(wo),
        ],
        out_specs=pl.BlockSpec((tm, d), lambda i: (i, 0)),
        out_shape=jax.ShapeDtypeStruct((n, d), F32),
        compiler_params=_cparams("parallel"),
        name="mix_out",
    )(x2d, h_ml, h_ns, act2d, act2d, wm, wn, wo)


def _xattn_kernel(x_ref, g_ref, wq_ref, k_ref, v_ref, wo_ref, o_ref):
    x = x_ref[...]
    d = x.shape[1]
    dh = d // XA_HEADS
    h = _rms(x, g_ref[...]).astype(BF16)
    q = (_dot(h, wq_ref[...]) * (dh ** -0.5)).astype(BF16)
    k = k_ref[0]
    v = v_ref[0]
    outs = []
    for hd in range(XA_HEADS):
        sl = slice(hd * dh, (hd + 1) * dh)
        s = _dot_nt(q[:, sl], k[:, sl])
        e = jnp.exp(s - jnp.max(s, axis=1, keepdims=True))
        p = e / jnp.sum(e, axis=1, keepdims=True)
        outs.append(_dot(p.astype(BF16), v[:, sl]))
    o = jnp.concatenate(outs, axis=1).astype(BF16)
    o_ref[...] = x + _dot(o, wo_ref[...])


def _xattn(x2d, g, wq, kv, wo, *, seq, tm):
    n, d = x2d.shape
    n_mem = kv.shape[1]
    tiles_per_batch = seq // tm
    return pl.pallas_call(
        _xattn_kernel,
        grid=(n // tm,),
        in_specs=[
            pl.BlockSpec((tm, d), lambda i: (i, 0)),
            pl.BlockSpec((1, d), lambda i: (0, 0)),
            pl.BlockSpec(wq.shape, lambda i: (0, 0)),
            pl.BlockSpec((1, n_mem, d), lambda i: (i // tiles_per_batch, 0, 0)),
            pl.BlockSpec((1, n_mem, d), lambda i: (i // tiles_per_batch, 0, 1)),
            pl.BlockSpec(wo.shape, lambda i: (0, 0)),
        ],
        out_specs=pl.BlockSpec((tm, d), lambda i: (i, 0)),
        out_shape=jax.ShapeDtypeStruct((n, d), F32),
        compiler_params=_cparams("parallel"),
        name="xattn",
    )(x2d, g.reshape(1, d), wq, kv, kv, wo)


def _route(logits, bg, be):
    G, EG = MOE_GROUPS, MOE_EXPERTS_PER_GROUP
    tm = logits.shape[1]
    lg = logits[0:G] + bg
    eg = jnp.exp(lg - jnp.max(lg, axis=0, keepdims=True))
    pg = eg / jnp.sum(eg, axis=0, keepdims=True)
    g_val = jnp.max(pg, axis=0, keepdims=True)
    g_row = lax.broadcasted_iota(jnp.int32, (G, tm), 0)
    g_idx = jnp.min(jnp.where(pg == g_val, g_row, G), axis=0, keepdims=True)
    el = logits[8:8 + G * EG] + be
    e_in = jnp.zeros((EG, tm), F32)
    for gg in range(G):
        e_in = jnp.where(g_idx == gg, el[gg * EG:(gg + 1) * EG], e_in)
    e_row = lax.broadcasted_iota(jnp.int32, (EG, tm), 0)
    v1 = jnp.max(e_in, axis=0, keepdims=True)
    i1 = jnp.min(jnp.where(e_in == v1, e_row, EG), axis=0, keepdims=True)
    rest = jnp.where(e_row == i1, -jnp.inf, e_in)
    v2 = jnp.max(rest, axis=0, keepdims=True)
    i2 = jnp.min(jnp.where(rest == v2, e_row, EG), axis=0, keepdims=True)
    e2 = jnp.exp(v2 - v1)
    c1 = g_val / (1.0 + e2)
    c2 = g_val * e2 / (1.0 + e2)
    ex = lax.broadcasted_iota(jnp.int32, (G * EG, tm), 0)
    base = g_idx * EG
    return jnp.where(ex == base + i1, c1, 0.0) + jnp.where(ex == base + i2, c2, 0.0), g_idx


def _moe_kernel(x_ref, g_ref, wr_ref, bg_ref, be_ref, before_ref, w1_ref, w3_ref, w2_ref, gf_ref, o_ref,
                h_scr, slot_scr, wt_scr, acc_scr, *, cap):
    G, EG = MOE_GROUPS, MOE_EXPERTS_PER_GROUP
    grp = pl.program_id(1)
    tm = x_ref.shape[0]

    @pl.when(grp == 0)
    def _():
        h = _rms(x_ref[...], g_ref[...])
        h_scr[...] = h.astype(BF16)
        logits = lax.dot_general(wr_ref[...], h, (((1,), (1,)), ((), ())), precision=lax.Precision.HIGHEST,
                                 preferred_element_type=F32)
        wt, g_idx = _route(logits, bg_ref[...], be_ref[...])
        member = jnp.where(lax.broadcasted_iota(jnp.int32, (8, tm), 0) == g_idx, 1.0, 0.0)
        prefix = _dot(member.astype(BF16), before_ref[...])
        slots = jnp.where(member > 0.0, prefix, -1.0)
        zeros = jnp.zeros((8 - EG, tm), F32)
        for gg in range(G):
            slot_scr[gg] = jnp.broadcast_to(slots[gg:gg + 1], (8, tm))
            wt_scr[gg] = jnp.concatenate([wt[gg * EG:(gg + 1) * EG], zeros], axis=0)
        acc_scr[...] = jnp.zeros_like(acc_scr)

    slot = slot_scr[grp][0:1, :]
    n_rows = jnp.max(slot).astype(jnp.int32) + 1
    row = lax.broadcasted_iota(jnp.int32, (cap, tm), 0).astype(F32)

    def one_pass(p, carry):
        onehot = row == (slot - (p * cap).astype(F32))
        pick = jnp.where(onehot, 1.0, 0.0)
        pick_b = pick.astype(BF16)
        hsub = _dot(pick_b, h_scr[...]).astype(BF16)
        wsub = lax.dot_general(pick, wt_scr[grp], (((1,), (1,)), ((), ())), precision=lax.Precision.HIGHEST,
                               preferred_element_type=F32)
        y = jnp.zeros((cap, x_ref.shape[1]), F32)
        for e in range(EG):
            a = _dot(hsub, w1_ref[e])
            a = a * jax.nn.sigmoid(a) * _dot(hsub, w3_ref[e]) * wsub[:, e:e + 1]
            y = y + _dot(a.astype(BF16), w2_ref[e])
        acc_scr[...] += _dot_tn(pick_b, y.astype(BF16))
        return carry

    lax.fori_loop(0, (n_rows + cap - 1) // cap, one_pass, 0)

    @pl.when(grp == pl.num_programs(1) - 1)
    def _():
        o_ref[...] = _rms(x_ref[...] + acc_scr[...], gf_ref[...])


def _moe(x2d, g, wr, bg, be, w1, w3, w2, g_final, *, tm, cap):
    n, d = x2d.shape
    E, _, hid = w1.shape
    G, EG = MOE_GROUPS, MOE_EXPERTS_PER_GROUP
    before = jnp.asarray(np.arange(tm)[:, None] < np.arange(tm)[None, :], BF16)
    once = pl.Buffered(1)

    def const(shape):
        return pl.BlockSpec(shape, lambda i, e: (0,) * len(shape), pipeline_mode=once)

    return pl.pallas_call(
        functools.partial(_moe_kernel, cap=cap),
        grid=(n // tm, G),
        in_specs=[
            pl.BlockSpec((tm, d), lambda i, e: (i, 0), pipeline_mode=once),
            const((1, d)), const(wr.shape), const(bg.shape), const(be.shape), const((tm, tm)),
            pl.BlockSpec((EG, d, hid), lambda i, e: (e, 0, 0)),
            pl.BlockSpec((EG, d, hid), lambda i, e: (e, 0, 0)),
            pl.BlockSpec((EG, hid, d), lambda i, e: (e, 0, 0)),
            const((1, d)),
        ],
        out_specs=pl.BlockSpec((tm, d), lambda i, e: (i, 0)),
        out_shape=jax.ShapeDtypeStruct((n, d), F32),
        scratch_shapes=[pltpu.VMEM((tm, d), BF16), pltpu.VMEM((G, 8, tm), F32), pltpu.VMEM((G, 8, tm), F32),
                        pltpu.VMEM((tm, d), F32)],
        compiler_params=_cparams("parallel", "arbitrary"),
        name="moe",
    )(x2d, g.reshape(1, d), wr, bg, be, before, w1, w3, w2, g_final.reshape(1, d))


def _overlap_t(seq):
    n_cmp_pad = seq // CMP_STRIDE
    cs = np.arange(n_cmp_pad)[None, :] * CMP_STRIDE
    ss = np.arange(seq // SEL_BLOCK)[:, None] * SEL_BLOCK
    ov = np.clip(np.minimum(cs + CMP_BLOCK, ss + SEL_BLOCK) - np.maximum(cs, ss), 0, None) / CMP_BLOCK
    ov[:, n_cmp_pad - 1] = 0.0
    return jnp.asarray(ov, BF16)


def _layer(x, mem, norm_mix, w_in, conv_qk, b_igate, b_fgate, mlstm_norm, cmp_pos_k, cmp_pos_v, cmp_k_w1, cmp_k_w2,
           cmp_v_w1, cmp_v_w2, w_br_mlstm, w_br_nsa, w_mix_out, norm_xattn, norm_mem, xa_wq, xa_wkv, xa_wo, norm_ffn,
           router_group_w, router_group_b, router_expert_w, router_expert_b, moe_w1, moe_w3, moe_w2, norm_final):
    B, S, D = x.shape
    N = B * S
    H, d = ML_HEADS, ML_HEAD_DIM
    G, HG, dh = NSA_KV_GROUPS, NSA_GROUP_HEADS, NSA_HEAD_DIM
    x2d = x.reshape(N, D)

    o_mlqkvo = 0
    o_mlif = 4 * ML_WIDTH
    o_nsq = o_mlif + 2 * H
    o_kv = o_nsq + NSA_WIDTH
    o_nsg = o_kv + 6 * NSA_KV_WIDTH
    o_merge = o_nsg + NSA_N_BRANCH * NSA_HEADS
    w_act = jnp.concatenate([w_in[:, o_merge:o_merge + 2 * D], w_in[:, o_mlqkvo:o_mlif], w_in[:, o_nsq:o_kv],
                             w_in[:, o_kv:o_nsg]], axis=1).astype(BF16)
    n_small = 2 * H + NSA_N_BRANCH * NSA_HEADS
    w_small = jnp.concatenate([w_in[:, o_mlif:o_nsq], w_in[:, o_nsg:o_merge], jnp.zeros((D, 128 - n_small), F32)],
                              axis=1).astype(BF16)
    act = _norm_matmul(x2d, norm_mix, w_act, BF16, tm=1024, tn=768)
    small = _norm_matmul(x2d, norm_mix, w_small, F32, tm=1024, tn=128)
    act3 = act.reshape(B, S, act.shape[1])
    c_ml = 2 * D
    c_nsq = c_ml + 4 * ML_WIDTH
    c_kv = c_nsq + NSA_WIDTH

    gi = small[:, 0:H].reshape(B, S, H)
    gf = small[:, H:2 * H].reshape(B, S, H)
    g_if = jnp.stack([gi, gf], axis=-1)
    gates_col = g_if.transpose(0, 2, 1, 3)
    gates_row = g_if.transpose(0, 2, 3, 1)
    bias = jnp.stack([b_igate, b_fgate], axis=-1)
    conv_w = jnp.concatenate([conv_qk[:, :ML_WIDTH].reshape(ML_CONV, H, d), conv_qk[:, ML_WIDTH:].reshape(ML_CONV, H, d)],
                             axis=-1).transpose(1, 0, 2)
    h_ml = _mlstm(act3, gates_col, gates_row, bias.reshape(H, 1, 2), bias.reshape(H, 2, 1), conv_w,
                  mlstm_norm.reshape(H, 1, d), col0=c_ml // d, chunk=256)

    kv = act3[:, :, c_kv:c_kv + 6 * NSA_KV_WIDTH].reshape(B, S, 6, G, dh).transpose(2, 0, 3, 1, 4)
    chunks = kv[0:2].reshape(2, B, G, S // CMP_STRIDE, CMP_STRIDE * dh)
    pos = jnp.stack([cmp_pos_k, cmp_pos_v]).reshape(2, 2, 1, CMP_STRIDE * dh)
    w1c = jnp.stack([cmp_k_w1, cmp_v_w1]).reshape(2, 2, CMP_STRIDE * dh, -1).astype(BF16)
    w2c = jnp.stack([cmp_k_w2, cmp_v_w2]).astype(BF16)
    kvc = _compress(chunks, pos, w1c, w2c)
    n_slc = S // SEL_BLOCK
    onehot = (jnp.arange(S)[:, None] // SEL_BLOCK == jnp.arange(n_slc)[None, :]).astype(BF16)
    tok = jnp.arange(S)
    kpos = jnp.zeros((S, 2 * dh), BF16).at[:, 0].set((tok // SEL_BLOCK).astype(BF16)).at[:, 1].set((tok % SEL_BLOCK).astype(BF16))
    ks_aug = jnp.concatenate([kv[2], jnp.broadcast_to(onehot, (B, G, S, n_slc)), jnp.broadcast_to(kpos, (B, G, S, 2 * dh))],
                             axis=-1)
    kw_aug = jnp.concatenate([kv[4], jnp.broadcast_to(kpos[:, :dh], (B, G, S, dh))], axis=-1)
    ns_gates = small[:, 2 * H:n_small].reshape(B, S, G, HG * NSA_N_BRANCH).transpose(0, 2, 1, 3)
    h_ns = _nsa_attention(act3, ns_gates, kvc[0], kvc[1], ks_aug, kv[3], kw_aug, kv[5], _overlap_t(S),
                          q_col0=c_nsq // (HG * dh), tq=256)

    x1 = _mix_out(x2d, h_ml.reshape(N, ML_WIDTH), h_ns.reshape(N, NSA_WIDTH), act, w_br_mlstm.astype(BF16),
                  w_br_nsa.astype(BF16), w_mix_out.astype(BF16), gate_col0=0, tm=512)

    n_mem = mem.shape[1]
    kv_mem = _norm_matmul(mem.reshape(B * n_mem, D), norm_mem, xa_wkv.astype(BF16), BF16, tm=B * n_mem, tn=512)
    x2 = _xattn(x1, norm_xattn, xa_wq.astype(BF16), kv_mem.reshape(B, n_mem, 2 * D), xa_wo.astype(BF16), seq=S, tm=512)

    wr = jnp.zeros((ROUTER_ROWS, D), F32)
    wr = wr.at[0:MOE_GROUPS].set(router_group_w.T).at[8:8 + MOE_EXPERTS].set(router_expert_w.T)
    return _moe(x2, norm_ffn, wr, router_group_b.reshape(MOE_GROUPS, 1), router_expert_b.reshape(MOE_EXPERTS, 1),
                moe_w1.astype(BF16), moe_w3.astype(BF16), moe_w2.astype(BF16), norm_final, tm=1024, cap=320).reshape(B, S, D)


def kernel(x, mem, norm_mix, w_in, conv_qk, b_igate, b_fgate, mlstm_norm, cmp_pos_k, cmp_pos_v, cmp_k_w1, cmp_k_w2, cmp_v_w1, cmp_v_w2, w_br_mlstm, w_br_nsa, w_mix_out, norm_xattn, norm_mem, xa_wq, xa_wkv, xa_wo, norm_ffn, router_group_w, router_group_b, router_expert_w, router_expert_b, moe_w1, moe_w3, moe_w2, norm_final):
    depth = w_in.shape[0]
    assert depth == 1, "the fused final norm assumes a single layer"
    layer = 0
    return _layer(x, mem, norm_mix[layer], w_in[layer], conv_qk[layer], b_igate[layer], b_fgate[layer], mlstm_norm[layer],
                  cmp_pos_k[layer], cmp_pos_v[layer], cmp_k_w1[layer], cmp_k_w2[layer], cmp_v_w1[layer], cmp_v_w2[layer],
                  w_br_mlstm[layer], w_br_nsa[layer], w_mix_out[layer], norm_xattn[layer], norm_mem[layer], xa_wq[layer],
                  xa_wkv[layer], xa_wo[layer], norm_ffn[layer], router_group_w[layer], router_group_b[layer],
                  router_expert_w[layer], router_expert_b[layer], moe_w1[layer], moe_w3[layer], moe_w2[layer], norm_final)
```

```python
import functools
import math

import numpy as np
import jax
import jax.numpy as jnp
from jax import lax
from jax.experimental import pallas as pl
from jax.experimental.pallas import tpu as pltpu

F32 = jnp.float32
BF16 = jnp.bfloat16

ML_HEADS = 4
ML_HEAD_DIM = 128
ML_WIDTH = ML_HEADS * ML_HEAD_DIM
ML_CONV = 4
NSA_HEADS = 8
NSA_KV_GROUPS = 2
NSA_HEAD_DIM = 64
NSA_GROUP_HEADS = NSA_HEADS // NSA_KV_GROUPS
NSA_WIDTH = NSA_HEADS * NSA_HEAD_DIM
NSA_KV_WIDTH = NSA_KV_GROUPS * NSA_HEAD_DIM
NSA_N_BRANCH = 3
CMP_BLOCK = 32
CMP_STRIDE = 16
SEL_BLOCK = 64
SEL_COUNT = 16
SEL_FORCE = 1e4
WINDOW = 512
XA_HEADS = 4
MOE_GROUPS = 4
MOE_EXPERTS_PER_GROUP = 4
MOE_EXPERTS = MOE_GROUPS * MOE_EXPERTS_PER_GROUP
RMS_EPS = 1e-6
NEG_INF = -1e30
SEL_MASK = -float(2 ** 30)

VMEM_LIMIT_BYTES = 56 * 1024 * 1024
ROUTER_ROWS = 32


def _cparams(*sem):
    return pltpu.CompilerParams(dimension_semantics=sem, vmem_limit_bytes=VMEM_LIMIT_BYTES)


def _rms(x, g):
    return x * lax.rsqrt(jnp.mean(x * x, axis=-1, keepdims=True) + RMS_EPS) * g


def _dot(a, b):
    return jnp.dot(a, b, preferred_element_type=F32)


def _dot_nt(a, b):
    return lax.dot_general(a, b, (((1,), (1,)), ((), ())), preferred_element_type=F32)


def _dot_tn(a, b):
    return lax.dot_general(a, b, (((0,), (0,)), ((), ())), preferred_element_type=F32)


def _norm_matmul_kernel(x_ref, g_ref, w_ref, o_ref, h_ref):
    @pl.when(pl.program_id(1) == 0)
    def _():
        h_ref[...] = _rms(x_ref[...], g_ref[...]).astype(BF16)

    o_ref[...] = _dot(h_ref[...], w_ref[...]).astype(o_ref.dtype)


def _norm_matmul(x2d, g, w, out_dtype, tm, tn):
    m, d = x2d.shape
    n = w.shape[1]
    return pl.pallas_call(
        _norm_matmul_kernel,
        grid=(m // tm, n // tn),
        in_specs=[
            pl.BlockSpec((tm, d), lambda i, j: (i, 0)),
            pl.BlockSpec((1, d), lambda i, j: (0, 0)),
            pl.BlockSpec((d, tn), lambda i, j: (0, j)),
        ],
        out_specs=pl.BlockSpec((tm, tn), lambda i, j: (i, j)),
        out_shape=jax.ShapeDtypeStruct((m, n), out_dtype),
        scratch_shapes=[pltpu.VMEM((tm, d), BF16)],
        compiler_params=_cparams("parallel", "arbitrary"),
        name="norm_matmul",
    )(x2d, g.reshape(1, d), w)


def _mlstm_kernel(q_ref, k_ref, v_ref, o_ref, qt_ref, kt_ref, gcol_ref, grow_ref, bcol_ref, brow_ref,
                  cw_ref, ng_ref, out_ref, c_scr, n_scr, m_scr, *, chunk):
    L = chunk
    c = pl.program_id(2)

    @pl.when(c == 0)
    def _():
        c_scr[...] = jnp.zeros_like(c_scr)
        n_scr[...] = jnp.zeros_like(n_scr)
        m_scr[...] = jnp.zeros_like(m_scr)

    row8 = lax.broadcasted_iota(jnp.int32, (8, ML_HEAD_DIM), 0)

    def conv_silu(x_ref, tail_ref, w):
        x = x_ref[0].astype(F32)
        tail = jnp.where(c > 0, tail_ref[0].astype(F32), 0.0)
        y = x * w[ML_CONV - 1:ML_CONV, :]
        for j in range(ML_CONV - 1):
            s = ML_CONV - 1 - j
            r = pltpu.roll(x, s, 0)
            head = jnp.where(row8 < s, pltpu.roll(tail, s, 0), r[:8])
            y = y + jnp.concatenate([head, r[8:]], axis=0) * w[j:j + 1, :]
        return y * jax.nn.sigmoid(y)

    cw = cw_ref[0]
    q = conv_silu(q_ref, qt_ref, cw[:, :ML_HEAD_DIM])
    k = conv_silu(k_ref, kt_ref, cw[:, ML_HEAD_DIM:]) * (ML_HEAD_DIM ** -0.5)
    v = v_ref[0]
    qb = q.astype(BF16)
    kb = k.astype(BF16)

    gcol = gcol_ref[0, 0] + bcol_ref[0]
    grow = grow_ref[0, 0] + brow_ref[0]
    i_col = gcol[:, 0:1]
    f_col = jax.nn.log_sigmoid(gcol[:, 1:2])
    i_row = grow[0:1, :]
    f_row = jax.nn.log_sigmoid(grow[1:2, :])

    t_idx = lax.broadcasted_iota(jnp.int32, (L, L), 0)
    s_idx = lax.broadcasted_iota(jnp.int32, (L, L), 1)
    tri = s_idx <= t_idx
    b_col = jnp.sum(jnp.where(tri, f_row, 0.0), axis=1, keepdims=True)
    b_row = jnp.sum(jnp.where(t_idx <= s_idx, f_col, 0.0), axis=0, keepdims=True)
    dlog = jnp.where(tri, b_col + (i_row - b_row), NEG_INF)
    m_prev = m_scr[...]
    inter = b_col + m_prev
    mt = jnp.maximum(jnp.max(dlog, axis=1, keepdims=True), inter)
    w_intra = jnp.exp(dlog - mt)
    w_inter = jnp.exp(inter - mt)

    s = _dot_nt(qb, kb) * w_intra
    cmat = c_scr[...]
    nvec = n_scr[...]
    num = _dot(s.astype(BF16), v) + w_inter * _dot(qb, cmat.astype(BF16))
    den = jnp.sum(s, axis=1, keepdims=True) + w_inter * jnp.sum(q * nvec, axis=1, keepdims=True)
    h = num / jnp.maximum(jnp.abs(den), jnp.exp(-mt))

    bl = jnp.sum(f_row, axis=1, keepdims=True)
    logw = bl - b_col + i_col
    m_new = jnp.maximum(bl + m_prev, jnp.max(logw, axis=0, keepdims=True))
    decay = jnp.exp(bl + m_prev - m_new)
    kw = k * jnp.exp(logw - m_new)
    c_scr[...] = decay * cmat + _dot_tn(kw.astype(BF16), v)
    n_scr[...] = decay * nvec + jnp.sum(kw, axis=0, keepdims=True)
    m_scr[...] = m_new

    hn = _rms(h, ng_ref[0])
    out_ref[0] = (jax.nn.sigmoid(o_ref[0].astype(F32)) * hn).astype(out_ref.dtype)


def _mlstm(act, gates_col, gates_row, b_col, b_row, conv_w, norm_g, *, col0, chunk):
    B, S, _ = act.shape
    H, d, L = ML_HEADS, ML_HEAD_DIM, chunk
    tail_blocks = L // 8

    def blk(off):
        return pl.BlockSpec((1, L, d), lambda b, h, c: (b, c, col0 + off + h))

    def tail(off):
        return pl.BlockSpec((1, 8, d), lambda b, h, c: (b, jnp.maximum(c * tail_blocks - 1, 0), col0 + off + h))

    return pl.pallas_call(
        functools.partial(_mlstm_kernel, chunk=L),
        grid=(B, H, S // L),
        in_specs=[
            blk(0), blk(H), blk(2 * H), blk(3 * H), tail(0), tail(H),
            pl.BlockSpec((1, 1, L, 2), lambda b, h, c: (b, h, c, 0)),
            pl.BlockSpec((1, 1, 2, L), lambda b, h, c: (b, h, 0, c)),
            pl.BlockSpec((1, 1, 2), lambda b, h, c: (h, 0, 0)),
            pl.BlockSpec((1, 2, 1), lambda b, h, c: (h, 0, 0)),
            pl.BlockSpec((1, ML_CONV, 2 * d), lambda b, h, c: (h, 0, 0)),
            pl.BlockSpec((1, 1, d), lambda b, h, c: (h, 0, 0)),
        ],
        out_specs=pl.BlockSpec((1, L, d), lambda b, h, c: (b, c, h)),
        out_shape=jax.ShapeDtypeStruct((B, S, H * d), BF16),
        scratch_shapes=[pltpu.VMEM((d, d), F32), pltpu.VMEM((1, d), F32), pltpu.VMEM((1, 1), F32)],
        compiler_params=_cparams("parallel", "parallel", "arbitrary"),
        name="mlstm",
    )(act, act, act, act, act, act, gates_col, gates_row, b_col, b_row, conv_w, norm_g)


def _compress_kernel(x_ref, pos_ref, w1_ref, w2_ref, o_ref):
    x = x_ref[0, 0, 0].astype(F32)
    n = x.shape[0]
    top = _dot((x + pos_ref[0, 0]).astype(BF16), w1_ref[0, 0])
    bot = _dot((x + pos_ref[0, 1]).astype(BF16), w1_ref[0, 1])
    pre = top + pltpu.roll(bot, n - 1, 0)
    hid = jax.nn.gelu(pre, approximate=True)
    out = _dot(hid.astype(BF16), w2_ref[0])
    row = lax.broadcasted_iota(jnp.int32, out.shape, 0)
    out = jnp.where(row < n - 1, out, 0.0)
    j = lax.broadcasted_iota(jnp.int32, out.shape, 0)
    lane = lax.broadcasted_iota(jnp.int32, out.shape, 1)
    cols = jnp.where(lane == 0, j // 16, jnp.where(lane == 1, j % 16, jnp.where(lane == 2, 1, 0))).astype(F32)
    cols = jnp.where(pl.program_id(0) == 0, cols, 0.0)
    o_ref[0, 0, 0] = jnp.concatenate([out, cols], axis=1).astype(o_ref.dtype)


def _compress(chunks, pos, w1, w2):
    _, B, G, n, width = chunks.shape
    hidden = w1.shape[-1]
    dh = w2.shape[-1]
    return pl.pallas_call(
        _compress_kernel,
        grid=(2, B, G),
        in_specs=[
            pl.BlockSpec((1, 1, 1, n, width), lambda a, b, g: (a, b, g, 0, 0)),
            pl.BlockSpec((1, 2, 1, width), lambda a, b, g: (a, 0, 0, 0)),
            pl.BlockSpec((1, 2, width, hidden), lambda a, b, g: (a, 0, 0, 0)),
            pl.BlockSpec((1, hidden, dh), lambda a, b, g: (a, 0, 0)),
        ],
        out_specs=pl.BlockSpec((1, 1, 1, n, 2 * dh), lambda a, b, g: (a, b, g, 0, 0)),
        out_shape=jax.ShapeDtypeStruct((2, B, G, n, 2 * dh), BF16),
        compiler_params=_cparams("parallel", "parallel", "parallel"),
        name="nsa_compress",
    )(chunks, pos, w1, w2)


def _nsa_kernel(q_ref, gate_ref, kc_ref, vc_ref, ksl_ref, vsl_ref, kwn_ref, vwn_ref, kconst_ref, ovl_ref, out_ref,
                ks_scr, vs_scr, kw_scr, vw_scr, s_scr, mrun_scr, lrun_scr, acc_scr, *, tq, seq):
    HG, dh = NSA_GROUP_HEADS, NSA_HEAD_DIM
    g = pl.program_id(1)
    i = pl.program_id(2)
    t0 = i * tq
    rows = HG * tq
    n_cmp_pad = kc_ref.shape[2]
    n_slc = seq // SEL_BLOCK
    half = tq // 2
    blocks_per_chunk = tq // SEL_BLOCK

    def assemble(lo):
        ks_scr[...] = jnp.concatenate([ksl_ref[0][:, lo:lo + dh], kconst_ref[...]], axis=1)
        kw_scr[...] = jnp.concatenate([kwn_ref[0][:, lo:lo + dh], kconst_ref[:, n_slc:n_slc + dh]], axis=1)
        vs_scr[...] = vsl_ref[0][:, lo:lo + dh]
        vw_scr[...] = vwn_ref[0][:, lo:lo + dh]

    for gg in range(NSA_KV_GROUPS):
        @pl.when((i == 0) & (g == gg))
        def _(gg=gg):
            assemble(gg * dh)

    q_all = q_ref[0]
    q_heads = [(q_all[:, x * dh:(x + 1) * dh].astype(F32) * (dh ** -0.5)).astype(BF16) for x in range(HG)]

    def head_slope(x):
        sl = jnp.float32(0.0)
        for hh in range(NSA_HEADS):
            sl = jnp.where(g * HG + x == hh, 2.0 ** (-8.0 * (hh + 1) / NSA_HEADS), sl)
        return sl

    slopes = [head_slope(x) for x in range(HG)]

    def alibi_cols(x, width, coef):
        lane = lax.broadcasted_iota(jnp.int32, (tq, width), 1)
        v = jnp.zeros((tq, width), F32)
        for c, val in enumerate(coef):
            v = jnp.where(lane == c, slopes[x] * val, v)
        return v.astype(BF16)

    def stack_q(width, coef, extra=None):
        parts = []
        for x in range(HG):
            cols = [q_heads[x]] + ([extra] if extra is not None else []) + [alibi_cols(x, width, coef)]
            parts.append(jnp.concatenate(cols, axis=1))
        return jnp.concatenate(parts, axis=0)

    q_cmp = stack_q(dh, (16.0 * CMP_STRIDE, 1.0 * CMP_STRIDE, (CMP_BLOCK - 1) / 2))
    sc = _dot_nt(q_cmp, kc_ref[0, 0])
    t_c = t0 + lax.broadcasted_iota(jnp.int32, (rows, n_cmp_pad), 0) % tq
    cmp_end = lax.broadcasted_iota(jnp.int32, (rows, n_cmp_pad), 1) * CMP_STRIDE + (CMP_BLOCK - 1)
    sc = jnp.where(cmp_end <= t_c, sc, NEG_INF)
    e = jnp.exp(sc - jnp.max(sc, axis=1, keepdims=True))
    t_row = t0 + lax.broadcasted_iota(jnp.int32, (rows, 1), 0) % tq
    any_valid = jnp.where(t_row >= CMP_BLOCK - 1, 1.0, 0.0)
    p_cmp = e * (any_valid / jnp.sum(e, axis=1, keepdims=True))
    o_cmp = _dot(p_cmp.astype(BF16), vc_ref[0, 0])[:, :dh]

    p_sum = p_cmp[0:tq]
    for x in range(1, HG):
        p_sum = p_sum + p_cmp[x * tq:(x + 1) * tq]
    p_hi = p_sum.astype(BF16)
    p_lo = (p_sum - p_hi.astype(F32)).astype(BF16)
    ovl = ovl_ref[...]
    p_slc = _dot_nt(ovl, p_hi) + _dot_nt(ovl, p_lo)
    blk = lax.broadcasted_iota(jnp.int32, (n_slc, tq), 0)
    cur = (t0 + lax.broadcasted_iota(jnp.int32, (n_slc, tq), 1)) // SEL_BLOCK
    forced = (blk == 0) | (blk == cur) | (blk == cur - 1)
    score = jnp.where(forced, SEL_FORCE, jnp.where(blk > cur, -SEL_FORCE, p_slc))
    n_tiles = n_slc // 8
    tiles = [score[8 * a:8 * a + 8] for a in range(n_tiles)]
    ranks = [jnp.zeros((8, tq), F32) for _ in range(n_tiles)]
    sub = lax.broadcasted_iota(jnp.int32, (8, tq), 0)
    for kk in range(n_slc):
        sk = score[kk:kk + 1, :]
        for a in range(n_tiles):
            if a < kk // 8:
                before = sk > tiles[a]
            elif a > kk // 8:
                before = sk >= tiles[a]
            else:
                before = jnp.where(sub > kk % 8, jnp.where(sk >= tiles[a], 1.0, 0.0), jnp.where(sk > tiles[a], 1.0, 0.0)) > 0.5
            ranks[a] = ranks[a] + jnp.where(before, 1.0, 0.0)
    rank = jnp.concatenate(ranks, axis=0)
    selected = rank < min(SEL_COUNT, n_slc)
    sel_bias = jnp.where(selected, 0.0, SEL_MASK).T.astype(BF16)
    first_blk = jnp.min(jnp.where(selected & (blk >= blocks_per_chunk), blk, n_slc))
    c_lo = jnp.minimum(jnp.maximum(first_blk // blocks_per_chunk, 1), i)
    q_aug = stack_q(2 * dh, (1.0 * SEL_BLOCK, 1.0), extra=sel_bias)

    rr = lax.broadcasted_iota(jnp.int32, (rows, tq), 0) % tq
    ll = lax.broadcasted_iota(jnp.int32, (rows, tq), 1)
    causal = ll <= rr

    mrun_scr[...] = jnp.full(mrun_scr.shape, NEG_INF, F32)
    lrun_scr[...] = jnp.zeros(lrun_scr.shape, F32)
    acc_scr[...] = jnp.zeros(acc_scr.shape, F32)

    def scores(c, diagonal):
        start = pl.multiple_of(c * tq, tq)
        s = _dot_nt(q_aug, ks_scr[pl.ds(start, tq), :])
        if diagonal:
            s = jnp.where(causal, s, NEG_INF)
        s_scr[c] = s
        mrun_scr[...] = jnp.maximum(mrun_scr[...], jnp.maximum(s[:, :half], s[:, half:]))

    def weights(c):
        start = pl.multiple_of(c * tq, tq)
        s = s_scr[c]
        m_b = mrun_scr[...]
        p0 = jnp.exp(s[:, :half] - m_b)
        p1 = jnp.exp(s[:, half:] - m_b)
        lrun_scr[...] += p0 + p1
        p = jnp.concatenate([p0, p1], axis=1).astype(BF16)
        acc_scr[...] += _dot(p, vs_scr[pl.ds(start, tq), :])

    def pass1(c, carry):
        scores(c, False)
        return carry

    def pass2(c, carry):
        weights(c)
        return carry

    @pl.when(i > 0)
    def _():
        scores(0, False)

    lax.fori_loop(c_lo, i, pass1, 0)
    scores(i, True)
    m_row = jnp.max(mrun_scr[...], axis=1, keepdims=True)
    mrun_scr[...] = jnp.broadcast_to(m_row, mrun_scr.shape)

    @pl.when(i > 0)
    def _():
        weights(0)

    lax.fori_loop(c_lo, i + 1, pass2, 0)
    gates = jax.nn.sigmoid(gate_ref[0, 0])

    def gate_col(r):
        return jnp.concatenate([gates[:, x * NSA_N_BRANCH + r:x * NSA_N_BRANCH + r + 1] for x in range(HG)], axis=0)

    o = gate_col(0) * o_cmp + acc_scr[...] * (gate_col(1) / jnp.sum(lrun_scr[...], axis=1, keepdims=True))

    qw_aug = stack_q(dh, (1.0 * SEL_BLOCK, 1.0))
    start_a = pl.multiple_of(jnp.maximum(t0 - 2 * tq, 0), tq)
    start_b = pl.multiple_of(jnp.maximum(t0 - tq, 0), tq)
    start_c = pl.multiple_of(t0, tq)
    s_a = jnp.where((ll > rr) & (i >= 2), _dot_nt(qw_aug, kw_scr[pl.ds(start_a, tq), :]), NEG_INF)
    s_b = jnp.where(i >= 1, _dot_nt(qw_aug, kw_scr[pl.ds(start_b, tq), :]), NEG_INF)
    s_c = jnp.where(causal, _dot_nt(qw_aug, kw_scr[pl.ds(start_c, tq), :]), NEG_INF)
    m_w = jnp.max(jnp.maximum(jnp.maximum(s_a, s_b), s_c), axis=1, keepdims=True)
    e_a = jnp.exp(s_a - m_w)
    e_b = jnp.exp(s_b - m_w)
    e_c = jnp.exp(s_c - m_w)
    l_w = jnp.sum(e_a + e_b + e_c, axis=1, keepdims=True)
    o_win = (_dot(e_a.astype(BF16), vw_scr[pl.ds(start_a, tq), :]) + _dot(e_b.astype(BF16), vw_scr[pl.ds(start_b, tq), :])
             + _dot(e_c.astype(BF16), vw_scr[pl.ds(start_c, tq), :]))
    o = o + o_win * (gate_col(2) / l_w)

    out_ref[0] = jnp.concatenate([o[x * tq:(x + 1) * tq] for x in range(HG)], axis=1).astype(out_ref.dtype)


def _nsa_attention(act, gates, kc, vc, kconst, overlap_t, *, q_col0, kv_col0, tq):
    B, S, _ = act.shape
    G, HG, dh = NSA_KV_GROUPS, NSA_GROUP_HEADS, NSA_HEAD_DIM
    n_slc = S // SEL_BLOCK
    rows = HG * tq
    assert WINDOW == 2 * tq, "the window branch reads exactly the two chunks before the diagonal one"

    def kv_spec(off):
        return pl.BlockSpec((1, S, G * dh), lambda b, g, i: (b, 0, kv_col0 + off))

    def full(a):
        return pl.BlockSpec((1, 1) + a.shape[2:], lambda b, g, i: (b, g, 0, 0))

    return pl.pallas_call(
        functools.partial(_nsa_kernel, tq=tq, seq=S),
        grid=(B, G, S // tq),
        in_specs=[
            pl.BlockSpec((1, tq, HG * dh), lambda b, g, i: (b, i, q_col0 + g)),
            pl.BlockSpec((1, 1, tq, HG * NSA_N_BRANCH), lambda b, g, i: (b, g, i, 0)),
            full(kc), full(vc), kv_spec(0), kv_spec(1), kv_spec(2), kv_spec(3),
            pl.BlockSpec(kconst.shape, lambda b, g, i: (0, 0)),
            pl.BlockSpec(overlap_t.shape, lambda b, g, i: (0, 0)),
        ],
        out_specs=pl.BlockSpec((1, tq, HG * dh), lambda b, g, i: (b, i, g)),
        out_shape=jax.ShapeDtypeStruct((B, S, NSA_WIDTH), BF16),
        scratch_shapes=[pltpu.VMEM((S, dh + n_slc + 2 * dh), BF16), pltpu.VMEM((S, dh), BF16),
                        pltpu.VMEM((S, 2 * dh), BF16), pltpu.VMEM((S, dh), BF16),
                        pltpu.VMEM((S // tq, rows, tq), F32), pltpu.VMEM((rows, tq // 2), F32),
                        pltpu.VMEM((rows, tq // 2), F32), pltpu.VMEM((rows, dh), F32)],
        compiler_params=_cparams("parallel", "parallel", "arbitrary"),
        name="nsa_attn",
    )(act, gates, kc, vc, act, act, act, act, kconst, overlap_t)


def _mix_out_kernel(x_ref, hm_ref, hn_ref, gm_ref, gn_ref, wm_ref, wn_ref, wo_ref, o_ref):
    y_ml = _dot(hm_ref[...], wm_ref[...])
    y_ns = _dot(hn_ref[...], wn_ref[...])
    mix = jax.nn.sigmoid(gm_ref[...].astype(F32)) * y_ml + jax.nn.sigmoid(gn_ref[...].astype(F32)) * y_ns
    o_ref[...] = x_ref[...] + _dot(mix.astype(BF16), wo_ref[...])


def _mix_out(x2d, h_ml, h_ns, act2d, wm, wn, wo, *, gate_col0, tm):
    n, d = x2d.shape

    def res(w):
        return pl.BlockSpec(w.shape, lambda i: (0, 0))

    return pl.pallas_call(
        _mix_out_kernel,
        grid=(n // tm,),
        in_specs=[
            pl.BlockSpec((tm, d), lambda i: (i, 0)),
            pl.BlockSpec((tm, h_ml.shape[1]), lambda i: (i, 0)),
            pl.BlockSpec((tm, h_ns.shape[1]), lambda i: (i, 0)),
            pl.BlockSpec((tm, d), lambda i: (i, gate_col0)),
            pl.BlockSpec((tm, d), lambda i: (i, gate_col0 + 1)),
            res(wm), res(wn), res(wo),
        ],
        out_specs=pl.BlockSpec((tm, d), lambda i: (i, 0)),
        out_shape=jax.ShapeDtypeStruct((n, d), F32),
        compiler_params=_cparams("parallel"),
        name="mix_out",
    )(x2d, h_ml, h_ns, act2d, act2d, wm, wn, wo)


def _xattn_kernel(x_ref, g_ref, wq_ref, k_ref, v_ref, wo_ref, o_ref):
    x = x_ref[...]
    d = x.shape[1]
    dh = d // XA_HEADS
    h = _rms(x, g_ref[...]).astype(BF16)
    q = (_dot(h, wq_ref[...]) * (dh ** -0.5)).astype(BF16)
    k = k_ref[0]
    v = v_ref[0]
    outs = []
    for hd in range(XA_HEADS):
        sl = slice(hd * dh, (hd + 1) * dh)
        s = _dot_nt(q[:, sl], k[:, sl])
        e = jnp.exp(s - jnp.max(s, axis=1, keepdims=True))
        p = e / jnp.sum(e, axis=1, keepdims=True)
        outs.append(_dot(p.astype(BF16), v[:, sl]))
    o = jnp.concatenate(outs, axis=1).astype(BF16)
    o_ref[...] = x + _dot(o, wo_ref[...])


def _xattn(x2d, g, wq, kv, wo, *, seq, tm):
    n, d = x2d.shape
    n_mem = kv.shape[1]
    tiles_per_batch = seq // tm
    return pl.pallas_call(
        _xattn_kernel,
        grid=(n // tm,),
        in_specs=[
            pl.BlockSpec((tm, d), lambda i: (i, 0)),
            pl.BlockSpec((1, d), lambda i: (0, 0)),
            pl.BlockSpec(wq.shape, lambda i: (0, 0)),
            pl.BlockSpec((1, n_mem, d), lambda i: (i // tiles_per_batch, 0, 0)),
            pl.BlockSpec((1, n_mem, d), lambda i: (i // tiles_per_batch, 0, 1)),
            pl.BlockSpec(wo.shape, lambda i: (0, 0)),
        ],
        out_specs=pl.BlockSpec((tm, d), lambda i: (i, 0)),
        out_shape=jax.ShapeDtypeStruct((n, d), F32),
        compiler_params=_cparams("parallel"),
        name="xattn",
    )(x2d, g.reshape(1, d), wq, kv, kv, wo)


def _route(logits, bg, be):
    G, EG = MOE_GROUPS, MOE_EXPERTS_PER_GROUP
    tm = logits.shape[1]
    lg = logits[0:G] + bg
    eg = jnp.exp(lg - jnp.max(lg, axis=0, keepdims=True))
    pg = eg / jnp.sum(eg, axis=0, keepdims=True)
    g_val = jnp.max(pg, axis=0, keepdims=True)
    g_row = lax.broadcasted_iota(jnp.int32, (G, tm), 0)
    g_idx = jnp.min(jnp.where(pg == g_val, g_row, G), axis=0, keepdims=True)
    el = logits[8:8 + G * EG] + be
    e_in = jnp.zeros((EG, tm), F32)
    for gg in range(G):
        e_in = jnp.where(g_idx == gg, el[gg * EG:(gg + 1) * EG], e_in)
    e_row = lax.broadcasted_iota(jnp.int32, (EG, tm), 0)
    v1 = jnp.max(e_in, axis=0, keepdims=True)
    i1 = jnp.min(jnp.where(e_in == v1, e_row, EG), axis=0, keepdims=True)
    rest = jnp.where(e_row == i1, -jnp.inf, e_in)
    v2 = jnp.max(rest, axis=0, keepdims=True)
    i2 = jnp.min(jnp.where(rest == v2, e_row, EG), axis=0, keepdims=True)
    e2 = jnp.exp(v2 - v1)
    c1 = g_val / (1.0 + e2)
    c2 = g_val * e2 / (1.0 + e2)
    ex = lax.broadcasted_iota(jnp.int32, (G * EG, tm), 0)
    base = g_idx * EG
    return jnp.where(ex == base + i1, c1, 0.0) + jnp.where(ex == base + i2, c2, 0.0), g_idx


def _moe_kernel(x_ref, g_ref, wr_ref, bg_ref, be_ref, before_ref, w1_ref, w3_ref, w2_ref, gf_ref, o_ref,
                h_scr, slot_scr, wt_scr, acc_scr, *, cap):
    G, EG = MOE_GROUPS, MOE_EXPERTS_PER_GROUP
    grp = pl.program_id(1)
    tm = x_ref.shape[0]

    @pl.when(grp == 0)
    def _():
        h = _rms(x_ref[...], g_ref[...])
        h_scr[...] = h.astype(BF16)
        logits = lax.dot_general(wr_ref[...], h, (((1,), (1,)), ((), ())), precision=lax.Precision.HIGHEST,
                                 preferred_element_type=F32)
        wt, g_idx = _route(logits, bg_ref[...], be_ref[...])
        member = jnp.where(lax.broadcasted_iota(jnp.int32, (8, tm), 0) == g_idx, 1.0, 0.0)
        prefix = _dot(member.astype(BF16), before_ref[...])
        slots = jnp.where(member > 0.0, prefix, -1.0)
        zeros = jnp.zeros((8 - EG, tm), F32)
        for gg in range(G):
            slot_scr[gg] = jnp.broadcast_to(slots[gg:gg + 1], (8, tm))
            wt_scr[gg] = jnp.concatenate([wt[gg * EG:(gg + 1) * EG], zeros], axis=0)
        acc_scr[...] = jnp.zeros_like(acc_scr)

    slot = slot_scr[grp][0:1, :]
    n_rows = jnp.max(slot).astype(jnp.int32) + 1
    row = lax.broadcasted_iota(jnp.int32, (cap, tm), 0).astype(F32)

    def one_pass(p, carry):
        onehot = row == (slot - (p * cap).astype(F32))
        pick = jnp.where(onehot, 1.0, 0.0)
        pick_b = pick.astype(BF16)
        hsub = _dot(pick_b, h_scr[...]).astype(BF16)
        wsub = lax.dot_general(pick, wt_scr[grp], (((1,), (1,)), ((), ())), precision=lax.Precision.HIGHEST,
                               preferred_element_type=F32)
        y = jnp.zeros((cap, x_ref.shape[1]), F32)
        for e in range(EG):
            a = _dot(hsub, w1_ref[e])
            a = a * jax.nn.sigmoid(a) * _dot(hsub, w3_ref[e]) * wsub[:, e:e + 1]
            y = y + _dot(a.astype(BF16), w2_ref[e])
        acc_scr[...] += _dot_tn(pick_b, y.astype(BF16))
        return carry

    lax.fori_loop(0, (n_rows + cap - 1) // cap, one_pass, 0)

    @pl.when(grp == pl.num_programs(1) - 1)
    def _():
        o_ref[...] = _rms(x_ref[...] + acc_scr[...], gf_ref[...])


def _moe(x2d, g, wr, bg, be, w1, w3, w2, g_final, *, tm, cap):
    n, d = x2d.shape
    E, _, hid = w1.shape
    G, EG = MOE_GROUPS, MOE_EXPERTS_PER_GROUP
    before = jnp.asarray(np.arange(tm)[:, None] < np.arange(tm)[None, :], BF16)
    once = pl.Buffered(1)

    def const(shape):
        return pl.BlockSpec(shape, lambda i, e: (0,) * len(shape), pipeline_mode=once)

    return pl.pallas_call(
        functools.partial(_moe_kernel, cap=cap),
        grid=(n // tm, G),
        in_specs=[
            pl.BlockSpec((tm, d), lambda i, e: (i, 0), pipeline_mode=once),
            const((1, d)), const(wr.shape), const(bg.shape), const(be.shape), const((tm, tm)),
            pl.BlockSpec((EG, d, hid), lambda i, e: (e, 0, 0)),
            pl.BlockSpec((EG, d, hid), lambda i, e: (e, 0, 0)),
            pl.BlockSpec((EG, hid, d), lambda i, e: (e, 0, 0)),
            const((1, d)),
        ],
        out_specs=pl.BlockSpec((tm, d), lambda i, e: (i, 0)),
        out_shape=jax.ShapeDtypeStruct((n, d), F32),
        scratch_shapes=[pltpu.VMEM((tm, d), BF16), pltpu.VMEM((G, 8, tm), F32), pltpu.VMEM((G, 8, tm), F32),
                        pltpu.VMEM((tm, d), F32)],
        compiler_params=_cparams("parallel", "arbitrary"),
        name="moe",
    )(x2d, g.reshape(1, d), wr, bg, be, before, w1, w3, w2, g_final.reshape(1, d))


def _overlap_t(seq):
    n_cmp_pad = seq // CMP_STRIDE
    cs = np.arange(n_cmp_pad)[None, :] * CMP_STRIDE
    ss = np.arange(seq // SEL_BLOCK)[:, None] * SEL_BLOCK
    ov = np.clip(np.minimum(cs + CMP_BLOCK, ss + SEL_BLOCK) - np.maximum(cs, ss), 0, None) / CMP_BLOCK
    ov[:, n_cmp_pad - 1] = 0.0
    return jnp.asarray(ov, BF16)


def _layer(x, mem, norm_mix, w_in, conv_qk, b_igate, b_fgate, mlstm_norm, cmp_pos_k, cmp_pos_v, cmp_k_w1, cmp_k_w2,
           cmp_v_w1, cmp_v_w2, w_br_mlstm, w_br_nsa, w_mix_out, norm_xattn, norm_mem, xa_wq, xa_wkv, xa_wo, norm_ffn,
           router_group_w, router_group_b, router_expert_w, router_expert_b, moe_w1, moe_w3, moe_w2, norm_final):
    B, S, D = x.shape
    N = B * S
    H, d = ML_HEADS, ML_HEAD_DIM
    G, HG, dh = NSA_KV_GROUPS, NSA_GROUP_HEADS, NSA_HEAD_DIM
    x2d = x.reshape(N, D)

    o_mlqkvo = 0
    o_mlif = 4 * ML_WIDTH
    o_nsq = o_mlif + 2 * H
    o_kv = o_nsq + NSA_WIDTH
    o_nsg = o_kv + 6 * NSA_KV_WIDTH
    o_merge = o_nsg + NSA_N_BRANCH * NSA_HEADS
    w_act = jnp.concatenate([w_in[:, o_merge:o_merge + 2 * D], w_in[:, o_mlqkvo:o_mlif], w_in[:, o_nsq:o_kv],
                             w_in[:, o_kv:o_nsg]], axis=1).astype(BF16)
    n_small = 2 * H + NSA_N_BRANCH * NSA_HEADS
    w_small = jnp.concatenate([w_in[:, o_mlif:o_nsq], w_in[:, o_nsg:o_merge], jnp.zeros((D, 128 - n_small), F32)],
                              axis=1).astype(BF16)
    act = _norm_matmul(x2d, norm_mix, w_act, BF16, tm=1024, tn=768)
    small = _norm_matmul(x2d, norm_mix, w_small, F32, tm=1024, tn=128)
    act3 = act.reshape(B, S, act.shape[1])
    c_ml = 2 * D
    c_nsq = c_ml + 4 * ML_WIDTH
    c_kv = c_nsq + NSA_WIDTH

    gi = small[:, 0:H].reshape(B, S, H)
    gf = small[:, H:2 * H].reshape(B, S, H)
    g_if = jnp.stack([gi, gf], axis=-1)
    gates_col = g_if.transpose(0, 2, 1, 3)
    gates_row = g_if.transpose(0, 2, 3, 1)
    bias = jnp.stack([b_igate, b_fgate], axis=-1)
    conv_w = jnp.concatenate([conv_qk[:, :ML_WIDTH].reshape(ML_CONV, H, d), conv_qk[:, ML_WIDTH:].reshape(ML_CONV, H, d)],
                             axis=-1).transpose(1, 0, 2)
    h_ml = _mlstm(act3, gates_col, gates_row, bias.reshape(H, 1, 2), bias.reshape(H, 2, 1), conv_w,
                  mlstm_norm.reshape(H, 1, d), col0=c_ml // d, chunk=256)

    kv_cmp = act3[:, :, c_kv:c_kv + 2 * NSA_KV_WIDTH].reshape(B, S, 2, G, dh).transpose(2, 0, 3, 1, 4)
    chunks = kv_cmp.reshape(2, B, G, S // CMP_STRIDE, CMP_STRIDE * dh)
    pos = jnp.stack([cmp_pos_k, cmp_pos_v]).reshape(2, 2, 1, CMP_STRIDE * dh)
    w1c = jnp.stack([cmp_k_w1, cmp_v_w1]).reshape(2, 2, CMP_STRIDE * dh, -1).astype(BF16)
    w2c = jnp.stack([cmp_k_w2, cmp_v_w2]).astype(BF16)
    kvc = _compress(chunks, pos, w1c, w2c)
    n_slc = S // SEL_BLOCK
    tok = np.arange(S)
    kconst = np.zeros((S, n_slc + 2 * dh), np.float32)
    kconst[tok, tok // SEL_BLOCK] = 1.0
    kconst[:, n_slc] = tok // SEL_BLOCK
    kconst[:, n_slc + 1] = tok % SEL_BLOCK
    ns_gates = small[:, 2 * H:n_small].reshape(B, S, G, HG * NSA_N_BRANCH).transpose(0, 2, 1, 3)
    h_ns = _nsa_attention(act3, ns_gates, kvc[0], kvc[1], jnp.asarray(kconst, BF16), _overlap_t(S),
                          q_col0=c_nsq // (HG * dh), kv_col0=(c_kv + 2 * NSA_KV_WIDTH) // NSA_KV_WIDTH, tq=256)

    x1 = _mix_out(x2d, h_ml.reshape(N, ML_WIDTH), h_ns.reshape(N, NSA_WIDTH), act, w_br_mlstm.astype(BF16),
                  w_br_nsa.astype(BF16), w_mix_out.astype(BF16), gate_col0=0, tm=512)

    n_mem = mem.shape[1]
    kv_mem = _norm_matmul(mem.reshape(B * n_mem, D), norm_mem, xa_wkv.astype(BF16), BF16, tm=B * n_mem, tn=512)
    x2 = _xattn(x1, norm_xattn, xa_wq.astype(BF16), kv_mem.reshape(B, n_mem, 2 * D), xa_wo.astype(BF16), seq=S, tm=512)

    wr = jnp.zeros((ROUTER_ROWS, D), F32)
    wr = wr.at[0:MOE_GROUPS].set(router_group_w.T).at[8:8 + MOE_EXPERTS].set(router_expert_w.T)
    return _moe(x2, norm_ffn, wr, router_group_b.reshape(MOE_GROUPS, 1), router_expert_b.reshape(MOE_EXPERTS, 1),
                moe_w1.astype(BF16), moe_w3.astype(BF16), moe_w2.astype(BF16), norm_final, tm=1024, cap=320).reshape(B, S, D)


def kernel(x, mem, norm_mix, w_in, conv_qk, b_igate, b_fgate, mlstm_norm, cmp_pos_k, cmp_pos_v, cmp_k_w1, cmp_k_w2, cmp_v_w1, cmp_v_w2, w_br_mlstm, w_br_nsa, w_mix_out, norm_xattn, norm_mem, xa_wq, xa_wkv, xa_wo, norm_ffn, router_group_w, router_group_b, router_expert_w, router_expert_b, moe_w1, moe_w3, moe_w2, norm_final):
    depth = w_in.shape[0]
    assert depth == 1, "the fused final norm assumes a single layer"
    layer = 0
    return _layer(x, mem, norm_mix[layer], w_in[layer], conv_qk[layer], b_igate[layer], b_fgate[layer], mlstm_norm[layer],
                  cmp_pos_k[layer], cmp_pos_v[layer], cmp_k_w1[layer], cmp_k_w2[layer], cmp_v_w1[layer], cmp_v_w2[layer],
                  w_br_mlstm[layer], w_br_nsa[layer], w_mix_out[layer], norm_xattn[layer], norm_mem[layer], xa_wq[layer],
                  xa_wkv[layer], xa_wo[layer], norm_ffn[layer], router_group_w[layer], router_group_b[layer],
                  router_expert_w[layer], router_expert_b[layer], moe_w1[layer], moe_w3[layer], moe_w2[layer], norm_final)
```

```python
import functools
import math

import numpy as np
import jax
import jax.numpy as jnp
from jax import lax
from jax.experimental import pallas as pl
from jax.experimental.pallas import tpu as pltpu

F32 = jnp.float32
BF16 = jnp.bfloat16

ML_HEADS = 4
ML_HEAD_DIM = 128
ML_WIDTH = ML_HEADS * ML_HEAD_DIM
ML_CONV = 4
NSA_HEADS = 8
NSA_KV_GROUPS = 2
NSA_HEAD_DIM = 64
NSA_GROUP_HEADS = NSA_HEADS // NSA_KV_GROUPS
NSA_WIDTH = NSA_HEADS * NSA_HEAD_DIM
NSA_KV_WIDTH = NSA_KV_GROUPS * NSA_HEAD_DIM
NSA_N_BRANCH = 3
CMP_BLOCK = 32
CMP_STRIDE = 16
SEL_BLOCK = 64
SEL_COUNT = 16
SEL_FORCE = 1e4
WINDOW = 512
XA_HEADS = 4
MOE_GROUPS = 4
MOE_EXPERTS_PER_GROUP = 4
MOE_EXPERTS = MOE_GROUPS * MOE_EXPERTS_PER_GROUP
RMS_EPS = 1e-6
NEG_INF = -1e30
SEL_MASK = -float(2 ** 30)

VMEM_LIMIT_BYTES = 56 * 1024 * 1024
ROUTER_ROWS = 32


def _cparams(*sem):
    return pltpu.CompilerParams(dimension_semantics=sem, vmem_limit_bytes=VMEM_LIMIT_BYTES)


def _rms(x, g):
    return x * lax.rsqrt(jnp.mean(x * x, axis=-1, keepdims=True) + RMS_EPS) * g


def _dot(a, b):
    return jnp.dot(a, b, preferred_element_type=F32)


def _dot_nt(a, b):
    return lax.dot_general(a, b, (((1,), (1,)), ((), ())), preferred_element_type=F32)


def _dot_tn(a, b):
    return lax.dot_general(a, b, (((0,), (0,)), ((), ())), preferred_element_type=F32)


def _norm_matmul_kernel(x_ref, g_ref, w_ref, o_ref, h_ref):
    @pl.when(pl.program_id(1) == 0)
    def _():
        h_ref[...] = _rms(x_ref[...], g_ref[...]).astype(BF16)

    o_ref[...] = _dot(h_ref[...], w_ref[...]).astype(o_ref.dtype)


def _norm_matmul(x2d, g, w, out_dtype, tm, tn):
    m, d = x2d.shape
    n = w.shape[1]
    return pl.pallas_call(
        _norm_matmul_kernel,
        grid=(m // tm, n // tn),
        in_specs=[
            pl.BlockSpec((tm, d), lambda i, j: (i, 0)),
            pl.BlockSpec((1, d), lambda i, j: (0, 0)),
            pl.BlockSpec((d, tn), lambda i, j: (0, j)),
        ],
        out_specs=pl.BlockSpec((tm, tn), lambda i, j: (i, j)),
        out_shape=jax.ShapeDtypeStruct((m, n), out_dtype),
        scratch_shapes=[pltpu.VMEM((tm, d), BF16)],
        compiler_params=_cparams("parallel", "arbitrary"),
        name="norm_matmul",
    )(x2d, g.reshape(1, d), w)


def _in_proj_kernel(x_ref, g_ref, w_ref, ws_ref, o_ref, os_ref, h_ref):
    @pl.when(pl.program_id(1) == 0)
    def _():
        h_ref[...] = _rms(x_ref[...], g_ref[...]).astype(BF16)
        os_ref[...] = _dot(h_ref[...], ws_ref[...])

    o_ref[...] = _dot(h_ref[...], w_ref[...]).astype(o_ref.dtype)


def _in_proj(x2d, g, w, w_small, tm, tn):
    m, d = x2d.shape
    n = w.shape[1]
    ns = w_small.shape[1]
    return pl.pallas_call(
        _in_proj_kernel,
        grid=(m // tm, n // tn),
        in_specs=[
            pl.BlockSpec((tm, d), lambda i, j: (i, 0)),
            pl.BlockSpec((1, d), lambda i, j: (0, 0)),
            pl.BlockSpec((d, tn), lambda i, j: (0, j)),
            pl.BlockSpec((d, ns), lambda i, j: (0, 0)),
        ],
        out_specs=[pl.BlockSpec((tm, tn), lambda i, j: (i, j)), pl.BlockSpec((tm, ns), lambda i, j: (i, 0))],
        out_shape=[jax.ShapeDtypeStruct((m, n), BF16), jax.ShapeDtypeStruct((m, ns), F32)],
        scratch_shapes=[pltpu.VMEM((tm, d), BF16)],
        compiler_params=_cparams("parallel", "arbitrary"),
        name="in_proj",
    )(x2d, g.reshape(1, d), w, w_small)


def _mlstm_kernel(q_ref, k_ref, v_ref, o_ref, qt_ref, kt_ref, gcol_ref, grow_ref, bcol_ref, brow_ref,
                  cw_ref, ng_ref, out_ref, c_scr, n_scr, m_scr, *, chunk):
    L, d = chunk, ML_HEAD_DIM
    c = pl.program_id(1)

    @pl.when(c == 0)
    def _():
        c_scr[...] = jnp.zeros_like(c_scr)
        n_scr[...] = jnp.zeros_like(n_scr)
        m_scr[...] = jnp.zeros_like(m_scr)

    t_idx = lax.broadcasted_iota(jnp.int32, (L, L), 0)
    s_idx = lax.broadcasted_iota(jnp.int32, (L, L), 1)
    tri = s_idx <= t_idx

    row8 = lax.broadcasted_iota(jnp.int32, (8, d), 0)

    def conv_silu(x, tail, w):
        x = x.astype(F32)
        tail = jnp.where(c > 0, tail.astype(F32), 0.0)
        y = x * w[ML_CONV - 1:ML_CONV, :]
        for j in range(ML_CONV - 1):
            s = ML_CONV - 1 - j
            r = pltpu.roll(x, s, 0)
            head = jnp.where(row8 < s, pltpu.roll(tail, s, 0), r[:8])
            y = y + jnp.concatenate([head, r[8:]], axis=0) * w[j:j + 1, :]
        return y * (0.5 * jnp.tanh(0.5 * y) + 0.5)

    outs = []
    for h in range(ML_HEADS):
        sl = slice(h * d, (h + 1) * d)
        cw = cw_ref[h]
        q = conv_silu(q_ref[0, :, sl], qt_ref[0, :, sl], cw[:, :d])
        k = conv_silu(k_ref[0, :, sl], kt_ref[0, :, sl], cw[:, d:]) * (d ** -0.5)
        v = v_ref[0, :, sl]
        qb = q.astype(BF16)
        kb = k.astype(BF16)

        gcol = gcol_ref[0, h] + bcol_ref[h]
        grow = grow_ref[0, h] + brow_ref[h]
        i_col = gcol[:, 0:1]
        f_col = jax.nn.log_sigmoid(gcol[:, 1:2])
        i_row = grow[0:1, :]
        f_row = jax.nn.log_sigmoid(grow[1:2, :])

        b_col = jnp.sum(jnp.where(tri, f_row, 0.0), axis=1, keepdims=True)
        b_row = jnp.sum(jnp.where(t_idx <= s_idx, f_col, 0.0), axis=0, keepdims=True)
        dlog = jnp.where(tri, b_col + (i_row - b_row), NEG_INF)
        m_prev = m_scr[h]
        inter = b_col + m_prev
        mt = jnp.maximum(jnp.max(dlog, axis=1, keepdims=True), inter)
        w_intra = jnp.exp(dlog - mt)
        w_inter = jnp.exp(inter - mt)

        s = _dot_nt(qb, kb) * w_intra
        cmat = c_scr[h]
        nvec = n_scr[h]
        num = _dot(s.astype(BF16), v) + w_inter * _dot(qb, cmat.astype(BF16))
        den = jnp.sum(s, axis=1, keepdims=True) + w_inter * jnp.sum(q * nvec, axis=1, keepdims=True)
        hc = num / jnp.maximum(jnp.abs(den), jnp.exp(-mt))

        bl = jnp.sum(f_row, axis=1, keepdims=True)
        logw = bl - b_col + i_col
        m_new = jnp.maximum(bl + m_prev, jnp.max(logw, axis=0, keepdims=True))
        decay = jnp.exp(bl + m_prev - m_new)
        kw = k * jnp.exp(logw - m_new)
        c_scr[h] = decay * cmat + _dot_tn(kw.astype(BF16), v)
        n_scr[h] = decay * nvec + jnp.sum(kw, axis=0, keepdims=True)
        m_scr[h] = m_new

        outs.append(jax.nn.sigmoid(o_ref[0, :, sl].astype(F32)) * _rms(hc, ng_ref[h]))
    out_ref[0] = jnp.concatenate(outs, axis=1).astype(out_ref.dtype)


def _mlstm(act, gates_col, gates_row, b_col, b_row, conv_w, norm_g, *, col0, chunk):
    B, S, _ = act.shape
    H, d, L = ML_HEADS, ML_HEAD_DIM, chunk
    tail_blocks = L // 8

    def blk(off):
        return pl.BlockSpec((1, L, H * d), lambda b, c: (b, c, col0 + off))

    def tail(off):
        return pl.BlockSpec((1, 8, H * d), lambda b, c: (b, jnp.maximum(c * tail_blocks - 1, 0), col0 + off))

    def const(a):
        return pl.BlockSpec(a.shape, lambda b, c: (0,) * a.ndim)

    return pl.pallas_call(
        functools.partial(_mlstm_kernel, chunk=L),
        grid=(B, S // L),
        in_specs=[
            blk(0), blk(1), blk(2), blk(3), tail(0), tail(1),
            pl.BlockSpec((1, H, L, 2), lambda b, c: (b, 0, c, 0)),
            pl.BlockSpec((1, H, 2, L), lambda b, c: (b, 0, 0, c)),
            const(b_col), const(b_row), const(conv_w), const(norm_g),
        ],
        out_specs=pl.BlockSpec((1, L, H * d), lambda b, c: (b, c, 0)),
        out_shape=jax.ShapeDtypeStruct((B, S, H * d), BF16),
        scratch_shapes=[pltpu.VMEM((H, d, d), F32), pltpu.VMEM((H, 1, d), F32), pltpu.VMEM((H, 1, 1), F32)],
        compiler_params=_cparams("parallel", "arbitrary"),
        name="mlstm",
    )(act, act, act, act, act, act, gates_col, gates_row, b_col, b_row, conv_w, norm_g)


def _compress_kernel(x_ref, pos_ref, w1_ref, w2_ref, o_ref):
    x = x_ref[0, 0, 0].astype(F32)
    n = x.shape[0]
    top = _dot((x + pos_ref[0, 0]).astype(BF16), w1_ref[0, 0])
    bot = _dot((x + pos_ref[0, 1]).astype(BF16), w1_ref[0, 1])
    pre = top + pltpu.roll(bot, n - 1, 0)
    hid = jax.nn.gelu(pre, approximate=True)
    out = _dot(hid.astype(BF16), w2_ref[0])
    row = lax.broadcasted_iota(jnp.int32, out.shape, 0)
    out = jnp.where(row < n - 1, out, 0.0)
    j = lax.broadcasted_iota(jnp.int32, out.shape, 0)
    lane = lax.broadcasted_iota(jnp.int32, out.shape, 1)
    cols = jnp.where(lane == 0, j // 16, jnp.where(lane == 1, j % 16, jnp.where(lane == 2, 1, 0))).astype(F32)
    cols = jnp.where(pl.program_id(0) == 0, cols, 0.0)
    o_ref[0, 0, 0] = jnp.concatenate([out, cols], axis=1).astype(o_ref.dtype)


def _compress(chunks, pos, w1, w2):
    _, B, G, n, width = chunks.shape
    hidden = w1.shape[-1]
    dh = w2.shape[-1]
    return pl.pallas_call(
        _compress_kernel,
        grid=(2, B, G),
        in_specs=[
            pl.BlockSpec((1, 1, 1, n, width), lambda a, b, g: (a, b, g, 0, 0)),
            pl.BlockSpec((1, 2, 1, width), lambda a, b, g: (a, 0, 0, 0)),
            pl.BlockSpec((1, 2, width, hidden), lambda a, b, g: (a, 0, 0, 0)),
            pl.BlockSpec((1, hidden, dh), lambda a, b, g: (a, 0, 0)),
        ],
        out_specs=pl.BlockSpec((1, 1, 1, n, 2 * dh), lambda a, b, g: (a, b, g, 0, 0)),
        out_shape=jax.ShapeDtypeStruct((2, B, G, n, 2 * dh), BF16),
        compiler_params=_cparams("parallel", "parallel", "parallel"),
        name="nsa_compress",
    )(chunks, pos, w1, w2)


def _nsa_kernel(q_ref, gate_ref, kc_ref, vc_ref, ksl_ref, vsl_ref, kwn_ref, vwn_ref, kconst_ref, ovl_ref, out_ref,
                ks_scr, vs_scr, kw_scr, vw_scr, s_scr, mrun_scr, lrun_scr, acc_scr, *, tq, seq):
    HG, dh = NSA_GROUP_HEADS, NSA_HEAD_DIM
    g = pl.program_id(1)
    i = pl.program_id(2)
    t0 = i * tq
    rows = HG * tq
    n_cmp_pad = kc_ref.shape[2]
    n_slc = seq // SEL_BLOCK
    half = tq // 2
    blocks_per_chunk = tq // SEL_BLOCK

    def assemble(lo):
        ks_scr[...] = jnp.concatenate([ksl_ref[0][:, lo:lo + dh], kconst_ref[...]], axis=1)
        kw_scr[...] = jnp.concatenate([kwn_ref[0][:, lo:lo + dh], kconst_ref[:, n_slc:n_slc + dh]], axis=1)
        vs_scr[...] = vsl_ref[0][:, lo:lo + dh]
        vw_scr[...] = vwn_ref[0][:, lo:lo + dh]

    for gg in range(NSA_KV_GROUPS):
        @pl.when((i == 0) & (g == gg))
        def _(gg=gg):
            assemble(gg * dh)

    q_all = q_ref[0]
    q_heads = [(q_all[:, x * dh:(x + 1) * dh].astype(F32) * (dh ** -0.5)).astype(BF16) for x in range(HG)]

    def head_slope(x):
        sl = jnp.float32(0.0)
        for hh in range(NSA_HEADS):
            sl = jnp.where(g * HG + x == hh, 2.0 ** (-8.0 * (hh + 1) / NSA_HEADS), sl)
        return sl

    slopes = [head_slope(x) for x in range(HG)]

    def alibi_cols(x, width, coef):
        lane = lax.broadcasted_iota(jnp.int32, (tq, width), 1)
        v = jnp.zeros((tq, width), F32)
        for c, val in enumerate(coef):
            v = jnp.where(lane == c, slopes[x] * val, v)
        return v.astype(BF16)

    def stack_q(width, coef, extra=None):
        parts = []
        for x in range(HG):
            cols = [q_heads[x]] + ([extra] if extra is not None else []) + [alibi_cols(x, width, coef)]
            parts.append(jnp.concatenate(cols, axis=1))
        return jnp.concatenate(parts, axis=0)

    q_cmp = stack_q(dh, (16.0 * CMP_STRIDE, 1.0 * CMP_STRIDE, (CMP_BLOCK - 1) / 2))
    sc = _dot_nt(q_cmp, kc_ref[0, 0])
    t_c = t0 + lax.broadcasted_iota(jnp.int32, (rows, n_cmp_pad), 0) % tq
    cmp_end = lax.broadcasted_iota(jnp.int32, (rows, n_cmp_pad), 1) * CMP_STRIDE + (CMP_BLOCK - 1)
    sc = jnp.where(cmp_end <= t_c, sc, NEG_INF)
    e = jnp.exp(sc - jnp.max(sc, axis=1, keepdims=True))
    t_row = t0 + lax.broadcasted_iota(jnp.int32, (rows, 1), 0) % tq
    any_valid = jnp.where(t_row >= CMP_BLOCK - 1, 1.0, 0.0)
    p_cmp = e * (any_valid / jnp.sum(e, axis=1, keepdims=True))
    o_cmp = _dot(p_cmp.astype(BF16), vc_ref[0, 0])[:, :dh]

    p_sum = p_cmp[0:tq]
    for x in range(1, HG):
        p_sum = p_sum + p_cmp[x * tq:(x + 1) * tq]
    p_hi = p_sum.astype(BF16)
    p_lo = (p_sum - p_hi.astype(F32)).astype(BF16)
    ovl = ovl_ref[...]
    p_slc = _dot_nt(ovl, p_hi) + _dot_nt(ovl, p_lo)
    blk = lax.broadcasted_iota(jnp.int32, (n_slc, tq), 0)
    cur = (t0 + lax.broadcasted_iota(jnp.int32, (n_slc, tq), 1)) // SEL_BLOCK
    forced = (blk == 0) | (blk == cur) | (blk == cur - 1)
    score = jnp.where(forced, SEL_FORCE, jnp.where(blk > cur, -SEL_FORCE, p_slc))
    n_tiles = n_slc // 8
    tiles = [score[8 * a:8 * a + 8] for a in range(n_tiles)]
    ranks = [jnp.zeros((8, tq), F32) for _ in range(n_tiles)]
    sub = lax.broadcasted_iota(jnp.int32, (8, tq), 0)
    for kk in range(n_slc):
        sk = score[kk:kk + 1, :]
        for a in range(n_tiles):
            if a < kk // 8:
                before = sk > tiles[a]
            elif a > kk // 8:
                before = sk >= tiles[a]
            else:
                before = jnp.where(sub > kk % 8, jnp.where(sk >= tiles[a], 1.0, 0.0), jnp.where(sk > tiles[a], 1.0, 0.0)) > 0.5
            ranks[a] = ranks[a] + jnp.where(before, 1.0, 0.0)
    rank = jnp.concatenate(ranks, axis=0)
    selected = rank < min(SEL_COUNT, n_slc)
    sel_bias = jnp.where(selected, 0.0, SEL_MASK).T.astype(BF16)
    first_blk = jnp.min(jnp.where(selected & (blk >= blocks_per_chunk), blk, n_slc))
    c_lo = jnp.minimum(jnp.maximum(first_blk // blocks_per_chunk, 1), i)
    q_aug = stack_q(2 * dh, (1.0 * SEL_BLOCK, 1.0), extra=sel_bias)

    rr = lax.broadcasted_iota(jnp.int32, (rows, tq), 0) % tq
    ll = lax.broadcasted_iota(jnp.int32, (rows, tq), 1)
    causal = ll <= rr

    mrun_scr[...] = jnp.full(mrun_scr.shape, NEG_INF, F32)
    lrun_scr[...] = jnp.zeros(lrun_scr.shape, F32)
    acc_scr[...] = jnp.zeros(acc_scr.shape, F32)

    def scores(c, diagonal):
        start = pl.multiple_of(c * tq, tq)
        s = _dot_nt(q_aug, ks_scr[pl.ds(start, tq), :])
        if diagonal:
            s = jnp.where(causal, s, NEG_INF)
        s_scr[c] = s
        mrun_scr[...] = jnp.maximum(mrun_scr[...], jnp.maximum(s[:, :half], s[:, half:]))

    def weights(c):
        start = pl.multiple_of(c * tq, tq)
        s = s_scr[c]
        m_b = mrun_scr[...]
        p0 = jnp.exp(s[:, :half] - m_b)
        p1 = jnp.exp(s[:, half:] - m_b)
        lrun_scr[...] += p0 + p1
        p = jnp.concatenate([p0, p1], axis=1).astype(BF16)
        acc_scr[...] += _dot(p, vs_scr[pl.ds(start, tq), :])

    def pass1(c, carry):
        scores(c, False)
        return carry

    def pass2(c, carry):
        weights(c)
        return carry

    @pl.when(i > 0)
    def _():
        scores(0, False)

    lax.fori_loop(c_lo, i, pass1, 0)
    scores(i, True)
    m_row = jnp.max(mrun_scr[...], axis=1, keepdims=True)
    mrun_scr[...] = jnp.broadcast_to(m_row, mrun_scr.shape)

    @pl.when(i > 0)
    def _():
        weights(0)

    lax.fori_loop(c_lo, i + 1, pass2, 0)
    gates = jax.nn.sigmoid(gate_ref[0, 0])

    def gate_col(r):
        return jnp.concatenate([gates[:, x * NSA_N_BRANCH + r:x * NSA_N_BRANCH + r + 1] for x in range(HG)], axis=0)

    o = gate_col(0) * o_cmp + acc_scr[...] * (gate_col(1) / jnp.sum(lrun_scr[...], axis=1, keepdims=True))

    qw_aug = stack_q(dh, (1.0 * SEL_BLOCK, 1.0))
    start_a = pl.multiple_of(jnp.maximum(t0 - 2 * tq, 0), tq)
    start_b = pl.multiple_of(jnp.maximum(t0 - tq, 0), tq)
    start_c = pl.multiple_of(t0, tq)
    s_a = jnp.where((ll > rr) & (i >= 2), _dot_nt(qw_aug, kw_scr[pl.ds(start_a, tq), :]), NEG_INF)
    s_b = jnp.where(i >= 1, _dot_nt(qw_aug, kw_scr[pl.ds(start_b, tq), :]), NEG_INF)
    s_c = jnp.where(causal, _dot_nt(qw_aug, kw_scr[pl.ds(start_c, tq), :]), NEG_INF)
    m_w = jnp.max(jnp.maximum(jnp.maximum(s_a, s_b), s_c), axis=1, keepdims=True)
    e_a = jnp.exp(s_a - m_w)
    e_b = jnp.exp(s_b - m_w)
    e_c = jnp.exp(s_c - m_w)
    l_w = jnp.sum(e_a + e_b + e_c, axis=1, keepdims=True)
    o_win = (_dot(e_a.astype(BF16), vw_scr[pl.ds(start_a, tq), :]) + _dot(e_b.astype(BF16), vw_scr[pl.ds(start_b, tq), :])
             + _dot(e_c.astype(BF16), vw_scr[pl.ds(start_c, tq), :]))
    o = o + o_win * (gate_col(2) / l_w)

    out_ref[0] = jnp.concatenate([o[x * tq:(x + 1) * tq] for x in range(HG)], axis=1).astype(out_ref.dtype)


def _nsa_attention(act, gates, kc, vc, kconst, overlap_t, *, q_col0, kv_col0, tq):
    B, S, _ = act.shape
    G, HG, dh = NSA_KV_GROUPS, NSA_GROUP_HEADS, NSA_HEAD_DIM
    n_slc = S // SEL_BLOCK
    rows = HG * tq
    assert WINDOW == 2 * tq, "the window branch reads exactly the two chunks before the diagonal one"

    def kv_spec(off):
        return pl.BlockSpec((1, S, G * dh), lambda b, g, i: (b, 0, kv_col0 + off))

    def full(a):
        return pl.BlockSpec((1, 1) + a.shape[2:], lambda b, g, i: (b, g, 0, 0))

    return pl.pallas_call(
        functools.partial(_nsa_kernel, tq=tq, seq=S),
        grid=(B, G, S // tq),
        in_specs=[
            pl.BlockSpec((1, tq, HG * dh), lambda b, g, i: (b, i, q_col0 + g)),
            pl.BlockSpec((1, 1, tq, HG * NSA_N_BRANCH), lambda b, g, i: (b, g, i, 0)),
            full(kc), full(vc), kv_spec(0), kv_spec(1), kv_spec(2), kv_spec(3),
            pl.BlockSpec(kconst.shape, lambda b, g, i: (0, 0)),
            pl.BlockSpec(overlap_t.shape, lambda b, g, i: (0, 0)),
        ],
        out_specs=pl.BlockSpec((1, tq, HG * dh), lambda b, g, i: (b, i, g)),
        out_shape=jax.ShapeDtypeStruct((B, S, NSA_WIDTH), BF16),
        scratch_shapes=[pltpu.VMEM((S, dh + n_slc + 2 * dh), BF16), pltpu.VMEM((S, dh), BF16),
                        pltpu.VMEM((S, 2 * dh), BF16), pltpu.VMEM((S, dh), BF16),
                        pltpu.VMEM((S // tq, rows, tq), F32), pltpu.VMEM((rows, tq // 2), F32),
                        pltpu.VMEM((rows, tq // 2), F32), pltpu.VMEM((rows, dh), F32)],
        compiler_params=_cparams("parallel", "parallel", "arbitrary"),
        name="nsa_attn",
    )(act, gates, kc, vc, act, act, act, act, kconst, overlap_t)


def _mix_out_kernel(x_ref, hm_ref, hn_ref, gm_ref, gn_ref, wm_ref, wn_ref, wo_ref, o_ref):
    y_ml = _dot(hm_ref[...], wm_ref[...])
    y_ns = _dot(hn_ref[...], wn_ref[...])
    mix = jax.nn.sigmoid(gm_ref[...].astype(F32)) * y_ml + jax.nn.sigmoid(gn_ref[...].astype(F32)) * y_ns
    o_ref[...] = x_ref[...] + _dot(mix.astype(BF16), wo_ref[...])


def _mix_out(x2d, h_ml, h_ns, act2d, wm, wn, wo, *, gate_col0, tm):
    n, d = x2d.shape

    def res(w):
        return pl.BlockSpec(w.shape, lambda i: (0, 0))

    return pl.pallas_call(
        _mix_out_kernel,
        grid=(n // tm,),
        in_specs=[
            pl.BlockSpec((tm, d), lambda i: (i, 0)),
            pl.BlockSpec((tm, h_ml.shape[1]), lambda i: (i, 0)),
            pl.BlockSpec((tm, h_ns.shape[1]), lambda i: (i, 0)),
            pl.BlockSpec((tm, d), lambda i: (i, gate_col0)),
            pl.BlockSpec((tm, d), lambda i: (i, gate_col0 + 1)),
            res(wm), res(wn), res(wo),
        ],
        out_specs=pl.BlockSpec((tm, d), lambda i: (i, 0)),
        out_shape=jax.ShapeDtypeStruct((n, d), F32),
        compiler_params=_cparams("parallel"),
        name="mix_out",
    )(x2d, h_ml, h_ns, act2d, act2d, wm, wn, wo)


def _xattn_kernel(x_ref, g_ref, wq_ref, k_ref, v_ref, wo_ref, o_ref):
    x = x_ref[...]
    d = x.shape[1]
    dh = d // XA_HEADS
    h = _rms(x, g_ref[...]).astype(BF16)
    q = (_dot(h, wq_ref[...]) * (dh ** -0.5)).astype(BF16)
    k = k_ref[0]
    v = v_ref[0]
    outs = []
    for hd in range(XA_HEADS):
        sl = slice(hd * dh, (hd + 1) * dh)
        s = _dot_nt(q[:, sl], k[:, sl])
        e = jnp.exp(s - jnp.max(s, axis=1, keepdims=True))
        p = e / jnp.sum(e, axis=1, keepdims=True)
        outs.append(_dot(p.astype(BF16), v[:, sl]))
    o = jnp.concatenate(outs, axis=1).astype(BF16)
    o_ref[...] = x + _dot(o, wo_ref[...])


def _xattn(x2d, g, wq, kv, wo, *, seq, tm):
    n, d = x2d.shape
    n_mem = kv.shape[1]
    tiles_per_batch = seq // tm
    return pl.pallas_call(
        _xattn_kernel,
        grid=(n // tm,),
        in_specs=[
            pl.BlockSpec((tm, d), lambda i: (i, 0)),
            pl.BlockSpec((1, d), lambda i: (0, 0)),
            pl.BlockSpec(wq.shape, lambda i: (0, 0)),
            pl.BlockSpec((1, n_mem, d), lambda i: (i // tiles_per_batch, 0, 0)),
            pl.BlockSpec((1, n_mem, d), lambda i: (i // tiles_per_batch, 0, 1)),
            pl.BlockSpec(wo.shape, lambda i: (0, 0)),
        ],
        out_specs=pl.BlockSpec((tm, d), lambda i: (i, 0)),
        out_shape=jax.ShapeDtypeStruct((n, d), F32),
        compiler_params=_cparams("parallel"),
        name="xattn",
    )(x2d, g.reshape(1, d), wq, kv, kv, wo)


def _route(logits, bg, be):
    G, EG = MOE_GROUPS, MOE_EXPERTS_PER_GROUP
    tm = logits.shape[1]
    lg = logits[0:G] + bg
    eg = jnp.exp(lg - jnp.max(lg, axis=0, keepdims=True))
    pg = eg / jnp.sum(eg, axis=0, keepdims=True)
    g_val = jnp.max(pg, axis=0, keepdims=True)
    g_row = lax.broadcasted_iota(jnp.int32, (G, tm), 0)
    g_idx = jnp.min(jnp.where(pg == g_val, g_row, G), axis=0, keepdims=True)
    el = logits[8:8 + G * EG] + be
    e_in = jnp.zeros((EG, tm), F32)
    for gg in range(G):
        e_in = jnp.where(g_idx == gg, el[gg * EG:(gg + 1) * EG], e_in)
    e_row = lax.broadcasted_iota(jnp.int32, (EG, tm), 0)
    v1 = jnp.max(e_in, axis=0, keepdims=True)
    i1 = jnp.min(jnp.where(e_in == v1, e_row, EG), axis=0, keepdims=True)
    rest = jnp.where(e_row == i1, -jnp.inf, e_in)
    v2 = jnp.max(rest, axis=0, keepdims=True)
    i2 = jnp.min(jnp.where(rest == v2, e_row, EG), axis=0, keepdims=True)
    e2 = jnp.exp(v2 - v1)
    c1 = g_val / (1.0 + e2)
    c2 = g_val * e2 / (1.0 + e2)
    ex = lax.broadcasted_iota(jnp.int32, (G * EG, tm), 0)
    base = g_idx * EG
    return jnp.where(ex == base + i1, c1, 0.0) + jnp.where(ex == base + i2, c2, 0.0), g_idx


def _moe_kernel(x_ref, g_ref, wr_ref, bg_ref, be_ref, before_ref, w1_ref, w3_ref, w2_ref, gf_ref, o_ref,
                h_scr, slot_scr, wt_scr, acc_scr, *, cap):
    G, EG = MOE_GROUPS, MOE_EXPERTS_PER_GROUP
    grp = pl.program_id(1)
    tm = x_ref.shape[0]

    @pl.when(grp == 0)
    def _():
        h = _rms(x_ref[...], g_ref[...])
        h_hi = h.astype(BF16)
        h_scr[...] = h_hi
        h_lo = (h - h_hi.astype(F32)).astype(BF16)
        w_hi = wr_ref[...].astype(BF16)
        w_lo = (wr_ref[...] - w_hi.astype(F32)).astype(BF16)
        logits = _dot_nt(w_hi, h_hi) + _dot_nt(w_hi, h_lo) + _dot_nt(w_lo, h_hi)
        wt, g_idx = _route(logits, bg_ref[...], be_ref[...])
        member = jnp.where(lax.broadcasted_iota(jnp.int32, (8, tm), 0) == g_idx, 1.0, 0.0)
        prefix = _dot(member.astype(BF16), before_ref[...])
        slots = jnp.where(member > 0.0, prefix, -1.0)
        zeros = jnp.zeros((8 - EG, tm), F32)
        for gg in range(G):
            slot_scr[gg] = jnp.broadcast_to(slots[gg:gg + 1], (8, tm))
            wt_scr[gg] = jnp.concatenate([wt[gg * EG:(gg + 1) * EG], zeros], axis=0)
        acc_scr[...] = jnp.zeros_like(acc_scr)

    slot = slot_scr[grp][0:1, :]
    n_rows = jnp.max(slot).astype(jnp.int32) + 1
    row = lax.broadcasted_iota(jnp.int32, (cap, tm), 0).astype(F32)

    def one_pass(p, carry):
        onehot = row == (slot - (p * cap).astype(F32))
        pick = jnp.where(onehot, 1.0, 0.0)
        pick_b = pick.astype(BF16)
        hsub = _dot(pick_b, h_scr[...]).astype(BF16)
        wts = wt_scr[grp]
        y = jnp.zeros((cap, x_ref.shape[1]), F32)
        for e in range(EG):
            a = _dot(hsub, w1_ref[e])
            w_e = jnp.sum(pick * wts[e:e + 1, :], axis=1, keepdims=True)
            a = a * jax.nn.sigmoid(a) * _dot(hsub, w3_ref[e]) * w_e
            y = y + _dot(a.astype(BF16), w2_ref[e])
        acc_scr[...] += _dot_tn(pick_b, y.astype(BF16))
        return carry

    lax.fori_loop(0, (n_rows + cap - 1) // cap, one_pass, 0)

    @pl.when(grp == pl.num_programs(1) - 1)
    def _():
        o_ref[...] = _rms(x_ref[...] + acc_scr[...], gf_ref[...])


def _moe(x2d, g, wr, bg, be, w1, w3, w2, g_final, *, tm, cap):
    n, d = x2d.shape
    E, _, hid = w1.shape
    G, EG = MOE_GROUPS, MOE_EXPERTS_PER_GROUP
    before = jnp.asarray(np.arange(tm)[:, None] < np.arange(tm)[None, :], BF16)
    once = pl.Buffered(1)

    def const(shape):
        return pl.BlockSpec(shape, lambda i, e: (0,) * len(shape), pipeline_mode=once)

    return pl.pallas_call(
        functools.partial(_moe_kernel, cap=cap),
        grid=(n // tm, G),
        in_specs=[
            pl.BlockSpec((tm, d), lambda i, e: (i, 0), pipeline_mode=once),
            const((1, d)), const(wr.shape), const(bg.shape), const(be.shape), const((tm, tm)),
            pl.BlockSpec((EG, d, hid), lambda i, e: (e, 0, 0)),
            pl.BlockSpec((EG, d, hid), lambda i, e: (e, 0, 0)),
            pl.BlockSpec((EG, hid, d), lambda i, e: (e, 0, 0)),
            const((1, d)),
        ],
        out_specs=pl.BlockSpec((tm, d), lambda i, e: (i, 0)),
        out_shape=jax.ShapeDtypeStruct((n, d), F32),
        scratch_shapes=[pltpu.VMEM((tm, d), BF16), pltpu.VMEM((G, 8, tm), F32), pltpu.VMEM((G, 8, tm), F32),
                        pltpu.VMEM((tm, d), F32)],
        compiler_params=_cparams("parallel", "arbitrary"),
        name="moe",
    )(x2d, g.reshape(1, d), wr, bg, be, before, w1, w3, w2, g_final.reshape(1, d))


def _overlap_t(seq):
    n_cmp_pad = seq // CMP_STRIDE
    cs = np.arange(n_cmp_pad)[None, :] * CMP_STRIDE
    ss = np.arange(seq // SEL_BLOCK)[:, None] * SEL_BLOCK
    ov = np.clip(np.minimum(cs + CMP_BLOCK, ss + SEL_BLOCK) - np.maximum(cs, ss), 0, None) / CMP_BLOCK
    ov[:, n_cmp_pad - 1] = 0.0
    return jnp.asarray(ov, BF16)


def _layer(x, mem, norm_mix, w_in, conv_qk, b_igate, b_fgate, mlstm_norm, cmp_pos_k, cmp_pos_v, cmp_k_w1, cmp_k_w2,
           cmp_v_w1, cmp_v_w2, w_br_mlstm, w_br_nsa, w_mix_out, norm_xattn, norm_mem, xa_wq, xa_wkv, xa_wo, norm_ffn,
           router_group_w, router_group_b, router_expert_w, router_expert_b, moe_w1, moe_w3, moe_w2, norm_final):
    B, S, D = x.shape
    N = B * S
    H, d = ML_HEADS, ML_HEAD_DIM
    G, HG, dh = NSA_KV_GROUPS, NSA_GROUP_HEADS, NSA_HEAD_DIM
    x2d = x.reshape(N, D)

    o_mlqkvo = 0
    o_mlif = 4 * ML_WIDTH
    o_nsq = o_mlif + 2 * H
    o_kv = o_nsq + NSA_WIDTH
    o_nsg = o_kv + 6 * NSA_KV_WIDTH
    o_merge = o_nsg + NSA_N_BRANCH * NSA_HEADS
    w_act = jnp.concatenate([w_in[:, o_merge:o_merge + 2 * D], w_in[:, o_mlqkvo:o_mlif], w_in[:, o_nsq:o_kv],
                             w_in[:, o_kv:o_nsg]], axis=1).astype(BF16)
    n_small = 2 * H + NSA_N_BRANCH * NSA_HEADS
    w_small = jnp.concatenate([w_in[:, o_mlif:o_nsq], w_in[:, o_nsg:o_merge], jnp.zeros((D, 128 - n_small), F32)],
                              axis=1).astype(BF16)
    act, small = _in_proj(x2d, norm_mix, w_act, w_small, tm=1024, tn=1792)
    act3 = act.reshape(B, S, act.shape[1])
    c_ml = 2 * D
    c_nsq = c_ml + 4 * ML_WIDTH
    c_kv = c_nsq + NSA_WIDTH

    gi = small[:, 0:H].reshape(B, S, H)
    gf = small[:, H:2 * H].reshape(B, S, H)
    g_if = jnp.stack([gi, gf], axis=-1)
    gates_col = g_if.transpose(0, 2, 1, 3)
    gates_row = g_if.transpose(0, 2, 3, 1)
    bias = jnp.stack([b_igate, b_fgate], axis=-1)
    conv_w = jnp.concatenate([conv_qk[:, :ML_WIDTH].reshape(ML_CONV, H, d), conv_qk[:, ML_WIDTH:].reshape(ML_CONV, H, d)],
                             axis=-1).transpose(1, 0, 2)
    h_ml = _mlstm(act3, gates_col, gates_row, bias.reshape(H, 1, 2), bias.reshape(H, 2, 1), conv_w,
                  mlstm_norm.reshape(H, 1, d), col0=c_ml // ML_WIDTH, chunk=256)

    kv_cmp = act3[:, :, c_kv:c_kv + 2 * NSA_KV_WIDTH].reshape(B, S, 2, G, dh).transpose(2, 0, 3, 1, 4)
    chunks = kv_cmp.reshape(2, B, G, S // CMP_STRIDE, CMP_STRIDE * dh)
    pos = jnp.stack([cmp_pos_k, cmp_pos_v]).reshape(2, 2, 1, CMP_STRIDE * dh)
    w1c = jnp.stack([cmp_k_w1, cmp_v_w1]).reshape(2, 2, CMP_STRIDE * dh, -1).astype(BF16)
    w2c = jnp.stack([cmp_k_w2, cmp_v_w2]).astype(BF16)
    kvc = _compress(chunks, pos, w1c, w2c)
    n_slc = S // SEL_BLOCK
    tok = np.arange(S)
    kconst = np.zeros((S, n_slc + 2 * dh), np.float32)
    kconst[tok, tok // SEL_BLOCK] = 1.0
    kconst[:, n_slc] = tok // SEL_BLOCK
    kconst[:, n_slc + 1] = tok % SEL_BLOCK
    ns_gates = small[:, 2 * H:n_small].reshape(B, S, G, HG * NSA_N_BRANCH).transpose(0, 2, 1, 3)
    h_ns = _nsa_attention(act3, ns_gates, kvc[0], kvc[1], jnp.asarray(kconst, BF16), _overlap_t(S),
                          q_col0=c_nsq // (HG * dh), kv_col0=(c_kv + 2 * NSA_KV_WIDTH) // NSA_KV_WIDTH, tq=256)

    x1 = _mix_out(x2d, h_ml.reshape(N, ML_WIDTH), h_ns.reshape(N, NSA_WIDTH), act, w_br_mlstm.astype(BF16),
                  w_br_nsa.astype(BF16), w_mix_out.astype(BF16), gate_col0=0, tm=512)

    n_mem = mem.shape[1]
    kv_mem = _norm_matmul(mem.reshape(B * n_mem, D), norm_mem, xa_wkv.astype(BF16), BF16, tm=B * n_mem, tn=512)
    x2 = _xattn(x1, norm_xattn, xa_wq.astype(BF16), kv_mem.reshape(B, n_mem, 2 * D), xa_wo.astype(BF16), seq=S, tm=512)

    wr = jnp.zeros((ROUTER_ROWS, D), F32)
    wr = wr.at[0:MOE_GROUPS].set(router_group_w.T).at[8:8 + MOE_EXPERTS].set(router_expert_w.T)
    return _moe(x2, norm_ffn, wr, router_group_b.reshape(MOE_GROUPS, 1), router_expert_b.reshape(MOE_EXPERTS, 1),
                moe_w1.astype(BF16), moe_w3.astype(BF16), moe_w2.astype(BF16), norm_final, tm=1024, cap=288).reshape(B, S, D)


def kernel(x, mem, norm_mix, w_in, conv_qk, b_igate, b_fgate, mlstm_norm, cmp_pos_k, cmp_pos_v, cmp_k_w1, cmp_k_w2, cmp_v_w1, cmp_v_w2, w_br_mlstm, w_br_nsa, w_mix_out, norm_xattn, norm_mem, xa_wq, xa_wkv, xa_wo, norm_ffn, router_group_w, router_group_b, router_expert_w, router_expert_b, moe_w1, moe_w3, moe_w2, norm_final):
    depth = w_in.shape[0]
    assert depth == 1, "the fused final norm assumes a single layer"
    layer = 0
    return _layer(x, mem, norm_mix[layer], w_in[layer], conv_qk[layer], b_igate[layer], b_fgate[layer], mlstm_norm[layer],
                  cmp_pos_k[layer], cmp_pos_v[layer], cmp_k_w1[layer], cmp_k_w2[layer], cmp_v_w1[layer], cmp_v_w2[layer],
                  w_br_mlstm[layer], w_br_nsa[layer], w_mix_out[layer], norm_xattn[layer], norm_mem[layer], xa_wq[layer],
                  xa_wkv[layer], xa_wo[layer], norm_ffn[layer], router_group_w[layer], router_group_b[layer],
                  router_expert_w[layer], router_expert_b[layer], moe_w1[layer], moe_w3[layer], moe_w2[layer], norm_final)
```

```python
import functools
import math

import numpy as np
import jax
import jax.numpy as jnp
from jax import lax
from jax.experimental import pallas as pl
from jax.experimental.pallas import tpu as pltpu

F32 = jnp.float32
BF16 = jnp.bfloat16

ML_HEADS = 4
ML_HEAD_DIM = 128
ML_WIDTH = ML_HEADS * ML_HEAD_DIM
ML_CONV = 4
NSA_HEADS = 8
NSA_KV_GROUPS = 2
NSA_HEAD_DIM = 64
NSA_GROUP_HEADS = NSA_HEADS // NSA_KV_GROUPS
NSA_WIDTH = NSA_HEADS * NSA_HEAD_DIM
NSA_KV_WIDTH = NSA_KV_GROUPS * NSA_HEAD_DIM
NSA_N_BRANCH = 3
CMP_BLOCK = 32
CMP_STRIDE = 16
SEL_BLOCK = 64
SEL_COUNT = 16
SEL_FORCE = 1e4
WINDOW = 512
XA_HEADS = 4
MOE_GROUPS = 4
MOE_EXPERTS_PER_GROUP = 4
MOE_EXPERTS = MOE_GROUPS * MOE_EXPERTS_PER_GROUP
RMS_EPS = 1e-6
NEG_INF = -1e30
SEL_MASK = -float(2 ** 30)

VMEM_LIMIT_BYTES = 56 * 1024 * 1024
ROUTER_ROWS = 32


def _cparams(*sem):
    return pltpu.CompilerParams(dimension_semantics=sem, vmem_limit_bytes=VMEM_LIMIT_BYTES)


def _rms(x, g):
    return x * lax.rsqrt(jnp.mean(x * x, axis=-1, keepdims=True) + RMS_EPS) * g


def _dot(a, b):
    return jnp.dot(a, b, preferred_element_type=F32)


def _dot_nt(a, b):
    return lax.dot_general(a, b, (((1,), (1,)), ((), ())), preferred_element_type=F32)


def _dot_tn(a, b):
    return lax.dot_general(a, b, (((0,), (0,)), ((), ())), preferred_element_type=F32)


def _norm_matmul_kernel(x_ref, g_ref, w_ref, o_ref, h_ref):
    @pl.when(pl.program_id(1) == 0)
    def _():
        h_ref[...] = _rms(x_ref[...], g_ref[...]).astype(BF16)

    o_ref[...] = _dot(h_ref[...], w_ref[...]).astype(o_ref.dtype)


def _norm_matmul(x2d, g, w, out_dtype, tm, tn):
    m, d = x2d.shape
    n = w.shape[1]
    return pl.pallas_call(
        _norm_matmul_kernel,
        grid=(m // tm, n // tn),
        in_specs=[
            pl.BlockSpec((tm, d), lambda i, j: (i, 0)),
            pl.BlockSpec((1, d), lambda i, j: (0, 0)),
            pl.BlockSpec((d, tn), lambda i, j: (0, j)),
        ],
        out_specs=pl.BlockSpec((tm, tn), lambda i, j: (i, j)),
        out_shape=jax.ShapeDtypeStruct((m, n), out_dtype),
        scratch_shapes=[pltpu.VMEM((tm, d), BF16)],
        compiler_params=_cparams("parallel", "arbitrary"),
        name="norm_matmul",
    )(x2d, g.reshape(1, d), w)


def _in_proj_kernel(x_ref, g_ref, w_ref, ws_ref, o_ref, os_ref, h_ref):
    @pl.when(pl.program_id(1) == 0)
    def _():
        h_ref[...] = _rms(x_ref[...], g_ref[...]).astype(BF16)
        os_ref[...] = _dot(h_ref[...], ws_ref[...])

    o_ref[...] = _dot(h_ref[...], w_ref[...]).astype(o_ref.dtype)


def _in_proj(x2d, g, w, w_small, tm, tn):
    m, d = x2d.shape
    n = w.shape[1]
    ns = w_small.shape[1]
    return pl.pallas_call(
        _in_proj_kernel,
        grid=(m // tm, n // tn),
        in_specs=[
            pl.BlockSpec((tm, d), lambda i, j: (i, 0)),
            pl.BlockSpec((1, d), lambda i, j: (0, 0)),
            pl.BlockSpec((d, tn), lambda i, j: (0, j)),
            pl.BlockSpec((d, ns), lambda i, j: (0, 0)),
        ],
        out_specs=[pl.BlockSpec((tm, tn), lambda i, j: (i, j)), pl.BlockSpec((tm, ns), lambda i, j: (i, 0))],
        out_shape=[jax.ShapeDtypeStruct((m, n), BF16), jax.ShapeDtypeStruct((m, ns), F32)],
        scratch_shapes=[pltpu.VMEM((tm, d), BF16)],
        compiler_params=_cparams("parallel", "arbitrary"),
        name="in_proj",
    )(x2d, g.reshape(1, d), w, w_small)


def _mlstm_kernel(q_ref, k_ref, v_ref, o_ref, qt_ref, kt_ref, gate_ref, gbias_ref,
                  cw_ref, ng_ref, out_ref, c_scr, n_scr, m_scr, *, chunk):
    L, d = chunk, ML_HEAD_DIM
    c = pl.program_id(1)

    @pl.when(c == 0)
    def _():
        c_scr[...] = jnp.zeros_like(c_scr)
        n_scr[...] = jnp.zeros_like(n_scr)
        m_scr[...] = jnp.zeros_like(m_scr)

    t_idx = lax.broadcasted_iota(jnp.int32, (L, L), 0)
    s_idx = lax.broadcasted_iota(jnp.int32, (L, L), 1)
    tri = s_idx <= t_idx

    row8 = lax.broadcasted_iota(jnp.int32, (8, d), 0)

    def conv_silu(x, tail, w):
        x = x.astype(F32)
        tail = jnp.where(c > 0, tail.astype(F32), 0.0)
        y = x * w[ML_CONV - 1:ML_CONV, :]
        for j in range(ML_CONV - 1):
            s = ML_CONV - 1 - j
            r = pltpu.roll(x, s, 0)
            head = jnp.where(row8 < s, pltpu.roll(tail, s, 0), r[:8])
            y = y + jnp.concatenate([head, r[8:]], axis=0) * w[j:j + 1, :]
        return y * (0.5 * jnp.tanh(0.5 * y) + 0.5)

    g_cols = gate_ref[0] + gbias_ref[...]
    g_rows = g_cols.T

    outs = []
    for h in range(ML_HEADS):
        sl = slice(h * d, (h + 1) * d)
        cw = cw_ref[h]
        q = conv_silu(q_ref[0, :, sl], qt_ref[0, :, sl], cw[:, :d])
        k = conv_silu(k_ref[0, :, sl], kt_ref[0, :, sl], cw[:, d:]) * (d ** -0.5)
        v = v_ref[0, :, sl]
        qb = q.astype(BF16)
        kb = k.astype(BF16)

        i_col = g_cols[:, h:h + 1]
        f_col = jax.nn.log_sigmoid(g_cols[:, ML_HEADS + h:ML_HEADS + h + 1])
        i_row = g_rows[h:h + 1, :]
        f_row = jax.nn.log_sigmoid(g_rows[ML_HEADS + h:ML_HEADS + h + 1, :])

        b_col = jnp.sum(jnp.where(tri, f_row, 0.0), axis=1, keepdims=True)
        b_row = jnp.sum(jnp.where(t_idx <= s_idx, f_col, 0.0), axis=0, keepdims=True)
        dlog = jnp.where(tri, b_col + (i_row - b_row), NEG_INF)
        m_prev = m_scr[h]
        inter = b_col + m_prev
        mt = jnp.maximum(jnp.max(dlog, axis=1, keepdims=True), inter)
        w_intra = jnp.exp(dlog - mt)
        w_inter = jnp.exp(inter - mt)

        s = _dot_nt(qb, kb) * w_intra
        cmat = c_scr[h]
        nvec = n_scr[h]
        num = _dot(s.astype(BF16), v) + w_inter * _dot(qb, cmat.astype(BF16))
        den = jnp.sum(s, axis=1, keepdims=True) + w_inter * jnp.sum(q * nvec, axis=1, keepdims=True)
        hc = num / jnp.maximum(jnp.abs(den), jnp.exp(-mt))

        bl = jnp.sum(f_row, axis=1, keepdims=True)
        logw = bl - b_col + i_col
        m_new = jnp.maximum(bl + m_prev, jnp.max(logw, axis=0, keepdims=True))
        decay = jnp.exp(bl + m_prev - m_new)
        kw = k * jnp.exp(logw - m_new)
        c_scr[h] = decay * cmat + _dot_tn(kw.astype(BF16), v)
        n_scr[h] = decay * nvec + jnp.sum(kw, axis=0, keepdims=True)
        m_scr[h] = m_new

        outs.append(jax.nn.sigmoid(o_ref[0, :, sl].astype(F32)) * _rms(hc, ng_ref[h]))
    out_ref[0] = jnp.concatenate(outs, axis=1).astype(out_ref.dtype)


def _mlstm(act, gates, gate_bias, conv_w, norm_g, *, col0, chunk):
    B, S, _ = act.shape
    H, d, L = ML_HEADS, ML_HEAD_DIM, chunk
    tail_blocks = L // 8

    def blk(off):
        return pl.BlockSpec((1, L, H * d), lambda b, c: (b, c, col0 + off))

    def tail(off):
        return pl.BlockSpec((1, 8, H * d), lambda b, c: (b, jnp.maximum(c * tail_blocks - 1, 0), col0 + off))

    def const(a):
        return pl.BlockSpec(a.shape, lambda b, c: (0,) * a.ndim)

    return pl.pallas_call(
        functools.partial(_mlstm_kernel, chunk=L),
        grid=(B, S // L),
        in_specs=[
            blk(0), blk(1), blk(2), blk(3), tail(0), tail(1),
            pl.BlockSpec((1, L, gates.shape[2]), lambda b, c: (b, c, 0)),
            const(gate_bias), const(conv_w), const(norm_g),
        ],
        out_specs=pl.BlockSpec((1, L, H * d), lambda b, c: (b, c, 0)),
        out_shape=jax.ShapeDtypeStruct((B, S, H * d), BF16),
        scratch_shapes=[pltpu.VMEM((H, d, d), F32), pltpu.VMEM((H, 1, d), F32), pltpu.VMEM((H, 1, 1), F32)],
        compiler_params=_cparams("parallel", "arbitrary"),
        name="mlstm",
    )(act, act, act, act, act, act, gates, gate_bias, conv_w, norm_g)


def _compress_kernel(x_ref, pos_ref, w1_ref, w2_ref, o_ref):
    x = x_ref[0, 0, 0].astype(F32)
    n = x.shape[0]
    top = _dot((x + pos_ref[0, 0]).astype(BF16), w1_ref[0, 0])
    bot = _dot((x + pos_ref[0, 1]).astype(BF16), w1_ref[0, 1])
    pre = top + pltpu.roll(bot, n - 1, 0)
    hid = jax.nn.gelu(pre, approximate=True)
    out = _dot(hid.astype(BF16), w2_ref[0])
    row = lax.broadcasted_iota(jnp.int32, out.shape, 0)
    out = jnp.where(row < n - 1, out, 0.0)
    j = lax.broadcasted_iota(jnp.int32, out.shape, 0)
    lane = lax.broadcasted_iota(jnp.int32, out.shape, 1)
    cols = jnp.where(lane == 0, j // 16, jnp.where(lane == 1, j % 16, jnp.where(lane == 2, 1, 0))).astype(F32)
    cols = jnp.where(pl.program_id(0) == 0, cols, 0.0)
    o_ref[0, 0, 0] = jnp.concatenate([out, cols], axis=1).astype(o_ref.dtype)


def _compress(chunks, pos, w1, w2):
    _, B, G, n, width = chunks.shape
    hidden = w1.shape[-1]
    dh = w2.shape[-1]
    return pl.pallas_call(
        _compress_kernel,
        grid=(2, B, G),
        in_specs=[
            pl.BlockSpec((1, 1, 1, n, width), lambda a, b, g: (a, b, g, 0, 0)),
            pl.BlockSpec((1, 2, 1, width), lambda a, b, g: (a, 0, 0, 0)),
            pl.BlockSpec((1, 2, width, hidden), lambda a, b, g: (a, 0, 0, 0)),
            pl.BlockSpec((1, hidden, dh), lambda a, b, g: (a, 0, 0)),
        ],
        out_specs=pl.BlockSpec((1, 1, 1, n, 2 * dh), lambda a, b, g: (a, b, g, 0, 0)),
        out_shape=jax.ShapeDtypeStruct((2, B, G, n, 2 * dh), BF16),
        compiler_params=_cparams("parallel", "parallel", "parallel"),
        name="nsa_compress",
    )(chunks, pos, w1, w2)


def _nsa_kernel(q_ref, gate_ref, kc_ref, vc_ref, ksl_ref, vsl_ref, kwn_ref, vwn_ref, kconst_ref, ovl_ref, out_ref,
                ks_scr, vs_scr, kw_scr, vw_scr, s_scr, mrun_scr, lrun_scr, acc_scr, *, tq, seq, gate_col0):
    HG, dh = NSA_GROUP_HEADS, NSA_HEAD_DIM
    g = pl.program_id(1)
    i = pl.program_id(2)
    t0 = i * tq
    rows = HG * tq
    n_cmp_pad = kc_ref.shape[2]
    n_slc = seq // SEL_BLOCK
    half = tq // 2
    blocks_per_chunk = tq // SEL_BLOCK

    def assemble(lo):
        ks_scr[...] = jnp.concatenate([ksl_ref[0][:, lo:lo + dh], kconst_ref[...]], axis=1)
        kw_scr[...] = jnp.concatenate([kwn_ref[0][:, lo:lo + dh], kconst_ref[:, n_slc:n_slc + dh]], axis=1)
        vs_scr[...] = vsl_ref[0][:, lo:lo + dh]
        vw_scr[...] = vwn_ref[0][:, lo:lo + dh]

    for gg in range(NSA_KV_GROUPS):
        @pl.when((i == 0) & (g == gg))
        def _(gg=gg):
            assemble(gg * dh)

    q_all = q_ref[0]
    q_heads = [(q_all[:, x * dh:(x + 1) * dh].astype(F32) * (dh ** -0.5)).astype(BF16) for x in range(HG)]

    def head_slope(x):
        sl = jnp.float32(0.0)
        for hh in range(NSA_HEADS):
            sl = jnp.where(g * HG + x == hh, 2.0 ** (-8.0 * (hh + 1) / NSA_HEADS), sl)
        return sl

    slopes = [head_slope(x) for x in range(HG)]

    def alibi_cols(x, width, coef):
        lane = lax.broadcasted_iota(jnp.int32, (tq, width), 1)
        v = jnp.zeros((tq, width), F32)
        for c, val in enumerate(coef):
            v = jnp.where(lane == c, slopes[x] * val, v)
        return v.astype(BF16)

    def stack_q(width, coef, extra=None):
        parts = []
        for x in range(HG):
            cols = [q_heads[x]] + ([extra] if extra is not None else []) + [alibi_cols(x, width, coef)]
            parts.append(jnp.concatenate(cols, axis=1))
        return jnp.concatenate(parts, axis=0)

    q_cmp = stack_q(dh, (16.0 * CMP_STRIDE, 1.0 * CMP_STRIDE, (CMP_BLOCK - 1) / 2))
    sc = _dot_nt(q_cmp, kc_ref[0, 0])
    t_c = t0 + lax.broadcasted_iota(jnp.int32, (rows, n_cmp_pad), 0) % tq
    cmp_end = lax.broadcasted_iota(jnp.int32, (rows, n_cmp_pad), 1) * CMP_STRIDE + (CMP_BLOCK - 1)
    sc = jnp.where(cmp_end <= t_c, sc, NEG_INF)
    e = jnp.exp(sc - jnp.max(sc, axis=1, keepdims=True))
    t_row = t0 + lax.broadcasted_iota(jnp.int32, (rows, 1), 0) % tq
    any_valid = jnp.where(t_row >= CMP_BLOCK - 1, 1.0, 0.0)
    p_cmp = e * (any_valid / jnp.sum(e, axis=1, keepdims=True))
    o_cmp = _dot(p_cmp.astype(BF16), vc_ref[0, 0])[:, :dh]

    p_sum = p_cmp[0:tq]
    for x in range(1, HG):
        p_sum = p_sum + p_cmp[x * tq:(x + 1) * tq]
    p_hi = p_sum.astype(BF16)
    p_lo = (p_sum - p_hi.astype(F32)).astype(BF16)
    ovl = ovl_ref[...]
    p_slc = _dot_nt(ovl, p_hi) + _dot_nt(ovl, p_lo)
    blk = lax.broadcasted_iota(jnp.int32, (n_slc, tq), 0)
    cur = (t0 + lax.broadcasted_iota(jnp.int32, (n_slc, tq), 1)) // SEL_BLOCK
    forced = (blk == 0) | (blk == cur) | (blk == cur - 1)
    score = jnp.where(forced, SEL_FORCE, jnp.where(blk > cur, -SEL_FORCE, p_slc))
    n_tiles = n_slc // 8
    tiles = [score[8 * a:8 * a + 8] for a in range(n_tiles)]
    ranks = [jnp.zeros((8, tq), F32) for _ in range(n_tiles)]
    sub = lax.broadcasted_iota(jnp.int32, (8, tq), 0)
    for kk in range(n_slc):
        sk = score[kk:kk + 1, :]
        for a in range(n_tiles):
            if a < kk // 8:
                before = sk > tiles[a]
            elif a > kk // 8:
                before = sk >= tiles[a]
            else:
                before = jnp.where(sub > kk % 8, jnp.where(sk >= tiles[a], 1.0, 0.0), jnp.where(sk > tiles[a], 1.0, 0.0)) > 0.5
            ranks[a] = ranks[a] + jnp.where(before, 1.0, 0.0)
    rank = jnp.concatenate(ranks, axis=0)
    selected = rank < min(SEL_COUNT, n_slc)
    sel_bias = jnp.where(selected, 0.0, SEL_MASK).T.astype(BF16)
    first_blk = jnp.min(jnp.where(selected & (blk >= blocks_per_chunk), blk, n_slc))
    c_lo = jnp.minimum(jnp.maximum(first_blk // blocks_per_chunk, 1), i)
    q_aug = stack_q(2 * dh, (1.0 * SEL_BLOCK, 1.0), extra=sel_bias)

    rr = lax.broadcasted_iota(jnp.int32, (rows, tq), 0) % tq
    ll = lax.broadcasted_iota(jnp.int32, (rows, tq), 1)
    causal = ll <= rr

    mrun_scr[...] = jnp.full(mrun_scr.shape, NEG_INF, F32)
    lrun_scr[...] = jnp.zeros(lrun_scr.shape, F32)
    acc_scr[...] = jnp.zeros(acc_scr.shape, F32)

    def scores(c, diagonal):
        start = pl.multiple_of(c * tq, tq)
        s = _dot_nt(q_aug, ks_scr[pl.ds(start, tq), :])
        if diagonal:
            s = jnp.where(causal, s, NEG_INF)
        s_scr[c] = s
        mrun_scr[...] = jnp.maximum(mrun_scr[...], jnp.maximum(s[:, :half], s[:, half:]))

    def weights(c):
        start = pl.multiple_of(c * tq, tq)
        s = s_scr[c]
        m_b = mrun_scr[...]
        p0 = jnp.exp(s[:, :half] - m_b)
        p1 = jnp.exp(s[:, half:] - m_b)
        lrun_scr[...] += p0 + p1
        p = jnp.concatenate([p0, p1], axis=1).astype(BF16)
        acc_scr[...] += _dot(p, vs_scr[pl.ds(start, tq), :])

    def chunk_pairs(lo, hi, fn):
        def pair(t, carry):
            fn(lo + 2 * t)
            fn(lo + 2 * t + 1)
            return carry

        n = hi - lo
        lax.fori_loop(0, n // 2, pair, 0)

        @pl.when(n % 2 == 1)
        def _():
            fn(hi - 1)

    @pl.when(i > 0)
    def _():
        scores(0, False)

    chunk_pairs(c_lo, i, lambda c: scores(c, False))
    scores(i, True)
    m_row = jnp.max(mrun_scr[...], axis=1, keepdims=True)
    mrun_scr[...] = jnp.broadcast_to(m_row, mrun_scr.shape)

    @pl.when(i > 0)
    def _():
        weights(0)

    chunk_pairs(c_lo, i + 1, weights)
    gates = jax.nn.sigmoid(gate_ref[0])

    def gate_col(r):
        cols = []
        for x in range(HG):
            per_group = [gate_col0 + (gg * HG + x) * NSA_N_BRANCH + r for gg in range(NSA_KV_GROUPS)]
            col = gates[:, per_group[0]:per_group[0] + 1]
            for gg in range(1, NSA_KV_GROUPS):
                col = jnp.where(g == gg, gates[:, per_group[gg]:per_group[gg] + 1], col)
            cols.append(col)
        return jnp.concatenate(cols, axis=0)

    o = gate_col(0) * o_cmp + acc_scr[...] * (gate_col(1) / jnp.sum(lrun_scr[...], axis=1, keepdims=True))

    qw_aug = stack_q(dh, (1.0 * SEL_BLOCK, 1.0))
    start_a = pl.multiple_of(jnp.maximum(t0 - 2 * tq, 0), tq)
    start_b = pl.multiple_of(jnp.maximum(t0 - tq, 0), tq)
    start_c = pl.multiple_of(t0, tq)
    s_a = jnp.where((ll > rr) & (i >= 2), _dot_nt(qw_aug, kw_scr[pl.ds(start_a, tq), :]), NEG_INF)
    s_b = jnp.where(i >= 1, _dot_nt(qw_aug, kw_scr[pl.ds(start_b, tq), :]), NEG_INF)
    s_c = jnp.where(causal, _dot_nt(qw_aug, kw_scr[pl.ds(start_c, tq), :]), NEG_INF)
    m_w = jnp.max(jnp.maximum(jnp.maximum(s_a, s_b), s_c), axis=1, keepdims=True)
    e_a = jnp.exp(s_a - m_w)
    e_b = jnp.exp(s_b - m_w)
    e_c = jnp.exp(s_c - m_w)
    l_w = jnp.sum(e_a + e_b + e_c, axis=1, keepdims=True)
    o_win = (_dot(e_a.astype(BF16), vw_scr[pl.ds(start_a, tq), :]) + _dot(e_b.astype(BF16), vw_scr[pl.ds(start_b, tq), :])
             + _dot(e_c.astype(BF16), vw_scr[pl.ds(start_c, tq), :]))
    o = o + o_win * (gate_col(2) / l_w)

    out_ref[0] = jnp.concatenate([o[x * tq:(x + 1) * tq] for x in range(HG)], axis=1).astype(out_ref.dtype)


def _nsa_attention(act, gates, kc, vc, kconst, overlap_t, *, q_col0, kv_col0, gate_col0, tq):
    B, S, _ = act.shape
    G, HG, dh = NSA_KV_GROUPS, NSA_GROUP_HEADS, NSA_HEAD_DIM
    n_slc = S // SEL_BLOCK
    rows = HG * tq
    assert WINDOW == 2 * tq, "the window branch reads exactly the two chunks before the diagonal one"

    def kv_spec(off):
        return pl.BlockSpec((1, S, G * dh), lambda b, g, i: (b, 0, kv_col0 + off))

    def full(a):
        return pl.BlockSpec((1, 1) + a.shape[2:], lambda b, g, i: (b, g, 0, 0))

    return pl.pallas_call(
        functools.partial(_nsa_kernel, tq=tq, seq=S, gate_col0=gate_col0),
        grid=(B, G, S // tq),
        in_specs=[
            pl.BlockSpec((1, tq, HG * dh), lambda b, g, i: (b, i, q_col0 + g)),
            pl.BlockSpec((1, tq, gates.shape[2]), lambda b, g, i: (b, i, 0)),
            full(kc), full(vc), kv_spec(0), kv_spec(1), kv_spec(2), kv_spec(3),
            pl.BlockSpec(kconst.shape, lambda b, g, i: (0, 0)),
            pl.BlockSpec(overlap_t.shape, lambda b, g, i: (0, 0)),
        ],
        out_specs=pl.BlockSpec((1, tq, HG * dh), lambda b, g, i: (b, i, g)),
        out_shape=jax.ShapeDtypeStruct((B, S, NSA_WIDTH), BF16),
        scratch_shapes=[pltpu.VMEM((S, dh + n_slc + 2 * dh), BF16), pltpu.VMEM((S, dh), BF16),
                        pltpu.VMEM((S, 2 * dh), BF16), pltpu.VMEM((S, dh), BF16),
                        pltpu.VMEM((S // tq, rows, tq), F32), pltpu.VMEM((rows, tq // 2), F32),
                        pltpu.VMEM((rows, tq // 2), F32), pltpu.VMEM((rows, dh), F32)],
        compiler_params=_cparams("parallel", "parallel", "arbitrary"),
        name="nsa_attn",
    )(act, gates, kc, vc, act, act, act, act, kconst, overlap_t)


def _mix_out_kernel(x_ref, hm_ref, hn_ref, gm_ref, gn_ref, wm_ref, wn_ref, wo_ref, o_ref):
    y_ml = _dot(hm_ref[...], wm_ref[...])
    y_ns = _dot(hn_ref[...], wn_ref[...])
    mix = jax.nn.sigmoid(gm_ref[...].astype(F32)) * y_ml + jax.nn.sigmoid(gn_ref[...].astype(F32)) * y_ns
    o_ref[...] = x_ref[...] + _dot(mix.astype(BF16), wo_ref[...])


def _mix_out(x2d, h_ml, h_ns, act2d, wm, wn, wo, *, gate_col0, tm):
    n, d = x2d.shape

    def res(w):
        return pl.BlockSpec(w.shape, lambda i: (0, 0))

    return pl.pallas_call(
        _mix_out_kernel,
        grid=(n // tm,),
        in_specs=[
            pl.BlockSpec((tm, d), lambda i: (i, 0)),
            pl.BlockSpec((tm, h_ml.shape[1]), lambda i: (i, 0)),
            pl.BlockSpec((tm, h_ns.shape[1]), lambda i: (i, 0)),
            pl.BlockSpec((tm, d), lambda i: (i, gate_col0)),
            pl.BlockSpec((tm, d), lambda i: (i, gate_col0 + 1)),
            res(wm), res(wn), res(wo),
        ],
        out_specs=pl.BlockSpec((tm, d), lambda i: (i, 0)),
        out_shape=jax.ShapeDtypeStruct((n, d), F32),
        compiler_params=_cparams("parallel"),
        name="mix_out",
    )(x2d, h_ml, h_ns, act2d, act2d, wm, wn, wo)


def _xattn_kernel(x_ref, g_ref, wq_ref, k_ref, v_ref, wo_ref, o_ref):
    x = x_ref[...]
    d = x.shape[1]
    dh = d // XA_HEADS
    h = _rms(x, g_ref[...]).astype(BF16)
    q = (_dot(h, wq_ref[...]) * (dh ** -0.5)).astype(BF16)
    k = k_ref[0]
    v = v_ref[0]
    outs = []
    for hd in range(XA_HEADS):
        sl = slice(hd * dh, (hd + 1) * dh)
        s = _dot_nt(q[:, sl], k[:, sl])
        e = jnp.exp(s - jnp.max(s, axis=1, keepdims=True))
        p = e / jnp.sum(e, axis=1, keepdims=True)
        outs.append(_dot(p.astype(BF16), v[:, sl]))
    o = jnp.concatenate(outs, axis=1).astype(BF16)
    o_ref[...] = x + _dot(o, wo_ref[...])


def _xattn(x2d, g, wq, kv, wo, *, seq, tm):
    n, d = x2d.shape
    n_mem = kv.shape[1]
    tiles_per_batch = seq // tm
    return pl.pallas_call(
        _xattn_kernel,
        grid=(n // tm,),
        in_specs=[
            pl.BlockSpec((tm, d), lambda i: (i, 0)),
            pl.BlockSpec((1, d), lambda i: (0, 0)),
            pl.BlockSpec(wq.shape, lambda i: (0, 0)),
            pl.BlockSpec((1, n_mem, d), lambda i: (i // tiles_per_batch, 0, 0)),
            pl.BlockSpec((1, n_mem, d), lambda i: (i // tiles_per_batch, 0, 1)),
            pl.BlockSpec(wo.shape, lambda i: (0, 0)),
        ],
        out_specs=pl.BlockSpec((tm, d), lambda i: (i, 0)),
        out_shape=jax.ShapeDtypeStruct((n, d), F32),
        compiler_params=_cparams("parallel"),
        name="xattn",
    )(x2d, g.reshape(1, d), wq, kv, kv, wo)


def _route(logits, bg, be):
    G, EG = MOE_GROUPS, MOE_EXPERTS_PER_GROUP
    tm = logits.shape[1]
    lg = logits[0:G] + bg
    eg = jnp.exp(lg - jnp.max(lg, axis=0, keepdims=True))
    pg = eg / jnp.sum(eg, axis=0, keepdims=True)
    g_val = jnp.max(pg, axis=0, keepdims=True)
    g_row = lax.broadcasted_iota(jnp.int32, (G, tm), 0)
    g_idx = jnp.min(jnp.where(pg == g_val, g_row, G), axis=0, keepdims=True)
    el = logits[8:8 + G * EG] + be
    e_in = jnp.zeros((EG, tm), F32)
    for gg in range(G):
        e_in = jnp.where(g_idx == gg, el[gg * EG:(gg + 1) * EG], e_in)
    e_row = lax.broadcasted_iota(jnp.int32, (EG, tm), 0)
    v1 = jnp.max(e_in, axis=0, keepdims=True)
    i1 = jnp.min(jnp.where(e_in == v1, e_row, EG), axis=0, keepdims=True)
    rest = jnp.where(e_row == i1, -jnp.inf, e_in)
    v2 = jnp.max(rest, axis=0, keepdims=True)
    i2 = jnp.min(jnp.where(rest == v2, e_row, EG), axis=0, keepdims=True)
    e2 = jnp.exp(v2 - v1)
    c1 = g_val / (1.0 + e2)
    c2 = g_val * e2 / (1.0 + e2)
    ex = lax.broadcasted_iota(jnp.int32, (G * EG, tm), 0)
    base = g_idx * EG
    return jnp.where(ex == base + i1, c1, 0.0) + jnp.where(ex == base + i2, c2, 0.0), g_idx


def _moe_kernel(x_ref, g_ref, wr_ref, bg_ref, be_ref, before_ref, w1_ref, w3_ref, w2_ref, gf_ref, o_ref,
                h_scr, slot_scr, wt_scr, acc_scr, *, caps):
    G, EG = MOE_GROUPS, MOE_EXPERTS_PER_GROUP
    grp = pl.program_id(1)
    tm = x_ref.shape[0]

    @pl.when(grp == 0)
    def _():
        h = _rms(x_ref[...], g_ref[...])
        h_hi = h.astype(BF16)
        h_scr[...] = h_hi
        h_lo = (h - h_hi.astype(F32)).astype(BF16)
        w_hi = wr_ref[...].astype(BF16)
        w_lo = (wr_ref[...] - w_hi.astype(F32)).astype(BF16)
        logits = _dot_nt(w_hi, h_hi) + _dot_nt(w_hi, h_lo) + _dot_nt(w_lo, h_hi)
        wt, g_idx = _route(logits, bg_ref[...], be_ref[...])
        member = jnp.where(lax.broadcasted_iota(jnp.int32, (8, tm), 0) == g_idx, 1.0, 0.0)
        prefix = _dot(member.astype(BF16), before_ref[...])
        slots = jnp.where(member > 0.0, prefix, -1.0)
        zeros = jnp.zeros((8 - EG, tm), F32)
        for gg in range(G):
            slot_scr[gg] = jnp.broadcast_to(slots[gg:gg + 1], (8, tm))
            wt_scr[gg] = jnp.concatenate([wt[gg * EG:(gg + 1) * EG], zeros], axis=0)
        acc_scr[...] = jnp.zeros_like(acc_scr)

    slot = slot_scr[grp][0:1, :]
    n_rows = jnp.max(slot).astype(jnp.int32) + 1
    wts = wt_scr[grp]

    def one_pass(cap, first_row):
        row = lax.broadcasted_iota(jnp.int32, (cap, tm), 0).astype(F32)
        pick = jnp.where(row == slot - first_row.astype(F32), 1.0, 0.0)
        pick_b = pick.astype(BF16)
        hsub = _dot(pick_b, h_scr[...]).astype(BF16)
        y = jnp.zeros((cap, x_ref.shape[1]), F32)
        for e in range(EG):
            w_e = jnp.sum(pick * wts[e:e + 1, :], axis=1, keepdims=True)
            a = _dot(hsub, w1_ref[e])
            a = a * jax.nn.sigmoid(a) * _dot(hsub, w3_ref[e]) * w_e
            y = y + _dot(a.astype(BF16), w2_ref[e])
        acc_scr[...] += _dot_tn(pick_b, y.astype(BF16))

    cap_short, cap_long = caps

    @pl.when(n_rows <= cap_short)
    def _():
        one_pass(cap_short, jnp.int32(0))

    @pl.when(n_rows > cap_short)
    def _():
        def body(p, carry):
            one_pass(cap_long, p * cap_long)
            return carry

        lax.fori_loop(0, (n_rows + cap_long - 1) // cap_long, body, 0)

    @pl.when(grp == pl.num_programs(1) - 1)
    def _():
        o_ref[...] = _rms(x_ref[...] + acc_scr[...], gf_ref[...])


def _moe(x2d, g, wr, bg, be, w1, w3, w2, g_final, *, tm, caps):
    n, d = x2d.shape
    E, _, hid = w1.shape
    G, EG = MOE_GROUPS, MOE_EXPERTS_PER_GROUP
    before = jnp.asarray(np.arange(tm)[:, None] < np.arange(tm)[None, :], BF16)
    once = pl.Buffered(1)

    def const(shape):
        return pl.BlockSpec(shape, lambda i, e: (0,) * len(shape), pipeline_mode=once)

    return pl.pallas_call(
        functools.partial(_moe_kernel, caps=caps),
        grid=(n // tm, G),
        in_specs=[
            pl.BlockSpec((tm, d), lambda i, e: (i, 0), pipeline_mode=once),
            const((1, d)), const(wr.shape), const(bg.shape), const(be.shape), const((tm, tm)),
            pl.BlockSpec((EG, d, hid), lambda i, e: (e, 0, 0)),
            pl.BlockSpec((EG, d, hid), lambda i, e: (e, 0, 0)),
            pl.BlockSpec((EG, hid, d), lambda i, e: (e, 0, 0)),
            const((1, d)),
        ],
        out_specs=pl.BlockSpec((tm, d), lambda i, e: (i, 0)),
        out_shape=jax.ShapeDtypeStruct((n, d), F32),
        scratch_shapes=[pltpu.VMEM((tm, d), BF16), pltpu.VMEM((G, 8, tm), F32), pltpu.VMEM((G, 8, tm), F32),
                        pltpu.VMEM((tm, d), F32)],
        compiler_params=_cparams("parallel", "arbitrary"),
        name="moe",
    )(x2d, g.reshape(1, d), wr, bg, be, before, w1, w3, w2, g_final.reshape(1, d))


def _overlap_t(seq):
    n_cmp_pad = seq // CMP_STRIDE
    cs = np.arange(n_cmp_pad)[None, :] * CMP_STRIDE
    ss = np.arange(seq // SEL_BLOCK)[:, None] * SEL_BLOCK
    ov = np.clip(np.minimum(cs + CMP_BLOCK, ss + SEL_BLOCK) - np.maximum(cs, ss), 0, None) / CMP_BLOCK
    ov[:, n_cmp_pad - 1] = 0.0
    return jnp.asarray(ov, BF16)


def _layer(x, mem, norm_mix, w_in, conv_qk, b_igate, b_fgate, mlstm_norm, cmp_pos_k, cmp_pos_v, cmp_k_w1, cmp_k_w2,
           cmp_v_w1, cmp_v_w2, w_br_mlstm, w_br_nsa, w_mix_out, norm_xattn, norm_mem, xa_wq, xa_wkv, xa_wo, norm_ffn,
           router_group_w, router_group_b, router_expert_w, router_expert_b, moe_w1, moe_w3, moe_w2, norm_final):
    B, S, D = x.shape
    N = B * S
    H, d = ML_HEADS, ML_HEAD_DIM
    G, HG, dh = NSA_KV_GROUPS, NSA_GROUP_HEADS, NSA_HEAD_DIM
    x2d = x.reshape(N, D)

    o_mlqkvo = 0
    o_mlif = 4 * ML_WIDTH
    o_nsq = o_mlif + 2 * H
    o_kv = o_nsq + NSA_WIDTH
    o_nsg = o_kv + 6 * NSA_KV_WIDTH
    o_merge = o_nsg + NSA_N_BRANCH * NSA_HEADS
    w_act = jnp.concatenate([w_in[:, o_merge:o_merge + 2 * D], w_in[:, o_mlqkvo:o_mlif], w_in[:, o_nsq:o_kv],
                             w_in[:, o_kv:o_nsg]], axis=1).astype(BF16)
    n_small = 2 * H + NSA_N_BRANCH * NSA_HEADS
    w_small = jnp.concatenate([w_in[:, o_mlif:o_nsq], w_in[:, o_nsg:o_merge], jnp.zeros((D, 128 - n_small), F32)],
                              axis=1).astype(BF16)
    act, small = _in_proj(x2d, norm_mix, w_act, w_small, tm=1024, tn=1792)
    act3 = act.reshape(B, S, act.shape[1])
    c_ml = 2 * D
    c_nsq = c_ml + 4 * ML_WIDTH
    c_kv = c_nsq + NSA_WIDTH

    small3 = small.reshape(B, S, small.shape[1])
    gate_bias = jnp.zeros((1, small.shape[1]), F32).at[0, 0:H].set(b_igate).at[0, H:2 * H].set(b_fgate)
    conv_w = jnp.concatenate([conv_qk[:, :ML_WIDTH].reshape(ML_CONV, H, d), conv_qk[:, ML_WIDTH:].reshape(ML_CONV, H, d)],
                             axis=-1).transpose(1, 0, 2)
    h_ml = _mlstm(act3, small3, gate_bias, conv_w, mlstm_norm.reshape(H, 1, d), col0=c_ml // ML_WIDTH, chunk=256)

    kv_cmp = act3[:, :, c_kv:c_kv + 2 * NSA_KV_WIDTH].reshape(B, S, 2, G, dh).transpose(2, 0, 3, 1, 4)
    chunks = kv_cmp.reshape(2, B, G, S // CMP_STRIDE, CMP_STRIDE * dh)
    pos = jnp.stack([cmp_pos_k, cmp_pos_v]).reshape(2, 2, 1, CMP_STRIDE * dh)
    w1c = jnp.stack([cmp_k_w1, cmp_v_w1]).reshape(2, 2, CMP_STRIDE * dh, -1).astype(BF16)
    w2c = jnp.stack([cmp_k_w2, cmp_v_w2]).astype(BF16)
    kvc = _compress(chunks, pos, w1c, w2c)
    n_slc = S // SEL_BLOCK
    tok = np.arange(S)
    kconst = np.zeros((S, n_slc + 2 * dh), np.float32)
    kconst[tok, tok // SEL_BLOCK] = 1.0
    kconst[:, n_slc] = tok // SEL_BLOCK
    kconst[:, n_slc + 1] = tok % SEL_BLOCK
    h_ns = _nsa_attention(act3, small3, kvc[0], kvc[1], jnp.asarray(kconst, BF16), _overlap_t(S),
                          q_col0=c_nsq // (HG * dh), kv_col0=(c_kv + 2 * NSA_KV_WIDTH) // NSA_KV_WIDTH,
                          gate_col0=2 * H, tq=256)

    x1 = _mix_out(x2d, h_ml.reshape(N, ML_WIDTH), h_ns.reshape(N, NSA_WIDTH), act, w_br_mlstm.astype(BF16),
                  w_br_nsa.astype(BF16), w_mix_out.astype(BF16), gate_col0=0, tm=512)

    n_mem = mem.shape[1]
    kv_mem = _norm_matmul(mem.reshape(B * n_mem, D), norm_mem, xa_wkv.astype(BF16), BF16, tm=B * n_mem, tn=512)
    x2 = _xattn(x1, norm_xattn, xa_wq.astype(BF16), kv_mem.reshape(B, n_mem, 2 * D), xa_wo.astype(BF16), seq=S, tm=512)

    wr = jnp.zeros((ROUTER_ROWS, D), F32)
    wr = wr.at[0:MOE_GROUPS].set(router_group_w.T).at[8:8 + MOE_EXPERTS].set(router_expert_w.T)
    return _moe(x2, norm_ffn, wr, router_group_b.reshape(MOE_GROUPS, 1), router_expert_b.reshape(MOE_EXPERTS, 1),
                moe_w1.astype(BF16), moe_w3.astype(BF16), moe_w2.astype(BF16), norm_final, tm=1024, caps=(256, 352)).reshape(B, S, D)


def kernel(x, mem, norm_mix, w_in, conv_qk, b_igate, b_fgate, mlstm_norm, cmp_pos_k, cmp_pos_v, cmp_k_w1, cmp_k_w2, cmp_v_w1, cmp_v_w2, w_br_mlstm, w_br_nsa, w_mix_out, norm_xattn, norm_mem, xa_wq, xa_wkv, xa_wo, norm_ffn, router_group_w, router_group_b, router_expert_w, router_expert_b, moe_w1, moe_w3, moe_w2, norm_final):
    depth = w_in.shape[0]
    assert depth == 1, "the fused final norm assumes a single layer"
    layer = 0
    return _layer(x, mem, norm_mix[layer], w_in[layer], conv_qk[layer], b_igate[layer], b_fgate[layer], mlstm_norm[layer],
                  cmp_pos_k[layer], cmp_pos_v[layer], cmp_k_w1[layer], cmp_k_w2[layer], cmp_v_w1[layer], cmp_v_w2[layer],
                  w_br_mlstm[layer], w_br_nsa[layer], w_mix_out[layer], norm_xattn[layer], norm_mem[layer], xa_wq[layer],
                  xa_wkv[layer], xa_wo[layer], norm_ffn[layer], router_group_w[layer], router_group_b[layer],
                  router_expert_w[layer], router_expert_b[layer], moe_w1[layer], moe_w3[layer], moe_w2[layer], norm_final)
```

```python
import functools
import math

import numpy as np
import jax
import jax.numpy as jnp
from jax import lax
from jax.experimental import pallas as pl
from jax.experimental.pallas import tpu as pltpu

F32 = jnp.float32
BF16 = jnp.bfloat16

ML_HEADS = 4
ML_HEAD_DIM = 128
ML_WIDTH = ML_HEADS * ML_HEAD_DIM
ML_CONV = 4
NSA_HEADS = 8
NSA_KV_GROUPS = 2
NSA_HEAD_DIM = 64
NSA_GROUP_HEADS = NSA_HEADS // NSA_KV_GROUPS
NSA_WIDTH = NSA_HEADS * NSA_HEAD_DIM
NSA_KV_WIDTH = NSA_KV_GROUPS * NSA_HEAD_DIM
NSA_N_BRANCH = 3
CMP_BLOCK = 32
CMP_STRIDE = 16
SEL_BLOCK = 64
SEL_COUNT = 16
SEL_FORCE = 1e4
WINDOW = 512
XA_HEADS = 4
MOE_GROUPS = 4
MOE_EXPERTS_PER_GROUP = 4
MOE_EXPERTS = MOE_GROUPS * MOE_EXPERTS_PER_GROUP
RMS_EPS = 1e-6
NEG_INF = -1e30
SEL_MASK = -float(2 ** 30)

VMEM_LIMIT_BYTES = 56 * 1024 * 1024
ROUTER_ROWS = 32


def _cparams(*sem):
    return pltpu.CompilerParams(dimension_semantics=sem, vmem_limit_bytes=VMEM_LIMIT_BYTES)


def _rms(x, g):
    return x * lax.rsqrt(jnp.mean(x * x, axis=-1, keepdims=True) + RMS_EPS) * g


def _dot(a, b):
    return jnp.dot(a, b, preferred_element_type=F32)


def _dot_nt(a, b):
    return lax.dot_general(a, b, (((1,), (1,)), ((), ())), preferred_element_type=F32)


def _dot_tn(a, b):
    return lax.dot_general(a, b, (((0,), (0,)), ((), ())), preferred_element_type=F32)


def _norm_matmul_kernel(x_ref, g_ref, w_ref, o_ref, h_ref):
    @pl.when(pl.program_id(1) == 0)
    def _():
        h_ref[...] = _rms(x_ref[...], g_ref[...]).astype(BF16)

    o_ref[...] = _dot(h_ref[...], w_ref[...]).astype(o_ref.dtype)


def _norm_matmul(x2d, g, w, out_dtype, tm, tn):
    m, d = x2d.shape
    n = w.shape[1]
    return pl.pallas_call(
        _norm_matmul_kernel,
        grid=(m // tm, n // tn),
        in_specs=[
            pl.BlockSpec((tm, d), lambda i, j: (i, 0)),
            pl.BlockSpec((1, d), lambda i, j: (0, 0)),
            pl.BlockSpec((d, tn), lambda i, j: (0, j)),
        ],
        out_specs=pl.BlockSpec((tm, tn), lambda i, j: (i, j)),
        out_shape=jax.ShapeDtypeStruct((m, n), out_dtype),
        scratch_shapes=[pltpu.VMEM((tm, d), BF16)],
        compiler_params=_cparams("parallel", "arbitrary"),
        name="norm_matmul",
    )(x2d, g.reshape(1, d), w)


def _in_proj_kernel(x_ref, g_ref, w_ref, ws_ref, o_ref, os_ref, h_ref):
    @pl.when(pl.program_id(1) == 0)
    def _():
        h_ref[...] = _rms(x_ref[...], g_ref[...]).astype(BF16)
        os_ref[...] = _dot(h_ref[...], ws_ref[...])

    o_ref[...] = _dot(h_ref[...], w_ref[...]).astype(o_ref.dtype)


def _in_proj(x2d, g, w, w_small, tm, tn):
    m, d = x2d.shape
    n = w.shape[1]
    ns = w_small.shape[1]
    return pl.pallas_call(
        _in_proj_kernel,
        grid=(m // tm, n // tn),
        in_specs=[
            pl.BlockSpec((tm, d), lambda i, j: (i, 0)),
            pl.BlockSpec((1, d), lambda i, j: (0, 0)),
            pl.BlockSpec((d, tn), lambda i, j: (0, j)),
            pl.BlockSpec((d, ns), lambda i, j: (0, 0)),
        ],
        out_specs=[pl.BlockSpec((tm, tn), lambda i, j: (i, j)), pl.BlockSpec((tm, ns), lambda i, j: (i, 0))],
        out_shape=[jax.ShapeDtypeStruct((m, n), BF16), jax.ShapeDtypeStruct((m, ns), F32)],
        scratch_shapes=[pltpu.VMEM((tm, d), BF16)],
        compiler_params=_cparams("parallel", "arbitrary"),
        name="in_proj",
    )(x2d, g.reshape(1, d), w, w_small)


def _mlstm_kernel(q_ref, k_ref, v_ref, o_ref, qt_ref, kt_ref, gate_ref, gbias_ref,
                  cw_ref, ng_ref, out_ref, c_scr, n_scr, m_scr, *, chunk):
    L, d = chunk, ML_HEAD_DIM
    c = pl.program_id(1)

    @pl.when(c == 0)
    def _():
        c_scr[...] = jnp.zeros_like(c_scr)
        n_scr[...] = jnp.zeros_like(n_scr)
        m_scr[...] = jnp.zeros_like(m_scr)

    t_idx = lax.broadcasted_iota(jnp.int32, (L, L), 0)
    s_idx = lax.broadcasted_iota(jnp.int32, (L, L), 1)
    tri = s_idx <= t_idx

    row8 = lax.broadcasted_iota(jnp.int32, (8, d), 0)

    def conv_silu(x, tail, w):
        x = x.astype(F32)
        tail = jnp.where(c > 0, tail.astype(F32), 0.0)
        y = x * w[ML_CONV - 1:ML_CONV, :]
        for j in range(ML_CONV - 1):
            s = ML_CONV - 1 - j
            r = pltpu.roll(x, s, 0)
            head = jnp.where(row8 < s, pltpu.roll(tail, s, 0), r[:8])
            y = y + jnp.concatenate([head, r[8:]], axis=0) * w[j:j + 1, :]
        return y * (0.5 * jnp.tanh(0.5 * y) + 0.5)

    g_cols = gate_ref[0] + gbias_ref[...]
    g_rows = g_cols.T

    outs = []
    for h in range(ML_HEADS):
        sl = slice(h * d, (h + 1) * d)
        cw = cw_ref[h]
        q = conv_silu(q_ref[0, :, sl], qt_ref[0, :, sl], cw[:, :d])
        k = conv_silu(k_ref[0, :, sl], kt_ref[0, :, sl], cw[:, d:]) * (d ** -0.5)
        v = v_ref[0, :, sl]
        qb = q.astype(BF16)
        kb = k.astype(BF16)

        i_col = g_cols[:, h:h + 1]
        f_col = jax.nn.log_sigmoid(g_cols[:, ML_HEADS + h:ML_HEADS + h + 1])
        i_row = g_rows[h:h + 1, :]
        f_row = jax.nn.log_sigmoid(g_rows[ML_HEADS + h:ML_HEADS + h + 1, :])

        b_col = jnp.sum(jnp.where(tri, f_row, 0.0), axis=1, keepdims=True)
        b_row = jnp.sum(jnp.where(t_idx <= s_idx, f_col, 0.0), axis=0, keepdims=True)
        dlog = jnp.where(tri, b_col + (i_row - b_row), NEG_INF)
        m_prev = m_scr[h]
        inter = b_col + m_prev
        mt = jnp.maximum(jnp.max(dlog, axis=1, keepdims=True), inter)
        w_intra = jnp.exp(dlog - mt)
        w_inter = jnp.exp(inter - mt)

        s = _dot_nt(qb, kb) * w_intra
        cmat = c_scr[h]
        nvec = n_scr[h]
        num = _dot(s.astype(BF16), v) + w_inter * _dot(qb, cmat.astype(BF16))
        den = jnp.sum(s, axis=1, keepdims=True) + w_inter * jnp.sum(q * nvec, axis=1, keepdims=True)
        hc = num / jnp.maximum(jnp.abs(den), jnp.exp(-mt))

        bl = jnp.sum(f_row, axis=1, keepdims=True)
        logw = bl - b_col + i_col
        m_new = jnp.maximum(bl + m_prev, jnp.max(logw, axis=0, keepdims=True))
        decay = jnp.exp(bl + m_prev - m_new)
        kw = k * jnp.exp(logw - m_new)
        c_scr[h] = decay * cmat + _dot_tn(kw.astype(BF16), v)
        n_scr[h] = decay * nvec + jnp.sum(kw, axis=0, keepdims=True)
        m_scr[h] = m_new

        outs.append(jax.nn.sigmoid(o_ref[0, :, sl].astype(F32)) * _rms(hc, ng_ref[h]))
    out_ref[0] = jnp.concatenate(outs, axis=1).astype(out_ref.dtype)


def _mlstm(act, gates, gate_bias, conv_w, norm_g, *, col0, chunk):
    B, S, _ = act.shape
    H, d, L = ML_HEADS, ML_HEAD_DIM, chunk
    tail_blocks = L // 8

    def blk(off):
        return pl.BlockSpec((1, L, H * d), lambda b, c: (b, c, col0 + off))

    def tail(off):
        return pl.BlockSpec((1, 8, H * d), lambda b, c: (b, jnp.maximum(c * tail_blocks - 1, 0), col0 + off))

    def const(a):
        return pl.BlockSpec(a.shape, lambda b, c: (0,) * a.ndim)

    return pl.pallas_call(
        functools.partial(_mlstm_kernel, chunk=L),
        grid=(B, S // L),
        in_specs=[
            blk(0), blk(1), blk(2), blk(3), tail(0), tail(1),
            pl.BlockSpec((1, L, gates.shape[2]), lambda b, c: (b, c, 0)),
            const(gate_bias), const(conv_w), const(norm_g),
        ],
        out_specs=pl.BlockSpec((1, L, H * d), lambda b, c: (b, c, 0)),
        out_shape=jax.ShapeDtypeStruct((B, S, H * d), BF16),
        scratch_shapes=[pltpu.VMEM((H, d, d), F32), pltpu.VMEM((H, 1, d), F32), pltpu.VMEM((H, 1, 1), F32)],
        compiler_params=_cparams("parallel", "arbitrary"),
        name="mlstm",
    )(act, act, act, act, act, act, gates, gate_bias, conv_w, norm_g)


def _compress_kernel(x_ref, pos_ref, w1_ref, w2_ref, o_ref):
    x = x_ref[0, 0, 0].astype(F32)
    n = x.shape[0]
    top = _dot((x + pos_ref[0, 0]).astype(BF16), w1_ref[0, 0])
    bot = _dot((x + pos_ref[0, 1]).astype(BF16), w1_ref[0, 1])
    pre = top + pltpu.roll(bot, n - 1, 0)
    hid = jax.nn.gelu(pre, approximate=True)
    out = _dot(hid.astype(BF16), w2_ref[0])
    row = lax.broadcasted_iota(jnp.int32, out.shape, 0)
    out = jnp.where(row < n - 1, out, 0.0)
    j = lax.broadcasted_iota(jnp.int32, out.shape, 0)
    lane = lax.broadcasted_iota(jnp.int32, out.shape, 1)
    cols = jnp.where(lane == 0, j // 16, jnp.where(lane == 1, j % 16, jnp.where(lane == 2, 1, 0))).astype(F32)
    cols = jnp.where(pl.program_id(0) == 0, cols, jnp.where(lane == 0, 1.0, 0.0))
    o_ref[0, 0, 0] = jnp.concatenate([out, cols], axis=1).astype(o_ref.dtype)


def _compress(chunks, pos, w1, w2):
    _, B, G, n, width = chunks.shape
    hidden = w1.shape[-1]
    dh = w2.shape[-1]
    return pl.pallas_call(
        _compress_kernel,
        grid=(2, B, G),
        in_specs=[
            pl.BlockSpec((1, 1, 1, n, width), lambda a, b, g: (a, b, g, 0, 0)),
            pl.BlockSpec((1, 2, 1, width), lambda a, b, g: (a, 0, 0, 0)),
            pl.BlockSpec((1, 2, width, hidden), lambda a, b, g: (a, 0, 0, 0)),
            pl.BlockSpec((1, hidden, dh), lambda a, b, g: (a, 0, 0)),
        ],
        out_specs=pl.BlockSpec((1, 1, 1, n, 2 * dh), lambda a, b, g: (a, b, g, 0, 0)),
        out_shape=jax.ShapeDtypeStruct((2, B, G, n, 2 * dh), BF16),
        compiler_params=_cparams("parallel", "parallel", "parallel"),
        name="nsa_compress",
    )(chunks, pos, w1, w2)


def _nsa_kernel(q_ref, gate_ref, kc_ref, vc_ref, ksl_ref, vsl_ref, kwn_ref, vwn_ref, kconst_ref, ovl_ref, tri_ref, cpat_ref,
                out_ref, ks_scr, vs_scr, kw_scr, vw_scr, s_scr, mrun_scr, acc_scr, *, tq, seq, gate_col0):
    HG, dh = NSA_GROUP_HEADS, NSA_HEAD_DIM
    g = pl.program_id(1)
    i = pl.program_id(2)
    t0 = i * tq
    rows = HG * tq
    n_cmp_pad = kc_ref.shape[2]
    n_slc = seq // SEL_BLOCK
    half = tq // 2
    blocks_per_chunk = tq // SEL_BLOCK

    def assemble(lo):
        ks_scr[...] = jnp.concatenate([ksl_ref[0][:, lo:lo + dh], kconst_ref[...]], axis=1)
        kw_scr[...] = jnp.concatenate([kwn_ref[0][:, lo:lo + dh], kconst_ref[:, n_slc:n_slc + dh]], axis=1)
        ones_col = jnp.where(lax.broadcasted_iota(jnp.int32, (seq, dh), 1) == 0, 1.0, 0.0).astype(BF16)
        vs_scr[...] = jnp.concatenate([vsl_ref[0][:, lo:lo + dh], ones_col], axis=1)
        vw_scr[...] = jnp.concatenate([vwn_ref[0][:, lo:lo + dh], ones_col], axis=1)

    for gg in range(NSA_KV_GROUPS):
        @pl.when((i == 0) & (g == gg))
        def _(gg=gg):
            assemble(gg * dh)

    q_all = q_ref[0]
    q_heads = [(q_all[:, x * dh:(x + 1) * dh].astype(F32) * (dh ** -0.5)).astype(BF16) for x in range(HG)]

    def head_slope(x):
        sl = jnp.float32(0.0)
        for hh in range(NSA_HEADS):
            sl = jnp.where(g * HG + x == hh, 2.0 ** (-8.0 * (hh + 1) / NSA_HEADS), sl)
        return sl

    slopes = [head_slope(x) for x in range(HG)]

    def alibi_cols(x, width, coef):
        lane = lax.broadcasted_iota(jnp.int32, (tq, width), 1)
        v = jnp.zeros((tq, width), F32)
        for c, val in enumerate(coef):
            v = jnp.where(lane == c, slopes[x] * val, v)
        return v.astype(BF16)

    def stack_q(width, coef, extra=None):
        parts = []
        for x in range(HG):
            cols = [q_heads[x]] + ([extra] if extra is not None else []) + [alibi_cols(x, width, coef)]
            parts.append(jnp.concatenate(cols, axis=1))
        return jnp.concatenate(parts, axis=0)

    q_cmp = stack_q(dh, (16.0 * CMP_STRIDE, 1.0 * CMP_STRIDE, (CMP_BLOCK - 1) / 2))
    sc = _dot_nt(q_cmp, kc_ref[0, 0])
    block_done = jnp.concatenate([cpat_ref[...]] * HG, axis=0) <= t0
    sc = jnp.where(block_done, sc, NEG_INF)
    e = jnp.exp(sc - jnp.max(sc, axis=1, keepdims=True))
    t_row = t0 + lax.broadcasted_iota(jnp.int32, (rows, 1), 0) % tq
    any_valid = jnp.where(t_row >= CMP_BLOCK - 1, 1.0, 0.0)
    p_cmp = e * (any_valid / jnp.sum(e, axis=1, keepdims=True))
    o_cmp = _dot(p_cmp.astype(BF16), vc_ref[0, 0])[:, :dh]

    p_sum = p_cmp[0:tq]
    for x in range(1, HG):
        p_sum = p_sum + p_cmp[x * tq:(x + 1) * tq]
    p_hi = p_sum.astype(BF16)
    p_lo = (p_sum - p_hi.astype(F32)).astype(BF16)
    ovl = ovl_ref[...]
    p_slc = _dot_nt(ovl, p_hi) + _dot_nt(ovl, p_lo)
    blk = lax.broadcasted_iota(jnp.int32, (n_slc, tq), 0)
    cur = (t0 + lax.broadcasted_iota(jnp.int32, (n_slc, tq), 1)) // SEL_BLOCK
    forced = (blk == 0) | (blk == cur) | (blk == cur - 1)
    score = jnp.where(forced, SEL_FORCE, jnp.where(blk > cur, -SEL_FORCE, p_slc))
    n_tiles = n_slc // 8
    tiles = [score[8 * a:8 * a + 8] for a in range(n_tiles)]
    ranks = [jnp.zeros((8, tq), F32) for _ in range(n_tiles)]
    sub = lax.broadcasted_iota(jnp.int32, (8, tq), 0)
    for kk in range(n_slc):
        sk = score[kk:kk + 1, :]
        for a in range(n_tiles):
            if a < kk // 8:
                before = sk > tiles[a]
            elif a > kk // 8:
                before = sk >= tiles[a]
            else:
                before = jnp.where(sub > kk % 8, jnp.where(sk >= tiles[a], 1.0, 0.0), jnp.where(sk > tiles[a], 1.0, 0.0)) > 0.5
            ranks[a] = ranks[a] + jnp.where(before, 1.0, 0.0)
    rank = jnp.concatenate(ranks, axis=0)
    selected = rank < min(SEL_COUNT, n_slc)
    sel_bias = jnp.where(selected, 0.0, SEL_MASK).T.astype(BF16)
    first_blk = jnp.min(jnp.where(selected & (blk >= blocks_per_chunk), blk, n_slc))
    c_lo = jnp.minimum(jnp.maximum(first_blk // blocks_per_chunk, 1), i)
    q_aug = stack_q(2 * dh, (1.0 * SEL_BLOCK, 1.0), extra=sel_bias)

    causal_bias = jnp.concatenate([tri_ref[0]] * HG, axis=0)
    upper_bias = jnp.concatenate([tri_ref[1]] * HG, axis=0)

    mrun_scr[...] = jnp.full(mrun_scr.shape, NEG_INF, F32)
    acc_scr[...] = jnp.zeros(acc_scr.shape, F32)

    def scores(c, diagonal):
        start = pl.multiple_of(c * tq, tq)
        s = _dot_nt(q_aug, ks_scr[pl.ds(start, tq), :])
        if diagonal:
            s = s + causal_bias
        s_scr[c] = s
        mrun_scr[...] = jnp.maximum(mrun_scr[...], jnp.maximum(s[:, :half], s[:, half:]))

    def weights(c):
        start = pl.multiple_of(c * tq, tq)
        s = s_scr[c]
        m_b = mrun_scr[...]
        p0 = jnp.exp(s[:, :half] - m_b)
        p1 = jnp.exp(s[:, half:] - m_b)
        p = jnp.concatenate([p0, p1], axis=1).astype(BF16)
        acc_scr[...] += _dot(p, vs_scr[pl.ds(start, tq), :])

    def chunk_pairs(lo, hi, fn):
        def pair(t, carry):
            fn(lo + 2 * t)
            fn(lo + 2 * t + 1)
            return carry

        n = hi - lo
        lax.fori_loop(0, n // 2, pair, 0)

        @pl.when(n % 2 == 1)
        def _():
            fn(hi - 1)

    @pl.when(i > 0)
    def _():
        scores(0, False)

    chunk_pairs(c_lo, i, lambda c: scores(c, False))
    scores(i, True)
    m_row = jnp.max(mrun_scr[...], axis=1, keepdims=True)
    mrun_scr[...] = jnp.broadcast_to(m_row, mrun_scr.shape)

    @pl.when(i > 0)
    def _():
        weights(0)

    chunk_pairs(c_lo, i + 1, weights)
    gates = jax.nn.sigmoid(gate_ref[0])

    def gate_col(r):
        cols = []
        for x in range(HG):
            per_group = [gate_col0 + (gg * HG + x) * NSA_N_BRANCH + r for gg in range(NSA_KV_GROUPS)]
            col = gates[:, per_group[0]:per_group[0] + 1]
            for gg in range(1, NSA_KV_GROUPS):
                col = jnp.where(g == gg, gates[:, per_group[gg]:per_group[gg] + 1], col)
            cols.append(col)
        return jnp.concatenate(cols, axis=0)

    acc = acc_scr[...]
    o = gate_col(0) * o_cmp + acc[:, :dh] * (gate_col(1) / acc[:, dh:dh + 1])

    qw_aug = stack_q(dh, (1.0 * SEL_BLOCK, 1.0))
    start_a = pl.multiple_of(jnp.maximum(t0 - 2 * tq, 0), tq)
    start_b = pl.multiple_of(jnp.maximum(t0 - tq, 0), tq)
    start_c = pl.multiple_of(t0, tq)
    s_a = _dot_nt(qw_aug, kw_scr[pl.ds(start_a, tq), :]) + jnp.where(i >= 2, upper_bias, NEG_INF)
    s_b = _dot_nt(qw_aug, kw_scr[pl.ds(start_b, tq), :]) + jnp.where(i >= 1, 0.0, NEG_INF)
    s_c = _dot_nt(qw_aug, kw_scr[pl.ds(start_c, tq), :]) + causal_bias
    m_w = jnp.max(jnp.maximum(jnp.maximum(s_a, s_b), s_c), axis=1, keepdims=True)
    e_a = jnp.exp(s_a - m_w)
    e_b = jnp.exp(s_b - m_w)
    e_c = jnp.exp(s_c - m_w)
    o_win = (_dot(e_a.astype(BF16), vw_scr[pl.ds(start_a, tq), :]) + _dot(e_b.astype(BF16), vw_scr[pl.ds(start_b, tq), :])
             + _dot(e_c.astype(BF16), vw_scr[pl.ds(start_c, tq), :]))
    o = o + o_win[:, :dh] * (gate_col(2) / o_win[:, dh:dh + 1])

    out_ref[0] = jnp.concatenate([o[x * tq:(x + 1) * tq] for x in range(HG)], axis=1).astype(out_ref.dtype)


def _nsa_attention(act, gates, kc, vc, kconst, overlap_t, *, q_col0, kv_col0, gate_col0, tq):
    B, S, _ = act.shape
    G, HG, dh = NSA_KV_GROUPS, NSA_GROUP_HEADS, NSA_HEAD_DIM
    n_slc = S // SEL_BLOCK
    rows = HG * tq
    assert WINDOW == 2 * tq, "the window branch reads exactly the two chunks before the diagonal one"
    r = np.arange(tq)[:, None]
    l = np.arange(tq)[None, :]
    tri_bias = jnp.asarray(np.stack([np.where(l <= r, 0.0, NEG_INF), np.where(l > r, 0.0, NEG_INF)]), F32)
    cmp_pattern = jnp.asarray(np.arange(S // CMP_STRIDE)[None, :] * CMP_STRIDE + (CMP_BLOCK - 1) - r, jnp.int32)

    def kv_spec(off):
        return pl.BlockSpec((1, S, G * dh), lambda b, g, i: (b, 0, kv_col0 + off))

    def full(a):
        return pl.BlockSpec((1, 1) + a.shape[2:], lambda b, g, i: (b, g, 0, 0))

    return pl.pallas_call(
        functools.partial(_nsa_kernel, tq=tq, seq=S, gate_col0=gate_col0),
        grid=(B, G, S // tq),
        in_specs=[
            pl.BlockSpec((1, tq, HG * dh), lambda b, g, i: (b, i, q_col0 + g)),
            pl.BlockSpec((1, tq, gates.shape[2]), lambda b, g, i: (b, i, 0)),
            full(kc), full(vc), kv_spec(0), kv_spec(1), kv_spec(2), kv_spec(3),
            pl.BlockSpec(kconst.shape, lambda b, g, i: (0, 0)),
            pl.BlockSpec(overlap_t.shape, lambda b, g, i: (0, 0)),
            pl.BlockSpec(tri_bias.shape, lambda b, g, i: (0, 0, 0)),
            pl.BlockSpec(cmp_pattern.shape, lambda b, g, i: (0, 0)),
        ],
        out_specs=pl.BlockSpec((1, tq, HG * dh), lambda b, g, i: (b, i, g)),
        out_shape=jax.ShapeDtypeStruct((B, S, NSA_WIDTH), BF16),
        scratch_shapes=[pltpu.VMEM((S, dh + n_slc + 2 * dh), BF16), pltpu.VMEM((S, 2 * dh), BF16),
                        pltpu.VMEM((S, 2 * dh), BF16), pltpu.VMEM((S, 2 * dh), BF16),
                        pltpu.VMEM((S // tq, rows, tq), F32), pltpu.VMEM((rows, tq // 2), F32),
                        pltpu.VMEM((rows, 2 * dh), F32)],
        compiler_params=_cparams("parallel", "parallel", "arbitrary"),
        name="nsa_attn",
    )(act, gates, kc, vc, act, act, act, act, kconst, overlap_t, tri_bias, cmp_pattern)


def _mix_out_kernel(x_ref, hm_ref, hn_ref, gm_ref, gn_ref, wm_ref, wn_ref, wo_ref, o_ref):
    y_ml = _dot(hm_ref[...], wm_ref[...])
    y_ns = _dot(hn_ref[...], wn_ref[...])
    mix = jax.nn.sigmoid(gm_ref[...].astype(F32)) * y_ml + jax.nn.sigmoid(gn_ref[...].astype(F32)) * y_ns
    o_ref[...] = x_ref[...] + _dot(mix.astype(BF16), wo_ref[...])


def _mix_out(x2d, h_ml, h_ns, act2d, wm, wn, wo, *, gate_col0, tm):
    n, d = x2d.shape

    def res(w):
        return pl.BlockSpec(w.shape, lambda i: (0, 0))

    return pl.pallas_call(
        _mix_out_kernel,
        grid=(n // tm,),
        in_specs=[
            pl.BlockSpec((tm, d), lambda i: (i, 0)),
            pl.BlockSpec((tm, h_ml.shape[1]), lambda i: (i, 0)),
            pl.BlockSpec((tm, h_ns.shape[1]), lambda i: (i, 0)),
            pl.BlockSpec((tm, d), lambda i: (i, gate_col0)),
            pl.BlockSpec((tm, d), lambda i: (i, gate_col0 + 1)),
            res(wm), res(wn), res(wo),
        ],
        out_specs=pl.BlockSpec((tm, d), lambda i: (i, 0)),
        out_shape=jax.ShapeDtypeStruct((n, d), F32),
        compiler_params=_cparams("parallel"),
        name="mix_out",
    )(x2d, h_ml, h_ns, act2d, act2d, wm, wn, wo)


def _xattn_kernel(x_ref, g_ref, wq_ref, k_ref, v_ref, wo_ref, o_ref):
    x = x_ref[...]
    d = x.shape[1]
    dh = d // XA_HEADS
    h = _rms(x, g_ref[...]).astype(BF16)
    q = (_dot(h, wq_ref[...]) * (dh ** -0.5)).astype(BF16)
    k = k_ref[0]
    v = v_ref[0]
    outs = []
    for hd in range(XA_HEADS):
        sl = slice(hd * dh, (hd + 1) * dh)
        s = _dot_nt(q[:, sl], k[:, sl])
        e = jnp.exp(s - jnp.max(s, axis=1, keepdims=True))
        p = e / jnp.sum(e, axis=1, keepdims=True)
        outs.append(_dot(p.astype(BF16), v[:, sl]))
    o = jnp.concatenate(outs, axis=1).astype(BF16)
    o_ref[...] = x + _dot(o, wo_ref[...])


def _xattn(x2d, g, wq, kv, wo, *, seq, tm):
    n, d = x2d.shape
    n_mem = kv.shape[1]
    tiles_per_batch = seq // tm
    return pl.pallas_call(
        _xattn_kernel,
        grid=(n // tm,),
        in_specs=[
            pl.BlockSpec((tm, d), lambda i: (i, 0)),
            pl.BlockSpec((1, d), lambda i: (0, 0)),
            pl.BlockSpec(wq.shape, lambda i: (0, 0)),
            pl.BlockSpec((1, n_mem, d), lambda i: (i // tiles_per_batch, 0, 0)),
            pl.BlockSpec((1, n_mem, d), lambda i: (i // tiles_per_batch, 0, 1)),
            pl.BlockSpec(wo.shape, lambda i: (0, 0)),
        ],
        out_specs=pl.BlockSpec((tm, d), lambda i: (i, 0)),
        out_shape=jax.ShapeDtypeStruct((n, d), F32),
        compiler_params=_cparams("parallel"),
        name="xattn",
    )(x2d, g.reshape(1, d), wq, kv, kv, wo)


def _route(logits, bg, be):
    G, EG = MOE_GROUPS, MOE_EXPERTS_PER_GROUP
    tm = logits.shape[1]
    lg = logits[0:G] + bg
    eg = jnp.exp(lg - jnp.max(lg, axis=0, keepdims=True))
    pg = eg / jnp.sum(eg, axis=0, keepdims=True)
    g_val = jnp.max(pg, axis=0, keepdims=True)
    g_row = lax.broadcasted_iota(jnp.int32, (G, tm), 0)
    g_idx = jnp.min(jnp.where(pg == g_val, g_row, G), axis=0, keepdims=True)
    el = logits[8:8 + G * EG] + be
    e_in = jnp.zeros((EG, tm), F32)
    for gg in range(G):
        e_in = jnp.where(g_idx == gg, el[gg * EG:(gg + 1) * EG], e_in)
    e_row = lax.broadcasted_iota(jnp.int32, (EG, tm), 0)
    v1 = jnp.max(e_in, axis=0, keepdims=True)
    i1 = jnp.min(jnp.where(e_in == v1, e_row, EG), axis=0, keepdims=True)
    rest = jnp.where(e_row == i1, -jnp.inf, e_in)
    v2 = jnp.max(rest, axis=0, keepdims=True)
    i2 = jnp.min(jnp.where(rest == v2, e_row, EG), axis=0, keepdims=True)
    e2 = jnp.exp(v2 - v1)
    c1 = g_val / (1.0 + e2)
    c2 = g_val * e2 / (1.0 + e2)
    ex = lax.broadcasted_iota(jnp.int32, (G * EG, tm), 0)
    base = g_idx * EG
    return jnp.where(ex == base + i1, c1, 0.0) + jnp.where(ex == base + i2, c2, 0.0), g_idx


def _moe_kernel(x_ref, g_ref, wr_ref, bg_ref, be_ref, before_ref, w1_ref, w3_ref, w2_ref, gf_ref, o_ref,
                h_scr, slot_scr, wt_scr, acc_scr, *, caps):
    G, EG = MOE_GROUPS, MOE_EXPERTS_PER_GROUP
    grp = pl.program_id(1)
    tm = x_ref.shape[0]

    @pl.when(grp == 0)
    def _():
        h = _rms(x_ref[...], g_ref[...])
        h_hi = h.astype(BF16)
        h_scr[...] = h_hi
        h_lo = (h - h_hi.astype(F32)).astype(BF16)
        w_hi = wr_ref[...].astype(BF16)
        w_lo = (wr_ref[...] - w_hi.astype(F32)).astype(BF16)
        logits = _dot_nt(w_hi, h_hi) + _dot_nt(w_hi, h_lo) + _dot_nt(w_lo, h_hi)
        wt, g_idx = _route(logits, bg_ref[...], be_ref[...])
        member = jnp.where(lax.broadcasted_iota(jnp.int32, (8, tm), 0) == g_idx, 1.0, 0.0)
        prefix = _dot(member.astype(BF16), before_ref[...])
        slots = jnp.where(member > 0.0, prefix, -1.0)
        zeros = jnp.zeros((8 - EG, tm), F32)
        for gg in range(G):
            slot_scr[gg] = jnp.broadcast_to(slots[gg:gg + 1], (8, tm))
            wt_scr[gg] = jnp.concatenate([wt[gg * EG:(gg + 1) * EG], zeros], axis=0)
        acc_scr[...] = jnp.zeros_like(acc_scr)

    slot = slot_scr[grp][0:1, :]
    n_rows = jnp.max(slot).astype(jnp.int32) + 1
    wts = wt_scr[grp]

    def one_pass(cap, first_row):
        row = lax.broadcasted_iota(jnp.int32, (cap, tm), 0).astype(F32)
        pick = jnp.where(row == slot - first_row.astype(F32), 1.0, 0.0)
        pick_b = pick.astype(BF16)
        hsub = _dot(pick_b, h_scr[...]).astype(BF16)
        y = jnp.zeros((cap, x_ref.shape[1]), F32)
        for e in range(EG):
            w_e = jnp.sum(pick * wts[e:e + 1, :], axis=1, keepdims=True)
            a = _dot(hsub, w1_ref[e])
            a = a * jax.nn.sigmoid(a) * _dot(hsub, w3_ref[e]) * w_e
            y = y + _dot(a.astype(BF16), w2_ref[e])
        acc_scr[...] += _dot_tn(pick_b, y.astype(BF16))

    *small_caps, cap_long = caps
    lower = 0
    for cap in small_caps:
        @pl.when((n_rows > lower) & (n_rows <= cap))
        def _(cap=cap):
            one_pass(cap, jnp.int32(0))
        lower = cap

    @pl.when(n_rows > lower)
    def _():
        def body(p, carry):
            one_pass(cap_long, p * cap_long)
            return carry

        lax.fori_loop(0, (n_rows + cap_long - 1) // cap_long, body, 0)

    @pl.when(grp == pl.num_programs(1) - 1)
    def _():
        o_ref[...] = _rms(x_ref[...] + acc_scr[...], gf_ref[...])


def _moe(x2d, g, wr, bg, be, w1, w3, w2, g_final, *, tm, caps):
    n, d = x2d.shape
    E, _, hid = w1.shape
    G, EG = MOE_GROUPS, MOE_EXPERTS_PER_GROUP
    before = jnp.asarray(np.arange(tm)[:, None] < np.arange(tm)[None, :], BF16)
    once = pl.Buffered(1)

    def const(shape):
        return pl.BlockSpec(shape, lambda i, e: (0,) * len(shape), pipeline_mode=once)

    return pl.pallas_call(
        functools.partial(_moe_kernel, caps=caps),
        grid=(n // tm, G),
        in_specs=[
            pl.BlockSpec((tm, d), lambda i, e: (i, 0), pipeline_mode=once),
            const((1, d)), const(wr.shape), const(bg.shape), const(be.shape), const((tm, tm)),
            pl.BlockSpec((EG, d, hid), lambda i, e: (e, 0, 0)),
            pl.BlockSpec((EG, d, hid), lambda i, e: (e, 0, 0)),
            pl.BlockSpec((EG, hid, d), lambda i, e: (e, 0, 0)),
            const((1, d)),
        ],
        out_specs=pl.BlockSpec((tm, d), lambda i, e: (i, 0)),
        out_shape=jax.ShapeDtypeStruct((n, d), F32),
        scratch_shapes=[pltpu.VMEM((tm, d), BF16), pltpu.VMEM((G, 8, tm), F32), pltpu.VMEM((G, 8, tm), F32),
                        pltpu.VMEM((tm, d), F32)],
        compiler_params=_cparams("parallel", "arbitrary"),
        name="moe",
    )(x2d, g.reshape(1, d), wr, bg, be, before, w1, w3, w2, g_final.reshape(1, d))


def _overlap_t(seq):
    n_cmp_pad = seq // CMP_STRIDE
    cs = np.arange(n_cmp_pad)[None, :] * CMP_STRIDE
    ss = np.arange(seq // SEL_BLOCK)[:, None] * SEL_BLOCK
    ov = np.clip(np.minimum(cs + CMP_BLOCK, ss + SEL_BLOCK) - np.maximum(cs, ss), 0, None) / CMP_BLOCK
    ov[:, n_cmp_pad - 1] = 0.0
    return jnp.asarray(ov, BF16)


def _layer(x, mem, norm_mix, w_in, conv_qk, b_igate, b_fgate, mlstm_norm, cmp_pos_k, cmp_pos_v, cmp_k_w1, cmp_k_w2,
           cmp_v_w1, cmp_v_w2, w_br_mlstm, w_br_nsa, w_mix_out, norm_xattn, norm_mem, xa_wq, xa_wkv, xa_wo, norm_ffn,
           router_group_w, router_group_b, router_expert_w, router_expert_b, moe_w1, moe_w3, moe_w2, norm_final):
    B, S, D = x.shape
    N = B * S
    H, d = ML_HEADS, ML_HEAD_DIM
    G, HG, dh = NSA_KV_GROUPS, NSA_GROUP_HEADS, NSA_HEAD_DIM
    x2d = x.reshape(N, D)

    o_mlqkvo = 0
    o_mlif = 4 * ML_WIDTH
    o_nsq = o_mlif + 2 * H
    o_kv = o_nsq + NSA_WIDTH
    o_nsg = o_kv + 6 * NSA_KV_WIDTH
    o_merge = o_nsg + NSA_N_BRANCH * NSA_HEADS
    w_act = jnp.concatenate([w_in[:, o_merge:o_merge + 2 * D], w_in[:, o_mlqkvo:o_mlif], w_in[:, o_nsq:o_kv],
                             w_in[:, o_kv:o_nsg]], axis=1).astype(BF16)
    n_small = 2 * H + NSA_N_BRANCH * NSA_HEADS
    w_small = jnp.concatenate([w_in[:, o_mlif:o_nsq], w_in[:, o_nsg:o_merge], jnp.zeros((D, 128 - n_small), F32)],
                              axis=1).astype(BF16)
    act, small = _in_proj(x2d, norm_mix, w_act, w_small, tm=1024, tn=1792)
    act3 = act.reshape(B, S, act.shape[1])
    c_ml = 2 * D
    c_nsq = c_ml + 4 * ML_WIDTH
    c_kv = c_nsq + NSA_WIDTH

    small3 = small.reshape(B, S, small.shape[1])
    gate_bias = jnp.zeros((1, small.shape[1]), F32).at[0, 0:H].set(b_igate).at[0, H:2 * H].set(b_fgate)
    conv_w = jnp.concatenate([conv_qk[:, :ML_WIDTH].reshape(ML_CONV, H, d), conv_qk[:, ML_WIDTH:].reshape(ML_CONV, H, d)],
                             axis=-1).transpose(1, 0, 2)
    h_ml = _mlstm(act3, small3, gate_bias, conv_w, mlstm_norm.reshape(H, 1, d), col0=c_ml // ML_WIDTH, chunk=512)

    kv_cmp = act3[:, :, c_kv:c_kv + 2 * NSA_KV_WIDTH].reshape(B, S, 2, G, dh).transpose(2, 0, 3, 1, 4)
    chunks = kv_cmp.reshape(2, B, G, S // CMP_STRIDE, CMP_STRIDE * dh)
    pos = jnp.stack([cmp_pos_k, cmp_pos_v]).reshape(2, 2, 1, CMP_STRIDE * dh)
    w1c = jnp.stack([cmp_k_w1, cmp_v_w1]).reshape(2, 2, CMP_STRIDE * dh, -1).astype(BF16)
    w2c = jnp.stack([cmp_k_w2, cmp_v_w2]).astype(BF16)
    kvc = _compress(chunks, pos, w1c, w2c)
    n_slc = S // SEL_BLOCK
    tok = np.arange(S)
    kconst = np.zeros((S, n_slc + 2 * dh), np.float32)
    kconst[tok, tok // SEL_BLOCK] = 1.0
    kconst[:, n_slc] = tok // SEL_BLOCK
    kconst[:, n_slc + 1] = tok % SEL_BLOCK
    h_ns = _nsa_attention(act3, small3, kvc[0], kvc[1], jnp.asarray(kconst, BF16), _overlap_t(S),
                          q_col0=c_nsq // (HG * dh), kv_col0=(c_kv + 2 * NSA_KV_WIDTH) // NSA_KV_WIDTH,
                          gate_col0=2 * H, tq=256)

    x1 = _mix_out(x2d, h_ml.reshape(N, ML_WIDTH), h_ns.reshape(N, NSA_WIDTH), act, w_br_mlstm.astype(BF16),
                  w_br_nsa.astype(BF16), w_mix_out.astype(BF16), gate_col0=0, tm=512)

    n_mem = mem.shape[1]
    kv_mem = _norm_matmul(mem.reshape(B * n_mem, D), norm_mem, xa_wkv.astype(BF16), BF16, tm=B * n_mem, tn=512)
    x2 = _xattn(x1, norm_xattn, xa_wq.astype(BF16), kv_mem.reshape(B, n_mem, 2 * D), xa_wo.astype(BF16), seq=S, tm=512)

    wr = jnp.zeros((ROUTER_ROWS, D), F32)
    wr = wr.at[0:MOE_GROUPS].set(router_group_w.T).at[8:8 + MOE_EXPERTS].set(router_expert_w.T)
    return _moe(x2, norm_ffn, wr, router_group_b.reshape(MOE_GROUPS, 1), router_expert_b.reshape(MOE_EXPERTS, 1),
                moe_w1.astype(BF16), moe_w3.astype(BF16), moe_w2.astype(BF16), norm_final, tm=1024, caps=(256, 304, 352)).reshape(B, S, D)


def kernel(x, mem, norm_mix, w_in, conv_qk, b_igate, b_fgate, mlstm_norm, cmp_pos_k, cmp_pos_v, cmp_k_w1, cmp_k_w2, cmp_v_w1, cmp_v_w2, w_br_mlstm, w_br_nsa, w_mix_out, norm_xattn, norm_mem, xa_wq, xa_wkv, xa_wo, norm_ffn, router_group_w, router_group_b, router_expert_w, router_expert_b, moe_w1, moe_w3, moe_w2, norm_final):
    depth = w_in.shape[0]
    assert depth == 1, "the fused final norm assumes a single layer"
    layer = 0
    return _layer(x, mem, norm_mix[layer], w_in[layer], conv_qk[layer], b_igate[layer], b_fgate[layer], mlstm_norm[layer],
                  cmp_pos_k[layer], cmp_pos_v[layer], cmp_k_w1[layer], cmp_k_w2[layer], cmp_v_w1[layer], cmp_v_w2[layer],
                  w_br_mlstm[layer], w_br_nsa[layer], w_mix_out[layer], norm_xattn[layer], norm_mem[layer], xa_wq[layer],
                  xa_wkv[layer], xa_wo[layer], norm_ffn[layer], router_group_w[layer], router_group_b[layer],
                  router_expert_w[layer], router_expert_b[layer], moe_w1[layer], moe_w3[layer], moe_w2[layer], norm_final)
```

```python
import functools
import math

import numpy as np
import jax
import jax.numpy as jnp
from jax import lax
from jax.experimental import pallas as pl
from jax.experimental.pallas import tpu as pltpu

F32 = jnp.float32
BF16 = jnp.bfloat16

ML_HEADS = 4
ML_HEAD_DIM = 128
ML_WIDTH = ML_HEADS * ML_HEAD_DIM
ML_CONV = 4
NSA_HEADS = 8
NSA_KV_GROUPS = 2
NSA_HEAD_DIM = 64
NSA_GROUP_HEADS = NSA_HEADS // NSA_KV_GROUPS
NSA_WIDTH = NSA_HEADS * NSA_HEAD_DIM
NSA_KV_WIDTH = NSA_KV_GROUPS * NSA_HEAD_DIM
NSA_N_BRANCH = 3
CMP_BLOCK = 32
CMP_STRIDE = 16
SEL_BLOCK = 64
SEL_COUNT = 16
SEL_FORCE = 1e4
WINDOW = 512
XA_HEADS = 4
MOE_GROUPS = 4
MOE_EXPERTS_PER_GROUP = 4
MOE_EXPERTS = MOE_GROUPS * MOE_EXPERTS_PER_GROUP
RMS_EPS = 1e-6
NEG_INF = -1e30
SEL_MASK = -float(2 ** 30)
LOG2E = 1.4426950408889634

VMEM_LIMIT_BYTES = 56 * 1024 * 1024
ROUTER_ROWS = 32


def _cparams(*sem):
    return pltpu.CompilerParams(dimension_semantics=sem, vmem_limit_bytes=VMEM_LIMIT_BYTES)


def _rms(x, g):
    return x * lax.rsqrt(jnp.mean(x * x, axis=-1, keepdims=True) + RMS_EPS) * g


def _dot(a, b):
    return jnp.dot(a, b, preferred_element_type=F32)


def _dot_nt(a, b):
    return lax.dot_general(a, b, (((1,), (1,)), ((), ())), preferred_element_type=F32)


def _dot_tn(a, b):
    return lax.dot_general(a, b, (((0,), (0,)), ((), ())), preferred_element_type=F32)


def _norm_matmul_kernel(x_ref, g_ref, w_ref, o_ref, h_ref):
    @pl.when(pl.program_id(1) == 0)
    def _():
        h_ref[...] = _rms(x_ref[...], g_ref[...]).astype(BF16)

    o_ref[...] = _dot(h_ref[...], w_ref[...]).astype(o_ref.dtype)


def _norm_matmul(x2d, g, w, out_dtype, tm, tn):
    m, d = x2d.shape
    n = w.shape[1]
    return pl.pallas_call(
        _norm_matmul_kernel,
        grid=(m // tm, n // tn),
        in_specs=[
            pl.BlockSpec((tm, d), lambda i, j: (i, 0)),
            pl.BlockSpec((1, d), lambda i, j: (0, 0)),
            pl.BlockSpec((d, tn), lambda i, j: (0, j)),
        ],
        out_specs=pl.BlockSpec((tm, tn), lambda i, j: (i, j)),
        out_shape=jax.ShapeDtypeStruct((m, n), out_dtype),
        scratch_shapes=[pltpu.VMEM((tm, d), BF16)],
        compiler_params=_cparams("parallel", "arbitrary"),
        name="norm_matmul",
    )(x2d, g.reshape(1, d), w)


def _in_proj_kernel(x_ref, g_ref, w_ref, ws_ref, o_ref, os_ref, h_ref):
    @pl.when(pl.program_id(1) == 0)
    def _():
        h_ref[...] = _rms(x_ref[...], g_ref[...]).astype(BF16)
        os_ref[...] = _dot(h_ref[...], ws_ref[...])

    o_ref[...] = _dot(h_ref[...], w_ref[...]).astype(o_ref.dtype)


def _in_proj(x2d, g, w, w_small, tm, tn):
    m, d = x2d.shape
    n = w.shape[1]
    ns = w_small.shape[1]
    return pl.pallas_call(
        _in_proj_kernel,
        grid=(m // tm, n // tn),
        in_specs=[
            pl.BlockSpec((tm, d), lambda i, j: (i, 0)),
            pl.BlockSpec((1, d), lambda i, j: (0, 0)),
            pl.BlockSpec((d, tn), lambda i, j: (0, j)),
            pl.BlockSpec((d, ns), lambda i, j: (0, 0)),
        ],
        out_specs=[pl.BlockSpec((tm, tn), lambda i, j: (i, j)), pl.BlockSpec((tm, ns), lambda i, j: (i, 0))],
        out_shape=[jax.ShapeDtypeStruct((m, n), BF16), jax.ShapeDtypeStruct((m, ns), F32)],
        scratch_shapes=[pltpu.VMEM((tm, d), BF16)],
        compiler_params=_cparams("parallel", "arbitrary"),
        name="in_proj",
    )(x2d, g.reshape(1, d), w, w_small)


def _mlstm_kernel(q_ref, k_ref, v_ref, o_ref, qt_ref, kt_ref, gate_ref, gbias_ref,
                  cw_ref, ng_ref, out_ref, c_scr, n_scr, m_scr, *, chunk):
    L, d = chunk, ML_HEAD_DIM
    c = pl.program_id(1)

    @pl.when(c == 0)
    def _():
        c_scr[...] = jnp.zeros_like(c_scr)
        n_scr[...] = jnp.zeros_like(n_scr)
        m_scr[...] = jnp.zeros_like(m_scr)

    t_idx = lax.broadcasted_iota(jnp.int32, (L, L), 0)
    s_idx = lax.broadcasted_iota(jnp.int32, (L, L), 1)
    tri = s_idx <= t_idx

    row8 = lax.broadcasted_iota(jnp.int32, (8, d), 0)

    def conv_silu(x, tail, w):
        x = x.astype(F32)
        tail = jnp.where(c > 0, tail.astype(F32), 0.0)
        y = x * w[ML_CONV - 1:ML_CONV, :]
        for j in range(ML_CONV - 1):
            s = ML_CONV - 1 - j
            r = pltpu.roll(x, s, 0)
            head = jnp.where(row8 < s, pltpu.roll(tail, s, 0), r[:8])
            y = y + jnp.concatenate([head, r[8:]], axis=0) * w[j:j + 1, :]
        return y * (0.5 * jnp.tanh(0.5 * y) + 0.5)

    g_cols = gate_ref[0] + gbias_ref[...]
    g_rows = g_cols.T

    outs = []
    for h in range(ML_HEADS):
        sl = slice(h * d, (h + 1) * d)
        cw = cw_ref[h]
        q = conv_silu(q_ref[0, :, sl], qt_ref[0, :, sl], cw[:, :d])
        k = conv_silu(k_ref[0, :, sl], kt_ref[0, :, sl], cw[:, d:]) * (d ** -0.5)
        v = v_ref[0, :, sl]
        qb = q.astype(BF16)
        kb = k.astype(BF16)

        i_col = g_cols[:, h:h + 1]
        f_col = jax.nn.log_sigmoid(g_cols[:, ML_HEADS + h:ML_HEADS + h + 1])
        i_row = g_rows[h:h + 1, :]
        f_row = jax.nn.log_sigmoid(g_rows[ML_HEADS + h:ML_HEADS + h + 1, :])

        b_col = jnp.sum(jnp.where(tri, f_row, 0.0), axis=1, keepdims=True)
        b_row = jnp.sum(jnp.where(t_idx <= s_idx, f_col, 0.0), axis=0, keepdims=True)
        dlog = jnp.where(tri, b_col + (i_row - b_row), NEG_INF)
        m_prev = m_scr[h]
        inter = b_col + m_prev
        mt = jnp.maximum(jnp.max(dlog, axis=1, keepdims=True), inter)
        w_intra = jnp.exp(dlog - mt)
        w_inter = jnp.exp(inter - mt)

        s = _dot_nt(qb, kb) * w_intra
        cmat = c_scr[h]
        nvec = n_scr[h]
        num = _dot(s.astype(BF16), v) + w_inter * _dot(qb, cmat.astype(BF16))
        den = jnp.sum(s, axis=1, keepdims=True) + w_inter * jnp.sum(q * nvec, axis=1, keepdims=True)
        hc = num / jnp.maximum(jnp.abs(den), jnp.exp(-mt))

        bl = jnp.sum(f_row, axis=1, keepdims=True)
        logw = bl - b_col + i_col
        m_new = jnp.maximum(bl + m_prev, jnp.max(logw, axis=0, keepdims=True))
        decay = jnp.exp(bl + m_prev - m_new)
        kw = k * jnp.exp(logw - m_new)
        c_scr[h] = decay * cmat + _dot_tn(kw.astype(BF16), v)
        n_scr[h] = decay * nvec + jnp.sum(kw, axis=0, keepdims=True)
        m_scr[h] = m_new

        outs.append(jax.nn.sigmoid(o_ref[0, :, sl].astype(F32)) * _rms(hc, ng_ref[h]))
    out_ref[0] = jnp.concatenate(outs, axis=1).astype(out_ref.dtype)


def _mlstm(act, gates, gate_bias, conv_w, norm_g, *, col0, chunk):
    B, S, _ = act.shape
    H, d, L = ML_HEADS, ML_HEAD_DIM, chunk
    tail_blocks = L // 8

    def blk(off):
        return pl.BlockSpec((1, L, H * d), lambda b, c: (b, c, col0 + off))

    def tail(off):
        return pl.BlockSpec((1, 8, H * d), lambda b, c: (b, jnp.maximum(c * tail_blocks - 1, 0), col0 + off))

    def const(a):
        return pl.BlockSpec(a.shape, lambda b, c: (0,) * a.ndim)

    return pl.pallas_call(
        functools.partial(_mlstm_kernel, chunk=L),
        grid=(B, S // L),
        in_specs=[
            blk(0), blk(1), blk(2), blk(3), tail(0), tail(1),
            pl.BlockSpec((1, L, gates.shape[2]), lambda b, c: (b, c, 0)),
            const(gate_bias), const(conv_w), const(norm_g),
        ],
        out_specs=pl.BlockSpec((1, L, H * d), lambda b, c: (b, c, 0)),
        out_shape=jax.ShapeDtypeStruct((B, S, H * d), BF16),
        scratch_shapes=[pltpu.VMEM((H, d, d), F32), pltpu.VMEM((H, 1, d), F32), pltpu.VMEM((H, 1, 1), F32)],
        compiler_params=_cparams("parallel", "arbitrary"),
        name="mlstm",
    )(act, act, act, act, act, act, gates, gate_bias, conv_w, norm_g)


def _compress_kernel(x_ref, pos_ref, w1_ref, w2_ref, o_ref):
    x = x_ref[0, 0, 0].astype(F32)
    n = x.shape[0]
    top = _dot((x + pos_ref[0, 0]).astype(BF16), w1_ref[0, 0])
    bot = _dot((x + pos_ref[0, 1]).astype(BF16), w1_ref[0, 1])
    pre = top + pltpu.roll(bot, n - 1, 0)
    hid = jax.nn.gelu(pre, approximate=True)
    out = _dot(hid.astype(BF16), w2_ref[0])
    row = lax.broadcasted_iota(jnp.int32, out.shape, 0)
    out = jnp.where(row < n - 1, out, 0.0)
    j = lax.broadcasted_iota(jnp.int32, out.shape, 0)
    lane = lax.broadcasted_iota(jnp.int32, out.shape, 1)
    cols = jnp.where(lane < 3, j // 16, jnp.where(lane < 6, j % 16, jnp.where(lane < 9, 1, 0))).astype(F32)
    cols = jnp.where(pl.program_id(0) == 0, cols, jnp.where(lane == 0, 1.0, 0.0))
    o_ref[0, 0, 0] = jnp.concatenate([out, cols], axis=1).astype(o_ref.dtype)


def _compress(chunks, pos, w1, w2):
    _, B, G, n, width = chunks.shape
    hidden = w1.shape[-1]
    dh = w2.shape[-1]
    return pl.pallas_call(
        _compress_kernel,
        grid=(2, B, G),
        in_specs=[
            pl.BlockSpec((1, 1, 1, n, width), lambda a, b, g: (a, b, g, 0, 0)),
            pl.BlockSpec((1, 2, 1, width), lambda a, b, g: (a, 0, 0, 0)),
            pl.BlockSpec((1, 2, width, hidden), lambda a, b, g: (a, 0, 0, 0)),
            pl.BlockSpec((1, hidden, dh), lambda a, b, g: (a, 0, 0)),
        ],
        out_specs=pl.BlockSpec((1, 1, 1, n, 2 * dh), lambda a, b, g: (a, b, g, 0, 0)),
        out_shape=jax.ShapeDtypeStruct((2, B, G, n, 2 * dh), BF16),
        compiler_params=_cparams("parallel", "parallel", "parallel"),
        name="nsa_compress",
    )(chunks, pos, w1, w2)


def _nsa_kernel(q_ref, gate_ref, kc_ref, vc_ref, ksl_ref, vsl_ref, kwn_ref, vwn_ref, kconst_ref, ovl_ref, tri_ref, cpat_ref,
                out_ref, ks_scr, vs_scr, kw_scr, vw_scr, s_scr, mrun_scr, acc_scr, *, tq, seq, gate_col0):
    HG, dh = NSA_GROUP_HEADS, NSA_HEAD_DIM
    g = pl.program_id(1)
    i = pl.program_id(2)
    t0 = i * tq
    rows = HG * tq
    n_cmp_pad = kc_ref.shape[2]
    n_slc = seq // SEL_BLOCK
    half = tq // 2
    blocks_per_chunk = tq // SEL_BLOCK

    def assemble(lo):
        ks_scr[...] = jnp.concatenate([ksl_ref[0][:, lo:lo + dh], kconst_ref[...]], axis=1)
        kw_scr[...] = jnp.concatenate([kwn_ref[0][:, lo:lo + dh], kconst_ref[:, n_slc:n_slc + dh]], axis=1)
        ones_col = jnp.where(lax.broadcasted_iota(jnp.int32, (seq, dh), 1) == 0, 1.0, 0.0).astype(BF16)
        vs_scr[...] = jnp.concatenate([vsl_ref[0][:, lo:lo + dh], ones_col], axis=1)
        vw_scr[...] = jnp.concatenate([vwn_ref[0][:, lo:lo + dh], ones_col], axis=1)

    for gg in range(NSA_KV_GROUPS):
        @pl.when((i == 0) & (g == gg))
        def _(gg=gg):
            assemble(gg * dh)

    q_all = q_ref[0]
    q_heads = [(q_all[:, x * dh:(x + 1) * dh].astype(F32) * (dh ** -0.5 * LOG2E)).astype(BF16) for x in range(HG)]

    def head_slope(x):
        sl = jnp.float32(0.0)
        for hh in range(NSA_HEADS):
            sl = jnp.where(g * HG + x == hh, 2.0 ** (-8.0 * (hh + 1) / NSA_HEADS), sl)
        return sl

    slopes = [head_slope(x) for x in range(HG)]

    def alibi_cols(x, width, coef):
        lane = lax.broadcasted_iota(jnp.int32, (tq, width), 1)
        v = jnp.zeros((tq, width), F32)
        for c, val in enumerate(coef):
            v = jnp.where(lane // 3 == c, slopes[x] * (val * LOG2E), v)
        hi = v.astype(BF16).astype(F32)
        mid = (v - hi).astype(BF16).astype(F32)
        lo = v - hi - mid
        return jnp.where(lane % 3 == 0, hi, jnp.where(lane % 3 == 1, mid, lo)).astype(BF16)

    def stack_q(width, coef, extra=None):
        parts = []
        for x in range(HG):
            cols = [q_heads[x]] + ([extra] if extra is not None else []) + [alibi_cols(x, width, coef)]
            parts.append(jnp.concatenate(cols, axis=1))
        return jnp.concatenate(parts, axis=0)

    q_cmp = stack_q(dh, (16.0 * CMP_STRIDE, 1.0 * CMP_STRIDE, (CMP_BLOCK - 1) / 2))
    sc = _dot_nt(q_cmp, kc_ref[0, 0])
    block_done = jnp.concatenate([cpat_ref[...]] * HG, axis=0) <= t0
    sc = jnp.where(block_done, sc, NEG_INF)
    e = jnp.exp2(sc - jnp.max(sc, axis=1, keepdims=True))
    t_row = t0 + lax.broadcasted_iota(jnp.int32, (rows, 1), 0) % tq
    any_valid = jnp.where(t_row >= CMP_BLOCK - 1, 1.0, 0.0)
    p_cmp = e * (any_valid / jnp.sum(e, axis=1, keepdims=True))
    o_cmp = _dot(p_cmp.astype(BF16), vc_ref[0, 0])[:, :dh]

    p_sum = p_cmp[0:tq]
    for x in range(1, HG):
        p_sum = p_sum + p_cmp[x * tq:(x + 1) * tq]
    p_hi = p_sum.astype(BF16)
    p_lo = (p_sum - p_hi.astype(F32)).astype(BF16)
    ovl = ovl_ref[...]
    p_slc = _dot_nt(ovl, p_hi) + _dot_nt(ovl, p_lo)
    blk = lax.broadcasted_iota(jnp.int32, (n_slc, tq), 0)
    cur = (t0 + lax.broadcasted_iota(jnp.int32, (n_slc, tq), 1)) // SEL_BLOCK
    forced = (blk == 0) | (blk == cur) | (blk == cur - 1)
    score = jnp.where(forced, SEL_FORCE, jnp.where(blk > cur, -SEL_FORCE, p_slc))
    n_tiles = n_slc // 8
    tiles = [score[8 * a:8 * a + 8] for a in range(n_tiles)]
    ranks = [jnp.zeros((8, tq), F32) for _ in range(n_tiles)]
    sub = lax.broadcasted_iota(jnp.int32, (8, tq), 0)
    for kk in range(n_slc):
        sk = score[kk:kk + 1, :]
        for a in range(n_tiles):
            if a < kk // 8:
                before = sk > tiles[a]
            elif a > kk // 8:
                before = sk >= tiles[a]
            else:
                before = jnp.where(sub > kk % 8, jnp.where(sk >= tiles[a], 1.0, 0.0), jnp.where(sk > tiles[a], 1.0, 0.0)) > 0.5
            ranks[a] = ranks[a] + jnp.where(before, 1.0, 0.0)
    rank = jnp.concatenate(ranks, axis=0)
    selected = rank < min(SEL_COUNT, n_slc)
    sel_bias = jnp.where(selected, 0.0, SEL_MASK).T.astype(BF16)
    first_blk = jnp.min(jnp.where(selected & (blk >= blocks_per_chunk), blk, n_slc))
    c_lo = jnp.minimum(jnp.maximum(first_blk // blocks_per_chunk, 1), i)
    q_aug = stack_q(2 * dh, (1.0 * SEL_BLOCK, 1.0), extra=sel_bias)

    causal_bias = jnp.concatenate([tri_ref[0]] * HG, axis=0)
    upper_bias = jnp.concatenate([tri_ref[1]] * HG, axis=0)

    qw_aug = stack_q(dh, (1.0 * SEL_BLOCK, 1.0))
    start_a = pl.multiple_of(jnp.maximum(t0 - 2 * tq, 0), tq)
    start_b = pl.multiple_of(jnp.maximum(t0 - tq, 0), tq)
    start_c = pl.multiple_of(t0, tq)
    s_a = _dot_nt(qw_aug, kw_scr[pl.ds(start_a, tq), :]) + jnp.where(i >= 2, upper_bias, NEG_INF)
    s_b = _dot_nt(qw_aug, kw_scr[pl.ds(start_b, tq), :]) + jnp.where(i >= 1, 0.0, NEG_INF)
    s_c = _dot_nt(qw_aug, kw_scr[pl.ds(start_c, tq), :]) + causal_bias
    m_w = jnp.max(jnp.maximum(jnp.maximum(s_a, s_b), s_c), axis=1, keepdims=True)
    e_a = jnp.exp2(s_a - m_w)
    e_b = jnp.exp2(s_b - m_w)
    e_c = jnp.exp2(s_c - m_w)
    o_win = (_dot(e_a.astype(BF16), vw_scr[pl.ds(start_a, tq), :]) + _dot(e_b.astype(BF16), vw_scr[pl.ds(start_b, tq), :])
             + _dot(e_c.astype(BF16), vw_scr[pl.ds(start_c, tq), :]))

    mrun_scr[...] = jnp.full(mrun_scr.shape, NEG_INF, F32)
    acc_scr[...] = jnp.zeros(acc_scr.shape, F32)

    def scores(c, diagonal):
        start = pl.multiple_of(c * tq, tq)
        s = _dot_nt(q_aug, ks_scr[pl.ds(start, tq), :])
        if diagonal:
            s = s + causal_bias
        s_scr[c] = s
        mrun_scr[...] = jnp.maximum(mrun_scr[...], jnp.maximum(s[:, :half], s[:, half:]))

    def weights(c):
        start = pl.multiple_of(c * tq, tq)
        s = s_scr[c]
        m_b = mrun_scr[...]
        p0 = jnp.exp2(s[:, :half] - m_b)
        p1 = jnp.exp2(s[:, half:] - m_b)
        p = jnp.concatenate([p0, p1], axis=1).astype(BF16)
        acc_scr[...] += _dot(p, vs_scr[pl.ds(start, tq), :])

    def chunk_pairs(lo, hi, fn):
        def pair(t, carry):
            fn(lo + 2 * t)
            fn(lo + 2 * t + 1)
            return carry

        n = hi - lo
        lax.fori_loop(0, n // 2, pair, 0)

        @pl.when(n % 2 == 1)
        def _():
            fn(hi - 1)

    scores(i, True)

    @pl.when(i > 0)
    def _():
        scores(0, False)

    chunk_pairs(c_lo, i, lambda c: scores(c, False))
    m_row = jnp.max(mrun_scr[...], axis=1, keepdims=True)
    mrun_scr[...] = jnp.broadcast_to(m_row, mrun_scr.shape)

    @pl.when(i > 0)
    def _():
        weights(0)

    chunk_pairs(c_lo, i, weights)
    weights(i)
    gates = jax.nn.sigmoid(gate_ref[0])

    def gate_col(r):
        cols = []
        for x in range(HG):
            per_group = [gate_col0 + (gg * HG + x) * NSA_N_BRANCH + r for gg in range(NSA_KV_GROUPS)]
            col = gates[:, per_group[0]:per_group[0] + 1]
            for gg in range(1, NSA_KV_GROUPS):
                col = jnp.where(g == gg, gates[:, per_group[gg]:per_group[gg] + 1], col)
            cols.append(col)
        return jnp.concatenate(cols, axis=0)

    acc = acc_scr[...]
    o = gate_col(0) * o_cmp + acc[:, :dh] * (gate_col(1) / acc[:, dh:dh + 1])

    o = o + o_win[:, :dh] * (gate_col(2) / o_win[:, dh:dh + 1])

    out_ref[0] = jnp.concatenate([o[x * tq:(x + 1) * tq] for x in range(HG)], axis=1).astype(out_ref.dtype)


def _nsa_attention(act, gates, kc, vc, kconst, overlap_t, *, q_col0, kv_col0, gate_col0, tq):
    B, S, _ = act.shape
    G, HG, dh = NSA_KV_GROUPS, NSA_GROUP_HEADS, NSA_HEAD_DIM
    n_slc = S // SEL_BLOCK
    rows = HG * tq
    assert WINDOW == 2 * tq, "the window branch reads exactly the two chunks before the diagonal one"
    r = np.arange(tq)[:, None]
    l = np.arange(tq)[None, :]
    tri_bias = jnp.asarray(np.stack([np.where(l <= r, 0.0, NEG_INF), np.where(l > r, 0.0, NEG_INF)]), F32)
    cmp_pattern = jnp.asarray(np.arange(S // CMP_STRIDE)[None, :] * CMP_STRIDE + (CMP_BLOCK - 1) - r, jnp.int32)

    def kv_spec(off):
        return pl.BlockSpec((1, S, G * dh), lambda b, g, i: (b, 0, kv_col0 + off))

    def full(a):
        return pl.BlockSpec((1, 1) + a.shape[2:], lambda b, g, i: (b, g, 0, 0))

    return pl.pallas_call(
        functools.partial(_nsa_kernel, tq=tq, seq=S, gate_col0=gate_col0),
        grid=(B, G, S // tq),
        in_specs=[
            pl.BlockSpec((1, tq, HG * dh), lambda b, g, i: (b, i, q_col0 + g)),
            pl.BlockSpec((1, tq, gates.shape[2]), lambda b, g, i: (b, i, 0)),
            full(kc), full(vc), kv_spec(0), kv_spec(1), kv_spec(2), kv_spec(3),
            pl.BlockSpec(kconst.shape, lambda b, g, i: (0, 0)),
            pl.BlockSpec(overlap_t.shape, lambda b, g, i: (0, 0)),
            pl.BlockSpec(tri_bias.shape, lambda b, g, i: (0, 0, 0)),
            pl.BlockSpec(cmp_pattern.shape, lambda b, g, i: (0, 0)),
        ],
        out_specs=pl.BlockSpec((1, tq, HG * dh), lambda b, g, i: (b, i, g)),
        out_shape=jax.ShapeDtypeStruct((B, S, NSA_WIDTH), BF16),
        scratch_shapes=[pltpu.VMEM((S, dh + n_slc + 2 * dh), BF16), pltpu.VMEM((S, 2 * dh), BF16),
                        pltpu.VMEM((S, 2 * dh), BF16), pltpu.VMEM((S, 2 * dh), BF16),
                        pltpu.VMEM((S // tq, rows, tq), F32), pltpu.VMEM((rows, tq // 2), F32),
                        pltpu.VMEM((rows, 2 * dh), F32)],
        compiler_params=_cparams("parallel", "parallel", "arbitrary"),
        name="nsa_attn",
    )(act, gates, kc, vc, act, act, act, act, kconst, overlap_t, tri_bias, cmp_pattern)


def _mix_out_kernel(x_ref, hm_ref, hn_ref, gm_ref, gn_ref, wm_ref, wn_ref, wo_ref, o_ref):
    y_ml = _dot(hm_ref[...], wm_ref[...])
    y_ns = _dot(hn_ref[...], wn_ref[...])
    mix = jax.nn.sigmoid(gm_ref[...].astype(F32)) * y_ml + jax.nn.sigmoid(gn_ref[...].astype(F32)) * y_ns
    o_ref[...] = x_ref[...] + _dot(mix.astype(BF16), wo_ref[...])


def _mix_out(x2d, h_ml, h_ns, act2d, wm, wn, wo, *, gate_col0, tm):
    n, d = x2d.shape

    def res(w):
        return pl.BlockSpec(w.shape, lambda i: (0, 0))

    return pl.pallas_call(
        _mix_out_kernel,
        grid=(n // tm,),
        in_specs=[
            pl.BlockSpec((tm, d), lambda i: (i, 0)),
            pl.BlockSpec((tm, h_ml.shape[1]), lambda i: (i, 0)),
            pl.BlockSpec((tm, h_ns.shape[1]), lambda i: (i, 0)),
            pl.BlockSpec((tm, d), lambda i: (i, gate_col0)),
            pl.BlockSpec((tm, d), lambda i: (i, gate_col0 + 1)),
            res(wm), res(wn), res(wo),
        ],
        out_specs=pl.BlockSpec((tm, d), lambda i: (i, 0)),
        out_shape=jax.ShapeDtypeStruct((n, d), F32),
        compiler_params=_cparams("parallel"),
        name="mix_out",
    )(x2d, h_ml, h_ns, act2d, act2d, wm, wn, wo)


def _xattn_kernel(x_ref, g_ref, wq_ref, k_ref, v_ref, wo_ref, o_ref):
    x = x_ref[...]
    d = x.shape[1]
    dh = d // XA_HEADS
    h = _rms(x, g_ref[...]).astype(BF16)
    q = (_dot(h, wq_ref[...]) * (dh ** -0.5)).astype(BF16)
    k = k_ref[0]
    v = v_ref[0]
    outs = []
    for hd in range(XA_HEADS):
        sl = slice(hd * dh, (hd + 1) * dh)
        s = _dot_nt(q[:, sl], k[:, sl])
        e = jnp.exp(s - jnp.max(s, axis=1, keepdims=True))
        p = e / jnp.sum(e, axis=1, keepdims=True)
        outs.append(_dot(p.astype(BF16), v[:, sl]))
    o = jnp.concatenate(outs, axis=1).astype(BF16)
    o_ref[...] = x + _dot(o, wo_ref[...])


def _xattn(x2d, g, wq, kv, wo, *, seq, tm):
    n, d = x2d.shape
    n_mem = kv.shape[1]
    tiles_per_batch = seq // tm
    return pl.pallas_call(
        _xattn_kernel,
        grid=(n // tm,),
        in_specs=[
            pl.BlockSpec((tm, d), lambda i: (i, 0)),
            pl.BlockSpec((1, d), lambda i: (0, 0)),
            pl.BlockSpec(wq.shape, lambda i: (0, 0)),
            pl.BlockSpec((1, n_mem, d), lambda i: (i // tiles_per_batch, 0, 0)),
            pl.BlockSpec((1, n_mem, d), lambda i: (i // tiles_per_batch, 0, 1)),
            pl.BlockSpec(wo.shape, lambda i: (0, 0)),
        ],
        out_specs=pl.BlockSpec((tm, d), lambda i: (i, 0)),
        out_shape=jax.ShapeDtypeStruct((n, d), F32),
        compiler_params=_cparams("parallel"),
        name="xattn",
    )(x2d, g.reshape(1, d), wq, kv, kv, wo)


def _route(logits, bg, be):
    G, EG = MOE_GROUPS, MOE_EXPERTS_PER_GROUP
    tm = logits.shape[1]
    lg = logits[0:G] + bg
    eg = jnp.exp(lg - jnp.max(lg, axis=0, keepdims=True))
    pg = eg / jnp.sum(eg, axis=0, keepdims=True)
    g_val = jnp.max(pg, axis=0, keepdims=True)
    g_row = lax.broadcasted_iota(jnp.int32, (G, tm), 0)
    g_idx = jnp.min(jnp.where(pg == g_val, g_row, G), axis=0, keepdims=True)
    el = logits[8:8 + G * EG] + be
    e_in = jnp.zeros((EG, tm), F32)
    for gg in range(G):
        e_in = jnp.where(g_idx == gg, el[gg * EG:(gg + 1) * EG], e_in)
    e_row = lax.broadcasted_iota(jnp.int32, (EG, tm), 0)
    v1 = jnp.max(e_in, axis=0, keepdims=True)
    i1 = jnp.min(jnp.where(e_in == v1, e_row, EG), axis=0, keepdims=True)
    rest = jnp.where(e_row == i1, -jnp.inf, e_in)
    v2 = jnp.max(rest, axis=0, keepdims=True)
    i2 = jnp.min(jnp.where(rest == v2, e_row, EG), axis=0, keepdims=True)
    e2 = jnp.exp(v2 - v1)
    c1 = g_val / (1.0 + e2)
    c2 = g_val * e2 / (1.0 + e2)
    ex = lax.broadcasted_iota(jnp.int32, (G * EG, tm), 0)
    base = g_idx * EG
    return jnp.where(ex == base + i1, c1, 0.0) + jnp.where(ex == base + i2, c2, 0.0), g_idx


def _moe_kernel(x_ref, g_ref, wr_ref, bg_ref, be_ref, before_ref, w1_ref, w3_ref, w2_ref, gf_ref, o_ref,
                h_scr, slot_scr, wt_scr, acc_scr, *, caps):
    G, EG = MOE_GROUPS, MOE_EXPERTS_PER_GROUP
    grp = pl.program_id(1)
    tm = x_ref.shape[0]

    @pl.when(grp == 0)
    def _():
        h = _rms(x_ref[...], g_ref[...])
        h_hi = h.astype(BF16)
        h_scr[...] = h_hi
        h_lo = (h - h_hi.astype(F32)).astype(BF16)
        w_hi = wr_ref[...].astype(BF16)
        w_lo = (wr_ref[...] - w_hi.astype(F32)).astype(BF16)
        logits = _dot_nt(w_hi, h_hi) + _dot_nt(w_hi, h_lo) + _dot_nt(w_lo, h_hi)
        wt, g_idx = _route(logits, bg_ref[...], be_ref[...])
        member = jnp.where(lax.broadcasted_iota(jnp.int32, (8, tm), 0) == g_idx, 1.0, 0.0)
        prefix = _dot(member.astype(BF16), before_ref[...])
        slots = jnp.where(member > 0.0, prefix, -1.0)
        zeros = jnp.zeros((8 - EG, tm), F32)
        for gg in range(G):
            slot_scr[gg] = jnp.broadcast_to(slots[gg:gg + 1], (8, tm))
            wt_scr[gg] = jnp.concatenate([wt[gg * EG:(gg + 1) * EG], zeros], axis=0)
        acc_scr[...] = jnp.zeros_like(acc_scr)

    slot = slot_scr[grp][0:1, :]
    n_rows = jnp.max(slot).astype(jnp.int32) + 1
    wts = wt_scr[grp]

    def one_pass(cap, first_row):
        row = lax.broadcasted_iota(jnp.int32, (cap, tm), 0).astype(F32)
        pick = jnp.where(row == slot - first_row.astype(F32), 1.0, 0.0)
        pick_b = pick.astype(BF16)
        hsub = _dot(pick_b, h_scr[...]).astype(BF16)
        y = jnp.zeros((cap, x_ref.shape[1]), F32)
        for e in range(EG):
            w_e = jnp.sum(pick * wts[e:e + 1, :], axis=1, keepdims=True)
            a = _dot(hsub, w1_ref[e])
            a = a * jax.nn.sigmoid(a) * _dot(hsub, w3_ref[e]) * w_e
            y = y + _dot(a.astype(BF16), w2_ref[e])
        acc_scr[...] += _dot_tn(pick_b, y.astype(BF16))

    *small_caps, cap_long = caps
    lower = 0
    for cap in small_caps:
        @pl.when((n_rows > lower) & (n_rows <= cap))
        def _(cap=cap):
            one_pass(cap, jnp.int32(0))
        lower = cap

    @pl.when(n_rows > lower)
    def _():
        def body(p, carry):
            one_pass(cap_long, p * cap_long)
            return carry

        lax.fori_loop(0, (n_rows + cap_long - 1) // cap_long, body, 0)

    @pl.when(grp == pl.num_programs(1) - 1)
    def _():
        o_ref[...] = _rms(x_ref[...] + acc_scr[...], gf_ref[...])


def _moe(x2d, g, wr, bg, be, w1, w3, w2, g_final, *, tm, caps):
    n, d = x2d.shape
    E, _, hid = w1.shape
    G, EG = MOE_GROUPS, MOE_EXPERTS_PER_GROUP
    before = jnp.asarray(np.arange(tm)[:, None] < np.arange(tm)[None, :], BF16)
    once = pl.Buffered(1)

    def const(shape):
        return pl.BlockSpec(shape, lambda i, e: (0,) * len(shape), pipeline_mode=once)

    return pl.pallas_call(
        functools.partial(_moe_kernel, caps=caps),
        grid=(n // tm, G),
        in_specs=[
            pl.BlockSpec((tm, d), lambda i, e: (i, 0), pipeline_mode=once),
            const((1, d)), const(wr.shape), const(bg.shape), const(be.shape), const((tm, tm)),
            pl.BlockSpec((EG, d, hid), lambda i, e: (e, 0, 0)),
            pl.BlockSpec((EG, d, hid), lambda i, e: (e, 0, 0)),
            pl.BlockSpec((EG, hid, d), lambda i, e: (e, 0, 0)),
            const((1, d)),
        ],
        out_specs=pl.BlockSpec((tm, d), lambda i, e: (i, 0)),
        out_shape=jax.ShapeDtypeStruct((n, d), F32),
        scratch_shapes=[pltpu.VMEM((tm, d), BF16), pltpu.VMEM((G, 8, tm), F32), pltpu.VMEM((G, 8, tm), F32),
                        pltpu.VMEM((tm, d), F32)],
        compiler_params=_cparams("parallel", "arbitrary"),
        name="moe",
    )(x2d, g.reshape(1, d), wr, bg, be, before, w1, w3, w2, g_final.reshape(1, d))


def _overlap_t(seq):
    n_cmp_pad = seq // CMP_STRIDE
    cs = np.arange(n_cmp_pad)[None, :] * CMP_STRIDE
    ss = np.arange(seq // SEL_BLOCK)[:, None] * SEL_BLOCK
    ov = np.clip(np.minimum(cs + CMP_BLOCK, ss + SEL_BLOCK) - np.maximum(cs, ss), 0, None) / CMP_BLOCK
    ov[:, n_cmp_pad - 1] = 0.0
    return jnp.asarray(ov, BF16)


def _layer(x, mem, norm_mix, w_in, conv_qk, b_igate, b_fgate, mlstm_norm, cmp_pos_k, cmp_pos_v, cmp_k_w1, cmp_k_w2,
           cmp_v_w1, cmp_v_w2, w_br_mlstm, w_br_nsa, w_mix_out, norm_xattn, norm_mem, xa_wq, xa_wkv, xa_wo, norm_ffn,
           router_group_w, router_group_b, router_expert_w, router_expert_b, moe_w1, moe_w3, moe_w2, norm_final):
    B, S, D = x.shape
    N = B * S
    H, d = ML_HEADS, ML_HEAD_DIM
    G, HG, dh = NSA_KV_GROUPS, NSA_GROUP_HEADS, NSA_HEAD_DIM
    x2d = x.reshape(N, D)

    o_mlqkvo = 0
    o_mlif = 4 * ML_WIDTH
    o_nsq = o_mlif + 2 * H
    o_kv = o_nsq + NSA_WIDTH
    o_nsg = o_kv + 6 * NSA_KV_WIDTH
    o_merge = o_nsg + NSA_N_BRANCH * NSA_HEADS
    w_act = jnp.concatenate([w_in[:, o_merge:o_merge + 2 * D], w_in[:, o_mlqkvo:o_mlif], w_in[:, o_nsq:o_kv],
                             w_in[:, o_kv:o_nsg]], axis=1).astype(BF16)
    n_small = 2 * H + NSA_N_BRANCH * NSA_HEADS
    w_small = jnp.concatenate([w_in[:, o_mlif:o_nsq], w_in[:, o_nsg:o_merge], jnp.zeros((D, 128 - n_small), F32)],
                              axis=1).astype(BF16)
    act, small = _in_proj(x2d, norm_mix, w_act, w_small, tm=1024, tn=1792)
    act3 = act.reshape(B, S, act.shape[1])
    c_ml = 2 * D
    c_nsq = c_ml + 4 * ML_WIDTH
    c_kv = c_nsq + NSA_WIDTH

    small3 = small.reshape(B, S, small.shape[1])
    gate_bias = jnp.zeros((1, small.shape[1]), F32).at[0, 0:H].set(b_igate).at[0, H:2 * H].set(b_fgate)
    conv_w = jnp.concatenate([conv_qk[:, :ML_WIDTH].reshape(ML_CONV, H, d), conv_qk[:, ML_WIDTH:].reshape(ML_CONV, H, d)],
                             axis=-1).transpose(1, 0, 2)
    h_ml = _mlstm(act3, small3, gate_bias, conv_w, mlstm_norm.reshape(H, 1, d), col0=c_ml // ML_WIDTH, chunk=512)

    kv_cmp = act3[:, :, c_kv:c_kv + 2 * NSA_KV_WIDTH].reshape(B, S, 2, G, dh).transpose(2, 0, 3, 1, 4)
    chunks = kv_cmp.reshape(2, B, G, S // CMP_STRIDE, CMP_STRIDE * dh)
    pos = jnp.stack([cmp_pos_k, cmp_pos_v]).reshape(2, 2, 1, CMP_STRIDE * dh)
    w1c = jnp.stack([cmp_k_w1, cmp_v_w1]).reshape(2, 2, CMP_STRIDE * dh, -1).astype(BF16)
    w2c = jnp.stack([cmp_k_w2, cmp_v_w2]).astype(BF16)
    kvc = _compress(chunks, pos, w1c, w2c)
    n_slc = S // SEL_BLOCK
    tok = np.arange(S)
    kconst = np.zeros((S, n_slc + 2 * dh), np.float32)
    kconst[tok, tok // SEL_BLOCK] = 1.0
    kconst[:, n_slc:n_slc + 3] = (tok // SEL_BLOCK)[:, None]
    kconst[:, n_slc + 3:n_slc + 6] = (tok % SEL_BLOCK)[:, None]
    h_ns = _nsa_attention(act3, small3, kvc[0], kvc[1], jnp.asarray(kconst, BF16), _overlap_t(S),
                          q_col0=c_nsq // (HG * dh), kv_col0=(c_kv + 2 * NSA_KV_WIDTH) // NSA_KV_WIDTH,
                          gate_col0=2 * H, tq=256)

    x1 = _mix_out(x2d, h_ml.reshape(N, ML_WIDTH), h_ns.reshape(N, NSA_WIDTH), act, w_br_mlstm.astype(BF16),
                  w_br_nsa.astype(BF16), w_mix_out.astype(BF16), gate_col0=0, tm=512)

    n_mem = mem.shape[1]
    kv_mem = _norm_matmul(mem.reshape(B * n_mem, D), norm_mem, xa_wkv.astype(BF16), BF16, tm=B * n_mem, tn=512)
    x2 = _xattn(x1, norm_xattn, xa_wq.astype(BF16), kv_mem.reshape(B, n_mem, 2 * D), xa_wo.astype(BF16), seq=S, tm=1024)

    wr = jnp.zeros((ROUTER_ROWS, D), F32)
    wr = wr.at[0:MOE_GROUPS].set(router_group_w.T).at[8:8 + MOE_EXPERTS].set(router_expert_w.T)
    return _moe(x2, norm_ffn, wr, router_group_b.reshape(MOE_GROUPS, 1), router_expert_b.reshape(MOE_EXPERTS, 1),
                moe_w1.astype(BF16), moe_w3.astype(BF16), moe_w2.astype(BF16), norm_final, tm=1024, caps=(256, 304, 352)).reshape(B, S, D)


def kernel(x, mem, norm_mix, w_in, conv_qk, b_igate, b_fgate, mlstm_norm, cmp_pos_k, cmp_pos_v, cmp_k_w1, cmp_k_w2, cmp_v_w1, cmp_v_w2, w_br_mlstm, w_br_nsa, w_mix_out, norm_xattn, norm_mem, xa_wq, xa_wkv, xa_wo, norm_ffn, router_group_w, router_group_b, router_expert_w, router_expert_b, moe_w1, moe_w3, moe_w2, norm_final):
    depth = w_in.shape[0]
    assert depth == 1, "the fused final norm assumes a single layer"
    layer = 0
    return _layer(x, mem, norm_mix[layer], w_in[layer], conv_qk[layer], b_igate[layer], b_fgate[layer], mlstm_norm[layer],
                  cmp_pos_k[layer], cmp_pos_v[layer], cmp_k_w1[layer], cmp_k_w2[layer], cmp_v_w1[layer], cmp_v_w2[layer],
                  w_br_mlstm[layer], w_br_nsa[layer], w_mix_out[layer], norm_xattn[layer], norm_mem[layer], xa_wq[layer],
                  xa_wkv[layer], xa_wo[layer], norm_ffn[layer], router_group_w[layer], router_group_b[layer],
                  router_expert_w[layer], router_expert_b[layer], moe_w1[layer], moe_w3[layer], moe_w2[layer], norm_final)
```

```python
import functools
import math

import numpy as np
import jax
import jax.numpy as jnp
from jax import lax
from jax.experimental import pallas as pl
from jax.experimental.pallas import tpu as pltpu

F32 = jnp.float32
BF16 = jnp.bfloat16

ML_HEADS = 4
ML_HEAD_DIM = 128
ML_WIDTH = ML_HEADS * ML_HEAD_DIM
ML_CONV = 4
NSA_HEADS = 8
NSA_KV_GROUPS = 2
NSA_HEAD_DIM = 64
NSA_GROUP_HEADS = NSA_HEADS // NSA_KV_GROUPS
NSA_WIDTH = NSA_HEADS * NSA_HEAD_DIM
NSA_KV_WIDTH = NSA_KV_GROUPS * NSA_HEAD_DIM
NSA_N_BRANCH = 3
CMP_BLOCK = 32
CMP_STRIDE = 16
SEL_BLOCK = 64
SEL_COUNT = 16
SEL_FORCE = 1e4
WINDOW = 512
XA_HEADS = 4
MOE_GROUPS = 4
MOE_EXPERTS_PER_GROUP = 4
MOE_EXPERTS = MOE_GROUPS * MOE_EXPERTS_PER_GROUP
RMS_EPS = 1e-6
NEG_INF = -1e30
SEL_MASK = -float(2 ** 30)
LOG2E = 1.4426950408889634

VMEM_LIMIT_BYTES = 56 * 1024 * 1024
ROUTER_ROWS = 32


def _cparams(*sem):
    return pltpu.CompilerParams(dimension_semantics=sem, vmem_limit_bytes=VMEM_LIMIT_BYTES)


def _rms(x, g):
    return x * lax.rsqrt(jnp.mean(x * x, axis=-1, keepdims=True) + RMS_EPS) * g


def _dot(a, b):
    return jnp.dot(a, b, preferred_element_type=F32)


def _dot_nt(a, b):
    return lax.dot_general(a, b, (((1,), (1,)), ((), ())), preferred_element_type=F32)


def _dot_tn(a, b):
    return lax.dot_general(a, b, (((0,), (0,)), ((), ())), preferred_element_type=F32)


def _norm_matmul_kernel(x_ref, g_ref, w_ref, o_ref, h_ref):
    @pl.when(pl.program_id(1) == 0)
    def _():
        h_ref[...] = _rms(x_ref[...], g_ref[...]).astype(BF16)

    o_ref[...] = _dot(h_ref[...], w_ref[...]).astype(o_ref.dtype)


def _norm_matmul(x2d, g, w, out_dtype, tm, tn):
    m, d = x2d.shape
    n = w.shape[1]
    return pl.pallas_call(
        _norm_matmul_kernel,
        grid=(m // tm, n // tn),
        in_specs=[
            pl.BlockSpec((tm, d), lambda i, j: (i, 0)),
            pl.BlockSpec((1, d), lambda i, j: (0, 0)),
            pl.BlockSpec((d, tn), lambda i, j: (0, j)),
        ],
        out_specs=pl.BlockSpec((tm, tn), lambda i, j: (i, j)),
        out_shape=jax.ShapeDtypeStruct((m, n), out_dtype),
        scratch_shapes=[pltpu.VMEM((tm, d), BF16)],
        compiler_params=_cparams("parallel", "arbitrary"),
        name="norm_matmul",
    )(x2d, g.reshape(1, d), w)


def _in_proj_kernel(x_ref, g_ref, w_ref, ws_ref, o_ref, os_ref, h_ref):
    @pl.when(pl.program_id(1) == 0)
    def _():
        h_ref[...] = _rms(x_ref[...], g_ref[...]).astype(BF16)
        os_ref[...] = _dot(h_ref[...], ws_ref[...])

    o_ref[...] = _dot(h_ref[...], w_ref[...]).astype(o_ref.dtype)


def _in_proj(x2d, g, w, w_small, tm, tn):
    m, d = x2d.shape
    n = w.shape[1]
    ns = w_small.shape[1]
    return pl.pallas_call(
        _in_proj_kernel,
        grid=(m // tm, n // tn),
        in_specs=[
            pl.BlockSpec((tm, d), lambda i, j: (i, 0)),
            pl.BlockSpec((1, d), lambda i, j: (0, 0)),
            pl.BlockSpec((d, tn), lambda i, j: (0, j)),
            pl.BlockSpec((d, ns), lambda i, j: (0, 0)),
        ],
        out_specs=[pl.BlockSpec((tm, tn), lambda i, j: (i, j)), pl.BlockSpec((tm, ns), lambda i, j: (i, 0))],
        out_shape=[jax.ShapeDtypeStruct((m, n), BF16), jax.ShapeDtypeStruct((m, ns), F32)],
        scratch_shapes=[pltpu.VMEM((tm, d), BF16)],
        compiler_params=_cparams("parallel", "arbitrary"),
        name="in_proj",
    )(x2d, g.reshape(1, d), w, w_small)


def _mlstm_kernel(q_ref, k_ref, v_ref, o_ref, qt_ref, kt_ref, gate_ref, gbias_ref,
                  cw_ref, ng_ref, out_ref, c_scr, n_scr, m_scr, *, chunk):
    L, d = chunk, ML_HEAD_DIM
    c = pl.program_id(1)

    @pl.when(c == 0)
    def _():
        c_scr[...] = jnp.zeros_like(c_scr)
        n_scr[...] = jnp.zeros_like(n_scr)
        m_scr[...] = jnp.zeros_like(m_scr)

    t_idx = lax.broadcasted_iota(jnp.int32, (L, L), 0)
    s_idx = lax.broadcasted_iota(jnp.int32, (L, L), 1)
    tri = s_idx <= t_idx

    row8 = lax.broadcasted_iota(jnp.int32, (8, d), 0)

    def conv_silu(x, tail, w):
        x = x.astype(F32)
        tail = jnp.where(c > 0, tail.astype(F32), 0.0)
        y = x * w[ML_CONV - 1:ML_CONV, :]
        for j in range(ML_CONV - 1):
            s = ML_CONV - 1 - j
            r = pltpu.roll(x, s, 0)
            head = jnp.where(row8 < s, pltpu.roll(tail, s, 0), r[:8])
            y = y + jnp.concatenate([head, r[8:]], axis=0) * w[j:j + 1, :]
        return y * (0.5 * jnp.tanh(0.5 * y) + 0.5)

    g_cols = gate_ref[0] + gbias_ref[...]
    g_rows = g_cols.T

    outs = []
    for h in range(ML_HEADS):
        sl = slice(h * d, (h + 1) * d)
        cw = cw_ref[h]
        q = conv_silu(q_ref[0, :, sl], qt_ref[0, :, sl], cw[:, :d])
        k = conv_silu(k_ref[0, :, sl], kt_ref[0, :, sl], cw[:, d:]) * (d ** -0.5)
        v = v_ref[0, :, sl]
        qb = q.astype(BF16)
        kb = k.astype(BF16)

        i_col = g_cols[:, h:h + 1]
        f_col = jax.nn.log_sigmoid(g_cols[:, ML_HEADS + h:ML_HEADS + h + 1])
        i_row = g_rows[h:h + 1, :]
        f_row = jax.nn.log_sigmoid(g_rows[ML_HEADS + h:ML_HEADS + h + 1, :])

        b_col = jnp.sum(jnp.where(tri, f_row, 0.0), axis=1, keepdims=True)
        b_row = jnp.sum(jnp.where(t_idx <= s_idx, f_col, 0.0), axis=0, keepdims=True)
        dlog = jnp.where(tri, b_col + (i_row - b_row), NEG_INF)
        m_prev = m_scr[h]
        inter = b_col + m_prev
        mt = jnp.maximum(jnp.max(dlog, axis=1, keepdims=True), inter)
        w_intra = jnp.exp(dlog - mt)
        w_inter = jnp.exp(inter - mt)

        s = _dot_nt(qb, kb) * w_intra
        cmat = c_scr[h]
        nvec = n_scr[h]
        num = _dot(s.astype(BF16), v) + w_inter * _dot(qb, cmat.astype(BF16))
        den = jnp.sum(s, axis=1, keepdims=True) + w_inter * jnp.sum(q * nvec, axis=1, keepdims=True)
        hc = num / jnp.maximum(jnp.abs(den), jnp.exp(-mt))

        bl = jnp.sum(f_row, axis=1, keepdims=True)
        logw = bl - b_col + i_col
        m_new = jnp.maximum(bl + m_prev, jnp.max(logw, axis=0, keepdims=True))
        decay = jnp.exp(bl + m_prev - m_new)
        kw = k * jnp.exp(logw - m_new)
        c_scr[h] = decay * cmat + _dot_tn(kw.astype(BF16), v)
        n_scr[h] = decay * nvec + jnp.sum(kw, axis=0, keepdims=True)
        m_scr[h] = m_new

        outs.append(jax.nn.sigmoid(o_ref[0, :, sl].astype(F32)) * _rms(hc, ng_ref[h]))
    out_ref[0] = jnp.concatenate(outs, axis=1).astype(out_ref.dtype)


def _mlstm(act, gates, gate_bias, conv_w, norm_g, *, col0, chunk):
    B, S, _ = act.shape
    H, d, L = ML_HEADS, ML_HEAD_DIM, chunk
    tail_blocks = L // 8

    def blk(off):
        return pl.BlockSpec((1, L, H * d), lambda b, c: (b, c, col0 + off))

    def tail(off):
        return pl.BlockSpec((1, 8, H * d), lambda b, c: (b, jnp.maximum(c * tail_blocks - 1, 0), col0 + off))

    def const(a):
        return pl.BlockSpec(a.shape, lambda b, c: (0,) * a.ndim)

    return pl.pallas_call(
        functools.partial(_mlstm_kernel, chunk=L),
        grid=(B, S // L),
        in_specs=[
            blk(0), blk(1), blk(2), blk(3), tail(0), tail(1),
            pl.BlockSpec((1, L, gates.shape[2]), lambda b, c: (b, c, 0)),
            const(gate_bias), const(conv_w), const(norm_g),
        ],
        out_specs=pl.BlockSpec((1, L, H * d), lambda b, c: (b, c, 0)),
        out_shape=jax.ShapeDtypeStruct((B, S, H * d), BF16),
        scratch_shapes=[pltpu.VMEM((H, d, d), F32), pltpu.VMEM((H, 1, d), F32), pltpu.VMEM((H, 1, 1), F32)],
        compiler_params=_cparams("parallel", "arbitrary"),
        name="mlstm",
    )(act, act, act, act, act, act, gates, gate_bias, conv_w, norm_g)


def _compress_kernel(x_ref, pos_ref, w1_ref, w2_ref, o_ref):
    n = x_ref.shape[1] // CMP_STRIDE
    groups = o_ref.shape[2]
    for g in range(groups):
        top = jnp.zeros((n, w1_ref.shape[-1]), F32)
        bot = jnp.zeros((n, w1_ref.shape[-1]), F32)
        for r in range(CMP_STRIDE):
            x = x_ref[0, pl.ds(r, n, stride=CMP_STRIDE), :]
            top = top + _dot((x + pos_ref[0, r]).astype(BF16), w1_ref[0, r, g])
            bot = bot + _dot((x + pos_ref[0, CMP_STRIDE + r]).astype(BF16), w1_ref[0, CMP_STRIDE + r, g])
        pre = top + pltpu.roll(bot, n - 1, 0)
        hid = jax.nn.gelu(pre, approximate=True)
        out = _dot(hid.astype(BF16), w2_ref[0])
        row = lax.broadcasted_iota(jnp.int32, out.shape, 0)
        out = jnp.where(row < n - 1, out, 0.0)
        j = lax.broadcasted_iota(jnp.int32, out.shape, 0)
        lane = lax.broadcasted_iota(jnp.int32, out.shape, 1)
        cols = jnp.where(lane < 3, j // 16, jnp.where(lane < 6, j % 16, jnp.where(lane < 9, 1, 0))).astype(F32)
        cols = jnp.where(pl.program_id(0) == 0, cols, jnp.where(lane == 0, 1.0, 0.0))
        o_ref[0, 0, g] = jnp.concatenate([out, cols], axis=1).astype(o_ref.dtype)


def _compress(gates, pos, w1, w2, *, col0):
    B, S, _ = gates.shape
    G, width = w1.shape[2], w1.shape[3]
    hidden = w1.shape[-1]
    dh = w2.shape[-1]
    n = S // CMP_STRIDE
    return pl.pallas_call(
        _compress_kernel,
        grid=(2, B),
        in_specs=[
            pl.BlockSpec((1, S, width), lambda a, b: (b, 0, col0 + a)),
            pl.BlockSpec((1,) + pos.shape[1:], lambda a, b: (a, 0, 0, 0)),
            pl.BlockSpec((1,) + w1.shape[1:], lambda a, b: (a, 0, 0, 0, 0)),
            pl.BlockSpec((1, hidden, dh), lambda a, b: (a, 0, 0)),
        ],
        out_specs=pl.BlockSpec((1, 1, G, n, 2 * dh), lambda a, b: (a, b, 0, 0, 0)),
        out_shape=jax.ShapeDtypeStruct((2, B, G, n, 2 * dh), BF16),
        compiler_params=_cparams("parallel", "parallel"),
        name="nsa_compress",
    )(gates, pos, w1, w2)


def _nsa_kernel(q_ref, gate_ref, kc_ref, vc_ref, ksl_ref, vsl_ref, kwn_ref, vwn_ref, kconst_ref, ovl_ref, tri_ref, cpat_ref,
                out_ref, ks_scr, vs_scr, kw_scr, vw_scr, s_scr, mrun_scr, acc_scr, *, tq, seq, gate_col0):
    HG, dh = NSA_GROUP_HEADS, NSA_HEAD_DIM
    g = pl.program_id(1)
    i = pl.program_id(2)
    t0 = i * tq
    rows = HG * tq
    n_cmp_pad = kc_ref.shape[2]
    n_slc = seq // SEL_BLOCK
    half = tq // 2
    blocks_per_chunk = tq // SEL_BLOCK

    def assemble(lo):
        ks_scr[...] = jnp.concatenate([ksl_ref[0][:, lo:lo + dh], kconst_ref[...]], axis=1)
        kw_scr[...] = jnp.concatenate([kwn_ref[0][:, lo:lo + dh], kconst_ref[:, n_slc:n_slc + dh]], axis=1)
        ones_col = jnp.where(lax.broadcasted_iota(jnp.int32, (seq, dh), 1) == 0, 1.0, 0.0).astype(BF16)
        vs_scr[...] = jnp.concatenate([vsl_ref[0][:, lo:lo + dh], ones_col], axis=1)
        vw_scr[...] = jnp.concatenate([vwn_ref[0][:, lo:lo + dh], ones_col], axis=1)

    for gg in range(NSA_KV_GROUPS):
        @pl.when((i == 0) & (g == gg))
        def _(gg=gg):
            assemble(gg * dh)

    q_all = q_ref[0]
    q_heads = [(q_all[:, x * dh:(x + 1) * dh].astype(F32) * (dh ** -0.5 * LOG2E)).astype(BF16) for x in range(HG)]

    def head_slope(x):
        sl = jnp.float32(0.0)
        for hh in range(NSA_HEADS):
            sl = jnp.where(g * HG + x == hh, 2.0 ** (-8.0 * (hh + 1) / NSA_HEADS), sl)
        return sl

    slopes = [head_slope(x) for x in range(HG)]

    def alibi_cols(x, width, coef):
        lane = lax.broadcasted_iota(jnp.int32, (tq, width), 1)
        v = jnp.zeros((tq, width), F32)
        for c, val in enumerate(coef):
            v = jnp.where(lane // 3 == c, slopes[x] * (val * LOG2E), v)
        hi = v.astype(BF16).astype(F32)
        mid = (v - hi).astype(BF16).astype(F32)
        lo = v - hi - mid
        return jnp.where(lane % 3 == 0, hi, jnp.where(lane % 3 == 1, mid, lo)).astype(BF16)

    def stack_q(width, coef, extra=None):
        parts = []
        for x in range(HG):
            cols = [q_heads[x]] + ([extra] if extra is not None else []) + [alibi_cols(x, width, coef)]
            parts.append(jnp.concatenate(cols, axis=1))
        return jnp.concatenate(parts, axis=0)

    q_cmp = stack_q(dh, (16.0 * CMP_STRIDE, 1.0 * CMP_STRIDE, (CMP_BLOCK - 1) / 2))
    sc = _dot_nt(q_cmp, kc_ref[0, 0])
    block_done = jnp.concatenate([cpat_ref[...]] * HG, axis=0) <= t0
    sc = jnp.where(block_done, sc, NEG_INF)
    e = jnp.exp2(sc - jnp.max(sc, axis=1, keepdims=True))
    t_row = t0 + lax.broadcasted_iota(jnp.int32, (rows, 1), 0) % tq
    any_valid = jnp.where(t_row >= CMP_BLOCK - 1, 1.0, 0.0)
    p_cmp = e * (any_valid / jnp.sum(e, axis=1, keepdims=True))
    o_cmp = _dot(p_cmp.astype(BF16), vc_ref[0, 0])[:, :dh]

    p_sum = p_cmp[0:tq]
    for x in range(1, HG):
        p_sum = p_sum + p_cmp[x * tq:(x + 1) * tq]
    p_hi = p_sum.astype(BF16)
    p_lo = (p_sum - p_hi.astype(F32)).astype(BF16)
    ovl = ovl_ref[...]
    p_slc = _dot_nt(ovl, p_hi) + _dot_nt(ovl, p_lo)
    blk = lax.broadcasted_iota(jnp.int32, (n_slc, tq), 0)
    cur = (t0 + lax.broadcasted_iota(jnp.int32, (n_slc, tq), 1)) // SEL_BLOCK
    forced = (blk == 0) | (blk == cur) | (blk == cur - 1)
    score = jnp.where(forced, SEL_FORCE, jnp.where(blk > cur, -SEL_FORCE, p_slc))
    n_tiles = n_slc // 8
    tiles = [score[8 * a:8 * a + 8] for a in range(n_tiles)]
    ranks = [jnp.zeros((8, tq), F32) for _ in range(n_tiles)]
    sub = lax.broadcasted_iota(jnp.int32, (8, tq), 0)
    for kk in range(n_slc):
        sk = score[kk:kk + 1, :]
        for a in range(n_tiles):
            if a < kk // 8:
                before = sk > tiles[a]
            elif a > kk // 8:
                before = sk >= tiles[a]
            else:
                before = jnp.where(sub > kk % 8, jnp.where(sk >= tiles[a], 1.0, 0.0), jnp.where(sk > tiles[a], 1.0, 0.0)) > 0.5
            ranks[a] = ranks[a] + jnp.where(before, 1.0, 0.0)
    rank = jnp.concatenate(ranks, axis=0)
    selected = rank < min(SEL_COUNT, n_slc)
    sel_bias = jnp.where(selected, 0.0, SEL_MASK).T.astype(BF16)
    first_blk = jnp.min(jnp.where(selected & (blk >= blocks_per_chunk), blk, n_slc))
    c_lo = jnp.minimum(jnp.maximum(first_blk // blocks_per_chunk, 1), i)
    q_aug = stack_q(2 * dh, (1.0 * SEL_BLOCK, 1.0), extra=sel_bias)

    causal_bias = jnp.concatenate([tri_ref[0]] * HG, axis=0)
    upper_bias = jnp.concatenate([tri_ref[1]] * HG, axis=0)

    qw_aug = stack_q(dh, (1.0 * SEL_BLOCK, 1.0))
    start_a = pl.multiple_of(jnp.maximum(t0 - 2 * tq, 0), tq)
    start_b = pl.multiple_of(jnp.maximum(t0 - tq, 0), tq)
    start_c = pl.multiple_of(t0, tq)
    s_a = _dot_nt(qw_aug, kw_scr[pl.ds(start_a, tq), :]) + jnp.where(i >= 2, upper_bias, NEG_INF)
    s_b = _dot_nt(qw_aug, kw_scr[pl.ds(start_b, tq), :]) + jnp.where(i >= 1, 0.0, NEG_INF)
    s_c = _dot_nt(qw_aug, kw_scr[pl.ds(start_c, tq), :]) + causal_bias
    m_w = jnp.max(jnp.maximum(jnp.maximum(s_a, s_b), s_c), axis=1, keepdims=True)
    e_a = jnp.exp2(s_a - m_w)
    e_b = jnp.exp2(s_b - m_w)
    e_c = jnp.exp2(s_c - m_w)
    o_win = (_dot(e_a.astype(BF16), vw_scr[pl.ds(start_a, tq), :]) + _dot(e_b.astype(BF16), vw_scr[pl.ds(start_b, tq), :])
             + _dot(e_c.astype(BF16), vw_scr[pl.ds(start_c, tq), :]))

    mrun_scr[...] = jnp.full(mrun_scr.shape, NEG_INF, F32)
    acc_scr[...] = jnp.zeros(acc_scr.shape, F32)

    def scores(c, diagonal):
        start = pl.multiple_of(c * tq, tq)
        s = _dot_nt(q_aug, ks_scr[pl.ds(start, tq), :])
        if diagonal:
            s = s + causal_bias
        s_scr[c] = s
        mrun_scr[...] = jnp.maximum(mrun_scr[...], jnp.maximum(s[:, :half], s[:, half:]))

    def weights(c):
        start = pl.multiple_of(c * tq, tq)
        s = s_scr[c]
        m_b = mrun_scr[...]
        p0 = jnp.exp2(s[:, :half] - m_b)
        p1 = jnp.exp2(s[:, half:] - m_b)
        p = jnp.concatenate([p0, p1], axis=1).astype(BF16)
        acc_scr[...] += _dot(p, vs_scr[pl.ds(start, tq), :])

    def chunk_pairs(lo, hi, fn):
        def pair(t, carry):
            fn(lo + 2 * t)
            fn(lo + 2 * t + 1)
            return carry

        n = hi - lo
        lax.fori_loop(0, n // 2, pair, 0)

        @pl.when(n % 2 == 1)
        def _():
            fn(hi - 1)

    scores(i, True)

    @pl.when(i > 0)
    def _():
        scores(0, False)

    chunk_pairs(c_lo, i, lambda c: scores(c, False))
    m_row = jnp.max(mrun_scr[...], axis=1, keepdims=True)
    mrun_scr[...] = jnp.broadcast_to(m_row, mrun_scr.shape)

    @pl.when(i > 0)
    def _():
        weights(0)

    chunk_pairs(c_lo, i, weights)
    weights(i)
    gates = jax.nn.sigmoid(gate_ref[0])

    def gate_col(r):
        cols = []
        for x in range(HG):
            per_group = [gate_col0 + (gg * HG + x) * NSA_N_BRANCH + r for gg in range(NSA_KV_GROUPS)]
            col = gates[:, per_group[0]:per_group[0] + 1]
            for gg in range(1, NSA_KV_GROUPS):
                col = jnp.where(g == gg, gates[:, per_group[gg]:per_group[gg] + 1], col)
            cols.append(col)
        return jnp.concatenate(cols, axis=0)

    acc = acc_scr[...]
    o = gate_col(0) * o_cmp + acc[:, :dh] * (gate_col(1) / acc[:, dh:dh + 1])

    o = o + o_win[:, :dh] * (gate_col(2) / o_win[:, dh:dh + 1])

    out_ref[0] = jnp.concatenate([o[x * tq:(x + 1) * tq] for x in range(HG)], axis=1).astype(out_ref.dtype)


def _nsa_attention(act, gates, kc, vc, kconst, overlap_t, *, q_col0, kv_col0, gate_col0, tq):
    B, S, _ = act.shape
    G, HG, dh = NSA_KV_GROUPS, NSA_GROUP_HEADS, NSA_HEAD_DIM
    n_slc = S // SEL_BLOCK
    rows = HG * tq
    assert WINDOW == 2 * tq, "the window branch reads exactly the two chunks before the diagonal one"
    r = np.arange(tq)[:, None]
    l = np.arange(tq)[None, :]
    tri_bias = jnp.asarray(np.stack([np.where(l <= r, 0.0, NEG_INF), np.where(l > r, 0.0, NEG_INF)]), F32)
    cmp_pattern = jnp.asarray(np.arange(S // CMP_STRIDE)[None, :] * CMP_STRIDE + (CMP_BLOCK - 1) - r, jnp.int32)

    def kv_spec(off):
        return pl.BlockSpec((1, S, G * dh), lambda b, g, i: (b, 0, kv_col0 + off))

    def full(a):
        return pl.BlockSpec((1, 1) + a.shape[2:], lambda b, g, i: (b, g, 0, 0))

    return pl.pallas_call(
        functools.partial(_nsa_kernel, tq=tq, seq=S, gate_col0=gate_col0),
        grid=(B, G, S // tq),
        in_specs=[
            pl.BlockSpec((1, tq, HG * dh), lambda b, g, i: (b, i, q_col0 + g)),
            pl.BlockSpec((1, tq, gates.shape[2]), lambda b, g, i: (b, i, 0)),
            full(kc), full(vc), kv_spec(0), kv_spec(1), kv_spec(2), kv_spec(3),
            pl.BlockSpec(kconst.shape, lambda b, g, i: (0, 0)),
            pl.BlockSpec(overlap_t.shape, lambda b, g, i: (0, 0)),
            pl.BlockSpec(tri_bias.shape, lambda b, g, i: (0, 0, 0)),
            pl.BlockSpec(cmp_pattern.shape, lambda b, g, i: (0, 0)),
        ],
        out_specs=pl.BlockSpec((1, tq, HG * dh), lambda b, g, i: (b, i, g)),
        out_shape=jax.ShapeDtypeStruct((B, S, NSA_WIDTH), BF16),
        scratch_shapes=[pltpu.VMEM((S, dh + n_slc + 2 * dh), BF16), pltpu.VMEM((S, 2 * dh), BF16),
                        pltpu.VMEM((S, 2 * dh), BF16), pltpu.VMEM((S, 2 * dh), BF16),
                        pltpu.VMEM((S // tq, rows, tq), F32), pltpu.VMEM((rows, tq // 2), F32),
                        pltpu.VMEM((rows, 2 * dh), F32)],
        compiler_params=_cparams("parallel", "parallel", "arbitrary"),
        name="nsa_attn",
    )(act, gates, kc, vc, act, act, act, act, kconst, overlap_t, tri_bias, cmp_pattern)


def _mix_out_kernel(x_ref, hm_ref, hn_ref, gm_ref, gn_ref, wm_ref, wn_ref, wo_ref, o_ref):
    y_ml = _dot(hm_ref[...], wm_ref[...])
    y_ns = _dot(hn_ref[...], wn_ref[...])
    mix = jax.nn.sigmoid(gm_ref[...].astype(F32)) * y_ml + jax.nn.sigmoid(gn_ref[...].astype(F32)) * y_ns
    o_ref[...] = x_ref[...] + _dot(mix.astype(BF16), wo_ref[...])


def _mix_out(x2d, h_ml, h_ns, act2d, wm, wn, wo, *, gate_col0, tm):
    n, d = x2d.shape

    def res(w):
        return pl.BlockSpec(w.shape, lambda i: (0, 0))

    return pl.pallas_call(
        _mix_out_kernel,
        grid=(n // tm,),
        in_specs=[
            pl.BlockSpec((tm, d), lambda i: (i, 0)),
            pl.BlockSpec((tm, h_ml.shape[1]), lambda i: (i, 0)),
            pl.BlockSpec((tm, h_ns.shape[1]), lambda i: (i, 0)),
            pl.BlockSpec((tm, d), lambda i: (i, gate_col0)),
            pl.BlockSpec((tm, d), lambda i: (i, gate_col0 + 1)),
            res(wm), res(wn), res(wo),
        ],
        out_specs=pl.BlockSpec((tm, d), lambda i: (i, 0)),
        out_shape=jax.ShapeDtypeStruct((n, d), F32),
        compiler_params=_cparams("parallel"),
        name="mix_out",
    )(x2d, h_ml, h_ns, act2d, act2d, wm, wn, wo)


def _xattn_kernel(x_ref, g_ref, wq_ref, k_ref, v_ref, wo_ref, o_ref):
    x = x_ref[...]
    d = x.shape[1]
    dh = d // XA_HEADS
    h = _rms(x, g_ref[...]).astype(BF16)
    q = (_dot(h, wq_ref[...]) * (dh ** -0.5)).astype(BF16)
    k = k_ref[0]
    v = v_ref[0]
    outs = []
    for hd in range(XA_HEADS):
        sl = slice(hd * dh, (hd + 1) * dh)
        s = _dot_nt(q[:, sl], k[:, sl])
        e = jnp.exp(s - jnp.max(s, axis=1, keepdims=True))
        p = e / jnp.sum(e, axis=1, keepdims=True)
        outs.append(_dot(p.astype(BF16), v[:, sl]))
    o = jnp.concatenate(outs, axis=1).astype(BF16)
    o_ref[...] = x + _dot(o, wo_ref[...])


def _xattn(x2d, g, wq, kv, wo, *, seq, tm):
    n, d = x2d.shape
    n_mem = kv.shape[1]
    tiles_per_batch = seq // tm
    return pl.pallas_call(
        _xattn_kernel,
        grid=(n // tm,),
        in_specs=[
            pl.BlockSpec((tm, d), lambda i: (i, 0)),
            pl.BlockSpec((1, d), lambda i: (0, 0)),
            pl.BlockSpec(wq.shape, lambda i: (0, 0)),
            pl.BlockSpec((1, n_mem, d), lambda i: (i // tiles_per_batch, 0, 0)),
            pl.BlockSpec((1, n_mem, d), lambda i: (i // tiles_per_batch, 0, 1)),
            pl.BlockSpec(wo.shape, lambda i: (0, 0)),
        ],
        out_specs=pl.BlockSpec((tm, d), lambda i: (i, 0)),
        out_shape=jax.ShapeDtypeStruct((n, d), F32),
        compiler_params=_cparams("parallel"),
        name="xattn",
    )(x2d, g.reshape(1, d), wq, kv, kv, wo)


def _route(logits, bg, be):
    G, EG = MOE_GROUPS, MOE_EXPERTS_PER_GROUP
    tm = logits.shape[1]
    lg = logits[0:G] + bg
    eg = jnp.exp(lg - jnp.max(lg, axis=0, keepdims=True))
    pg = eg / jnp.sum(eg, axis=0, keepdims=True)
    g_val = jnp.max(pg, axis=0, keepdims=True)
    g_row = lax.broadcasted_iota(jnp.int32, (G, tm), 0)
    g_idx = jnp.min(jnp.where(pg == g_val, g_row, G), axis=0, keepdims=True)
    el = logits[8:8 + G * EG] + be
    e_in = jnp.zeros((EG, tm), F32)
    for gg in range(G):
        e_in = jnp.where(g_idx == gg, el[gg * EG:(gg + 1) * EG], e_in)
    e_row = lax.broadcasted_iota(jnp.int32, (EG, tm), 0)
    v1 = jnp.max(e_in, axis=0, keepdims=True)
    i1 = jnp.min(jnp.where(e_in == v1, e_row, EG), axis=0, keepdims=True)
    rest = jnp.where(e_row == i1, -jnp.inf, e_in)
    v2 = jnp.max(rest, axis=0, keepdims=True)
    i2 = jnp.min(jnp.where(rest == v2, e_row, EG), axis=0, keepdims=True)
    e2 = jnp.exp(v2 - v1)
    c1 = g_val / (1.0 + e2)
    c2 = g_val * e2 / (1.0 + e2)
    ex = lax.broadcasted_iota(jnp.int32, (G * EG, tm), 0)
    base = g_idx * EG
    return jnp.where(ex == base + i1, c1, 0.0) + jnp.where(ex == base + i2, c2, 0.0), g_idx


def _moe_kernel(x_ref, g_ref, wr_ref, bg_ref, be_ref, before_ref, w1_ref, w3_ref, w2_ref, gf_ref, o_ref,
                h_scr, slot_scr, wt_scr, acc_scr, *, caps):
    G, EG = MOE_GROUPS, MOE_EXPERTS_PER_GROUP
    grp = pl.program_id(1)
    tm = x_ref.shape[0]

    @pl.when(grp == 0)
    def _():
        h = _rms(x_ref[...], g_ref[...])
        h_hi = h.astype(BF16)
        h_scr[...] = h_hi
        h_lo = (h - h_hi.astype(F32)).astype(BF16)
        w_hi = wr_ref[...].astype(BF16)
        w_lo = (wr_ref[...] - w_hi.astype(F32)).astype(BF16)
        logits = _dot_nt(w_hi, h_hi) + _dot_nt(w_hi, h_lo) + _dot_nt(w_lo, h_hi)
        wt, g_idx = _route(logits, bg_ref[...], be_ref[...])
        member = jnp.where(lax.broadcasted_iota(jnp.int32, (8, tm), 0) == g_idx, 1.0, 0.0)
        prefix = _dot(member.astype(BF16), before_ref[...])
        slots = jnp.where(member > 0.0, prefix, -1.0)
        zeros = jnp.zeros((8 - EG, tm), F32)
        for gg in range(G):
            slot_scr[gg] = jnp.broadcast_to(slots[gg:gg + 1], (8, tm))
            wt_scr[gg] = jnp.concatenate([wt[gg * EG:(gg + 1) * EG], zeros], axis=0)
        acc_scr[...] = jnp.zeros_like(acc_scr)

    slot = slot_scr[grp][0:1, :]
    n_rows = jnp.max(slot).astype(jnp.int32) + 1
    wts = wt_scr[grp]

    def one_pass(cap, first_row):
        row = lax.broadcasted_iota(jnp.int32, (cap, tm), 0).astype(F32)
        pick = jnp.where(row == slot - first_row.astype(F32), 1.0, 0.0)
        pick_b = pick.astype(BF16)
        hsub = _dot(pick_b, h_scr[...]).astype(BF16)
        y = jnp.zeros((cap, x_ref.shape[1]), F32)
        for e in range(EG):
            w_e = jnp.sum(pick * wts[e:e + 1, :], axis=1, keepdims=True)
            a = _dot(hsub, w1_ref[e])
            a = a * jax.nn.sigmoid(a) * _dot(hsub, w3_ref[e]) * w_e
            y = y + _dot(a.astype(BF16), w2_ref[e])
        acc_scr[...] += _dot_tn(pick_b, y.astype(BF16))

    *small_caps, cap_long = caps
    lower = 0
    for cap in small_caps:
        @pl.when((n_rows > lower) & (n_rows <= cap))
        def _(cap=cap):
            one_pass(cap, jnp.int32(0))
        lower = cap

    @pl.when(n_rows > lower)
    def _():
        def body(p, carry):
            one_pass(cap_long, p * cap_long)
            return carry

        lax.fori_loop(0, (n_rows + cap_long - 1) // cap_long, body, 0)

    @pl.when(grp == pl.num_programs(1) - 1)
    def _():
        o_ref[...] = _rms(x_ref[...] + acc_scr[...], gf_ref[...])


def _moe(x2d, g, wr, bg, be, w1, w3, w2, g_final, *, tm, caps):
    n, d = x2d.shape
    E, _, hid = w1.shape
    G, EG = MOE_GROUPS, MOE_EXPERTS_PER_GROUP
    before = jnp.asarray(np.arange(tm)[:, None] < np.arange(tm)[None, :], BF16)
    once = pl.Buffered(1)

    def const(shape):
        return pl.BlockSpec(shape, lambda i, e: (0,) * len(shape), pipeline_mode=once)

    return pl.pallas_call(
        functools.partial(_moe_kernel, caps=caps),
        grid=(n // tm, G),
        in_specs=[
            pl.BlockSpec((tm, d), lambda i, e: (i, 0), pipeline_mode=once),
            const((1, d)), const(wr.shape), const(bg.shape), const(be.shape), const((tm, tm)),
            pl.BlockSpec((EG, d, hid), lambda i, e: (e, 0, 0)),
            pl.BlockSpec((EG, d, hid), lambda i, e: (e, 0, 0)),
            pl.BlockSpec((EG, hid, d), lambda i, e: (e, 0, 0)),
            const((1, d)),
        ],
        out_specs=pl.BlockSpec((tm, d), lambda i, e: (i, 0)),
        out_shape=jax.ShapeDtypeStruct((n, d), F32),
        scratch_shapes=[pltpu.VMEM((tm, d), BF16), pltpu.VMEM((G, 8, tm), F32), pltpu.VMEM((G, 8, tm), F32),
                        pltpu.VMEM((tm, d), F32)],
        compiler_params=_cparams("parallel", "arbitrary"),
        name="moe",
    )(x2d, g.reshape(1, d), wr, bg, be, before, w1, w3, w2, g_final.reshape(1, d))


def _overlap_t(seq):
    n_cmp_pad = seq // CMP_STRIDE
    cs = np.arange(n_cmp_pad)[None, :] * CMP_STRIDE
    ss = np.arange(seq // SEL_BLOCK)[:, None] * SEL_BLOCK
    ov = np.clip(np.minimum(cs + CMP_BLOCK, ss + SEL_BLOCK) - np.maximum(cs, ss), 0, None) / CMP_BLOCK
    ov[:, n_cmp_pad - 1] = 0.0
    return jnp.asarray(ov, BF16)


def _layer(x, mem, norm_mix, w_in, conv_qk, b_igate, b_fgate, mlstm_norm, cmp_pos_k, cmp_pos_v, cmp_k_w1, cmp_k_w2,
           cmp_v_w1, cmp_v_w2, w_br_mlstm, w_br_nsa, w_mix_out, norm_xattn, norm_mem, xa_wq, xa_wkv, xa_wo, norm_ffn,
           router_group_w, router_group_b, router_expert_w, router_expert_b, moe_w1, moe_w3, moe_w2, norm_final):
    B, S, D = x.shape
    N = B * S
    H, d = ML_HEADS, ML_HEAD_DIM
    G, HG, dh = NSA_KV_GROUPS, NSA_GROUP_HEADS, NSA_HEAD_DIM
    x2d = x.reshape(N, D)

    o_mlqkvo = 0
    o_mlif = 4 * ML_WIDTH
    o_nsq = o_mlif + 2 * H
    o_kv = o_nsq + NSA_WIDTH
    o_nsg = o_kv + 6 * NSA_KV_WIDTH
    o_merge = o_nsg + NSA_N_BRANCH * NSA_HEADS
    o_kv_sel = o_kv + 2 * NSA_KV_WIDTH
    w_act = jnp.concatenate([w_in[:, o_merge:o_merge + 2 * D], w_in[:, o_mlqkvo:o_mlif], w_in[:, o_nsq:o_kv],
                             w_in[:, o_kv_sel:o_nsg]], axis=1).astype(BF16)
    n_small = 2 * H + NSA_N_BRANCH * NSA_HEADS
    w_small = jnp.concatenate([w_in[:, o_mlif:o_nsq], w_in[:, o_nsg:o_merge], jnp.zeros((D, 128 - n_small), F32),
                               w_in[:, o_kv:o_kv_sel]], axis=1).astype(BF16)
    act, small = _in_proj(x2d, norm_mix, w_act, w_small, tm=1024, tn=1280)
    act3 = act.reshape(B, S, act.shape[1])
    c_ml = 2 * D
    c_nsq = c_ml + 4 * ML_WIDTH
    c_kv = c_nsq + NSA_WIDTH

    small3 = small.reshape(B, S, small.shape[1])
    gate_bias = jnp.zeros((1, small.shape[1]), F32).at[0, 0:H].set(b_igate).at[0, H:2 * H].set(b_fgate)
    conv_w = jnp.concatenate([conv_qk[:, :ML_WIDTH].reshape(ML_CONV, H, d), conv_qk[:, ML_WIDTH:].reshape(ML_CONV, H, d)],
                             axis=-1).transpose(1, 0, 2)
    h_ml = _mlstm(act3, small3, gate_bias, conv_w, mlstm_norm.reshape(H, 1, d), col0=c_ml // ML_WIDTH, chunk=512)

    pos = jnp.tile(jnp.stack([cmp_pos_k, cmp_pos_v])[:, :, None, :], (1, 1, 1, G))
    w1c = jnp.stack([cmp_k_w1, cmp_v_w1]).reshape(2, CMP_BLOCK, dh, -1)
    w1c = jnp.einsum('arld,gh->arghld', w1c, jnp.eye(G, dtype=F32))
    w1c = w1c.reshape(2, CMP_BLOCK, G, G * dh, -1).astype(BF16)
    w2c = jnp.stack([cmp_k_w2, cmp_v_w2]).astype(BF16)
    kvc = _compress(small3, pos, w1c, w2c, col0=1)
    n_slc = S // SEL_BLOCK
    tok = np.arange(S)
    kconst = np.zeros((S, n_slc + 2 * dh), np.float32)
    kconst[tok, tok // SEL_BLOCK] = 1.0
    kconst[:, n_slc:n_slc + 3] = (tok // SEL_BLOCK)[:, None]
    kconst[:, n_slc + 3:n_slc + 6] = (tok % SEL_BLOCK)[:, None]
    h_ns = _nsa_attention(act3, small3, kvc[0], kvc[1], jnp.asarray(kconst, BF16), _overlap_t(S),
                          q_col0=c_nsq // (HG * dh), kv_col0=c_kv // NSA_KV_WIDTH,
                          gate_col0=2 * H, tq=256)

    x1 = _mix_out(x2d, h_ml.reshape(N, ML_WIDTH), h_ns.reshape(N, NSA_WIDTH), act, w_br_mlstm.astype(BF16),
                  w_br_nsa.astype(BF16), w_mix_out.astype(BF16), gate_col0=0, tm=512)

    n_mem = mem.shape[1]
    kv_mem = _norm_matmul(mem.reshape(B * n_mem, D), norm_mem, xa_wkv.astype(BF16), BF16, tm=B * n_mem, tn=512)
    x2 = _xattn(x1, norm_xattn, xa_wq.astype(BF16), kv_mem.reshape(B, n_mem, 2 * D), xa_wo.astype(BF16), seq=S, tm=1024)

    wr = jnp.zeros((ROUTER_ROWS, D), F32)
    wr = wr.at[0:MOE_GROUPS].set(router_group_w.T).at[8:8 + MOE_EXPERTS].set(router_expert_w.T)
    return _moe(x2, norm_ffn, wr, router_group_b.reshape(MOE_GROUPS, 1), router_expert_b.reshape(MOE_EXPERTS, 1),
                moe_w1.astype(BF16), moe_w3.astype(BF16), moe_w2.astype(BF16), norm_final, tm=1024, caps=(256, 304, 352)).reshape(B, S, D)


def kernel(x, mem, norm_mix, w_in, conv_qk, b_igate, b_fgate, mlstm_norm, cmp_pos_k, cmp_pos_v, cmp_k_w1, cmp_k_w2, cmp_v_w1, cmp_v_w2, w_br_mlstm, w_br_nsa, w_mix_out, norm_xattn, norm_mem, xa_wq, xa_wkv, xa_wo, norm_ffn, router_group_w, router_group_b, router_expert_w, router_expert_b, moe_w1, moe_w3, moe_w2, norm_final):
    depth = w_in.shape[0]
    assert depth == 1, "the fused final norm assumes a single layer"
    layer = 0
    return _layer(x, mem, norm_mix[layer], w_in[layer], conv_qk[layer], b_igate[layer], b_fgate[layer], mlstm_norm[layer],
                  cmp_pos_k[layer], cmp_pos_v[layer], cmp_k_w1[layer], cmp_k_w2[layer], cmp_v_w1[layer], cmp_v_w2[layer],
                  w_br_mlstm[layer], w_br_nsa[layer], w_mix_out[layer], norm_xattn[layer], norm_mem[layer], xa_wq[layer],
                  xa_wkv[layer], xa_wo[layer], norm_ffn[layer], router_group_w[layer], router_group_b[layer],
                  router_expert_w[layer], router_expert_b[layer], moe_w1[layer], moe_w3[layer], moe_w2[layer], norm_final)
```

```python
import functools
import math

import numpy as np
import jax
import jax.numpy as jnp
from jax import lax
from jax.experimental import pallas as pl
from jax.experimental.pallas import tpu as pltpu

F32 = jnp.float32
BF16 = jnp.bfloat16

ML_HEADS = 4
ML_HEAD_DIM = 128
ML_WIDTH = ML_HEADS * ML_HEAD_DIM
ML_CONV = 4
NSA_HEADS = 8
NSA_KV_GROUPS = 2
NSA_HEAD_DIM = 64
NSA_GROUP_HEADS = NSA_HEADS // NSA_KV_GROUPS
NSA_WIDTH = NSA_HEADS * NSA_HEAD_DIM
NSA_KV_WIDTH = NSA_KV_GROUPS * NSA_HEAD_DIM
NSA_N_BRANCH = 3
CMP_BLOCK = 32
CMP_STRIDE = 16
SEL_BLOCK = 64
SEL_COUNT = 16
SEL_FORCE = 1e4
WINDOW = 512
XA_HEADS = 4
MOE_GROUPS = 4
MOE_EXPERTS_PER_GROUP = 4
MOE_EXPERTS = MOE_GROUPS * MOE_EXPERTS_PER_GROUP
RMS_EPS = 1e-6
NEG_INF = -1e30
SEL_MASK = -float(2 ** 30)
LOG2E = 1.4426950408889634

LANES = 128
VMEM_LIMIT_BYTES = 56 * 1024 * 1024
ROUTER_ROWS = 32


def _cparams(*sem):
    return pltpu.CompilerParams(dimension_semantics=sem, vmem_limit_bytes=VMEM_LIMIT_BYTES)


def _rms(x, g):
    return x * lax.rsqrt(jnp.mean(x * x, axis=-1, keepdims=True) + RMS_EPS) * g


def _dot(a, b):
    return jnp.dot(a, b, preferred_element_type=F32)


def _dot_nt(a, b):
    return lax.dot_general(a, b, (((1,), (1,)), ((), ())), preferred_element_type=F32)


def _dot_tn(a, b):
    return lax.dot_general(a, b, (((0,), (0,)), ((), ())), preferred_element_type=F32)


def _norm_matmul_kernel(x_ref, g_ref, w_ref, o_ref, h_ref):
    @pl.when(pl.program_id(1) == 0)
    def _():
        h_ref[...] = _rms(x_ref[...], g_ref[...]).astype(BF16)

    o_ref[...] = _dot(h_ref[...], w_ref[...]).astype(o_ref.dtype)


def _norm_matmul(x2d, g, w, out_dtype, tm, tn):
    m, d = x2d.shape
    n = w.shape[1]
    return pl.pallas_call(
        _norm_matmul_kernel,
        grid=(m // tm, n // tn),
        in_specs=[
            pl.BlockSpec((tm, d), lambda i, j: (i, 0)),
            pl.BlockSpec((1, d), lambda i, j: (0, 0)),
            pl.BlockSpec((d, tn), lambda i, j: (0, j)),
        ],
        out_specs=pl.BlockSpec((tm, tn), lambda i, j: (i, j)),
        out_shape=jax.ShapeDtypeStruct((m, n), out_dtype),
        scratch_shapes=[pltpu.VMEM((tm, d), BF16)],
        compiler_params=_cparams("parallel", "arbitrary"),
        name="norm_matmul",
    )(x2d, g.reshape(1, d), w)


def _in_proj_kernel(x_ref, g_ref, w_ref, ws_ref, o_ref, os_ref, h_ref):
    @pl.when(pl.program_id(1) == 0)
    def _():
        h_ref[...] = _rms(x_ref[...], g_ref[...]).astype(BF16)
        os_ref[...] = _dot(h_ref[...], ws_ref[...])

    o_ref[...] = _dot(h_ref[...], w_ref[...]).astype(o_ref.dtype)


def _in_proj(x2d, g, w, w_small, tm, tn):
    m, d = x2d.shape
    n = w.shape[1]
    ns = w_small.shape[1]
    return pl.pallas_call(
        _in_proj_kernel,
        grid=(m // tm, n // tn),
        in_specs=[
            pl.BlockSpec((tm, d), lambda i, j: (i, 0)),
            pl.BlockSpec((1, d), lambda i, j: (0, 0)),
            pl.BlockSpec((d, tn), lambda i, j: (0, j)),
            pl.BlockSpec((d, ns), lambda i, j: (0, 0)),
        ],
        out_specs=[pl.BlockSpec((tm, tn), lambda i, j: (i, j)), pl.BlockSpec((tm, ns), lambda i, j: (i, 0))],
        out_shape=[jax.ShapeDtypeStruct((m, n), BF16), jax.ShapeDtypeStruct((m, ns), F32)],
        scratch_shapes=[pltpu.VMEM((tm, d), BF16)],
        compiler_params=_cparams("parallel", "arbitrary"),
        name="in_proj",
    )(x2d, g.reshape(1, d), w, w_small)


def _mlstm_kernel(q_ref, k_ref, v_ref, o_ref, qt_ref, kt_ref, gate_ref, gbias_ref,
                  cw_ref, ng_ref, out_ref, c_scr, n_scr, m_scr, *, chunk):
    L, d = chunk, ML_HEAD_DIM
    c = pl.program_id(1)

    @pl.when(c == 0)
    def _():
        c_scr[...] = jnp.zeros_like(c_scr)
        n_scr[...] = jnp.zeros_like(n_scr)
        m_scr[...] = jnp.zeros_like(m_scr)

    t_idx = lax.broadcasted_iota(jnp.int32, (L, L), 0)
    s_idx = lax.broadcasted_iota(jnp.int32, (L, L), 1)
    tri = s_idx <= t_idx

    row8 = lax.broadcasted_iota(jnp.int32, (8, d), 0)

    def conv_silu(x, tail, w):
        x = x.astype(F32)
        tail = jnp.where(c > 0, tail.astype(F32), 0.0)
        y = x * w[ML_CONV - 1:ML_CONV, :]
        for j in range(ML_CONV - 1):
            s = ML_CONV - 1 - j
            r = pltpu.roll(x, s, 0)
            head = jnp.where(row8 < s, pltpu.roll(tail, s, 0), r[:8])
            y = y + jnp.concatenate([head, r[8:]], axis=0) * w[j:j + 1, :]
        return y * (0.5 * jnp.tanh(0.5 * y) + 0.5)

    g_cols = gate_ref[0] + gbias_ref[...]
    g_rows = g_cols.T

    outs = []
    for h in range(ML_HEADS):
        sl = slice(h * d, (h + 1) * d)
        cw = cw_ref[h]
        q = conv_silu(q_ref[0, :, sl], qt_ref[0, :, sl], cw[:, :d])
        k = conv_silu(k_ref[0, :, sl], kt_ref[0, :, sl], cw[:, d:]) * (d ** -0.5)
        v = v_ref[0, :, sl]
        qb = q.astype(BF16)
        kb = k.astype(BF16)

        i_col = g_cols[:, h:h + 1]
        f_col = jax.nn.log_sigmoid(g_cols[:, ML_HEADS + h:ML_HEADS + h + 1])
        i_row = g_rows[h:h + 1, :]
        f_row = jax.nn.log_sigmoid(g_rows[ML_HEADS + h:ML_HEADS + h + 1, :])

        b_col = jnp.sum(jnp.where(tri, f_row, 0.0), axis=1, keepdims=True)
        b_row = jnp.sum(jnp.where(t_idx <= s_idx, f_col, 0.0), axis=0, keepdims=True)
        dlog = jnp.where(tri, b_col + (i_row - b_row), NEG_INF)
        m_prev = m_scr[h]
        inter = b_col + m_prev
        mt = jnp.maximum(jnp.max(dlog, axis=1, keepdims=True), inter)
        w_intra = jnp.exp(dlog - mt)
        w_inter = jnp.exp(inter - mt)

        s = _dot_nt(qb, kb) * w_intra
        cmat = c_scr[h]
        nvec = n_scr[h]
        num = _dot(s.astype(BF16), v) + w_inter * _dot(qb, cmat.astype(BF16))
        den = jnp.sum(s, axis=1, keepdims=True) + w_inter * jnp.sum(q * nvec, axis=1, keepdims=True)
        hc = num / jnp.maximum(jnp.abs(den), jnp.exp(-mt))

        bl = jnp.sum(f_row, axis=1, keepdims=True)
        logw = bl - b_col + i_col
        m_new = jnp.maximum(bl + m_prev, jnp.max(logw, axis=0, keepdims=True))
        decay = jnp.exp(bl + m_prev - m_new)
        kw = k * jnp.exp(logw - m_new)
        c_scr[h] = decay * cmat + _dot_tn(kw.astype(BF16), v)
        n_scr[h] = decay * nvec + jnp.sum(kw, axis=0, keepdims=True)
        m_scr[h] = m_new

        outs.append(jax.nn.sigmoid(o_ref[0, :, sl].astype(F32)) * _rms(hc, ng_ref[h]))
    out_ref[0] = jnp.concatenate(outs, axis=1).astype(out_ref.dtype)


def _mlstm(act, gates, gate_bias, conv_w, norm_g, *, col0, chunk):
    B, S, _ = act.shape
    H, d, L = ML_HEADS, ML_HEAD_DIM, chunk
    tail_blocks = L // 8

    def blk(off):
        return pl.BlockSpec((1, L, H * d), lambda b, c: (b, c, col0 + off))

    def tail(off):
        return pl.BlockSpec((1, 8, H * d), lambda b, c: (b, jnp.maximum(c * tail_blocks - 1, 0), col0 + off))

    def const(a):
        return pl.BlockSpec(a.shape, lambda b, c: (0,) * a.ndim)

    return pl.pallas_call(
        functools.partial(_mlstm_kernel, chunk=L),
        grid=(B, S // L),
        in_specs=[
            blk(0), blk(1), blk(2), blk(3), tail(0), tail(1),
            pl.BlockSpec((1, L, LANES), lambda b, c: (b, c, 0)),
            const(gate_bias), const(conv_w), const(norm_g),
        ],
        out_specs=pl.BlockSpec((1, L, H * d), lambda b, c: (b, c, 0)),
        out_shape=jax.ShapeDtypeStruct((B, S, H * d), BF16),
        scratch_shapes=[pltpu.VMEM((H, d, d), F32), pltpu.VMEM((H, 1, d), F32), pltpu.VMEM((H, 1, 1), F32)],
        compiler_params=_cparams("parallel", "arbitrary"),
        name="mlstm",
    )(act, act, act, act, act, act, gates, gate_bias, conv_w, norm_g)


def _compress_kernel(x_ref, pos_ref, w1_ref, w2_ref, o_ref):
    n = x_ref.shape[1] // CMP_STRIDE
    groups = o_ref.shape[2]
    for g in range(groups):
        top = jnp.zeros((n, w1_ref.shape[-1]), F32)
        bot = jnp.zeros((n, w1_ref.shape[-1]), F32)
        for r in range(CMP_STRIDE):
            x = x_ref[0, pl.ds(r, n, stride=CMP_STRIDE), :]
            top = top + _dot((x + pos_ref[0, r]).astype(BF16), w1_ref[0, r, g])
            bot = bot + _dot((x + pos_ref[0, CMP_STRIDE + r]).astype(BF16), w1_ref[0, CMP_STRIDE + r, g])
        pre = top + pltpu.roll(bot, n - 1, 0)
        hid = jax.nn.gelu(pre, approximate=True)
        out = _dot(hid.astype(BF16), w2_ref[0])
        row = lax.broadcasted_iota(jnp.int32, out.shape, 0)
        out = jnp.where(row < n - 1, out, 0.0)
        j = lax.broadcasted_iota(jnp.int32, out.shape, 0)
        lane = lax.broadcasted_iota(jnp.int32, out.shape, 1)
        cols = jnp.where(lane < 3, j // 16, jnp.where(lane < 6, j % 16, jnp.where(lane < 9, 1, 0))).astype(F32)
        cols = jnp.where(pl.program_id(0) == 0, cols, jnp.where(lane == 0, 1.0, 0.0))
        o_ref[0, 0, g] = jnp.concatenate([out, cols], axis=1).astype(o_ref.dtype)


def _compress(gates, pos, w1, w2, *, col0):
    B, S, _ = gates.shape
    G, width = w1.shape[2], w1.shape[3]
    hidden = w1.shape[-1]
    dh = w2.shape[-1]
    n = S // CMP_STRIDE
    return pl.pallas_call(
        _compress_kernel,
        grid=(2, B),
        in_specs=[
            pl.BlockSpec((1, S, width), lambda a, b: (b, 0, col0 + a)),
            pl.BlockSpec((1,) + pos.shape[1:], lambda a, b: (a, 0, 0, 0)),
            pl.BlockSpec((1,) + w1.shape[1:], lambda a, b: (a, 0, 0, 0, 0)),
            pl.BlockSpec((1, hidden, dh), lambda a, b: (a, 0, 0)),
        ],
        out_specs=pl.BlockSpec((1, 1, G, n, 2 * dh), lambda a, b: (a, b, 0, 0, 0)),
        out_shape=jax.ShapeDtypeStruct((2, B, G, n, 2 * dh), BF16),
        compiler_params=_cparams("parallel", "parallel"),
        name="nsa_compress",
    )(gates, pos, w1, w2)


def _nsa_kernel(q_ref, gate_ref, kc_ref, vc_ref, ksl_ref, vsl_ref, kwn_ref, vwn_ref, kconst_ref, ovl_ref, tri_ref, cpat_ref,
                out_ref, ks_scr, vs_scr, kw_scr, vw_scr, s_scr, mrun_scr, acc_scr, *, tq, seq, gate_col0):
    HG, dh = NSA_GROUP_HEADS, NSA_HEAD_DIM
    g = pl.program_id(1)
    i = pl.program_id(2)
    t0 = i * tq
    rows = HG * tq
    n_cmp_pad = kc_ref.shape[2]
    n_slc = seq // SEL_BLOCK
    half = tq // 2
    blocks_per_chunk = tq // SEL_BLOCK

    def assemble(lo):
        ks_scr[...] = jnp.concatenate([ksl_ref[0][:, lo:lo + dh], kconst_ref[...]], axis=1)
        kw_scr[...] = jnp.concatenate([kwn_ref[0][:, lo:lo + dh], kconst_ref[:, n_slc:n_slc + dh]], axis=1)
        ones_col = jnp.where(lax.broadcasted_iota(jnp.int32, (seq, dh), 1) == 0, 1.0, 0.0).astype(BF16)
        vs_scr[...] = jnp.concatenate([vsl_ref[0][:, lo:lo + dh], ones_col], axis=1)
        vw_scr[...] = jnp.concatenate([vwn_ref[0][:, lo:lo + dh], ones_col], axis=1)

    for gg in range(NSA_KV_GROUPS):
        @pl.when((i == 0) & (g == gg))
        def _(gg=gg):
            assemble(gg * dh)

    q_all = q_ref[0]
    q_heads = [(q_all[:, x * dh:(x + 1) * dh].astype(F32) * (dh ** -0.5 * LOG2E)).astype(BF16) for x in range(HG)]

    def head_slope(x):
        sl = jnp.float32(0.0)
        for hh in range(NSA_HEADS):
            sl = jnp.where(g * HG + x == hh, 2.0 ** (-8.0 * (hh + 1) / NSA_HEADS), sl)
        return sl

    slopes = [head_slope(x) for x in range(HG)]

    def alibi_cols(x, width, coef):
        lane = lax.broadcasted_iota(jnp.int32, (tq, width), 1)
        v = jnp.zeros((tq, width), F32)
        for c, val in enumerate(coef):
            v = jnp.where(lane // 3 == c, slopes[x] * (val * LOG2E), v)
        hi = v.astype(BF16).astype(F32)
        mid = (v - hi).astype(BF16).astype(F32)
        lo = v - hi - mid
        return jnp.where(lane % 3 == 0, hi, jnp.where(lane % 3 == 1, mid, lo)).astype(BF16)

    def stack_q(width, coef, extra=None):
        parts = []
        for x in range(HG):
            cols = [q_heads[x]] + ([extra] if extra is not None else []) + [alibi_cols(x, width, coef)]
            parts.append(jnp.concatenate(cols, axis=1))
        return jnp.concatenate(parts, axis=0)

    q_cmp = stack_q(dh, (16.0 * CMP_STRIDE, 1.0 * CMP_STRIDE, (CMP_BLOCK - 1) / 2))
    sc = _dot_nt(q_cmp, kc_ref[0, 0])
    block_done = jnp.concatenate([cpat_ref[...]] * HG, axis=0) <= t0
    sc = jnp.where(block_done, sc, NEG_INF)
    e = jnp.exp2(sc - jnp.max(sc, axis=1, keepdims=True))
    t_row = t0 + lax.broadcasted_iota(jnp.int32, (rows, 1), 0) % tq
    any_valid = jnp.where(t_row >= CMP_BLOCK - 1, 1.0, 0.0)
    p_cmp = e * (any_valid / jnp.sum(e, axis=1, keepdims=True))
    o_cmp = _dot(p_cmp.astype(BF16), vc_ref[0, 0])[:, :dh]

    p_sum = p_cmp[0:tq]
    for x in range(1, HG):
        p_sum = p_sum + p_cmp[x * tq:(x + 1) * tq]
    p_hi = p_sum.astype(BF16)
    p_lo = (p_sum - p_hi.astype(F32)).astype(BF16)
    ovl = ovl_ref[...]
    p_slc = _dot_nt(ovl, p_hi) + _dot_nt(ovl, p_lo)
    blk = lax.broadcasted_iota(jnp.int32, (n_slc, tq), 0)
    cur = (t0 + lax.broadcasted_iota(jnp.int32, (n_slc, tq), 1)) // SEL_BLOCK
    forced = (blk == 0) | (blk == cur) | (blk == cur - 1)
    score = jnp.where(forced, SEL_FORCE, jnp.where(blk > cur, -SEL_FORCE, p_slc))
    n_tiles = n_slc // 8
    tiles = [score[8 * a:8 * a + 8] for a in range(n_tiles)]
    ranks = [jnp.zeros((8, tq), F32) for _ in range(n_tiles)]
    sub = lax.broadcasted_iota(jnp.int32, (8, tq), 0)
    for kk in range(n_slc):
        sk = score[kk:kk + 1, :]
        for a in range(n_tiles):
            if a < kk // 8:
                before = sk > tiles[a]
            elif a > kk // 8:
                before = sk >= tiles[a]
            else:
                before = jnp.where(sub > kk % 8, jnp.where(sk >= tiles[a], 1.0, 0.0), jnp.where(sk > tiles[a], 1.0, 0.0)) > 0.5
            ranks[a] = ranks[a] + jnp.where(before, 1.0, 0.0)
    rank = jnp.concatenate(ranks, axis=0)
    selected = rank < min(SEL_COUNT, n_slc)
    sel_bias = jnp.where(selected, 0.0, SEL_MASK).T.astype(BF16)
    first_blk = jnp.min(jnp.where(selected & (blk >= blocks_per_chunk), blk, n_slc))
    c_lo = jnp.minimum(jnp.maximum(first_blk // blocks_per_chunk, 1), i)
    q_aug = stack_q(2 * dh, (1.0 * SEL_BLOCK, 1.0), extra=sel_bias)

    causal_bias = jnp.concatenate([tri_ref[0]] * HG, axis=0)
    upper_bias = jnp.concatenate([tri_ref[1]] * HG, axis=0)

    qw_aug = stack_q(dh, (1.0 * SEL_BLOCK, 1.0))
    start_a = pl.multiple_of(jnp.maximum(t0 - 2 * tq, 0), tq)
    start_b = pl.multiple_of(jnp.maximum(t0 - tq, 0), tq)
    start_c = pl.multiple_of(t0, tq)
    s_a = _dot_nt(qw_aug, kw_scr[pl.ds(start_a, tq), :]) + jnp.where(i >= 2, upper_bias, NEG_INF)
    s_b = _dot_nt(qw_aug, kw_scr[pl.ds(start_b, tq), :]) + jnp.where(i >= 1, 0.0, NEG_INF)
    s_c = _dot_nt(qw_aug, kw_scr[pl.ds(start_c, tq), :]) + causal_bias
    m_w = jnp.max(jnp.maximum(jnp.maximum(s_a, s_b), s_c), axis=1, keepdims=True)
    e_a = jnp.exp2(s_a - m_w)
    e_b = jnp.exp2(s_b - m_w)
    e_c = jnp.exp2(s_c - m_w)
    o_win = (_dot(e_a.astype(BF16), vw_scr[pl.ds(start_a, tq), :]) + _dot(e_b.astype(BF16), vw_scr[pl.ds(start_b, tq), :])
             + _dot(e_c.astype(BF16), vw_scr[pl.ds(start_c, tq), :]))

    mrun_scr[...] = jnp.full(mrun_scr.shape, NEG_INF, F32)
    acc_scr[...] = jnp.zeros(acc_scr.shape, F32)

    def scores(c, diagonal):
        start = pl.multiple_of(c * tq, tq)
        s = _dot_nt(q_aug, ks_scr[pl.ds(start, tq), :])
        if diagonal:
            s = s + causal_bias
        s_scr[c] = s
        mrun_scr[...] = jnp.maximum(mrun_scr[...], jnp.maximum(s[:, :half], s[:, half:]))

    def weights(c):
        start = pl.multiple_of(c * tq, tq)
        s = s_scr[c]
        m_b = mrun_scr[...]
        p0 = jnp.exp2(s[:, :half] - m_b)
        p1 = jnp.exp2(s[:, half:] - m_b)
        p = jnp.concatenate([p0, p1], axis=1).astype(BF16)
        acc_scr[...] += _dot(p, vs_scr[pl.ds(start, tq), :])

    def chunk_pairs(lo, hi, fn):
        def pair(t, carry):
            fn(lo + 2 * t)
            fn(lo + 2 * t + 1)
            return carry

        n = hi - lo
        lax.fori_loop(0, n // 2, pair, 0)

        @pl.when(n % 2 == 1)
        def _():
            fn(hi - 1)

    scores(i, True)

    @pl.when(i > 0)
    def _():
        scores(0, False)

    chunk_pairs(c_lo, i, lambda c: scores(c, False))
    m_row = jnp.max(mrun_scr[...], axis=1, keepdims=True)
    mrun_scr[...] = jnp.broadcast_to(m_row, mrun_scr.shape)

    @pl.when(i > 0)
    def _():
        weights(0)

    chunk_pairs(c_lo, i, weights)
    weights(i)
    gates = jax.nn.sigmoid(gate_ref[0])

    def gate_col(r):
        cols = []
        for x in range(HG):
            per_group = [gate_col0 + (gg * HG + x) * NSA_N_BRANCH + r for gg in range(NSA_KV_GROUPS)]
            col = gates[:, per_group[0]:per_group[0] + 1]
            for gg in range(1, NSA_KV_GROUPS):
                col = jnp.where(g == gg, gates[:, per_group[gg]:per_group[gg] + 1], col)
            cols.append(col)
        return jnp.concatenate(cols, axis=0)

    acc = acc_scr[...]
    o = gate_col(0) * o_cmp + acc[:, :dh] * (gate_col(1) / acc[:, dh:dh + 1])

    o = o + o_win[:, :dh] * (gate_col(2) / o_win[:, dh:dh + 1])

    out_ref[0] = jnp.concatenate([o[x * tq:(x + 1) * tq] for x in range(HG)], axis=1).astype(out_ref.dtype)


def _nsa_attention(act, gates, kc, vc, kconst, overlap_t, *, q_col0, kv_col0, gate_col0, tq):
    B, S, _ = act.shape
    G, HG, dh = NSA_KV_GROUPS, NSA_GROUP_HEADS, NSA_HEAD_DIM
    n_slc = S // SEL_BLOCK
    rows = HG * tq
    assert WINDOW == 2 * tq, "the window branch reads exactly the two chunks before the diagonal one"
    r = np.arange(tq)[:, None]
    l = np.arange(tq)[None, :]
    tri_bias = jnp.asarray(np.stack([np.where(l <= r, 0.0, NEG_INF), np.where(l > r, 0.0, NEG_INF)]), F32)
    cmp_pattern = jnp.asarray(np.arange(S // CMP_STRIDE)[None, :] * CMP_STRIDE + (CMP_BLOCK - 1) - r, jnp.int32)

    def kv_spec(off):
        return pl.BlockSpec((1, S, G * dh), lambda b, g, i: (b, 0, kv_col0 + off))

    def full(a):
        return pl.BlockSpec((1, 1) + a.shape[2:], lambda b, g, i: (b, g, 0, 0))

    return pl.pallas_call(
        functools.partial(_nsa_kernel, tq=tq, seq=S, gate_col0=gate_col0),
        grid=(B, G, S // tq),
        in_specs=[
            pl.BlockSpec((1, tq, HG * dh), lambda b, g, i: (b, i, q_col0 + g)),
            pl.BlockSpec((1, tq, LANES), lambda b, g, i: (b, i, 0)),
            full(kc), full(vc), kv_spec(0), kv_spec(1), kv_spec(2), kv_spec(3),
            pl.BlockSpec(kconst.shape, lambda b, g, i: (0, 0)),
            pl.BlockSpec(overlap_t.shape, lambda b, g, i: (0, 0)),
            pl.BlockSpec(tri_bias.shape, lambda b, g, i: (0, 0, 0)),
            pl.BlockSpec(cmp_pattern.shape, lambda b, g, i: (0, 0)),
        ],
        out_specs=pl.BlockSpec((1, tq, HG * dh), lambda b, g, i: (b, i, g)),
        out_shape=jax.ShapeDtypeStruct((B, S, NSA_WIDTH), BF16),
        scratch_shapes=[pltpu.VMEM((S, dh + n_slc + 2 * dh), BF16), pltpu.VMEM((S, 2 * dh), BF16),
                        pltpu.VMEM((S, 2 * dh), BF16), pltpu.VMEM((S, 2 * dh), BF16),
                        pltpu.VMEM((S // tq, rows, tq), F32), pltpu.VMEM((rows, tq // 2), F32),
                        pltpu.VMEM((rows, 2 * dh), F32)],
        compiler_params=_cparams("parallel", "parallel", "arbitrary"),
        name="nsa_attn",
    )(act, gates, kc, vc, act, act, act, act, kconst, overlap_t, tri_bias, cmp_pattern)


def _mix_out_kernel(x_ref, hm_ref, hn_ref, gm_ref, gn_ref, wm_ref, wn_ref, wo_ref, o_ref):
    y_ml = _dot(hm_ref[...], wm_ref[...])
    y_ns = _dot(hn_ref[...], wn_ref[...])
    mix = jax.nn.sigmoid(gm_ref[...].astype(F32)) * y_ml + jax.nn.sigmoid(gn_ref[...].astype(F32)) * y_ns
    o_ref[...] = x_ref[...] + _dot(mix.astype(BF16), wo_ref[...])


def _mix_out(x2d, h_ml, h_ns, act2d, wm, wn, wo, *, gate_col0, tm):
    n, d = x2d.shape

    def res(w):
        return pl.BlockSpec(w.shape, lambda i: (0, 0))

    return pl.pallas_call(
        _mix_out_kernel,
        grid=(n // tm,),
        in_specs=[
            pl.BlockSpec((tm, d), lambda i: (i, 0)),
            pl.BlockSpec((tm, h_ml.shape[1]), lambda i: (i, 0)),
            pl.BlockSpec((tm, h_ns.shape[1]), lambda i: (i, 0)),
            pl.BlockSpec((tm, d), lambda i: (i, gate_col0)),
            pl.BlockSpec((tm, d), lambda i: (i, gate_col0 + 1)),
            res(wm), res(wn), res(wo),
        ],
        out_specs=pl.BlockSpec((tm, d), lambda i: (i, 0)),
        out_shape=jax.ShapeDtypeStruct((n, d), F32),
        compiler_params=_cparams("parallel"),
        name="mix_out",
    )(x2d, h_ml, h_ns, act2d, act2d, wm, wn, wo)


def _xattn_kernel(x_ref, g_ref, wq_ref, k_ref, v_ref, wo_ref, o_ref):
    x = x_ref[...]
    d = x.shape[1]
    dh = d // XA_HEADS
    h = _rms(x, g_ref[...]).astype(BF16)
    q = (_dot(h, wq_ref[...]) * (dh ** -0.5)).astype(BF16)
    k = k_ref[0]
    v = v_ref[0]
    outs = []
    for hd in range(XA_HEADS):
        sl = slice(hd * dh, (hd + 1) * dh)
        s = _dot_nt(q[:, sl], k[:, sl])
        e = jnp.exp(s - jnp.max(s, axis=1, keepdims=True))
        p = e / jnp.sum(e, axis=1, keepdims=True)
        outs.append(_dot(p.astype(BF16), v[:, sl]))
    o = jnp.concatenate(outs, axis=1).astype(BF16)
    o_ref[...] = x + _dot(o, wo_ref[...])


def _xattn(x2d, g, wq, kv, wo, *, seq, tm):
    n, d = x2d.shape
    n_mem = kv.shape[1]
    tiles_per_batch = seq // tm
    return pl.pallas_call(
        _xattn_kernel,
        grid=(n // tm,),
        in_specs=[
            pl.BlockSpec((tm, d), lambda i: (i, 0)),
            pl.BlockSpec((1, d), lambda i: (0, 0)),
            pl.BlockSpec(wq.shape, lambda i: (0, 0)),
            pl.BlockSpec((1, n_mem, d), lambda i: (i // tiles_per_batch, 0, 0)),
            pl.BlockSpec((1, n_mem, d), lambda i: (i // tiles_per_batch, 0, 1)),
            pl.BlockSpec(wo.shape, lambda i: (0, 0)),
        ],
        out_specs=pl.BlockSpec((tm, d), lambda i: (i, 0)),
        out_shape=jax.ShapeDtypeStruct((n, d), F32),
        compiler_params=_cparams("parallel"),
        name="xattn",
    )(x2d, g.reshape(1, d), wq, kv, kv, wo)


def _route(logits, bg, be):
    G, EG = MOE_GROUPS, MOE_EXPERTS_PER_GROUP
    tm = logits.shape[1]
    lg = logits[0:G] + bg
    eg = jnp.exp(lg - jnp.max(lg, axis=0, keepdims=True))
    pg = eg / jnp.sum(eg, axis=0, keepdims=True)
    g_val = jnp.max(pg, axis=0, keepdims=True)
    g_row = lax.broadcasted_iota(jnp.int32, (G, tm), 0)
    g_idx = jnp.min(jnp.where(pg == g_val, g_row, G), axis=0, keepdims=True)
    el = logits[8:8 + G * EG] + be
    e_in = jnp.zeros((EG, tm), F32)
    for gg in range(G):
        e_in = jnp.where(g_idx == gg, el[gg * EG:(gg + 1) * EG], e_in)
    e_row = lax.broadcasted_iota(jnp.int32, (EG, tm), 0)
    v1 = jnp.max(e_in, axis=0, keepdims=True)
    i1 = jnp.min(jnp.where(e_in == v1, e_row, EG), axis=0, keepdims=True)
    rest = jnp.where(e_row == i1, -jnp.inf, e_in)
    v2 = jnp.max(rest, axis=0, keepdims=True)
    i2 = jnp.min(jnp.where(rest == v2, e_row, EG), axis=0, keepdims=True)
    e2 = jnp.exp(v2 - v1)
    c1 = g_val / (1.0 + e2)
    c2 = g_val * e2 / (1.0 + e2)
    ex = lax.broadcasted_iota(jnp.int32, (G * EG, tm), 0)
    base = g_idx * EG
    return jnp.where(ex == base + i1, c1, 0.0) + jnp.where(ex == base + i2, c2, 0.0), g_idx


def _moe_kernel(x_ref, g_ref, wr_ref, bg_ref, be_ref, before_ref, w1_ref, w3_ref, w2_ref, gf_ref, o_ref,
                h_scr, slot_scr, wt_scr, acc_scr, *, caps):
    G, EG = MOE_GROUPS, MOE_EXPERTS_PER_GROUP
    grp = pl.program_id(1)
    tm = x_ref.shape[0]

    @pl.when(grp == 0)
    def _():
        h = _rms(x_ref[...], g_ref[...])
        h_hi = h.astype(BF16)
        h_scr[...] = h_hi
        h_lo = (h - h_hi.astype(F32)).astype(BF16)
        w_hi = wr_ref[...].astype(BF16)
        w_lo = (wr_ref[...] - w_hi.astype(F32)).astype(BF16)
        logits = _dot_nt(w_hi, h_hi) + _dot_nt(w_hi, h_lo) + _dot_nt(w_lo, h_hi)
        wt, g_idx = _route(logits, bg_ref[...], be_ref[...])
        member = jnp.where(lax.broadcasted_iota(jnp.int32, (8, tm), 0) == g_idx, 1.0, 0.0)
        prefix = _dot(member.astype(BF16), before_ref[...])
        slots = jnp.where(member > 0.0, prefix, -1.0)
        zeros = jnp.zeros((8 - EG, tm), F32)
        for gg in range(G):
            slot_scr[gg] = jnp.broadcast_to(slots[gg:gg + 1], (8, tm))
            wt_scr[gg] = jnp.concatenate([wt[gg * EG:(gg + 1) * EG], zeros], axis=0)
        acc_scr[...] = jnp.zeros_like(acc_scr)

    slot = slot_scr[grp][0:1, :]
    n_rows = jnp.max(slot).astype(jnp.int32) + 1
    wts = wt_scr[grp]

    def one_pass(cap, first_row):
        row = lax.broadcasted_iota(jnp.int32, (cap, tm), 0).astype(F32)
        pick = jnp.where(row == slot - first_row.astype(F32), 1.0, 0.0)
        pick_b = pick.astype(BF16)
        hsub = _dot(pick_b, h_scr[...]).astype(BF16)
        y = jnp.zeros((cap, x_ref.shape[1]), F32)
        for e in range(EG):
            w_e = jnp.sum(pick * wts[e:e + 1, :], axis=1, keepdims=True)
            a = _dot(hsub, w1_ref[e])
            a = a * jax.nn.sigmoid(a) * _dot(hsub, w3_ref[e]) * w_e
            y = y + _dot(a.astype(BF16), w2_ref[e])
        acc_scr[...] += _dot_tn(pick_b, y.astype(BF16))

    *small_caps, cap_long = caps
    lower = 0
    for cap in small_caps:
        @pl.when((n_rows > lower) & (n_rows <= cap))
        def _(cap=cap):
            one_pass(cap, jnp.int32(0))
        lower = cap

    @pl.when(n_rows > lower)
    def _():
        def body(p, carry):
            one_pass(cap_long, p * cap_long)
            return carry

        lax.fori_loop(0, (n_rows + cap_long - 1) // cap_long, body, 0)

    @pl.when(grp == pl.num_programs(1) - 1)
    def _():
        o_ref[...] = _rms(x_ref[...] + acc_scr[...], gf_ref[...])


def _moe(x2d, g, wr, bg, be, w1, w3, w2, g_final, *, tm, caps):
    n, d = x2d.shape
    E, _, hid = w1.shape
    G, EG = MOE_GROUPS, MOE_EXPERTS_PER_GROUP
    before = jnp.asarray(np.arange(tm)[:, None] < np.arange(tm)[None, :], BF16)
    once = pl.Buffered(1)

    def const(shape):
        return pl.BlockSpec(shape, lambda i, e: (0,) * len(shape), pipeline_mode=once)

    return pl.pallas_call(
        functools.partial(_moe_kernel, caps=caps),
        grid=(n // tm, G),
        in_specs=[
            pl.BlockSpec((tm, d), lambda i, e: (i, 0), pipeline_mode=once),
            const((1, d)), const(wr.shape), const(bg.shape), const(be.shape), const((tm, tm)),
            pl.BlockSpec((EG, d, hid), lambda i, e: (e, 0, 0)),
            pl.BlockSpec((EG, d, hid), lambda i, e: (e, 0, 0)),
            pl.BlockSpec((EG, hid, d), lambda i, e: (e, 0, 0)),
            const((1, d)),
        ],
        out_specs=pl.BlockSpec((tm, d), lambda i, e: (i, 0)),
        out_shape=jax.ShapeDtypeStruct((n, d), F32),
        scratch_shapes=[pltpu.VMEM((tm, d), BF16), pltpu.VMEM((G, 8, tm), F32), pltpu.VMEM((G, 8, tm), F32),
                        pltpu.VMEM((tm, d), F32)],
        compiler_params=_cparams("parallel", "arbitrary"),
        name="moe",
    )(x2d, g.reshape(1, d), wr, bg, be, before, w1, w3, w2, g_final.reshape(1, d))


def _overlap_t(seq):
    n_cmp_pad = seq // CMP_STRIDE
    cs = np.arange(n_cmp_pad)[None, :] * CMP_STRIDE
    ss = np.arange(seq // SEL_BLOCK)[:, None] * SEL_BLOCK
    ov = np.clip(np.minimum(cs + CMP_BLOCK, ss + SEL_BLOCK) - np.maximum(cs, ss), 0, None) / CMP_BLOCK
    ov[:, n_cmp_pad - 1] = 0.0
    return jnp.asarray(ov, BF16)


def _layer(x, mem, norm_mix, w_in, conv_qk, b_igate, b_fgate, mlstm_norm, cmp_pos_k, cmp_pos_v, cmp_k_w1, cmp_k_w2,
           cmp_v_w1, cmp_v_w2, w_br_mlstm, w_br_nsa, w_mix_out, norm_xattn, norm_mem, xa_wq, xa_wkv, xa_wo, norm_ffn,
           router_group_w, router_group_b, router_expert_w, router_expert_b, moe_w1, moe_w3, moe_w2, norm_final):
    B, S, D = x.shape
    N = B * S
    H, d = ML_HEADS, ML_HEAD_DIM
    G, HG, dh = NSA_KV_GROUPS, NSA_GROUP_HEADS, NSA_HEAD_DIM
    x2d = x.reshape(N, D)

    o_mlqkvo = 0
    o_mlif = 4 * ML_WIDTH
    o_nsq = o_mlif + 2 * H
    o_kv = o_nsq + NSA_WIDTH
    o_nsg = o_kv + 6 * NSA_KV_WIDTH
    o_merge = o_nsg + NSA_N_BRANCH * NSA_HEADS
    o_kv_sel = o_kv + 2 * NSA_KV_WIDTH
    w_act = jnp.concatenate([w_in[:, o_merge:o_merge + 2 * D], w_in[:, o_mlqkvo:o_mlif], w_in[:, o_nsq:o_kv],
                             w_in[:, o_kv_sel:o_nsg]], axis=1).astype(BF16)
    n_small = 2 * H + NSA_N_BRANCH * NSA_HEADS
    w_small = jnp.concatenate([w_in[:, o_mlif:o_nsq], w_in[:, o_nsg:o_merge], jnp.zeros((D, LANES - n_small), F32),
                               w_in[:, o_kv:o_kv_sel]], axis=1).astype(BF16)
    act, small = _in_proj(x2d, norm_mix, w_act, w_small, tm=1024, tn=2560)
    act3 = act.reshape(B, S, act.shape[1])
    c_ml = 2 * D
    c_nsq = c_ml + 4 * ML_WIDTH
    c_kv = c_nsq + NSA_WIDTH

    small3 = small.reshape(B, S, small.shape[1])
    gate_bias = jnp.zeros((1, LANES), F32).at[0, 0:H].set(b_igate).at[0, H:2 * H].set(b_fgate)
    conv_w = jnp.concatenate([conv_qk[:, :ML_WIDTH].reshape(ML_CONV, H, d), conv_qk[:, ML_WIDTH:].reshape(ML_CONV, H, d)],
                             axis=-1).transpose(1, 0, 2)
    h_ml = _mlstm(act3, small3, gate_bias, conv_w, mlstm_norm.reshape(H, 1, d), col0=c_ml // ML_WIDTH, chunk=512)

    pos = jnp.tile(jnp.stack([cmp_pos_k, cmp_pos_v])[:, :, None, :], (1, 1, 1, G))
    w1c = jnp.stack([cmp_k_w1, cmp_v_w1]).reshape(2, CMP_BLOCK, dh, -1)
    w1c = jnp.einsum('arld,gh->arghld', w1c, jnp.eye(G, dtype=F32))
    w1c = w1c.reshape(2, CMP_BLOCK, G, G * dh, -1).astype(BF16)
    w2c = jnp.stack([cmp_k_w2, cmp_v_w2]).astype(BF16)
    kvc = _compress(small3, pos, w1c, w2c, col0=1)
    n_slc = S // SEL_BLOCK
    tok = np.arange(S)
    kconst = np.zeros((S, n_slc + 2 * dh), np.float32)
    kconst[tok, tok // SEL_BLOCK] = 1.0
    kconst[:, n_slc:n_slc + 3] = (tok // SEL_BLOCK)[:, None]
    kconst[:, n_slc + 3:n_slc + 6] = (tok % SEL_BLOCK)[:, None]
    h_ns = _nsa_attention(act3, small3, kvc[0], kvc[1], jnp.asarray(kconst, BF16), _overlap_t(S),
                          q_col0=c_nsq // (HG * dh), kv_col0=c_kv // NSA_KV_WIDTH,
                          gate_col0=2 * H, tq=256)

    x1 = _mix_out(x2d, h_ml.reshape(N, ML_WIDTH), h_ns.reshape(N, NSA_WIDTH), act, w_br_mlstm.astype(BF16),
                  w_br_nsa.astype(BF16), w_mix_out.astype(BF16), gate_col0=0, tm=512)

    n_mem = mem.shape[1]
    kv_mem = _norm_matmul(mem.reshape(B * n_mem, D), norm_mem, xa_wkv.astype(BF16), BF16, tm=B * n_mem, tn=512)
    x2 = _xattn(x1, norm_xattn, xa_wq.astype(BF16), kv_mem.reshape(B, n_mem, 2 * D), xa_wo.astype(BF16), seq=S, tm=1024)

    wr = jnp.zeros((ROUTER_ROWS, D), F32)
    wr = wr.at[0:MOE_GROUPS].set(router_group_w.T).at[8:8 + MOE_EXPERTS].set(router_expert_w.T)
    return _moe(x2, norm_ffn, wr, router_group_b.reshape(MOE_GROUPS, 1), router_expert_b.reshape(MOE_EXPERTS, 1),
                moe_w1.astype(BF16), moe_w3.astype(BF16), moe_w2.astype(BF16), norm_final, tm=1024, caps=(256, 304, 352)).reshape(B, S, D)


def kernel(x, mem, norm_mix, w_in, conv_qk, b_igate, b_fgate, mlstm_norm, cmp_pos_k, cmp_pos_v, cmp_k_w1, cmp_k_w2, cmp_v_w1, cmp_v_w2, w_br_mlstm, w_br_nsa, w_mix_out, norm_xattn, norm_mem, xa_wq, xa_wkv, xa_wo, norm_ffn, router_group_w, router_group_b, router_expert_w, router_expert_b, moe_w1, moe_w3, moe_w2, norm_final):
    depth = w_in.shape[0]
    assert depth == 1, "the fused final norm assumes a single layer"
    layer = 0
    return _layer(x, mem, norm_mix[layer], w_in[layer], conv_qk[layer], b_igate[layer], b_fgate[layer], mlstm_norm[layer],
                  cmp_pos_k[layer], cmp_pos_v[layer], cmp_k_w1[layer], cmp_k_w2[layer], cmp_v_w1[layer], cmp_v_w2[layer],
                  w_br_mlstm[layer], w_br_nsa[layer], w_mix_out[layer], norm_xattn[layer], norm_mem[layer], xa_wq[layer],
                  xa_wkv[layer], xa_wo[layer], norm_ffn[layer], router_group_w[layer], router_group_b[layer],
                  router_expert_w[layer], router_expert_b[layer], moe_w1[layer], moe_w3[layer], moe_w2[layer], norm_final)
```

```python
import functools
import math

import numpy as np
import jax
import jax.numpy as jnp
from jax import lax
from jax.experimental import pallas as pl
from jax.experimental.pallas import tpu as pltpu

F32 = jnp.float32
BF16 = jnp.bfloat16

ML_HEADS = 4
ML_HEAD_DIM = 128
ML_WIDTH = ML_HEADS * ML_HEAD_DIM
ML_CONV = 4
NSA_HEADS = 8
NSA_KV_GROUPS = 2
NSA_HEAD_DIM = 64
NSA_GROUP_HEADS = NSA_HEADS // NSA_KV_GROUPS
NSA_WIDTH = NSA_HEADS * NSA_HEAD_DIM
NSA_KV_WIDTH = NSA_KV_GROUPS * NSA_HEAD_DIM
NSA_N_BRANCH = 3
CMP_BLOCK = 32
CMP_STRIDE = 16
SEL_BLOCK = 64
SEL_COUNT = 16
SEL_FORCE = 1e4
WINDOW = 512
XA_HEADS = 4
MOE_GROUPS = 4
MOE_EXPERTS_PER_GROUP = 4
MOE_EXPERTS = MOE_GROUPS * MOE_EXPERTS_PER_GROUP
RMS_EPS = 1e-6
NEG_INF = -1e30
SEL_MASK = -float(2 ** 30)
LOG2E = 1.4426950408889634

LANES = 128
VMEM_LIMIT_BYTES = 56 * 1024 * 1024
ROUTER_ROWS = 32


def _cparams(*sem):
    return pltpu.CompilerParams(dimension_semantics=sem, vmem_limit_bytes=VMEM_LIMIT_BYTES)


def _rms(x, g):
    return x * lax.rsqrt(jnp.mean(x * x, axis=-1, keepdims=True) + RMS_EPS) * g


def _dot(a, b):
    return jnp.dot(a, b, preferred_element_type=F32)


def _dot_nt(a, b):
    return lax.dot_general(a, b, (((1,), (1,)), ((), ())), preferred_element_type=F32)


def _dot_tn(a, b):
    return lax.dot_general(a, b, (((0,), (0,)), ((), ())), preferred_element_type=F32)


def _norm_matmul_kernel(x_ref, g_ref, w_ref, o_ref, h_ref):
    @pl.when(pl.program_id(1) == 0)
    def _():
        h_ref[...] = _rms(x_ref[...], g_ref[...]).astype(BF16)

    o_ref[...] = _dot(h_ref[...], w_ref[...]).astype(o_ref.dtype)


def _norm_matmul(x2d, g, w, out_dtype, tm, tn):
    m, d = x2d.shape
    n = w.shape[1]
    return pl.pallas_call(
        _norm_matmul_kernel,
        grid=(m // tm, n // tn),
        in_specs=[
            pl.BlockSpec((tm, d), lambda i, j: (i, 0)),
            pl.BlockSpec((1, d), lambda i, j: (0, 0)),
            pl.BlockSpec((d, tn), lambda i, j: (0, j)),
        ],
        out_specs=pl.BlockSpec((tm, tn), lambda i, j: (i, j)),
        out_shape=jax.ShapeDtypeStruct((m, n), out_dtype),
        scratch_shapes=[pltpu.VMEM((tm, d), BF16)],
        compiler_params=_cparams("parallel", "arbitrary"),
        name="norm_matmul",
    )(x2d, g.reshape(1, d), w)


def _in_proj_kernel(x_ref, g_ref, w_ref, ws_ref, o_ref, og_ref, oc_ref, h_ref):
    @pl.when(pl.program_id(1) == 0)
    def _():
        h_ref[...] = _rms(x_ref[...], g_ref[...]).astype(BF16)
        small = _dot(h_ref[...], ws_ref[...])
        og_ref[...] = small[:, :LANES]
        oc_ref[...] = small[:, LANES:]

    o_ref[...] = _dot(h_ref[...], w_ref[...]).astype(o_ref.dtype)


def _in_proj(x2d, g, w, w_small, tm, tn):
    m, d = x2d.shape
    n = w.shape[1]
    ns = w_small.shape[1]
    return pl.pallas_call(
        _in_proj_kernel,
        grid=(m // tm, n // tn),
        in_specs=[
            pl.BlockSpec((tm, d), lambda i, j: (i, 0)),
            pl.BlockSpec((1, d), lambda i, j: (0, 0)),
            pl.BlockSpec((d, tn), lambda i, j: (0, j)),
            pl.BlockSpec((d, ns), lambda i, j: (0, 0)),
        ],
        out_specs=[pl.BlockSpec((tm, tn), lambda i, j: (i, j)), pl.BlockSpec((tm, LANES), lambda i, j: (i, 0)),
                   pl.BlockSpec((tm, ns - LANES), lambda i, j: (i, 0))],
        out_shape=[jax.ShapeDtypeStruct((m, n), BF16), jax.ShapeDtypeStruct((m, LANES), F32),
                   jax.ShapeDtypeStruct((m, ns - LANES), F32)],
        scratch_shapes=[pltpu.VMEM((tm, d), BF16)],
        compiler_params=_cparams("parallel", "arbitrary"),
        name="in_proj",
    )(x2d, g.reshape(1, d), w, w_small)


def _mlstm_kernel(q_ref, k_ref, v_ref, o_ref, qt_ref, kt_ref, gate_ref, gbias_ref,
                  cw_ref, ng_ref, out_ref, c_scr, n_scr, m_scr, *, chunk):
    L, d = chunk, ML_HEAD_DIM
    c = pl.program_id(1)

    @pl.when(c == 0)
    def _():
        c_scr[...] = jnp.zeros_like(c_scr)
        n_scr[...] = jnp.zeros_like(n_scr)
        m_scr[...] = jnp.zeros_like(m_scr)

    t_idx = lax.broadcasted_iota(jnp.int32, (L, L), 0)
    s_idx = lax.broadcasted_iota(jnp.int32, (L, L), 1)
    tri = s_idx <= t_idx

    row8 = lax.broadcasted_iota(jnp.int32, (8, d), 0)

    def conv_silu(x, tail, w):
        x = x.astype(F32)
        tail = jnp.where(c > 0, tail.astype(F32), 0.0)
        y = x * w[ML_CONV - 1:ML_CONV, :]
        for j in range(ML_CONV - 1):
            s = ML_CONV - 1 - j
            r = pltpu.roll(x, s, 0)
            head = jnp.where(row8 < s, pltpu.roll(tail, s, 0), r[:8])
            y = y + jnp.concatenate([head, r[8:]], axis=0) * w[j:j + 1, :]
        return y * (0.5 * jnp.tanh(0.5 * y) + 0.5)

    g_cols = gate_ref[0] + gbias_ref[...]
    g_rows = g_cols.T

    outs = []
    for h in range(ML_HEADS):
        sl = slice(h * d, (h + 1) * d)
        cw = cw_ref[h]
        q = conv_silu(q_ref[0, :, sl], qt_ref[0, :, sl], cw[:, :d])
        k = conv_silu(k_ref[0, :, sl], kt_ref[0, :, sl], cw[:, d:]) * (d ** -0.5)
        v = v_ref[0, :, sl]
        qb = q.astype(BF16)
        kb = k.astype(BF16)

        i_col = g_cols[:, h:h + 1]
        f_col = jax.nn.log_sigmoid(g_cols[:, ML_HEADS + h:ML_HEADS + h + 1])
        i_row = g_rows[h:h + 1, :]
        f_row = jax.nn.log_sigmoid(g_rows[ML_HEADS + h:ML_HEADS + h + 1, :])

        b_col = jnp.sum(jnp.where(tri, f_row, 0.0), axis=1, keepdims=True)
        b_row = jnp.sum(jnp.where(t_idx <= s_idx, f_col, 0.0), axis=0, keepdims=True)
        dlog = jnp.where(tri, b_col + (i_row - b_row), NEG_INF)
        m_prev = m_scr[h]
        inter = b_col + m_prev
        mt = jnp.maximum(jnp.max(dlog, axis=1, keepdims=True), inter)
        w_intra = jnp.exp(dlog - mt)
        w_inter = jnp.exp(inter - mt)

        s = _dot_nt(qb, kb) * w_intra
        cmat = c_scr[h]
        nvec = n_scr[h]
        num = _dot(s.astype(BF16), v) + w_inter * _dot(qb, cmat.astype(BF16))
        den = jnp.sum(s, axis=1, keepdims=True) + w_inter * jnp.sum(q * nvec, axis=1, keepdims=True)
        hc = num / jnp.maximum(jnp.abs(den), jnp.exp(-mt))

        bl = jnp.sum(f_row, axis=1, keepdims=True)
        logw = bl - b_col + i_col
        m_new = jnp.maximum(bl + m_prev, jnp.max(logw, axis=0, keepdims=True))
        decay = jnp.exp(bl + m_prev - m_new)
        kw = k * jnp.exp(logw - m_new)
        c_scr[h] = decay * cmat + _dot_tn(kw.astype(BF16), v)
        n_scr[h] = decay * nvec + jnp.sum(kw, axis=0, keepdims=True)
        m_scr[h] = m_new

        outs.append(jax.nn.sigmoid(o_ref[0, :, sl].astype(F32)) * _rms(hc, ng_ref[h]))
    out_ref[0] = jnp.concatenate(outs, axis=1).astype(out_ref.dtype)


def _mlstm(act, gates, gate_bias, conv_w, norm_g, *, col0, chunk):
    B, S, _ = act.shape
    H, d, L = ML_HEADS, ML_HEAD_DIM, chunk
    tail_blocks = L // 8

    def blk(off):
        return pl.BlockSpec((1, L, H * d), lambda b, c: (b, c, col0 + off))

    def tail(off):
        return pl.BlockSpec((1, 8, H * d), lambda b, c: (b, jnp.maximum(c * tail_blocks - 1, 0), col0 + off))

    def const(a):
        return pl.BlockSpec(a.shape, lambda b, c: (0,) * a.ndim)

    return pl.pallas_call(
        functools.partial(_mlstm_kernel, chunk=L),
        grid=(B, S // L),
        in_specs=[
            blk(0), blk(1), blk(2), blk(3), tail(0), tail(1),
            pl.BlockSpec((1, L, LANES), lambda b, c: (b, c, 0)),
            const(gate_bias), const(conv_w), const(norm_g),
        ],
        out_specs=pl.BlockSpec((1, L, H * d), lambda b, c: (b, c, 0)),
        out_shape=jax.ShapeDtypeStruct((B, S, H * d), BF16),
        scratch_shapes=[pltpu.VMEM((H, d, d), F32), pltpu.VMEM((H, 1, d), F32), pltpu.VMEM((H, 1, 1), F32)],
        compiler_params=_cparams("parallel", "arbitrary"),
        name="mlstm",
    )(act, act, act, act, act, act, gates, gate_bias, conv_w, norm_g)


def _compress_kernel(x_ref, pos_ref, w1_ref, w2_ref, o_ref):
    n = x_ref.shape[1] // CMP_STRIDE
    groups = o_ref.shape[2]
    for g in range(groups):
        top = jnp.zeros((n, w1_ref.shape[-1]), F32)
        bot = jnp.zeros((n, w1_ref.shape[-1]), F32)
        for r in range(CMP_STRIDE):
            x = x_ref[0, pl.ds(r, n, stride=CMP_STRIDE), :]
            top = top + _dot((x + pos_ref[0, r]).astype(BF16), w1_ref[0, r, g])
            bot = bot + _dot((x + pos_ref[0, CMP_STRIDE + r]).astype(BF16), w1_ref[0, CMP_STRIDE + r, g])
        pre = top + pltpu.roll(bot, n - 1, 0)
        hid = jax.nn.gelu(pre, approximate=True)
        out = _dot(hid.astype(BF16), w2_ref[0])
        row = lax.broadcasted_iota(jnp.int32, out.shape, 0)
        out = jnp.where(row < n - 1, out, 0.0)
        j = lax.broadcasted_iota(jnp.int32, out.shape, 0)
        lane = lax.broadcasted_iota(jnp.int32, out.shape, 1)
        cols = jnp.where(lane < 3, j // 16, jnp.where(lane < 6, j % 16, jnp.where(lane < 9, 1, 0))).astype(F32)
        cols = jnp.where(pl.program_id(0) == 0, cols, jnp.where(lane == 0, 1.0, 0.0))
        o_ref[0, 0, g] = jnp.concatenate([out, cols], axis=1).astype(o_ref.dtype)


def _compress(gates, pos, w1, w2, *, col0):
    B, S, _ = gates.shape
    G, width = w1.shape[2], w1.shape[3]
    hidden = w1.shape[-1]
    dh = w2.shape[-1]
    n = S // CMP_STRIDE
    return pl.pallas_call(
        _compress_kernel,
        grid=(2, B),
        in_specs=[
            pl.BlockSpec((1, S, width), lambda a, b: (b, 0, col0 + a)),
            pl.BlockSpec((1,) + pos.shape[1:], lambda a, b: (a, 0, 0, 0)),
            pl.BlockSpec((1,) + w1.shape[1:], lambda a, b: (a, 0, 0, 0, 0)),
            pl.BlockSpec((1, hidden, dh), lambda a, b: (a, 0, 0)),
        ],
        out_specs=pl.BlockSpec((1, 1, G, n, 2 * dh), lambda a, b: (a, b, 0, 0, 0)),
        out_shape=jax.ShapeDtypeStruct((2, B, G, n, 2 * dh), BF16),
        compiler_params=_cparams("parallel", "parallel"),
        name="nsa_compress",
    )(gates, pos, w1, w2)


def _nsa_kernel(q_ref, gate_ref, kc_ref, vc_ref, ksl_ref, vsl_ref, kwn_ref, vwn_ref, kconst_ref, ovl_ref, tri_ref, cpat_ref,
                out_ref, ks_scr, vs_scr, kw_scr, vw_scr, s_scr, mrun_scr, acc_scr, *, tq, seq, gate_col0):
    HG, dh = NSA_GROUP_HEADS, NSA_HEAD_DIM
    g = pl.program_id(1)
    i = pl.program_id(2)
    t0 = i * tq
    rows = HG * tq
    n_cmp_pad = kc_ref.shape[2]
    n_slc = seq // SEL_BLOCK
    half = tq // 2
    blocks_per_chunk = tq // SEL_BLOCK

    def assemble(lo):
        ks_scr[...] = jnp.concatenate([ksl_ref[0][:, lo:lo + dh], kconst_ref[...]], axis=1)
        kw_scr[...] = jnp.concatenate([kwn_ref[0][:, lo:lo + dh], kconst_ref[:, n_slc:n_slc + dh]], axis=1)
        ones_col = jnp.where(lax.broadcasted_iota(jnp.int32, (seq, dh), 1) == 0, 1.0, 0.0).astype(BF16)
        vs_scr[...] = jnp.concatenate([vsl_ref[0][:, lo:lo + dh], ones_col], axis=1)
        vw_scr[...] = jnp.concatenate([vwn_ref[0][:, lo:lo + dh], ones_col], axis=1)

    for gg in range(NSA_KV_GROUPS):
        @pl.when((i == 0) & (g == gg))
        def _(gg=gg):
            assemble(gg * dh)

    q_all = q_ref[0]
    q_heads = [(q_all[:, x * dh:(x + 1) * dh].astype(F32) * (dh ** -0.5 * LOG2E)).astype(BF16) for x in range(HG)]

    def head_slope(x):
        sl = jnp.float32(0.0)
        for hh in range(NSA_HEADS):
            sl = jnp.where(g * HG + x == hh, 2.0 ** (-8.0 * (hh + 1) / NSA_HEADS), sl)
        return sl

    slopes = [head_slope(x) for x in range(HG)]

    def alibi_cols(x, width, coef):
        lane = lax.broadcasted_iota(jnp.int32, (tq, width), 1)
        v = jnp.zeros((tq, width), F32)
        for c, val in enumerate(coef):
            v = jnp.where(lane // 3 == c, slopes[x] * (val * LOG2E), v)
        hi = v.astype(BF16).astype(F32)
        mid = (v - hi).astype(BF16).astype(F32)
        lo = v - hi - mid
        return jnp.where(lane % 3 == 0, hi, jnp.where(lane % 3 == 1, mid, lo)).astype(BF16)

    def stack_q(width, coef, extra=None):
        parts = []
        for x in range(HG):
            cols = [q_heads[x]] + ([extra] if extra is not None else []) + [alibi_cols(x, width, coef)]
            parts.append(jnp.concatenate(cols, axis=1))
        return jnp.concatenate(parts, axis=0)

    q_cmp = stack_q(dh, (16.0 * CMP_STRIDE, 1.0 * CMP_STRIDE, (CMP_BLOCK - 1) / 2))
    sc = _dot_nt(q_cmp, kc_ref[0, 0])
    block_done = jnp.concatenate([cpat_ref[...]] * HG, axis=0) <= t0
    sc = jnp.where(block_done, sc, NEG_INF)
    e = jnp.exp2(sc - jnp.max(sc, axis=1, keepdims=True))
    t_row = t0 + lax.broadcasted_iota(jnp.int32, (rows, 1), 0) % tq
    any_valid = jnp.where(t_row >= CMP_BLOCK - 1, 1.0, 0.0)
    p_cmp = e * (any_valid / jnp.sum(e, axis=1, keepdims=True))
    o_cmp = _dot(p_cmp.astype(BF16), vc_ref[0, 0])[:, :dh]

    p_sum = p_cmp[0:tq]
    for x in range(1, HG):
        p_sum = p_sum + p_cmp[x * tq:(x + 1) * tq]
    p_hi = p_sum.astype(BF16)
    p_lo = (p_sum - p_hi.astype(F32)).astype(BF16)
    ovl = ovl_ref[...]
    p_slc = _dot_nt(ovl, p_hi) + _dot_nt(ovl, p_lo)
    blk = lax.broadcasted_iota(jnp.int32, (n_slc, tq), 0)
    cur = (t0 + lax.broadcasted_iota(jnp.int32, (n_slc, tq), 1)) // SEL_BLOCK
    forced = (blk == 0) | (blk == cur) | (blk == cur - 1)
    score = jnp.where(forced, SEL_FORCE, jnp.where(blk > cur, -SEL_FORCE, p_slc))
    n_tiles = n_slc // 8
    tiles = [score[8 * a:8 * a + 8] for a in range(n_tiles)]
    ranks = [jnp.zeros((8, tq), F32) for _ in range(n_tiles)]
    sub = lax.broadcasted_iota(jnp.int32, (8, tq), 0)
    for kk in range(n_slc):
        sk = score[kk:kk + 1, :]
        for a in range(n_tiles):
            if a < kk // 8:
                before = sk > tiles[a]
            elif a > kk // 8:
                before = sk >= tiles[a]
            else:
                before = jnp.where(sub > kk % 8, jnp.where(sk >= tiles[a], 1.0, 0.0), jnp.where(sk > tiles[a], 1.0, 0.0)) > 0.5
            ranks[a] = ranks[a] + jnp.where(before, 1.0, 0.0)
    rank = jnp.concatenate(ranks, axis=0)
    selected = rank < min(SEL_COUNT, n_slc)
    sel_bias = jnp.where(selected, 0.0, SEL_MASK).T.astype(BF16)
    first_blk = jnp.min(jnp.where(selected & (blk >= blocks_per_chunk), blk, n_slc))
    c_lo = jnp.minimum(jnp.maximum(first_blk // blocks_per_chunk, 1), i)
    q_aug = stack_q(2 * dh, (1.0 * SEL_BLOCK, 1.0), extra=sel_bias)

    causal_bias = jnp.concatenate([tri_ref[0]] * HG, axis=0)
    upper_bias = jnp.concatenate([tri_ref[1]] * HG, axis=0)

    qw_aug = stack_q(dh, (1.0 * SEL_BLOCK, 1.0))
    start_a = pl.multiple_of(jnp.maximum(t0 - 2 * tq, 0), tq)
    start_b = pl.multiple_of(jnp.maximum(t0 - tq, 0), tq)
    start_c = pl.multiple_of(t0, tq)
    s_a = _dot_nt(qw_aug, kw_scr[pl.ds(start_a, tq), :]) + jnp.where(i >= 2, upper_bias, NEG_INF)
    s_b = _dot_nt(qw_aug, kw_scr[pl.ds(start_b, tq), :]) + jnp.where(i >= 1, 0.0, NEG_INF)
    s_c = _dot_nt(qw_aug, kw_scr[pl.ds(start_c, tq), :]) + causal_bias
    m_w = jnp.max(jnp.maximum(jnp.maximum(s_a, s_b), s_c), axis=1, keepdims=True)
    e_a = jnp.exp2(s_a - m_w)
    e_b = jnp.exp2(s_b - m_w)
    e_c = jnp.exp2(s_c - m_w)
    o_win = (_dot(e_a.astype(BF16), vw_scr[pl.ds(start_a, tq), :]) + _dot(e_b.astype(BF16), vw_scr[pl.ds(start_b, tq), :])
             + _dot(e_c.astype(BF16), vw_scr[pl.ds(start_c, tq), :]))

    mrun_scr[...] = jnp.full(mrun_scr.shape, NEG_INF, F32)
    acc_scr[...] = jnp.zeros(acc_scr.shape, F32)

    def scores(c, diagonal):
        start = pl.multiple_of(c * tq, tq)
        s = _dot_nt(q_aug, ks_scr[pl.ds(start, tq), :])
        if diagonal:
            s = s + causal_bias
        s_scr[c] = s
        mrun_scr[...] = jnp.maximum(mrun_scr[...], jnp.maximum(s[:, :half], s[:, half:]))

    def weights(c):
        start = pl.multiple_of(c * tq, tq)
        s = s_scr[c]
        m_b = mrun_scr[...]
        p0 = jnp.exp2(s[:, :half] - m_b)
        p1 = jnp.exp2(s[:, half:] - m_b)
        p = jnp.concatenate([p0, p1], axis=1).astype(BF16)
        acc_scr[...] += _dot(p, vs_scr[pl.ds(start, tq), :])

    def chunk_pairs(lo, hi, fn):
        def pair(t, carry):
            fn(lo + 2 * t)
            fn(lo + 2 * t + 1)
            return carry

        n = hi - lo
        lax.fori_loop(0, n // 2, pair, 0)

        @pl.when(n % 2 == 1)
        def _():
            fn(hi - 1)

    scores(i, True)

    @pl.when(i > 0)
    def _():
        scores(0, False)

    chunk_pairs(c_lo, i, lambda c: scores(c, False))
    m_row = jnp.max(mrun_scr[...], axis=1, keepdims=True)
    mrun_scr[...] = jnp.broadcast_to(m_row, mrun_scr.shape)

    @pl.when(i > 0)
    def _():
        weights(0)

    chunk_pairs(c_lo, i, weights)
    weights(i)
    gates = jax.nn.sigmoid(gate_ref[0])

    def gate_col(r):
        cols = []
        for x in range(HG):
            per_group = [gate_col0 + (gg * HG + x) * NSA_N_BRANCH + r for gg in range(NSA_KV_GROUPS)]
            col = gates[:, per_group[0]:per_group[0] + 1]
            for gg in range(1, NSA_KV_GROUPS):
                col = jnp.where(g == gg, gates[:, per_group[gg]:per_group[gg] + 1], col)
            cols.append(col)
        return jnp.concatenate(cols, axis=0)

    acc = acc_scr[...]
    o = gate_col(0) * o_cmp + acc[:, :dh] * (gate_col(1) / acc[:, dh:dh + 1])

    o = o + o_win[:, :dh] * (gate_col(2) / o_win[:, dh:dh + 1])

    out_ref[0] = jnp.concatenate([o[x * tq:(x + 1) * tq] for x in range(HG)], axis=1).astype(out_ref.dtype)


def _nsa_attention(act, gates, kc, vc, kconst, overlap_t, *, q_col0, kv_col0, gate_col0, tq):
    B, S, _ = act.shape
    G, HG, dh = NSA_KV_GROUPS, NSA_GROUP_HEADS, NSA_HEAD_DIM
    n_slc = S // SEL_BLOCK
    rows = HG * tq
    assert WINDOW == 2 * tq, "the window branch reads exactly the two chunks before the diagonal one"
    r = np.arange(tq)[:, None]
    l = np.arange(tq)[None, :]
    tri_bias = jnp.asarray(np.stack([np.where(l <= r, 0.0, NEG_INF), np.where(l > r, 0.0, NEG_INF)]), F32)
    cmp_pattern = jnp.asarray(np.arange(S // CMP_STRIDE)[None, :] * CMP_STRIDE + (CMP_BLOCK - 1) - r, jnp.int32)

    def kv_spec(off):
        return pl.BlockSpec((1, S, G * dh), lambda b, g, i: (b, 0, kv_col0 + off))

    def full(a):
        return pl.BlockSpec((1, 1) + a.shape[2:], lambda b, g, i: (b, g, 0, 0))

    return pl.pallas_call(
        functools.partial(_nsa_kernel, tq=tq, seq=S, gate_col0=gate_col0),
        grid=(B, G, S // tq),
        in_specs=[
            pl.BlockSpec((1, tq, HG * dh), lambda b, g, i: (b, i, q_col0 + g)),
            pl.BlockSpec((1, tq, LANES), lambda b, g, i: (b, i, 0)),
            full(kc), full(vc), kv_spec(0), kv_spec(1), kv_spec(2), kv_spec(3),
            pl.BlockSpec(kconst.shape, lambda b, g, i: (0, 0)),
            pl.BlockSpec(overlap_t.shape, lambda b, g, i: (0, 0)),
            pl.BlockSpec(tri_bias.shape, lambda b, g, i: (0, 0, 0)),
            pl.BlockSpec(cmp_pattern.shape, lambda b, g, i: (0, 0)),
        ],
        out_specs=pl.BlockSpec((1, tq, HG * dh), lambda b, g, i: (b, i, g)),
        out_shape=jax.ShapeDtypeStruct((B, S, NSA_WIDTH), BF16),
        scratch_shapes=[pltpu.VMEM((S, dh + n_slc + 2 * dh), BF16), pltpu.VMEM((S, 2 * dh), BF16),
                        pltpu.VMEM((S, 2 * dh), BF16), pltpu.VMEM((S, 2 * dh), BF16),
                        pltpu.VMEM((S // tq, rows, tq), F32), pltpu.VMEM((rows, tq // 2), F32),
                        pltpu.VMEM((rows, 2 * dh), F32)],
        compiler_params=_cparams("parallel", "parallel", "arbitrary"),
        name="nsa_attn",
    )(act, gates, kc, vc, act, act, act, act, kconst, overlap_t, tri_bias, cmp_pattern)


def _mix_out_kernel(x_ref, hm_ref, hn_ref, gm_ref, gn_ref, wm_ref, wn_ref, wo_ref, o_ref):
    y_ml = _dot(hm_ref[...], wm_ref[...])
    y_ns = _dot(hn_ref[...], wn_ref[...])
    mix = jax.nn.sigmoid(gm_ref[...].astype(F32)) * y_ml + jax.nn.sigmoid(gn_ref[...].astype(F32)) * y_ns
    o_ref[...] = x_ref[...] + _dot(mix.astype(BF16), wo_ref[...])


def _mix_out(x2d, h_ml, h_ns, act2d, wm, wn, wo, *, gate_col0, tm):
    n, d = x2d.shape

    def res(w):
        return pl.BlockSpec(w.shape, lambda i: (0, 0))

    return pl.pallas_call(
        _mix_out_kernel,
        grid=(n // tm,),
        in_specs=[
            pl.BlockSpec((tm, d), lambda i: (i, 0)),
            pl.BlockSpec((tm, h_ml.shape[1]), lambda i: (i, 0)),
            pl.BlockSpec((tm, h_ns.shape[1]), lambda i: (i, 0)),
            pl.BlockSpec((tm, d), lambda i: (i, gate_col0)),
            pl.BlockSpec((tm, d), lambda i: (i, gate_col0 + 1)),
            res(wm), res(wn), res(wo),
        ],
        out_specs=pl.BlockSpec((tm, d), lambda i: (i, 0)),
        out_shape=jax.ShapeDtypeStruct((n, d), F32),
        compiler_params=_cparams("parallel"),
        name="mix_out",
    )(x2d, h_ml, h_ns, act2d, act2d, wm, wn, wo)


def _xattn_kernel(x_ref, g_ref, wq_ref, k_ref, v_ref, wo_ref, o_ref):
    x = x_ref[...]
    d = x.shape[1]
    dh = d // XA_HEADS
    h = _rms(x, g_ref[...]).astype(BF16)
    q = (_dot(h, wq_ref[...]) * (dh ** -0.5)).astype(BF16)
    k = k_ref[0]
    v = v_ref[0]
    outs = []
    for hd in range(XA_HEADS):
        sl = slice(hd * dh, (hd + 1) * dh)
        s = _dot_nt(q[:, sl], k[:, sl])
        e = jnp.exp(s - jnp.max(s, axis=1, keepdims=True))
        p = e / jnp.sum(e, axis=1, keepdims=True)
        outs.append(_dot(p.astype(BF16), v[:, sl]))
    o = jnp.concatenate(outs, axis=1).astype(BF16)
    o_ref[...] = x + _dot(o, wo_ref[...])


def _xattn(x2d, g, wq, kv, wo, *, seq, tm):
    n, d = x2d.shape
    n_mem = kv.shape[1]
    tiles_per_batch = seq // tm
    return pl.pallas_call(
        _xattn_kernel,
        grid=(n // tm,),
        in_specs=[
            pl.BlockSpec((tm, d), lambda i: (i, 0)),
            pl.BlockSpec((1, d), lambda i: (0, 0)),
            pl.BlockSpec(wq.shape, lambda i: (0, 0)),
            pl.BlockSpec((1, n_mem, d), lambda i: (i // tiles_per_batch, 0, 0)),
            pl.BlockSpec((1, n_mem, d), lambda i: (i // tiles_per_batch, 0, 1)),
            pl.BlockSpec(wo.shape, lambda i: (0, 0)),
        ],
        out_specs=pl.BlockSpec((tm, d), lambda i: (i, 0)),
        out_shape=jax.ShapeDtypeStruct((n, d), F32),
        compiler_params=_cparams("parallel"),
        name="xattn",
    )(x2d, g.reshape(1, d), wq, kv, kv, wo)


def _route(logits, bg, be):
    G, EG = MOE_GROUPS, MOE_EXPERTS_PER_GROUP
    tm = logits.shape[1]
    lg = logits[0:G] + bg
    eg = jnp.exp(lg - jnp.max(lg, axis=0, keepdims=True))
    pg = eg / jnp.sum(eg, axis=0, keepdims=True)
    g_val = jnp.max(pg, axis=0, keepdims=True)
    g_row = lax.broadcasted_iota(jnp.int32, (G, tm), 0)
    g_idx = jnp.min(jnp.where(pg == g_val, g_row, G), axis=0, keepdims=True)
    el = logits[8:8 + G * EG] + be
    e_in = jnp.zeros((EG, tm), F32)
    for gg in range(G):
        e_in = jnp.where(g_idx == gg, el[gg * EG:(gg + 1) * EG], e_in)
    e_row = lax.broadcasted_iota(jnp.int32, (EG, tm), 0)
    v1 = jnp.max(e_in, axis=0, keepdims=True)
    i1 = jnp.min(jnp.where(e_in == v1, e_row, EG), axis=0, keepdims=True)
    rest = jnp.where(e_row == i1, -jnp.inf, e_in)
    v2 = jnp.max(rest, axis=0, keepdims=True)
    i2 = jnp.min(jnp.where(rest == v2, e_row, EG), axis=0, keepdims=True)
    e2 = jnp.exp(v2 - v1)
    c1 = g_val / (1.0 + e2)
    c2 = g_val * e2 / (1.0 + e2)
    ex = lax.broadcasted_iota(jnp.int32, (G * EG, tm), 0)
    base = g_idx * EG
    return jnp.where(ex == base + i1, c1, 0.0) + jnp.where(ex == base + i2, c2, 0.0), g_idx


def _moe_kernel(x_ref, g_ref, wr_ref, bg_ref, be_ref, before_ref, w1_ref, w3_ref, w2_ref, gf_ref, o_ref,
                h_scr, slot_scr, wt_scr, acc_scr, *, caps):
    G, EG = MOE_GROUPS, MOE_EXPERTS_PER_GROUP
    grp = pl.program_id(1)
    tm = x_ref.shape[0]

    @pl.when(grp == 0)
    def _():
        h = _rms(x_ref[...], g_ref[...])
        h_hi = h.astype(BF16)
        h_scr[...] = h_hi
        h_lo = (h - h_hi.astype(F32)).astype(BF16)
        w_hi = wr_ref[...].astype(BF16)
        w_lo = (wr_ref[...] - w_hi.astype(F32)).astype(BF16)
        logits = _dot_nt(w_hi, h_hi) + _dot_nt(w_hi, h_lo) + _dot_nt(w_lo, h_hi)
        wt, g_idx = _route(logits, bg_ref[...], be_ref[...])
        member = jnp.where(lax.broadcasted_iota(jnp.int32, (8, tm), 0) == g_idx, 1.0, 0.0)
        prefix = _dot(member.astype(BF16), before_ref[...])
        slots = jnp.where(member > 0.0, prefix, -1.0)
        zeros = jnp.zeros((8 - EG, tm), F32)
        for gg in range(G):
            slot_scr[gg] = jnp.broadcast_to(slots[gg:gg + 1], (8, tm))
            wt_scr[gg] = jnp.concatenate([wt[gg * EG:(gg + 1) * EG], zeros], axis=0)
        acc_scr[...] = jnp.zeros_like(acc_scr)

    slot = slot_scr[grp][0:1, :]
    n_rows = jnp.max(slot).astype(jnp.int32) + 1
    wts = wt_scr[grp]

    def one_pass(cap, first_row):
        row = lax.broadcasted_iota(jnp.int32, (cap, tm), 0).astype(F32)
        pick = jnp.where(row == slot - first_row.astype(F32), 1.0, 0.0)
        pick_b = pick.astype(BF16)
        hsub = _dot(pick_b, h_scr[...]).astype(BF16)
        y = jnp.zeros((cap, x_ref.shape[1]), F32)
        for e in range(EG):
            w_e = jnp.sum(pick * wts[e:e + 1, :], axis=1, keepdims=True)
            a = _dot(hsub, w1_ref[e])
            a = a * jax.nn.sigmoid(a) * _dot(hsub, w3_ref[e]) * w_e
            y = y + _dot(a.astype(BF16), w2_ref[e])
        acc_scr[...] += _dot_tn(pick_b, y.astype(BF16))

    *small_caps, cap_long = caps
    lower = 0
    for cap in small_caps:
        @pl.when((n_rows > lower) & (n_rows <= cap))
        def _(cap=cap):
            one_pass(cap, jnp.int32(0))
        lower = cap

    @pl.when(n_rows > lower)
    def _():
        def body(p, carry):
            one_pass(cap_long, p * cap_long)
            return carry

        lax.fori_loop(0, (n_rows + cap_long - 1) // cap_long, body, 0)

    @pl.when(grp == pl.num_programs(1) - 1)
    def _():
        o_ref[...] = _rms(x_ref[...] + acc_scr[...], gf_ref[...])


def _moe(x2d, g, wr, bg, be, w1, w3, w2, g_final, *, tm, caps):
    n, d = x2d.shape
    E, _, hid = w1.shape
    G, EG = MOE_GROUPS, MOE_EXPERTS_PER_GROUP
    before = jnp.asarray(np.arange(tm)[:, None] < np.arange(tm)[None, :], BF16)
    once = pl.Buffered(1)

    def const(shape):
        return pl.BlockSpec(shape, lambda i, e: (0,) * len(shape), pipeline_mode=once)

    return pl.pallas_call(
        functools.partial(_moe_kernel, caps=caps),
        grid=(n // tm, G),
        in_specs=[
            pl.BlockSpec((tm, d), lambda i, e: (i, 0), pipeline_mode=once),
            const((1, d)), const(wr.shape), const(bg.shape), const(be.shape), const((tm, tm)),
            pl.BlockSpec((EG, d, hid), lambda i, e: (e, 0, 0)),
            pl.BlockSpec((EG, d, hid), lambda i, e: (e, 0, 0)),
            pl.BlockSpec((EG, hid, d), lambda i, e: (e, 0, 0)),
            const((1, d)),
        ],
        out_specs=pl.BlockSpec((tm, d), lambda i, e: (i, 0)),
        out_shape=jax.ShapeDtypeStruct((n, d), F32),
        scratch_shapes=[pltpu.VMEM((tm, d), BF16), pltpu.VMEM((G, 8, tm), F32), pltpu.VMEM((G, 8, tm), F32),
                        pltpu.VMEM((tm, d), F32)],
        compiler_params=_cparams("parallel", "arbitrary"),
        name="moe",
    )(x2d, g.reshape(1, d), wr, bg, be, before, w1, w3, w2, g_final.reshape(1, d))


def _overlap_t(seq):
    n_cmp_pad = seq // CMP_STRIDE
    cs = np.arange(n_cmp_pad)[None, :] * CMP_STRIDE
    ss = np.arange(seq // SEL_BLOCK)[:, None] * SEL_BLOCK
    ov = np.clip(np.minimum(cs + CMP_BLOCK, ss + SEL_BLOCK) - np.maximum(cs, ss), 0, None) / CMP_BLOCK
    ov[:, n_cmp_pad - 1] = 0.0
    return jnp.asarray(ov, BF16)


def _layer(x, mem, norm_mix, w_in, conv_qk, b_igate, b_fgate, mlstm_norm, cmp_pos_k, cmp_pos_v, cmp_k_w1, cmp_k_w2,
           cmp_v_w1, cmp_v_w2, w_br_mlstm, w_br_nsa, w_mix_out, norm_xattn, norm_mem, xa_wq, xa_wkv, xa_wo, norm_ffn,
           router_group_w, router_group_b, router_expert_w, router_expert_b, moe_w1, moe_w3, moe_w2, norm_final):
    B, S, D = x.shape
    N = B * S
    H, d = ML_HEADS, ML_HEAD_DIM
    G, HG, dh = NSA_KV_GROUPS, NSA_GROUP_HEADS, NSA_HEAD_DIM
    x2d = x.reshape(N, D)

    o_mlqkvo = 0
    o_mlif = 4 * ML_WIDTH
    o_nsq = o_mlif + 2 * H
    o_kv = o_nsq + NSA_WIDTH
    o_nsg = o_kv + 6 * NSA_KV_WIDTH
    o_merge = o_nsg + NSA_N_BRANCH * NSA_HEADS
    o_kv_sel = o_kv + 2 * NSA_KV_WIDTH
    w_act = jnp.concatenate([w_in[:, o_merge:o_merge + 2 * D], w_in[:, o_mlqkvo:o_mlif], w_in[:, o_nsq:o_kv],
                             w_in[:, o_kv_sel:o_nsg]], axis=1).astype(BF16)
    n_small = 2 * H + NSA_N_BRANCH * NSA_HEADS
    w_small = jnp.concatenate([w_in[:, o_mlif:o_nsq], w_in[:, o_nsg:o_merge], jnp.zeros((D, LANES - n_small), F32),
                               w_in[:, o_kv:o_kv_sel]], axis=1).astype(BF16)
    act, small, cmp_kv = _in_proj(x2d, norm_mix, w_act, w_small, tm=1024, tn=2560)
    act3 = act.reshape(B, S, act.shape[1])
    c_ml = 2 * D
    c_nsq = c_ml + 4 * ML_WIDTH
    c_kv = c_nsq + NSA_WIDTH

    small3 = small.reshape(B, S, small.shape[1])
    gate_bias = jnp.zeros((1, LANES), F32).at[0, 0:H].set(b_igate).at[0, H:2 * H].set(b_fgate)
    conv_w = jnp.concatenate([conv_qk[:, :ML_WIDTH].reshape(ML_CONV, H, d), conv_qk[:, ML_WIDTH:].reshape(ML_CONV, H, d)],
                             axis=-1).transpose(1, 0, 2)
    h_ml = _mlstm(act3, small3, gate_bias, conv_w, mlstm_norm.reshape(H, 1, d), col0=c_ml // ML_WIDTH, chunk=512)

    pos = jnp.tile(jnp.stack([cmp_pos_k, cmp_pos_v])[:, :, None, :], (1, 1, 1, G))
    w1c = jnp.stack([cmp_k_w1, cmp_v_w1]).reshape(2, CMP_BLOCK, dh, -1)
    w1c = jnp.einsum('arld,gh->arghld', w1c, jnp.eye(G, dtype=F32))
    w1c = w1c.reshape(2, CMP_BLOCK, G, G * dh, -1).astype(BF16)
    w2c = jnp.stack([cmp_k_w2, cmp_v_w2]).astype(BF16)
    kvc = _compress(cmp_kv.reshape(B, S, cmp_kv.shape[1]), pos, w1c, w2c, col0=0)
    n_slc = S // SEL_BLOCK
    tok = np.arange(S)
    kconst = np.zeros((S, n_slc + 2 * dh), np.float32)
    kconst[tok, tok // SEL_BLOCK] = 1.0
    kconst[:, n_slc:n_slc + 3] = (tok // SEL_BLOCK)[:, None]
    kconst[:, n_slc + 3:n_slc + 6] = (tok % SEL_BLOCK)[:, None]
    h_ns = _nsa_attention(act3, small3, kvc[0], kvc[1], jnp.asarray(kconst, BF16), _overlap_t(S),
                          q_col0=c_nsq // (HG * dh), kv_col0=c_kv // NSA_KV_WIDTH,
                          gate_col0=2 * H, tq=256)

    x1 = _mix_out(x2d, h_ml.reshape(N, ML_WIDTH), h_ns.reshape(N, NSA_WIDTH), act, w_br_mlstm.astype(BF16),
                  w_br_nsa.astype(BF16), w_mix_out.astype(BF16), gate_col0=0, tm=512)

    n_mem = mem.shape[1]
    kv_mem = _norm_matmul(mem.reshape(B * n_mem, D), norm_mem, xa_wkv.astype(BF16), BF16, tm=B * n_mem, tn=512)
    x2 = _xattn(x1, norm_xattn, xa_wq.astype(BF16), kv_mem.reshape(B, n_mem, 2 * D), xa_wo.astype(BF16), seq=S, tm=1024)

    wr = jnp.zeros((ROUTER_ROWS, D), F32)
    wr = wr.at[0:MOE_GROUPS].set(router_group_w.T).at[8:8 + MOE_EXPERTS].set(router_expert_w.T)
    return _moe(x2, norm_ffn, wr, router_group_b.reshape(MOE_GROUPS, 1), router_expert_b.reshape(MOE_EXPERTS, 1),
                moe_w1.astype(BF16), moe_w3.astype(BF16), moe_w2.astype(BF16), norm_final, tm=1024, caps=(256, 304, 352)).reshape(B, S, D)


def kernel(x, mem, norm_mix, w_in, conv_qk, b_igate, b_fgate, mlstm_norm, cmp_pos_k, cmp_pos_v, cmp_k_w1, cmp_k_w2, cmp_v_w1, cmp_v_w2, w_br_mlstm, w_br_nsa, w_mix_out, norm_xattn, norm_mem, xa_wq, xa_wkv, xa_wo, norm_ffn, router_group_w, router_group_b, router_expert_w, router_expert_b, moe_w1, moe_w3, moe_w2, norm_final):
    depth = w_in.shape[0]
    assert depth == 1, "the fused final norm assumes a single layer"
    layer = 0
    return _layer(x, mem, norm_mix[layer], w_in[layer], conv_qk[layer], b_igate[layer], b_fgate[layer], mlstm_norm[layer],
                  cmp_pos_k[layer], cmp_pos_v[layer], cmp_k_w1[layer], cmp_k_w2[layer], cmp_v_w1[layer], cmp_v_w2[layer],
                  w_br_mlstm[layer], w_br_nsa[layer], w_mix_out[layer], norm_xattn[layer], norm_mem[layer], xa_wq[layer],
                  xa_wkv[layer], xa_wo[layer], norm_ffn[layer], router_group_w[layer], router_group_b[layer],
                  router_expert_w[layer], router_expert_b[layer], moe_w1[layer], moe_w3[layer], moe_w2[layer], norm_final)
```

```python
import functools
import math

import numpy as np
import jax
import jax.numpy as jnp
from jax import lax
from jax.experimental import pallas as pl
from jax.experimental.pallas import tpu as pltpu

F32 = jnp.float32
BF16 = jnp.bfloat16

ML_HEADS = 4
ML_HEAD_DIM = 128
ML_WIDTH = ML_HEADS * ML_HEAD_DIM
ML_CONV = 4
NSA_HEADS = 8
NSA_KV_GROUPS = 2
NSA_HEAD_DIM = 64
NSA_GROUP_HEADS = NSA_HEADS // NSA_KV_GROUPS
NSA_WIDTH = NSA_HEADS * NSA_HEAD_DIM
NSA_KV_WIDTH = NSA_KV_GROUPS * NSA_HEAD_DIM
NSA_N_BRANCH = 3
CMP_BLOCK = 32
CMP_STRIDE = 16
SEL_BLOCK = 64
SEL_COUNT = 16
SEL_FORCE = 1e4
WINDOW = 512
XA_HEADS = 4
MOE_GROUPS = 4
MOE_EXPERTS_PER_GROUP = 4
MOE_EXPERTS = MOE_GROUPS * MOE_EXPERTS_PER_GROUP
RMS_EPS = 1e-6
NEG_INF = -1e30
SEL_MASK = -float(2 ** 30)
LOG2E = 1.4426950408889634

LANES = 128
VMEM_LIMIT_BYTES = 56 * 1024 * 1024
ROUTER_ROWS = 32


def _cparams(*sem):
    return pltpu.CompilerParams(dimension_semantics=sem, vmem_limit_bytes=VMEM_LIMIT_BYTES)


def _rms(x, g):
    return x * lax.rsqrt(jnp.mean(x * x, axis=-1, keepdims=True) + RMS_EPS) * g


def _dot(a, b):
    return jnp.dot(a, b, preferred_element_type=F32)


def _dot_nt(a, b):
    return lax.dot_general(a, b, (((1,), (1,)), ((), ())), preferred_element_type=F32)


def _dot_tn(a, b):
    return lax.dot_general(a, b, (((0,), (0,)), ((), ())), preferred_element_type=F32)


def _norm_matmul_kernel(x_ref, g_ref, w_ref, o_ref, h_ref):
    @pl.when(pl.program_id(1) == 0)
    def _():
        h_ref[...] = _rms(x_ref[...], g_ref[...]).astype(BF16)

    o_ref[...] = _dot(h_ref[...], w_ref[...]).astype(o_ref.dtype)


def _norm_matmul(x2d, g, w, out_dtype, tm, tn):
    m, d = x2d.shape
    n = w.shape[1]
    return pl.pallas_call(
        _norm_matmul_kernel,
        grid=(m // tm, n // tn),
        in_specs=[
            pl.BlockSpec((tm, d), lambda i, j: (i, 0)),
            pl.BlockSpec((1, d), lambda i, j: (0, 0)),
            pl.BlockSpec((d, tn), lambda i, j: (0, j)),
        ],
        out_specs=pl.BlockSpec((tm, tn), lambda i, j: (i, j)),
        out_shape=jax.ShapeDtypeStruct((m, n), out_dtype),
        scratch_shapes=[pltpu.VMEM((tm, d), BF16)],
        compiler_params=_cparams("parallel", "arbitrary"),
        name="norm_matmul",
    )(x2d, g.reshape(1, d), w)


def _in_proj_kernel(x_ref, g_ref, w_ref, ws_ref, o_ref, og_ref, oc_ref, h_ref):
    @pl.when(pl.program_id(1) == 0)
    def _():
        h_ref[...] = _rms(x_ref[...], g_ref[...]).astype(BF16)
        small = _dot(h_ref[...], ws_ref[...])
        og_ref[...] = small[:, :LANES]
        oc_ref[...] = small[:, LANES:]

    o_ref[...] = _dot(h_ref[...], w_ref[...]).astype(o_ref.dtype)


def _in_proj(x2d, g, w, w_small, tm, tn):
    m, d = x2d.shape
    n = w.shape[1]
    ns = w_small.shape[1]
    return pl.pallas_call(
        _in_proj_kernel,
        grid=(m // tm, n // tn),
        in_specs=[
            pl.BlockSpec((tm, d), lambda i, j: (i, 0)),
            pl.BlockSpec((1, d), lambda i, j: (0, 0)),
            pl.BlockSpec((d, tn), lambda i, j: (0, j)),
            pl.BlockSpec((d, ns), lambda i, j: (0, 0)),
        ],
        out_specs=[pl.BlockSpec((tm, tn), lambda i, j: (i, j)), pl.BlockSpec((tm, LANES), lambda i, j: (i, 0)),
                   pl.BlockSpec((tm, ns - LANES), lambda i, j: (i, 0))],
        out_shape=[jax.ShapeDtypeStruct((m, n), BF16), jax.ShapeDtypeStruct((m, LANES), F32),
                   jax.ShapeDtypeStruct((m, ns - LANES), F32)],
        scratch_shapes=[pltpu.VMEM((tm, d), BF16)],
        compiler_params=_cparams("parallel", "arbitrary"),
        name="in_proj",
    )(x2d, g.reshape(1, d), w, w_small)


def _mlstm_kernel(q_ref, k_ref, v_ref, o_ref, qt_ref, kt_ref, gate_ref, gbias_ref,
                  cw_ref, ng_ref, out_ref, c_scr, n_scr, m_scr, *, chunk):
    L, d = chunk, ML_HEAD_DIM
    c = pl.program_id(1)

    @pl.when(c == 0)
    def _():
        c_scr[...] = jnp.zeros_like(c_scr)
        n_scr[...] = jnp.zeros_like(n_scr)
        m_scr[...] = jnp.zeros_like(m_scr)

    t_idx = lax.broadcasted_iota(jnp.int32, (L, L), 0)
    s_idx = lax.broadcasted_iota(jnp.int32, (L, L), 1)
    tri = s_idx <= t_idx

    row8 = lax.broadcasted_iota(jnp.int32, (8, d), 0)

    def conv_silu(x, tail, w):
        x = x.astype(F32)
        tail = jnp.where(c > 0, tail.astype(F32), 0.0)
        y = x * w[ML_CONV - 1:ML_CONV, :]
        for j in range(ML_CONV - 1):
            s = ML_CONV - 1 - j
            r = pltpu.roll(x, s, 0)
            head = jnp.where(row8 < s, pltpu.roll(tail, s, 0), r[:8])
            y = y + jnp.concatenate([head, r[8:]], axis=0) * w[j:j + 1, :]
        return y * (0.5 * jnp.tanh(0.5 * y) + 0.5)

    g_cols = gate_ref[0] + gbias_ref[...]
    g_rows = g_cols.T

    outs = []
    for h in range(ML_HEADS):
        sl = slice(h * d, (h + 1) * d)
        cw = cw_ref[h]
        q = conv_silu(q_ref[0, :, sl], qt_ref[0, :, sl], cw[:, :d])
        k = conv_silu(k_ref[0, :, sl], kt_ref[0, :, sl], cw[:, d:]) * (d ** -0.5)
        v = v_ref[0, :, sl]
        qb = q.astype(BF16)
        kb = k.astype(BF16)

        i_col = g_cols[:, h:h + 1]
        f_col = jax.nn.log_sigmoid(g_cols[:, ML_HEADS + h:ML_HEADS + h + 1])
        i_row = g_rows[h:h + 1, :]
        f_row = jax.nn.log_sigmoid(g_rows[ML_HEADS + h:ML_HEADS + h + 1, :])

        b_col = jnp.sum(jnp.where(tri, f_row, 0.0), axis=1, keepdims=True)
        b_row = jnp.sum(jnp.where(t_idx <= s_idx, f_col, 0.0), axis=0, keepdims=True)
        dlog = jnp.where(tri, b_col + (i_row - b_row), NEG_INF)
        m_prev = m_scr[h]
        inter = b_col + m_prev
        mt = jnp.maximum(jnp.max(dlog, axis=1, keepdims=True), inter)
        w_intra = jnp.exp(dlog - mt)
        w_inter = jnp.exp(inter - mt)

        s = _dot_nt(qb, kb) * w_intra
        cmat = c_scr[h]
        nvec = n_scr[h]
        num = _dot(s.astype(BF16), v) + w_inter * _dot(qb, cmat.astype(BF16))
        den = jnp.sum(s, axis=1, keepdims=True) + w_inter * jnp.sum(q * nvec, axis=1, keepdims=True)
        hc = num / jnp.maximum(jnp.abs(den), jnp.exp(-mt))

        bl = jnp.sum(f_row, axis=1, keepdims=True)
        logw = bl - b_col + i_col
        m_new = jnp.maximum(bl + m_prev, jnp.max(logw, axis=0, keepdims=True))
        decay = jnp.exp(bl + m_prev - m_new)
        kw = k * jnp.exp(logw - m_new)
        c_scr[h] = decay * cmat + _dot_tn(kw.astype(BF16), v)
        n_scr[h] = decay * nvec + jnp.sum(kw, axis=0, keepdims=True)
        m_scr[h] = m_new

        outs.append(jax.nn.sigmoid(o_ref[0, :, sl].astype(F32)) * _rms(hc, ng_ref[h]))
    out_ref[0] = jnp.concatenate(outs, axis=1).astype(out_ref.dtype)


def _mlstm(act, gates, gate_bias, conv_w, norm_g, *, col0, chunk):
    B, S, _ = act.shape
    H, d, L = ML_HEADS, ML_HEAD_DIM, chunk
    tail_blocks = L // 8

    def blk(off):
        return pl.BlockSpec((1, L, H * d), lambda b, c: (b, c, col0 + off))

    def tail(off):
        return pl.BlockSpec((1, 8, H * d), lambda b, c: (b, jnp.maximum(c * tail_blocks - 1, 0), col0 + off))

    def const(a):
        return pl.BlockSpec(a.shape, lambda b, c: (0,) * a.ndim)

    return pl.pallas_call(
        functools.partial(_mlstm_kernel, chunk=L),
        grid=(B, S // L),
        in_specs=[
            blk(0), blk(1), blk(2), blk(3), tail(0), tail(1),
            pl.BlockSpec((1, L, LANES), lambda b, c: (b, c, 0)),
            const(gate_bias), const(conv_w), const(norm_g),
        ],
        out_specs=pl.BlockSpec((1, L, H * d), lambda b, c: (b, c, 0)),
        out_shape=jax.ShapeDtypeStruct((B, S, H * d), BF16),
        scratch_shapes=[pltpu.VMEM((H, d, d), F32), pltpu.VMEM((H, 1, d), F32), pltpu.VMEM((H, 1, 1), F32)],
        compiler_params=_cparams("parallel", "arbitrary"),
        name="mlstm",
    )(act, act, act, act, act, act, gates, gate_bias, conv_w, norm_g)


def _compress_kernel(x_ref, pos_ref, w1_ref, w2_ref, o_ref):
    n = x_ref.shape[1] // CMP_STRIDE
    groups = o_ref.shape[2]
    for g in range(groups):
        top = jnp.zeros((n, w1_ref.shape[-1]), F32)
        bot = jnp.zeros((n, w1_ref.shape[-1]), F32)
        for r in range(CMP_STRIDE):
            x = x_ref[0, pl.ds(r, n, stride=CMP_STRIDE), :]
            top = top + _dot((x + pos_ref[0, r]).astype(BF16), w1_ref[0, r, g])
            bot = bot + _dot((x + pos_ref[0, CMP_STRIDE + r]).astype(BF16), w1_ref[0, CMP_STRIDE + r, g])
        pre = top + pltpu.roll(bot, n - 1, 0)
        hid = jax.nn.gelu(pre, approximate=True)
        out = _dot(hid.astype(BF16), w2_ref[0])
        row = lax.broadcasted_iota(jnp.int32, out.shape, 0)
        out = jnp.where(row < n - 1, out, 0.0)
        j = lax.broadcasted_iota(jnp.int32, out.shape, 0)
        lane = lax.broadcasted_iota(jnp.int32, out.shape, 1)
        cols = jnp.where(lane < 3, j // 16, jnp.where(lane < 6, j % 16, jnp.where(lane < 9, 1, 0))).astype(F32)
        cols = jnp.where(pl.program_id(0) == 0, cols, jnp.where(lane == 0, 1.0, 0.0))
        o_ref[0, 0, g] = jnp.concatenate([out, cols], axis=1).astype(o_ref.dtype)


def _compress(gates, pos, w1, w2, *, col0):
    B, S, _ = gates.shape
    G, width = w1.shape[2], w1.shape[3]
    hidden = w1.shape[-1]
    dh = w2.shape[-1]
    n = S // CMP_STRIDE
    return pl.pallas_call(
        _compress_kernel,
        grid=(2, B),
        in_specs=[
            pl.BlockSpec((1, S, width), lambda a, b: (b, 0, col0 + a)),
            pl.BlockSpec((1,) + pos.shape[1:], lambda a, b: (a, 0, 0, 0)),
            pl.BlockSpec((1,) + w1.shape[1:], lambda a, b: (a, 0, 0, 0, 0)),
            pl.BlockSpec((1, hidden, dh), lambda a, b: (a, 0, 0)),
        ],
        out_specs=pl.BlockSpec((1, 1, G, n, 2 * dh), lambda a, b: (a, b, 0, 0, 0)),
        out_shape=jax.ShapeDtypeStruct((2, B, G, n, 2 * dh), BF16),
        compiler_params=_cparams("parallel", "parallel"),
        name="nsa_compress",
    )(gates, pos, w1, w2)


def _nsa_kernel(q_ref, gate_ref, kc_ref, vc_ref, ksl_ref, vsl_ref, kwn_ref, vwn_ref, kconst_ref, ovl_ref, tri_ref, cpat_ref,
                out_ref, ks_scr, vs_scr, kw_scr, vw_scr, s_scr, mrun_scr, acc_scr, *, tq, seq, gate_col0):
    HG, dh = NSA_GROUP_HEADS, NSA_HEAD_DIM
    g = pl.program_id(1)
    i = pl.program_id(2)
    t0 = i * tq
    rows = HG * tq
    n_cmp_pad = kc_ref.shape[2]
    n_slc = seq // SEL_BLOCK
    half = tq // 2
    blocks_per_chunk = tq // SEL_BLOCK
    spare = seq // tq

    def assemble(lo):
        ks_scr[...] = jnp.concatenate([ksl_ref[0][:, lo:lo + dh], kconst_ref[...]], axis=1)
        kw_scr[...] = jnp.concatenate([kwn_ref[0][:, lo:lo + dh], kconst_ref[:, n_slc:n_slc + dh]], axis=1)
        ones_col = jnp.where(lax.broadcasted_iota(jnp.int32, (seq, dh), 1) == 0, 1.0, 0.0).astype(BF16)
        vs_scr[...] = jnp.concatenate([vsl_ref[0][:, lo:lo + dh], ones_col], axis=1)
        vw_scr[...] = jnp.concatenate([vwn_ref[0][:, lo:lo + dh], ones_col], axis=1)

    for gg in range(NSA_KV_GROUPS):
        @pl.when((i == 0) & (g == gg))
        def _(gg=gg):
            assemble(gg * dh)

    q_all = q_ref[0]
    q_heads = [(q_all[:, x * dh:(x + 1) * dh].astype(F32) * (dh ** -0.5 * LOG2E)).astype(BF16) for x in range(HG)]

    def head_slope(x):
        sl = jnp.float32(0.0)
        for hh in range(NSA_HEADS):
            sl = jnp.where(g * HG + x == hh, 2.0 ** (-8.0 * (hh + 1) / NSA_HEADS), sl)
        return sl

    slopes = [head_slope(x) for x in range(HG)]

    def alibi_cols(x, width, coef):
        lane = lax.broadcasted_iota(jnp.int32, (tq, width), 1)
        v = jnp.zeros((tq, width), F32)
        for c, val in enumerate(coef):
            v = jnp.where(lane // 3 == c, slopes[x] * (val * LOG2E), v)
        hi = v.astype(BF16).astype(F32)
        mid = (v - hi).astype(BF16).astype(F32)
        lo = v - hi - mid
        return jnp.where(lane % 3 == 0, hi, jnp.where(lane % 3 == 1, mid, lo)).astype(BF16)

    def stack_q(width, coef, extra=None):
        parts = []
        for x in range(HG):
            cols = [q_heads[x]] + ([extra] if extra is not None else []) + [alibi_cols(x, width, coef)]
            parts.append(jnp.concatenate(cols, axis=1))
        return jnp.concatenate(parts, axis=0)

    q_cmp = stack_q(dh, (16.0 * CMP_STRIDE, 1.0 * CMP_STRIDE, (CMP_BLOCK - 1) / 2))
    sc = _dot_nt(q_cmp, kc_ref[0, 0])
    block_done = jnp.concatenate([cpat_ref[...]] * HG, axis=0) <= t0
    sc = jnp.where(block_done, sc, NEG_INF)
    e = jnp.exp2(sc - jnp.max(sc, axis=1, keepdims=True))
    t_row = t0 + lax.broadcasted_iota(jnp.int32, (rows, 1), 0) % tq
    any_valid = jnp.where(t_row >= CMP_BLOCK - 1, 1.0, 0.0)
    p_cmp = e * (any_valid / jnp.sum(e, axis=1, keepdims=True))
    o_cmp = _dot(p_cmp.astype(BF16), vc_ref[0, 0])[:, :dh]

    p_sum = p_cmp[0:tq]
    for x in range(1, HG):
        p_sum = p_sum + p_cmp[x * tq:(x + 1) * tq]
    p_hi = p_sum.astype(BF16)
    p_lo = (p_sum - p_hi.astype(F32)).astype(BF16)
    ovl = ovl_ref[...]
    p_slc = _dot_nt(ovl, p_hi) + _dot_nt(ovl, p_lo)
    blk = lax.broadcasted_iota(jnp.int32, (n_slc, tq), 0)
    cur = (t0 + lax.broadcasted_iota(jnp.int32, (n_slc, tq), 1)) // SEL_BLOCK
    forced = (blk == 0) | (blk == cur) | (blk == cur - 1)
    score = jnp.where(forced, SEL_FORCE, jnp.where(blk > cur, -SEL_FORCE, p_slc))
    n_tiles = n_slc // 8
    tiles = [score[8 * a:8 * a + 8] for a in range(n_tiles)]
    ranks = [jnp.zeros((8, tq), F32) for _ in range(n_tiles)]
    sub = lax.broadcasted_iota(jnp.int32, (8, tq), 0)
    for kk in range(n_slc):
        sk = score[kk:kk + 1, :]
        for a in range(n_tiles):
            if a < kk // 8:
                before = sk > tiles[a]
            elif a > kk // 8:
                before = sk >= tiles[a]
            else:
                before = jnp.where(sub > kk % 8, jnp.where(sk >= tiles[a], 1.0, 0.0), jnp.where(sk > tiles[a], 1.0, 0.0)) > 0.5
            ranks[a] = ranks[a] + jnp.where(before, 1.0, 0.0)
    rank = jnp.concatenate(ranks, axis=0)
    selected = rank < min(SEL_COUNT, n_slc)
    sel_bias = jnp.where(selected, 0.0, SEL_MASK).T.astype(BF16)
    first_blk = jnp.min(jnp.where(selected & (blk >= blocks_per_chunk), blk, n_slc))
    c_lo = jnp.minimum(jnp.maximum(first_blk // blocks_per_chunk, 1), i)
    q_aug = stack_q(2 * dh, (1.0 * SEL_BLOCK, 1.0), extra=sel_bias)

    causal_bias = jnp.concatenate([tri_ref[0]] * HG, axis=0)
    upper_bias = jnp.concatenate([tri_ref[1]] * HG, axis=0)

    qw_aug = stack_q(dh, (1.0 * SEL_BLOCK, 1.0))
    start_a = pl.multiple_of(jnp.maximum(t0 - 2 * tq, 0), tq)
    start_b = pl.multiple_of(jnp.maximum(t0 - tq, 0), tq)
    start_c = pl.multiple_of(t0, tq)
    s_a = _dot_nt(qw_aug, kw_scr[pl.ds(start_a, tq), :]) + jnp.where(i >= 2, upper_bias, NEG_INF)
    s_b = _dot_nt(qw_aug, kw_scr[pl.ds(start_b, tq), :]) + jnp.where(i >= 1, 0.0, NEG_INF)
    s_c = _dot_nt(qw_aug, kw_scr[pl.ds(start_c, tq), :]) + causal_bias
    m_w = jnp.max(jnp.maximum(jnp.maximum(s_a, s_b), s_c), axis=1, keepdims=True)
    e_a = jnp.exp2(s_a - m_w)
    e_b = jnp.exp2(s_b - m_w)
    e_c = jnp.exp2(s_c - m_w)
    o_win = (_dot(e_a.astype(BF16), vw_scr[pl.ds(start_a, tq), :]) + _dot(e_b.astype(BF16), vw_scr[pl.ds(start_b, tq), :])
             + _dot(e_c.astype(BF16), vw_scr[pl.ds(start_c, tq), :]))

    mrun_scr[...] = jnp.full(mrun_scr.shape, NEG_INF, F32)
    acc_scr[...] = jnp.zeros(acc_scr.shape, F32)

    def scores(c, bias=None, slot=None):
        start = pl.multiple_of(c * tq, tq)
        s = _dot_nt(q_aug, ks_scr[pl.ds(start, tq), :])
        if bias is not None:
            s = s + bias
        s_scr[c if slot is None else slot] = s
        mrun_scr[...] = jnp.maximum(mrun_scr[...], jnp.maximum(s[:, :half], s[:, half:]))

    def weights(c, slot=None):
        start = pl.multiple_of(c * tq, tq)
        s = s_scr[c if slot is None else slot]
        m_b = mrun_scr[...]
        p0 = jnp.exp2(s[:, :half] - m_b)
        p1 = jnp.exp2(s[:, half:] - m_b)
        p = jnp.concatenate([p0, p1], axis=1).astype(BF16)
        acc_scr[...] += _dot(p, vs_scr[pl.ds(start, tq), :])

    def chunk_pairs(lo, hi, fn):
        def pair(t, carry):
            fn(lo + 2 * t)
            fn(lo + 2 * t + 1)
            return carry

        n = hi - lo
        lax.fori_loop(0, n // 2, pair, 0)

        @pl.when(n % 2 == 1)
        def _():
            fn(hi - 1)

    scores(i, bias=causal_bias)
    scores(0, bias=jnp.where(i > 0, 0.0, NEG_INF), slot=spare)
    chunk_pairs(c_lo, i, scores)
    m_row = jnp.max(mrun_scr[...], axis=1, keepdims=True)
    mrun_scr[...] = jnp.broadcast_to(m_row, mrun_scr.shape)

    chunk_pairs(c_lo, i, weights)
    weights(0, slot=spare)
    weights(i)
    gates = jax.nn.sigmoid(gate_ref[0])

    def gate_col(r):
        cols = []
        for x in range(HG):
            per_group = [gate_col0 + (gg * HG + x) * NSA_N_BRANCH + r for gg in range(NSA_KV_GROUPS)]
            col = gates[:, per_group[0]:per_group[0] + 1]
            for gg in range(1, NSA_KV_GROUPS):
                col = jnp.where(g == gg, gates[:, per_group[gg]:per_group[gg] + 1], col)
            cols.append(col)
        return jnp.concatenate(cols, axis=0)

    acc = acc_scr[...]
    o = gate_col(0) * o_cmp + acc[:, :dh] * (gate_col(1) / acc[:, dh:dh + 1])

    o = o + o_win[:, :dh] * (gate_col(2) / o_win[:, dh:dh + 1])

    out_ref[0] = jnp.concatenate([o[x * tq:(x + 1) * tq] for x in range(HG)], axis=1).astype(out_ref.dtype)


def _nsa_attention(act, gates, kc, vc, kconst, overlap_t, *, q_col0, kv_col0, gate_col0, tq):
    B, S, _ = act.shape
    G, HG, dh = NSA_KV_GROUPS, NSA_GROUP_HEADS, NSA_HEAD_DIM
    n_slc = S // SEL_BLOCK
    rows = HG * tq
    assert WINDOW == 2 * tq, "the window branch reads exactly the two chunks before the diagonal one"
    r = np.arange(tq)[:, None]
    l = np.arange(tq)[None, :]
    tri_bias = jnp.asarray(np.stack([np.where(l <= r, 0.0, NEG_INF), np.where(l > r, 0.0, NEG_INF)]), F32)
    cmp_pattern = jnp.asarray(np.arange(S // CMP_STRIDE)[None, :] * CMP_STRIDE + (CMP_BLOCK - 1) - r, jnp.int32)

    def kv_spec(off):
        return pl.BlockSpec((1, S, G * dh), lambda b, g, i: (b, 0, kv_col0 + off))

    def full(a):
        return pl.BlockSpec((1, 1) + a.shape[2:], lambda b, g, i: (b, g, 0, 0))

    return pl.pallas_call(
        functools.partial(_nsa_kernel, tq=tq, seq=S, gate_col0=gate_col0),
        grid=(B, G, S // tq),
        in_specs=[
            pl.BlockSpec((1, tq, HG * dh), lambda b, g, i: (b, i, q_col0 + g)),
            pl.BlockSpec((1, tq, LANES), lambda b, g, i: (b, i, 0)),
            full(kc), full(vc), kv_spec(0), kv_spec(1), kv_spec(2), kv_spec(3),
            pl.BlockSpec(kconst.shape, lambda b, g, i: (0, 0)),
            pl.BlockSpec(overlap_t.shape, lambda b, g, i: (0, 0)),
            pl.BlockSpec(tri_bias.shape, lambda b, g, i: (0, 0, 0)),
            pl.BlockSpec(cmp_pattern.shape, lambda b, g, i: (0, 0)),
        ],
        out_specs=pl.BlockSpec((1, tq, HG * dh), lambda b, g, i: (b, i, g)),
        out_shape=jax.ShapeDtypeStruct((B, S, NSA_WIDTH), BF16),
        scratch_shapes=[pltpu.VMEM((S, dh + n_slc + 2 * dh), BF16), pltpu.VMEM((S, 2 * dh), BF16),
                        pltpu.VMEM((S, 2 * dh), BF16), pltpu.VMEM((S, 2 * dh), BF16),
                        pltpu.VMEM((S // tq + 1, rows, tq), F32), pltpu.VMEM((rows, tq // 2), F32),
                        pltpu.VMEM((rows, 2 * dh), F32)],
        compiler_params=_cparams("parallel", "parallel", "arbitrary"),
        name="nsa_attn",
    )(act, gates, kc, vc, act, act, act, act, kconst, overlap_t, tri_bias, cmp_pattern)


def _mix_out_kernel(x_ref, hm_ref, hn_ref, gm_ref, gn_ref, wm_ref, wn_ref, wo_ref, o_ref):
    y_ml = _dot(hm_ref[...], wm_ref[...])
    y_ns = _dot(hn_ref[...], wn_ref[...])
    mix = jax.nn.sigmoid(gm_ref[...].astype(F32)) * y_ml + jax.nn.sigmoid(gn_ref[...].astype(F32)) * y_ns
    o_ref[...] = x_ref[...] + _dot(mix.astype(BF16), wo_ref[...])


def _mix_out(x2d, h_ml, h_ns, act2d, wm, wn, wo, *, gate_col0, tm):
    n, d = x2d.shape

    def res(w):
        return pl.BlockSpec(w.shape, lambda i: (0, 0))

    return pl.pallas_call(
        _mix_out_kernel,
        grid=(n // tm,),
        in_specs=[
            pl.BlockSpec((tm, d), lambda i: (i, 0)),
            pl.BlockSpec((tm, h_ml.shape[1]), lambda i: (i, 0)),
            pl.BlockSpec((tm, h_ns.shape[1]), lambda i: (i, 0)),
            pl.BlockSpec((tm, d), lambda i: (i, gate_col0)),
            pl.BlockSpec((tm, d), lambda i: (i, gate_col0 + 1)),
            res(wm), res(wn), res(wo),
        ],
        out_specs=pl.BlockSpec((tm, d), lambda i: (i, 0)),
        out_shape=jax.ShapeDtypeStruct((n, d), F32),
        compiler_params=_cparams("parallel"),
        name="mix_out",
    )(x2d, h_ml, h_ns, act2d, act2d, wm, wn, wo)


def _xattn_kernel(x_ref, g_ref, wq_ref, k_ref, v_ref, wo_ref, o_ref):
    x = x_ref[...]
    d = x.shape[1]
    dh = d // XA_HEADS
    h = _rms(x, g_ref[...]).astype(BF16)
    q = (_dot(h, wq_ref[...]) * (dh ** -0.5)).astype(BF16)
    k = k_ref[0]
    v = v_ref[0]
    outs = []
    for hd in range(XA_HEADS):
        sl = slice(hd * dh, (hd + 1) * dh)
        s = _dot_nt(q[:, sl], k[:, sl])
        e = jnp.exp(s - jnp.max(s, axis=1, keepdims=True))
        p = e / jnp.sum(e, axis=1, keepdims=True)
        outs.append(_dot(p.astype(BF16), v[:, sl]))
    o = jnp.concatenate(outs, axis=1).astype(BF16)
    o_ref[...] = x + _dot(o, wo_ref[...])


def _xattn(x2d, g, wq, kv, wo, *, seq, tm):
    n, d = x2d.shape
    n_mem = kv.shape[1]
    tiles_per_batch = seq // tm
    return pl.pallas_call(
        _xattn_kernel,
        grid=(n // tm,),
        in_specs=[
            pl.BlockSpec((tm, d), lambda i: (i, 0)),
            pl.BlockSpec((1, d), lambda i: (0, 0)),
            pl.BlockSpec(wq.shape, lambda i: (0, 0)),
            pl.BlockSpec((1, n_mem, d), lambda i: (i // tiles_per_batch, 0, 0)),
            pl.BlockSpec((1, n_mem, d), lambda i: (i // tiles_per_batch, 0, 1)),
            pl.BlockSpec(wo.shape, lambda i: (0, 0)),
        ],
        out_specs=pl.BlockSpec((tm, d), lambda i: (i, 0)),
        out_shape=jax.ShapeDtypeStruct((n, d), F32),
        compiler_params=_cparams("parallel"),
        name="xattn",
    )(x2d, g.reshape(1, d), wq, kv, kv, wo)


def _route(logits, bg, be):
    G, EG = MOE_GROUPS, MOE_EXPERTS_PER_GROUP
    tm = logits.shape[1]
    lg = logits[0:G] + bg
    eg = jnp.exp(lg - jnp.max(lg, axis=0, keepdims=True))
    pg = eg / jnp.sum(eg, axis=0, keepdims=True)
    g_val = jnp.max(pg, axis=0, keepdims=True)
    g_row = lax.broadcasted_iota(jnp.int32, (G, tm), 0)
    g_idx = jnp.min(jnp.where(pg == g_val, g_row, G), axis=0, keepdims=True)
    el = logits[8:8 + G * EG] + be
    e_in = jnp.zeros((EG, tm), F32)
    for gg in range(G):
        e_in = jnp.where(g_idx == gg, el[gg * EG:(gg + 1) * EG], e_in)
    e_row = lax.broadcasted_iota(jnp.int32, (EG, tm), 0)
    v1 = jnp.max(e_in, axis=0, keepdims=True)
    i1 = jnp.min(jnp.where(e_in == v1, e_row, EG), axis=0, keepdims=True)
    rest = jnp.where(e_row == i1, -jnp.inf, e_in)
    v2 = jnp.max(rest, axis=0, keepdims=True)
    i2 = jnp.min(jnp.where(rest == v2, e_row, EG), axis=0, keepdims=True)
    e2 = jnp.exp(v2 - v1)
    c1 = g_val / (1.0 + e2)
    c2 = g_val * e2 / (1.0 + e2)
    ex = lax.broadcasted_iota(jnp.int32, (G * EG, tm), 0)
    base = g_idx * EG
    return jnp.where(ex == base + i1, c1, 0.0) + jnp.where(ex == base + i2, c2, 0.0), g_idx


def _moe_kernel(x_ref, g_ref, wr_ref, bg_ref, be_ref, before_ref, w1_ref, w3_ref, w2_ref, gf_ref, o_ref,
                h_scr, slot_scr, wt_scr, acc_scr, *, caps):
    G, EG = MOE_GROUPS, MOE_EXPERTS_PER_GROUP
    grp = pl.program_id(1)
    tm = x_ref.shape[0]

    @pl.when(grp == 0)
    def _():
        h = _rms(x_ref[...], g_ref[...])
        h_hi = h.astype(BF16)
        h_scr[...] = h_hi
        h_lo = (h - h_hi.astype(F32)).astype(BF16)
        w_hi = wr_ref[...].astype(BF16)
        w_lo = (wr_ref[...] - w_hi.astype(F32)).astype(BF16)
        logits = _dot_nt(w_hi, h_hi) + _dot_nt(w_hi, h_lo) + _dot_nt(w_lo, h_hi)
        wt, g_idx = _route(logits, bg_ref[...], be_ref[...])
        member = jnp.where(lax.broadcasted_iota(jnp.int32, (8, tm), 0) == g_idx, 1.0, 0.0)
        prefix = _dot(member.astype(BF16), before_ref[...])
        slots = jnp.where(member > 0.0, prefix, -1.0)
        zeros = jnp.zeros((8 - EG, tm), F32)
        for gg in range(G):
            slot_scr[gg] = jnp.broadcast_to(slots[gg:gg + 1], (8, tm))
            wt_scr[gg] = jnp.concatenate([wt[gg * EG:(gg + 1) * EG], zeros], axis=0)
        acc_scr[...] = jnp.zeros_like(acc_scr)

    slot = slot_scr[grp][0:1, :]
    n_rows = jnp.max(slot).astype(jnp.int32) + 1
    wts = wt_scr[grp]

    def one_pass(cap, first_row):
        row = lax.broadcasted_iota(jnp.int32, (cap, tm), 0).astype(F32)
        pick = jnp.where(row == slot - first_row.astype(F32), 1.0, 0.0)
        pick_b = pick.astype(BF16)
        hsub = _dot(pick_b, h_scr[...]).astype(BF16)
        y = jnp.zeros((cap, x_ref.shape[1]), F32)
        for e in range(EG):
            w_e = jnp.sum(pick * wts[e:e + 1, :], axis=1, keepdims=True)
            a = _dot(hsub, w1_ref[e])
            a = a * jax.nn.sigmoid(a) * _dot(hsub, w3_ref[e]) * w_e
            y = y + _dot(a.astype(BF16), w2_ref[e])
        acc_scr[...] += _dot_tn(pick_b, y.astype(BF16))

    *small_caps, cap_long = caps
    lower = 0
    for cap in small_caps:
        @pl.when((n_rows > lower) & (n_rows <= cap))
        def _(cap=cap):
            one_pass(cap, jnp.int32(0))
        lower = cap

    @pl.when(n_rows > lower)
    def _():
        def body(p, carry):
            one_pass(cap_long, p * cap_long)
            return carry

        lax.fori_loop(0, (n_rows + cap_long - 1) // cap_long, body, 0)

    @pl.when(grp == pl.num_programs(1) - 1)
    def _():
        o_ref[...] = _rms(x_ref[...] + acc_scr[...], gf_ref[...])


def _moe(x2d, g, wr, bg, be, w1, w3, w2, g_final, *, tm, caps):
    n, d = x2d.shape
    E, _, hid = w1.shape
    G, EG = MOE_GROUPS, MOE_EXPERTS_PER_GROUP
    before = jnp.asarray(np.arange(tm)[:, None] < np.arange(tm)[None, :], BF16)
    once = pl.Buffered(1)

    def const(shape):
        return pl.BlockSpec(shape, lambda i, e: (0,) * len(shape), pipeline_mode=once)

    return pl.pallas_call(
        functools.partial(_moe_kernel, caps=caps),
        grid=(n // tm, G),
        in_specs=[
            pl.BlockSpec((tm, d), lambda i, e: (i, 0), pipeline_mode=once),
            const((1, d)), const(wr.shape), const(bg.shape), const(be.shape), const((tm, tm)),
            pl.BlockSpec((EG, d, hid), lambda i, e: (e, 0, 0)),
            pl.BlockSpec((EG, d, hid), lambda i, e: (e, 0, 0)),
            pl.BlockSpec((EG, hid, d), lambda i, e: (e, 0, 0)),
            const((1, d)),
        ],
        out_specs=pl.BlockSpec((tm, d), lambda i, e: (i, 0)),
        out_shape=jax.ShapeDtypeStruct((n, d), F32),
        scratch_shapes=[pltpu.VMEM((tm, d), BF16), pltpu.VMEM((G, 8, tm), F32), pltpu.VMEM((G, 8, tm), F32),
                        pltpu.VMEM((tm, d), F32)],
        compiler_params=_cparams("parallel", "arbitrary"),
        name="moe",
    )(x2d, g.reshape(1, d), wr, bg, be, before, w1, w3, w2, g_final.reshape(1, d))


def _overlap_t(seq):
    n_cmp_pad = seq // CMP_STRIDE
    cs = np.arange(n_cmp_pad)[None, :] * CMP_STRIDE
    ss = np.arange(seq // SEL_BLOCK)[:, None] * SEL_BLOCK
    ov = np.clip(np.minimum(cs + CMP_BLOCK, ss + SEL_BLOCK) - np.maximum(cs, ss), 0, None) / CMP_BLOCK
    ov[:, n_cmp_pad - 1] = 0.0
    return jnp.asarray(ov, BF16)


def _layer(x, mem, norm_mix, w_in, conv_qk, b_igate, b_fgate, mlstm_norm, cmp_pos_k, cmp_pos_v, cmp_k_w1, cmp_k_w2,
           cmp_v_w1, cmp_v_w2, w_br_mlstm, w_br_nsa, w_mix_out, norm_xattn, norm_mem, xa_wq, xa_wkv, xa_wo, norm_ffn,
           router_group_w, router_group_b, router_expert_w, router_expert_b, moe_w1, moe_w3, moe_w2, norm_final):
    B, S, D = x.shape
    N = B * S
    H, d = ML_HEADS, ML_HEAD_DIM
    G, HG, dh = NSA_KV_GROUPS, NSA_GROUP_HEADS, NSA_HEAD_DIM
    x2d = x.reshape(N, D)

    o_mlqkvo = 0
    o_mlif = 4 * ML_WIDTH
    o_nsq = o_mlif + 2 * H
    o_kv = o_nsq + NSA_WIDTH
    o_nsg = o_kv + 6 * NSA_KV_WIDTH
    o_merge = o_nsg + NSA_N_BRANCH * NSA_HEADS
    o_kv_sel = o_kv + 2 * NSA_KV_WIDTH
    w_act = jnp.concatenate([w_in[:, o_merge:o_merge + 2 * D], w_in[:, o_mlqkvo:o_mlif], w_in[:, o_nsq:o_kv],
                             w_in[:, o_kv_sel:o_nsg]], axis=1).astype(BF16)
    n_small = 2 * H + NSA_N_BRANCH * NSA_HEADS
    w_small = jnp.concatenate([w_in[:, o_mlif:o_nsq], w_in[:, o_nsg:o_merge], jnp.zeros((D, LANES - n_small), F32),
                               w_in[:, o_kv:o_kv_sel]], axis=1).astype(BF16)
    act, small, cmp_kv = _in_proj(x2d, norm_mix, w_act, w_small, tm=1024, tn=2560)
    act3 = act.reshape(B, S, act.shape[1])
    c_ml = 2 * D
    c_nsq = c_ml + 4 * ML_WIDTH
    c_kv = c_nsq + NSA_WIDTH

    small3 = small.reshape(B, S, small.shape[1])
    gate_bias = jnp.zeros((1, LANES), F32).at[0, 0:H].set(b_igate).at[0, H:2 * H].set(b_fgate)
    conv_w = jnp.concatenate([conv_qk[:, :ML_WIDTH].reshape(ML_CONV, H, d), conv_qk[:, ML_WIDTH:].reshape(ML_CONV, H, d)],
                             axis=-1).transpose(1, 0, 2)
    h_ml = _mlstm(act3, small3, gate_bias, conv_w, mlstm_norm.reshape(H, 1, d), col0=c_ml // ML_WIDTH, chunk=512)

    pos = jnp.tile(jnp.stack([cmp_pos_k, cmp_pos_v])[:, :, None, :], (1, 1, 1, G))
    w1c = jnp.stack([cmp_k_w1, cmp_v_w1]).reshape(2, CMP_BLOCK, dh, -1)
    w1c = jnp.einsum('arld,gh->arghld', w1c, jnp.eye(G, dtype=F32))
    w1c = w1c.reshape(2, CMP_BLOCK, G, G * dh, -1).astype(BF16)
    w2c = jnp.stack([cmp_k_w2, cmp_v_w2]).astype(BF16)
    kvc = _compress(cmp_kv.reshape(B, S, cmp_kv.shape[1]), pos, w1c, w2c, col0=0)
    n_slc = S // SEL_BLOCK
    tok = np.arange(S)
    kconst = np.zeros((S, n_slc + 2 * dh), np.float32)
    kconst[tok, tok // SEL_BLOCK] = 1.0
    kconst[:, n_slc:n_slc + 3] = (tok // SEL_BLOCK)[:, None]
    kconst[:, n_slc + 3:n_slc + 6] = (tok % SEL_BLOCK)[:, None]
    h_ns = _nsa_attention(act3, small3, kvc[0], kvc[1], jnp.asarray(kconst, BF16), _overlap_t(S),
                          q_col0=c_nsq // (HG * dh), kv_col0=c_kv // NSA_KV_WIDTH,
                          gate_col0=2 * H, tq=256)

    x1 = _mix_out(x2d, h_ml.reshape(N, ML_WIDTH), h_ns.reshape(N, NSA_WIDTH), act, w_br_mlstm.astype(BF16),
                  w_br_nsa.astype(BF16), w_mix_out.astype(BF16), gate_col0=0, tm=512)

    n_mem = mem.shape[1]
    kv_mem = _norm_matmul(mem.reshape(B * n_mem, D), norm_mem, xa_wkv.astype(BF16), BF16, tm=B * n_mem, tn=512)
    x2 = _xattn(x1, norm_xattn, xa_wq.astype(BF16), kv_mem.reshape(B, n_mem, 2 * D), xa_wo.astype(BF16), seq=S, tm=1024)

    wr = jnp.zeros((ROUTER_ROWS, D), F32)
    wr = wr.at[0:MOE_GROUPS].set(router_group_w.T).at[8:8 + MOE_EXPERTS].set(router_expert_w.T)
    return _moe(x2, norm_ffn, wr, router_group_b.reshape(MOE_GROUPS, 1), router_expert_b.reshape(MOE_EXPERTS, 1),
                moe_w1.astype(BF16), moe_w3.astype(BF16), moe_w2.astype(BF16), norm_final, tm=1024, caps=(256, 304, 352)).reshape(B, S, D)


def kernel(x, mem, norm_mix, w_in, conv_qk, b_igate, b_fgate, mlstm_norm, cmp_pos_k, cmp_pos_v, cmp_k_w1, cmp_k_w2, cmp_v_w1, cmp_v_w2, w_br_mlstm, w_br_nsa, w_mix_out, norm_xattn, norm_mem, xa_wq, xa_wkv, xa_wo, norm_ffn, router_group_w, router_group_b, router_expert_w, router_expert_b, moe_w1, moe_w3, moe_w2, norm_final):
    depth = w_in.shape[0]
    assert depth == 1, "the fused final norm assumes a single layer"
    layer = 0
    return _layer(x, mem, norm_mix[layer], w_in[layer], conv_qk[layer], b_igate[layer], b_fgate[layer], mlstm_norm[layer],
                  cmp_pos_k[layer], cmp_pos_v[layer], cmp_k_w1[layer], cmp_k_w2[layer], cmp_v_w1[layer], cmp_v_w2[layer],
                  w_br_mlstm[layer], w_br_nsa[layer], w_mix_out[layer], norm_xattn[layer], norm_mem[layer], xa_wq[layer],
                  xa_wkv[layer], xa_wo[layer], norm_ffn[layer], router_group_w[layer], router_group_b[layer],
                  router_expert_w[layer], router_expert_b[layer], moe_w1[layer], moe_w3[layer], moe_w2[layer], norm_final)
```

```python
import functools

import numpy as np
import jax
import jax.numpy as jnp
from jax import lax
from jax.experimental import pallas as pl
from jax.experimental.pallas import tpu as pltpu

F32 = jnp.float32
BF16 = jnp.bfloat16

ML_HEADS = 4
ML_HEAD_DIM = 128
ML_WIDTH = ML_HEADS * ML_HEAD_DIM
ML_CONV = 4
NSA_HEADS = 8
NSA_KV_GROUPS = 2
NSA_HEAD_DIM = 64
NSA_GROUP_HEADS = NSA_HEADS // NSA_KV_GROUPS
NSA_WIDTH = NSA_HEADS * NSA_HEAD_DIM
NSA_KV_WIDTH = NSA_KV_GROUPS * NSA_HEAD_DIM
NSA_N_BRANCH = 3
CMP_BLOCK = 32
CMP_STRIDE = 16
SEL_BLOCK = 64
SEL_COUNT = 16
SEL_FORCE = 1e4
WINDOW = 512
XA_HEADS = 4
MOE_GROUPS = 4
MOE_EXPERTS_PER_GROUP = 4
MOE_EXPERTS = MOE_GROUPS * MOE_EXPERTS_PER_GROUP
RMS_EPS = 1e-6
NEG_INF = -1e30
SEL_MASK = -float(2 ** 30)
LOG2E = 1.4426950408889634

LANES = 128
VMEM_LIMIT_BYTES = 56 * 1024 * 1024
ROUTER_ROWS = 32
NSA_PEEL = 2


def _cparams(*sem):
    return pltpu.CompilerParams(dimension_semantics=sem, vmem_limit_bytes=VMEM_LIMIT_BYTES)


def _rms(x, g):
    return x * lax.rsqrt(jnp.mean(x * x, axis=-1, keepdims=True) + RMS_EPS) * g


def _dot(a, b):
    return jnp.dot(a, b, preferred_element_type=F32)


def _dot_nt(a, b):
    return lax.dot_general(a, b, (((1,), (1,)), ((), ())), preferred_element_type=F32)


def _dot_tn(a, b):
    return lax.dot_general(a, b, (((0,), (0,)), ((), ())), preferred_element_type=F32)


def _norm_matmul_kernel(x_ref, g_ref, w_ref, o_ref, h_ref):
    @pl.when(pl.program_id(1) == 0)
    def _():
        h_ref[...] = _rms(x_ref[...], g_ref[...]).astype(BF16)

    o_ref[...] = _dot(h_ref[...], w_ref[...]).astype(o_ref.dtype)


def _norm_matmul(x2d, g, w, out_dtype, tm, tn):
    m, d = x2d.shape
    n = w.shape[1]
    return pl.pallas_call(
        _norm_matmul_kernel,
        grid=(m // tm, n // tn),
        in_specs=[
            pl.BlockSpec((tm, d), lambda i, j: (i, 0)),
            pl.BlockSpec((1, d), lambda i, j: (0, 0)),
            pl.BlockSpec((d, tn), lambda i, j: (0, j)),
        ],
        out_specs=pl.BlockSpec((tm, tn), lambda i, j: (i, j)),
        out_shape=jax.ShapeDtypeStruct((m, n), out_dtype),
        scratch_shapes=[pltpu.VMEM((tm, d), BF16)],
        compiler_params=_cparams("parallel", "arbitrary"),
        name="norm_matmul",
    )(x2d, g.reshape(1, d), w)


def _in_proj_kernel(x_ref, g_ref, w_ref, ws_ref, o_ref, og_ref, oc_ref, h_ref):
    @pl.when(pl.program_id(1) == 0)
    def _():
        h_ref[...] = _rms(x_ref[...], g_ref[...]).astype(BF16)
        small = _dot(h_ref[...], ws_ref[...])
        og_ref[...] = small[:, :LANES]
        oc_ref[...] = small[:, LANES:]

    o_ref[...] = _dot(h_ref[...], w_ref[...]).astype(o_ref.dtype)


def _in_proj(x2d, g, w, w_small, tm, tn):
    m, d = x2d.shape
    n = w.shape[1]
    ns = w_small.shape[1]
    return pl.pallas_call(
        _in_proj_kernel,
        grid=(m // tm, n // tn),
        in_specs=[
            pl.BlockSpec((tm, d), lambda i, j: (i, 0)),
            pl.BlockSpec((1, d), lambda i, j: (0, 0)),
            pl.BlockSpec((d, tn), lambda i, j: (0, j)),
            pl.BlockSpec((d, ns), lambda i, j: (0, 0)),
        ],
        out_specs=[pl.BlockSpec((tm, tn), lambda i, j: (i, j)), pl.BlockSpec((tm, LANES), lambda i, j: (i, 0)),
                   pl.BlockSpec((tm, ns - LANES), lambda i, j: (i, 0))],
        out_shape=[jax.ShapeDtypeStruct((m, n), BF16), jax.ShapeDtypeStruct((m, LANES), F32),
                   jax.ShapeDtypeStruct((m, ns - LANES), F32)],
        scratch_shapes=[pltpu.VMEM((tm, d), BF16)],
        compiler_params=_cparams("parallel", "arbitrary"),
        name="in_proj",
    )(x2d, g.reshape(1, d), w, w_small)


def _mlstm_kernel(q_ref, k_ref, v_ref, o_ref, qt_ref, kt_ref, gate_ref, gbias_ref,
                  cw_ref, ng_ref, upper_ref, out_ref, c_scr, n_scr, m_scr, *, chunk):
    L, d = chunk, ML_HEAD_DIM
    c = pl.program_id(1)

    @pl.when(c == 0)
    def _():
        c_scr[...] = jnp.zeros_like(c_scr)
        n_scr[...] = jnp.zeros_like(n_scr)
        m_scr[...] = jnp.zeros_like(m_scr)

    t_idx = lax.broadcasted_iota(jnp.int32, (L, L), 0)
    s_idx = lax.broadcasted_iota(jnp.int32, (L, L), 1)
    tri = s_idx <= t_idx

    row8 = lax.broadcasted_iota(jnp.int32, (8, d), 0)

    def conv_silu(x, tail, w):
        x = x.astype(F32)
        tail = jnp.where(c > 0, tail.astype(F32), 0.0)
        y = x * w[ML_CONV - 1:ML_CONV, :]
        for j in range(ML_CONV - 1):
            s = ML_CONV - 1 - j
            r = pltpu.roll(x, s, 0)
            head = jnp.where(row8 < s, pltpu.roll(tail, s, 0), r[:8])
            y = y + jnp.concatenate([head, r[8:]], axis=0) * w[j:j + 1, :]
        return y * (0.5 * jnp.tanh(0.5 * y) + 0.5)

    g_cols = gate_ref[0] + gbias_ref[...]
    g_rows = g_cols.T
    f_rows = jax.nn.log_sigmoid(g_rows[ML_HEADS:2 * ML_HEADS, :])
    f_hi = f_rows.astype(BF16)
    f_mid = (f_rows - f_hi.astype(F32)).astype(BF16)
    f_lo = (f_rows - f_hi.astype(F32) - f_mid.astype(F32)).astype(BF16)
    pad = jnp.zeros((16 - 3 * ML_HEADS, L), BF16)
    parts = _dot(jnp.concatenate([f_hi, f_mid, f_lo, pad], axis=0), upper_ref[...])
    b_rows = parts[0:ML_HEADS] + parts[ML_HEADS:2 * ML_HEADS] + parts[2 * ML_HEADS:3 * ML_HEADS]
    b_cols = jnp.concatenate([b_rows, jnp.zeros((8 - ML_HEADS, L), F32)], axis=0).T

    outs = []
    for h in range(ML_HEADS):
        sl = slice(h * d, (h + 1) * d)
        cw = cw_ref[h]
        q = conv_silu(q_ref[0, :, sl], qt_ref[0, :, sl], cw[:, :d])
        k = conv_silu(k_ref[0, :, sl], kt_ref[0, :, sl], cw[:, d:]) * (d ** -0.5)
        v = v_ref[0, :, sl]
        qb = q.astype(BF16)
        kb = k.astype(BF16)

        i_col = g_cols[:, h:h + 1]
        i_row = g_rows[h:h + 1, :]
        f_row = f_rows[h:h + 1, :]
        b_col = b_cols[:, h:h + 1]
        b_row = b_rows[h:h + 1, :]
        dlog = jnp.where(tri, b_col + (i_row - b_row), NEG_INF)
        m_prev = m_scr[h]
        inter = b_col + m_prev
        mt = jnp.maximum(jnp.max(dlog, axis=1, keepdims=True), inter)
        w_intra = jnp.exp(dlog - mt)
        w_inter = jnp.exp(inter - mt)

        s = _dot_nt(qb, kb) * w_intra
        cmat = c_scr[h]
        nvec = n_scr[h]
        num = _dot(s.astype(BF16), v) + w_inter * _dot(qb, cmat.astype(BF16))
        den = jnp.sum(s, axis=1, keepdims=True) + w_inter * jnp.sum(q * nvec, axis=1, keepdims=True)
        hc = num / jnp.maximum(jnp.abs(den), jnp.exp(-mt))

        bl = jnp.sum(f_row, axis=1, keepdims=True)
        logw = bl - b_col + i_col
        m_new = jnp.maximum(bl + m_prev, jnp.max(logw, axis=0, keepdims=True))
        decay = jnp.exp(bl + m_prev - m_new)
        kw = k * jnp.exp(logw - m_new)
        c_scr[h] = decay * cmat + _dot_tn(kw.astype(BF16), v)
        n_scr[h] = decay * nvec + jnp.sum(kw, axis=0, keepdims=True)
        m_scr[h] = m_new

        outs.append(jax.nn.sigmoid(o_ref[0, :, sl].astype(F32)) * _rms(hc, ng_ref[h]))
    out_ref[0] = jnp.concatenate(outs, axis=1).astype(out_ref.dtype)


def _mlstm(act, gates, gate_bias, conv_w, norm_g, *, col0, chunk):
    B, S, _ = act.shape
    H, d, L = ML_HEADS, ML_HEAD_DIM, chunk
    tail_blocks = L // 8
    upper = jnp.asarray(np.triu(np.ones((L, L))), BF16)

    def blk(off):
        return pl.BlockSpec((1, L, H * d), lambda b, c: (b, c, col0 + off))

    def tail(off):
        return pl.BlockSpec((1, 8, H * d), lambda b, c: (b, jnp.maximum(c * tail_blocks - 1, 0), col0 + off))

    def const(a):
        return pl.BlockSpec(a.shape, lambda b, c: (0,) * a.ndim)

    return pl.pallas_call(
        functools.partial(_mlstm_kernel, chunk=L),
        grid=(B, S // L),
        in_specs=[
            blk(0), blk(1), blk(2), blk(3), tail(0), tail(1),
            pl.BlockSpec((1, L, LANES), lambda b, c: (b, c, 0)),
            const(gate_bias), const(conv_w), const(norm_g), const(upper),
        ],
        out_specs=pl.BlockSpec((1, L, H * d), lambda b, c: (b, c, 0)),
        out_shape=jax.ShapeDtypeStruct((B, S, H * d), BF16),
        scratch_shapes=[pltpu.VMEM((H, d, d), F32), pltpu.VMEM((H, 1, d), F32), pltpu.VMEM((H, 1, 1), F32)],
        compiler_params=_cparams("parallel", "arbitrary"),
        name="mlstm",
    )(act, act, act, act, act, act, gates, gate_bias, conv_w, norm_g, upper)


def _compress_kernel(x_ref, pos_ref, w1_ref, w2_ref, o_ref):
    n = x_ref.shape[1] // CMP_STRIDE
    groups = o_ref.shape[2]
    for g in range(groups):
        top = jnp.zeros((n, w1_ref.shape[-1]), F32)
        bot = jnp.zeros((n, w1_ref.shape[-1]), F32)
        for r in range(CMP_STRIDE):
            x = x_ref[0, pl.ds(r, n, stride=CMP_STRIDE), :]
            top = top + _dot((x + pos_ref[0, r]).astype(BF16), w1_ref[0, r, g])
            bot = bot + _dot((x + pos_ref[0, CMP_STRIDE + r]).astype(BF16), w1_ref[0, CMP_STRIDE + r, g])
        pre = top + pltpu.roll(bot, n - 1, 0)
        hid = jax.nn.gelu(pre, approximate=True)
        out = _dot(hid.astype(BF16), w2_ref[0])
        row = lax.broadcasted_iota(jnp.int32, out.shape, 0)
        out = jnp.where(row < n - 1, out, 0.0)
        j = lax.broadcasted_iota(jnp.int32, out.shape, 0)
        lane = lax.broadcasted_iota(jnp.int32, out.shape, 1)
        cols = jnp.where(lane < 3, j // 16, jnp.where(lane < 6, j % 16, jnp.where(lane < 9, 1, 0))).astype(F32)
        cols = jnp.where(pl.program_id(0) == 0, cols, jnp.where(lane == 0, 1.0, 0.0))
        o_ref[0, 0, g] = jnp.concatenate([out, cols], axis=1).astype(o_ref.dtype)


def _compress(gates, pos, w1, w2, *, col0):
    B, S, _ = gates.shape
    G, width = w1.shape[2], w1.shape[3]
    hidden = w1.shape[-1]
    dh = w2.shape[-1]
    n = S // CMP_STRIDE
    return pl.pallas_call(
        _compress_kernel,
        grid=(2, B),
        in_specs=[
            pl.BlockSpec((1, S, width), lambda a, b: (b, 0, col0 + a)),
            pl.BlockSpec((1,) + pos.shape[1:], lambda a, b: (a, 0, 0, 0)),
            pl.BlockSpec((1,) + w1.shape[1:], lambda a, b: (a, 0, 0, 0, 0)),
            pl.BlockSpec((1, hidden, dh), lambda a, b: (a, 0, 0)),
        ],
        out_specs=pl.BlockSpec((1, 1, G, n, 2 * dh), lambda a, b: (a, b, 0, 0, 0)),
        out_shape=jax.ShapeDtypeStruct((2, B, G, n, 2 * dh), BF16),
        compiler_params=_cparams("parallel", "parallel"),
        name="nsa_compress",
    )(gates, pos, w1, w2)


def _nsa_kernel(q_ref, gate_ref, kc_ref, vc_ref, ksl_ref, vsl_ref, kwn_ref, vwn_ref, kconst_ref, ovl_ref, tri_ref, cpat_ref,
                out_ref, ks_scr, vs_scr, kw_scr, vw_scr, s_scr, mrun_scr, acc_scr, *, tq, seq, gate_col0):
    HG, dh = NSA_GROUP_HEADS, NSA_HEAD_DIM
    g = pl.program_id(1)
    i = pl.program_id(2)
    t0 = i * tq
    rows = HG * tq
    n_cmp_pad = kc_ref.shape[2]
    n_slc = seq // SEL_BLOCK
    half = tq // 2
    blocks_per_chunk = tq // SEL_BLOCK
    spare = seq // tq
    PEEL = NSA_PEEL

    def assemble(lo):
        ks_scr[...] = jnp.concatenate([ksl_ref[0][:, lo:lo + dh], kconst_ref[...]], axis=1)
        kw_scr[...] = jnp.concatenate([kwn_ref[0][:, lo:lo + dh], kconst_ref[:, n_slc:n_slc + dh]], axis=1)
        ones_col = jnp.where(lax.broadcasted_iota(jnp.int32, (seq, dh), 1) == 0, 1.0, 0.0).astype(BF16)
        vs_scr[...] = jnp.concatenate([vsl_ref[0][:, lo:lo + dh], ones_col], axis=1)
        vw_scr[...] = jnp.concatenate([vwn_ref[0][:, lo:lo + dh], ones_col], axis=1)

    for gg in range(NSA_KV_GROUPS):
        @pl.when((i == 0) & (g == gg))
        def _(gg=gg):
            assemble(gg * dh)

    q_all = q_ref[0]
    q_heads = [(q_all[:, x * dh:(x + 1) * dh].astype(F32) * (dh ** -0.5 * LOG2E)).astype(BF16) for x in range(HG)]

    def head_slope(x):
        sl = jnp.float32(0.0)
        for hh in range(NSA_HEADS):
            sl = jnp.where(g * HG + x == hh, 2.0 ** (-8.0 * (hh + 1) / NSA_HEADS), sl)
        return sl

    slopes = [head_slope(x) for x in range(HG)]

    def alibi_cols(x, width, coef):
        lane = lax.broadcasted_iota(jnp.int32, (tq, width), 1)
        v = jnp.zeros((tq, width), F32)
        for c, val in enumerate(coef):
            v = jnp.where(lane // 3 == c, slopes[x] * (val * LOG2E), v)
        hi = v.astype(BF16).astype(F32)
        mid = (v - hi).astype(BF16).astype(F32)
        lo = v - hi - mid
        return jnp.where(lane % 3 == 0, hi, jnp.where(lane % 3 == 1, mid, lo)).astype(BF16)

    def stack_q(width, coef, extra=None):
        parts = []
        for x in range(HG):
            cols = [q_heads[x]] + ([extra] if extra is not None else []) + [alibi_cols(x, width, coef)]
            parts.append(jnp.concatenate(cols, axis=1))
        return jnp.concatenate(parts, axis=0)

    q_cmp = stack_q(dh, (16.0 * CMP_STRIDE, 1.0 * CMP_STRIDE, (CMP_BLOCK - 1) / 2))
    sc = _dot_nt(q_cmp, kc_ref[0, 0])
    block_done = jnp.concatenate([cpat_ref[...]] * HG, axis=0) <= t0
    sc = jnp.where(block_done, sc, NEG_INF)
    e = jnp.exp2(sc - jnp.max(sc, axis=1, keepdims=True))
    t_row = t0 + lax.broadcasted_iota(jnp.int32, (rows, 1), 0) % tq
    any_valid = jnp.where(t_row >= CMP_BLOCK - 1, 1.0, 0.0)
    p_cmp = e * (any_valid / jnp.sum(e, axis=1, keepdims=True))
    o_cmp = _dot(p_cmp.astype(BF16), vc_ref[0, 0])[:, :dh]

    p_sum = p_cmp[0:tq]
    for x in range(1, HG):
        p_sum = p_sum + p_cmp[x * tq:(x + 1) * tq]
    p_hi = p_sum.astype(BF16)
    p_lo = (p_sum - p_hi.astype(F32)).astype(BF16)
    ovl = ovl_ref[...]
    p_slc = _dot_nt(ovl, p_hi) + _dot_nt(ovl, p_lo)
    blk = lax.broadcasted_iota(jnp.int32, (n_slc, tq), 0)
    cur = (t0 + lax.broadcasted_iota(jnp.int32, (n_slc, tq), 1)) // SEL_BLOCK
    forced = (blk == 0) | (blk == cur) | (blk == cur - 1)
    score = jnp.where(forced, SEL_FORCE, jnp.where(blk > cur, -SEL_FORCE, p_slc))
    n_tiles = n_slc // 8
    tiles = [score[8 * a:8 * a + 8] for a in range(n_tiles)]
    ranks = [jnp.zeros((8, tq), F32) for _ in range(n_tiles)]
    sub = lax.broadcasted_iota(jnp.int32, (8, tq), 0)
    for kk in range(n_slc):
        sk = score[kk:kk + 1, :]
        for a in range(n_tiles):
            if a < kk // 8:
                before = sk > tiles[a]
            elif a > kk // 8:
                before = sk >= tiles[a]
            else:
                before = jnp.where(sub > kk % 8, jnp.where(sk >= tiles[a], 1.0, 0.0), jnp.where(sk > tiles[a], 1.0, 0.0)) > 0.5
            ranks[a] = ranks[a] + jnp.where(before, 1.0, 0.0)
    rank = jnp.concatenate(ranks, axis=0)
    selected = rank < min(SEL_COUNT, n_slc)
    sel_bias = jnp.where(selected, 0.0, SEL_MASK).T.astype(BF16)
    first_blk = jnp.min(jnp.where(selected & (blk >= blocks_per_chunk), blk, n_slc))
    c_lo = jnp.minimum(jnp.maximum(first_blk // blocks_per_chunk, 1), i)
    q_aug = stack_q(2 * dh, (1.0 * SEL_BLOCK, 1.0), extra=sel_bias)

    causal_bias = jnp.concatenate([tri_ref[0]] * HG, axis=0)
    upper_bias = jnp.concatenate([tri_ref[1]] * HG, axis=0)

    qw_aug = stack_q(dh, (1.0 * SEL_BLOCK, 1.0))
    start_a = pl.multiple_of(jnp.maximum(t0 - 2 * tq, 0), tq)
    start_b = pl.multiple_of(jnp.maximum(t0 - tq, 0), tq)
    start_c = pl.multiple_of(t0, tq)
    s_a = _dot_nt(qw_aug, kw_scr[pl.ds(start_a, tq), :]) + jnp.where(i >= 2, upper_bias, NEG_INF)
    s_b = _dot_nt(qw_aug, kw_scr[pl.ds(start_b, tq), :]) + jnp.where(i >= 1, 0.0, NEG_INF)
    s_c = _dot_nt(qw_aug, kw_scr[pl.ds(start_c, tq), :]) + causal_bias
    m_w = jnp.max(jnp.maximum(jnp.maximum(s_a, s_b), s_c), axis=1, keepdims=True)
    e_a = jnp.exp2(s_a - m_w)
    e_b = jnp.exp2(s_b - m_w)
    e_c = jnp.exp2(s_c - m_w)
    o_win = (_dot(e_a.astype(BF16), vw_scr[pl.ds(start_a, tq), :]) + _dot(e_b.astype(BF16), vw_scr[pl.ds(start_b, tq), :])
             + _dot(e_c.astype(BF16), vw_scr[pl.ds(start_c, tq), :]))

    mrun_scr[...] = jnp.full(mrun_scr.shape, NEG_INF, F32)
    acc_scr[...] = jnp.zeros(acc_scr.shape, F32)

    def scores(c, bias=None, slot=None):
        start = pl.multiple_of(c * tq, tq)
        s = _dot_nt(q_aug, ks_scr[pl.ds(start, tq), :])
        if bias is not None:
            s = s + bias
        s_scr[c if slot is None else slot] = s
        mrun_scr[...] = jnp.maximum(mrun_scr[...], jnp.maximum(s[:, :half], s[:, half:]))

    def weights(c, slot=None):
        start = pl.multiple_of(c * tq, tq)
        s = s_scr[c if slot is None else slot]
        m_b = mrun_scr[...]
        p0 = jnp.exp2(s[:, :half] - m_b)
        p1 = jnp.exp2(s[:, half:] - m_b)
        p = jnp.concatenate([p0, p1], axis=1).astype(BF16)
        acc_scr[...] += _dot(p, vs_scr[pl.ds(start, tq), :])

    def chunk_pairs(lo, hi, fn):
        def pair(t, carry):
            fn(lo + 2 * t)
            fn(lo + 2 * t + 1)
            return carry

        n = jnp.maximum(hi - lo, 0)
        lax.fori_loop(0, n // 2, pair, 0)

        @pl.when(n % 2 == 1)
        def _():
            fn(hi - 1)

    scores(i, bias=causal_bias)
    scores(0, bias=jnp.where(i > 0, 0.0, NEG_INF), slot=spare)
    peeled = []
    for kk in range(PEEL):
        c = c_lo + kk
        valid = c < i
        peeled.append((jnp.minimum(c, jnp.maximum(i - 1, 0)), jnp.where(valid, c, spare + 1 + kk), valid))
    for c, slot, valid in peeled:
        scores(c, bias=jnp.where(valid, 0.0, NEG_INF), slot=slot)
    chunk_pairs(c_lo + PEEL, i, scores)
    m_row = jnp.max(mrun_scr[...], axis=1, keepdims=True)
    mrun_scr[...] = jnp.broadcast_to(m_row, mrun_scr.shape)

    chunk_pairs(c_lo + PEEL, i, weights)
    for c, slot, _ in peeled:
        weights(c, slot=slot)
    weights(0, slot=spare)
    weights(i)
    gates = jax.nn.sigmoid(gate_ref[0])

    def gate_col(r):
        cols = []
        for x in range(HG):
            per_group = [gate_col0 + (gg * HG + x) * NSA_N_BRANCH + r for gg in range(NSA_KV_GROUPS)]
            col = gates[:, per_group[0]:per_group[0] + 1]
            for gg in range(1, NSA_KV_GROUPS):
                col = jnp.where(g == gg, gates[:, per_group[gg]:per_group[gg] + 1], col)
            cols.append(col)
        return jnp.concatenate(cols, axis=0)

    acc = acc_scr[...]
    o = gate_col(0) * o_cmp + acc[:, :dh] * (gate_col(1) / acc[:, dh:dh + 1])

    o = o + o_win[:, :dh] * (gate_col(2) / o_win[:, dh:dh + 1])

    out_ref[0] = jnp.concatenate([o[x * tq:(x + 1) * tq] for x in range(HG)], axis=1).astype(out_ref.dtype)


def _nsa_attention(act, gates, kc, vc, kconst, overlap_t, *, q_col0, kv_col0, gate_col0, tq):
    B, S, _ = act.shape
    G, HG, dh = NSA_KV_GROUPS, NSA_GROUP_HEADS, NSA_HEAD_DIM
    n_slc = S // SEL_BLOCK
    rows = HG * tq
    assert WINDOW == 2 * tq, "the window branch reads exactly the two chunks before the diagonal one"
    r = np.arange(tq)[:, None]
    l = np.arange(tq)[None, :]
    tri_bias = jnp.asarray(np.stack([np.where(l <= r, 0.0, NEG_INF), np.where(l > r, 0.0, NEG_INF)]), F32)
    cmp_pattern = jnp.asarray(np.arange(S // CMP_STRIDE)[None, :] * CMP_STRIDE + (CMP_BLOCK - 1) - r, jnp.int32)

    def kv_spec(off):
        return pl.BlockSpec((1, S, G * dh), lambda b, g, i: (b, 0, kv_col0 + off))

    def full(a):
        return pl.BlockSpec((1, 1) + a.shape[2:], lambda b, g, i: (b, g, 0, 0))

    return pl.pallas_call(
        functools.partial(_nsa_kernel, tq=tq, seq=S, gate_col0=gate_col0),
        grid=(B, G, S // tq),
        in_specs=[
            pl.BlockSpec((1, tq, HG * dh), lambda b, g, i: (b, i, q_col0 + g)),
            pl.BlockSpec((1, tq, LANES), lambda b, g, i: (b, i, 0)),
            full(kc), full(vc), kv_spec(0), kv_spec(1), kv_spec(2), kv_spec(3),
            pl.BlockSpec(kconst.shape, lambda b, g, i: (0, 0)),
            pl.BlockSpec(overlap_t.shape, lambda b, g, i: (0, 0)),
            pl.BlockSpec(tri_bias.shape, lambda b, g, i: (0, 0, 0)),
            pl.BlockSpec(cmp_pattern.shape, lambda b, g, i: (0, 0)),
        ],
        out_specs=pl.BlockSpec((1, tq, HG * dh), lambda b, g, i: (b, i, g)),
        out_shape=jax.ShapeDtypeStruct((B, S, NSA_WIDTH), BF16),
        scratch_shapes=[pltpu.VMEM((S, dh + n_slc + 2 * dh), BF16), pltpu.VMEM((S, 2 * dh), BF16),
                        pltpu.VMEM((S, 2 * dh), BF16), pltpu.VMEM((S, 2 * dh), BF16),
                        pltpu.VMEM((S // tq + 1 + NSA_PEEL, rows, tq), F32), pltpu.VMEM((rows, tq // 2), F32),
                        pltpu.VMEM((rows, 2 * dh), F32)],
        compiler_params=_cparams("parallel", "parallel", "arbitrary"),
        name="nsa_attn",
    )(act, gates, kc, vc, act, act, act, act, kconst, overlap_t, tri_bias, cmp_pattern)


def _mix_out_kernel(x_ref, hm_ref, hn_ref, gm_ref, gn_ref, wm_ref, wn_ref, wo_ref, o_ref):
    y_ml = _dot(hm_ref[...], wm_ref[...])
    y_ns = _dot(hn_ref[...], wn_ref[...])
    mix = jax.nn.sigmoid(gm_ref[...].astype(F32)) * y_ml + jax.nn.sigmoid(gn_ref[...].astype(F32)) * y_ns
    o_ref[...] = x_ref[...] + _dot(mix.astype(BF16), wo_ref[...])


def _mix_out(x2d, h_ml, h_ns, act2d, wm, wn, wo, *, gate_col0, tm):
    n, d = x2d.shape

    def res(w):
        return pl.BlockSpec(w.shape, lambda i: (0, 0))

    return pl.pallas_call(
        _mix_out_kernel,
        grid=(n // tm,),
        in_specs=[
            pl.BlockSpec((tm, d), lambda i: (i, 0)),
            pl.BlockSpec((tm, h_ml.shape[1]), lambda i: (i, 0)),
            pl.BlockSpec((tm, h_ns.shape[1]), lambda i: (i, 0)),
            pl.BlockSpec((tm, d), lambda i: (i, gate_col0)),
            pl.BlockSpec((tm, d), lambda i: (i, gate_col0 + 1)),
            res(wm), res(wn), res(wo),
        ],
        out_specs=pl.BlockSpec((tm, d), lambda i: (i, 0)),
        out_shape=jax.ShapeDtypeStruct((n, d), F32),
        compiler_params=_cparams("parallel"),
        name="mix_out",
    )(x2d, h_ml, h_ns, act2d, act2d, wm, wn, wo)


def _xattn_kernel(x_ref, g_ref, wq_ref, k_ref, v_ref, wo_ref, o_ref):
    x = x_ref[...]
    d = x.shape[1]
    dh = d // XA_HEADS
    h = _rms(x, g_ref[...]).astype(BF16)
    q = (_dot(h, wq_ref[...]) * (dh ** -0.5)).astype(BF16)
    k = k_ref[0]
    v = v_ref[0]
    outs = []
    for hd in range(XA_HEADS):
        sl = slice(hd * dh, (hd + 1) * dh)
        s = _dot_nt(q[:, sl], k[:, sl])
        e = jnp.exp(s - jnp.max(s, axis=1, keepdims=True))
        p = e / jnp.sum(e, axis=1, keepdims=True)
        outs.append(_dot(p.astype(BF16), v[:, sl]))
    o = jnp.concatenate(outs, axis=1).astype(BF16)
    o_ref[...] = x + _dot(o, wo_ref[...])


def _xattn(x2d, g, wq, kv, wo, *, seq, tm):
    n, d = x2d.shape
    n_mem = kv.shape[1]
    tiles_per_batch = seq // tm
    return pl.pallas_call(
        _xattn_kernel,
        grid=(n // tm,),
        in_specs=[
            pl.BlockSpec((tm, d), lambda i: (i, 0)),
            pl.BlockSpec((1, d), lambda i: (0, 0)),
            pl.BlockSpec(wq.shape, lambda i: (0, 0)),
            pl.BlockSpec((1, n_mem, d), lambda i: (i // tiles_per_batch, 0, 0)),
            pl.BlockSpec((1, n_mem, d), lambda i: (i // tiles_per_batch, 0, 1)),
            pl.BlockSpec(wo.shape, lambda i: (0, 0)),
        ],
        out_specs=pl.BlockSpec((tm, d), lambda i: (i, 0)),
        out_shape=jax.ShapeDtypeStruct((n, d), F32),
        compiler_params=_cparams("parallel"),
        name="xattn",
    )(x2d, g.reshape(1, d), wq, kv, kv, wo)


def _route(logits, bg, be):
    G, EG = MOE_GROUPS, MOE_EXPERTS_PER_GROUP
    tm = logits.shape[1]
    lg = logits[0:G] + bg
    eg = jnp.exp(lg - jnp.max(lg, axis=0, keepdims=True))
    pg = eg / jnp.sum(eg, axis=0, keepdims=True)
    g_val = jnp.max(pg, axis=0, keepdims=True)
    g_row = lax.broadcasted_iota(jnp.int32, (G, tm), 0)
    g_idx = jnp.min(jnp.where(pg == g_val, g_row, G), axis=0, keepdims=True)
    el = logits[8:8 + G * EG] + be
    e_in = jnp.zeros((EG, tm), F32)
    for gg in range(G):
        e_in = jnp.where(g_idx == gg, el[gg * EG:(gg + 1) * EG], e_in)
    e_row = lax.broadcasted_iota(jnp.int32, (EG, tm), 0)
    v1 = jnp.max(e_in, axis=0, keepdims=True)
    i1 = jnp.min(jnp.where(e_in == v1, e_row, EG), axis=0, keepdims=True)
    rest = jnp.where(e_row == i1, -jnp.inf, e_in)
    v2 = jnp.max(rest, axis=0, keepdims=True)
    i2 = jnp.min(jnp.where(rest == v2, e_row, EG), axis=0, keepdims=True)
    e2 = jnp.exp(v2 - v1)
    c1 = g_val / (1.0 + e2)
    c2 = g_val * e2 / (1.0 + e2)
    ex = lax.broadcasted_iota(jnp.int32, (G * EG, tm), 0)
    base = g_idx * EG
    return jnp.where(ex == base + i1, c1, 0.0) + jnp.where(ex == base + i2, c2, 0.0), g_idx


def _moe_kernel(x_ref, g_ref, wr_ref, bg_ref, be_ref, before_ref, w1_ref, w3_ref, w2_ref, gf_ref, o_ref,
                h_scr, slot_scr, wt_scr, acc_scr, *, caps):
    G, EG = MOE_GROUPS, MOE_EXPERTS_PER_GROUP
    grp = pl.program_id(1)
    tm = x_ref.shape[0]

    @pl.when(grp == 0)
    def _():
        h = _rms(x_ref[...], g_ref[...])
        h_hi = h.astype(BF16)
        h_scr[...] = h_hi
        h_lo = (h - h_hi.astype(F32)).astype(BF16)
        w_hi = wr_ref[...].astype(BF16)
        w_lo = (wr_ref[...] - w_hi.astype(F32)).astype(BF16)
        logits = _dot_nt(w_hi, h_hi) + _dot_nt(w_hi, h_lo) + _dot_nt(w_lo, h_hi)
        wt, g_idx = _route(logits, bg_ref[...], be_ref[...])
        member = jnp.where(lax.broadcasted_iota(jnp.int32, (8, tm), 0) == g_idx, 1.0, 0.0)
        prefix = _dot(member.astype(BF16), before_ref[...])
        slots = jnp.where(member > 0.0, prefix, -1.0)
        zeros = jnp.zeros((8 - EG, tm), F32)
        for gg in range(G):
            slot_scr[gg] = jnp.broadcast_to(slots[gg:gg + 1], (8, tm))
            wt_scr[gg] = jnp.concatenate([wt[gg * EG:(gg + 1) * EG], zeros], axis=0)
        acc_scr[...] = jnp.zeros_like(acc_scr)

    slot = slot_scr[grp][0:1, :]
    n_rows = jnp.max(slot).astype(jnp.int32) + 1
    wts = wt_scr[grp]

    def one_pass(cap, first_row):
        row = lax.broadcasted_iota(jnp.int32, (cap, tm), 0).astype(F32)
        pick = jnp.where(row == slot - first_row.astype(F32), 1.0, 0.0)
        pick_b = pick.astype(BF16)
        hsub = _dot(pick_b, h_scr[...]).astype(BF16)
        y = jnp.zeros((cap, x_ref.shape[1]), F32)
        for e in range(EG):
            w_e = jnp.sum(pick * wts[e:e + 1, :], axis=1, keepdims=True)
            a = _dot(hsub, w1_ref[e])
            a = a * jax.nn.sigmoid(a) * _dot(hsub, w3_ref[e]) * w_e
            y = y + _dot(a.astype(BF16), w2_ref[e])
        acc_scr[...] += _dot_tn(pick_b, y.astype(BF16))

    *small_caps, cap_long = caps
    lower = 0
    for cap in small_caps:
        @pl.when((n_rows > lower) & (n_rows <= cap))
        def _(cap=cap):
            one_pass(cap, jnp.int32(0))
        lower = cap

    @pl.when(n_rows > lower)
    def _():
        def body(p, carry):
            one_pass(cap_long, p * cap_long)
            return carry

        lax.fori_loop(0, (n_rows + cap_long - 1) // cap_long, body, 0)

    @pl.when(grp == pl.num_programs(1) - 1)
    def _():
        o_ref[...] = _rms(x_ref[...] + acc_scr[...], gf_ref[...])


def _moe(x2d, g, wr, bg, be, w1, w3, w2, g_final, *, tm, caps):
    n, d = x2d.shape
    E, _, hid = w1.shape
    G, EG = MOE_GROUPS, MOE_EXPERTS_PER_GROUP
    before = jnp.asarray(np.arange(tm)[:, None] < np.arange(tm)[None, :], BF16)
    once = pl.Buffered(1)

    def const(shape):
        return pl.BlockSpec(shape, lambda i, e: (0,) * len(shape), pipeline_mode=once)

    return pl.pallas_call(
        functools.partial(_moe_kernel, caps=caps),
        grid=(n // tm, G),
        in_specs=[
            pl.BlockSpec((tm, d), lambda i, e: (i, 0), pipeline_mode=once),
            const((1, d)), const(wr.shape), const(bg.shape), const(be.shape), const((tm, tm)),
            pl.BlockSpec((EG, d, hid), lambda i, e: (e, 0, 0)),
            pl.BlockSpec((EG, d, hid), lambda i, e: (e, 0, 0)),
            pl.BlockSpec((EG, hid, d), lambda i, e: (e, 0, 0)),
            const((1, d)),
        ],
        out_specs=pl.BlockSpec((tm, d), lambda i, e: (i, 0)),
        out_shape=jax.ShapeDtypeStruct((n, d), F32),
        scratch_shapes=[pltpu.VMEM((tm, d), BF16), pltpu.VMEM((G, 8, tm), F32), pltpu.VMEM((G, 8, tm), F32),
                        pltpu.VMEM((tm, d), F32)],
        compiler_params=_cparams("parallel", "arbitrary"),
        name="moe",
    )(x2d, g.reshape(1, d), wr, bg, be, before, w1, w3, w2, g_final.reshape(1, d))


def _overlap_t(seq):
    n_cmp_pad = seq // CMP_STRIDE
    cs = np.arange(n_cmp_pad)[None, :] * CMP_STRIDE
    ss = np.arange(seq // SEL_BLOCK)[:, None] * SEL_BLOCK
    ov = np.clip(np.minimum(cs + CMP_BLOCK, ss + SEL_BLOCK) - np.maximum(cs, ss), 0, None) / CMP_BLOCK
    ov[:, n_cmp_pad - 1] = 0.0
    return jnp.asarray(ov, BF16)


def _layer(x, mem, norm_mix, w_in, conv_qk, b_igate, b_fgate, mlstm_norm, cmp_pos_k, cmp_pos_v, cmp_k_w1, cmp_k_w2,
           cmp_v_w1, cmp_v_w2, w_br_mlstm, w_br_nsa, w_mix_out, norm_xattn, norm_mem, xa_wq, xa_wkv, xa_wo, norm_ffn,
           router_group_w, router_group_b, router_expert_w, router_expert_b, moe_w1, moe_w3, moe_w2, norm_final):
    B, S, D = x.shape
    N = B * S
    H, d = ML_HEADS, ML_HEAD_DIM
    G, HG, dh = NSA_KV_GROUPS, NSA_GROUP_HEADS, NSA_HEAD_DIM
    x2d = x.reshape(N, D)

    o_mlqkvo = 0
    o_mlif = 4 * ML_WIDTH
    o_nsq = o_mlif + 2 * H
    o_kv = o_nsq + NSA_WIDTH
    o_nsg = o_kv + 6 * NSA_KV_WIDTH
    o_merge = o_nsg + NSA_N_BRANCH * NSA_HEADS
    o_kv_sel = o_kv + 2 * NSA_KV_WIDTH
    w_act = jnp.concatenate([w_in[:, o_merge:o_merge + 2 * D], w_in[:, o_mlqkvo:o_mlif], w_in[:, o_nsq:o_kv],
                             w_in[:, o_kv_sel:o_nsg]], axis=1).astype(BF16)
    n_small = 2 * H + NSA_N_BRANCH * NSA_HEADS
    w_small = jnp.concatenate([w_in[:, o_mlif:o_nsq], w_in[:, o_nsg:o_merge], jnp.zeros((D, LANES - n_small), F32),
                               w_in[:, o_kv:o_kv_sel]], axis=1).astype(BF16)
    act, small, cmp_kv = _in_proj(x2d, norm_mix, w_act, w_small, tm=1024, tn=2560)
    act3 = act.reshape(B, S, act.shape[1])
    c_ml = 2 * D
    c_nsq = c_ml + 4 * ML_WIDTH
    c_kv = c_nsq + NSA_WIDTH

    small3 = small.reshape(B, S, small.shape[1])
    gate_bias = jnp.zeros((1, LANES), F32).at[0, 0:H].set(b_igate).at[0, H:2 * H].set(b_fgate)
    conv_w = jnp.concatenate([conv_qk[:, :ML_WIDTH].reshape(ML_CONV, H, d), conv_qk[:, ML_WIDTH:].reshape(ML_CONV, H, d)],
                             axis=-1).transpose(1, 0, 2)
    h_ml = _mlstm(act3, small3, gate_bias, conv_w, mlstm_norm.reshape(H, 1, d), col0=c_ml // ML_WIDTH, chunk=512)

    pos = jnp.tile(jnp.stack([cmp_pos_k, cmp_pos_v])[:, :, None, :], (1, 1, 1, G))
    w1c = jnp.stack([cmp_k_w1, cmp_v_w1]).reshape(2, CMP_BLOCK, dh, -1)
    w1c = jnp.einsum('arld,gh->arghld', w1c, jnp.eye(G, dtype=F32))
    w1c = w1c.reshape(2, CMP_BLOCK, G, G * dh, -1).astype(BF16)
    w2c = jnp.stack([cmp_k_w2, cmp_v_w2]).astype(BF16)
    kvc = _compress(cmp_kv.reshape(B, S, cmp_kv.shape[1]), pos, w1c, w2c, col0=0)
    n_slc = S // SEL_BLOCK
    tok = np.arange(S)
    kconst = np.zeros((S, n_slc + 2 * dh), np.float32)
    kconst[tok, tok // SEL_BLOCK] = 1.0
    kconst[:, n_slc:n_slc + 3] = (tok // SEL_BLOCK)[:, None]
    kconst[:, n_slc + 3:n_slc + 6] = (tok % SEL_BLOCK)[:, None]
    h_ns = _nsa_attention(act3, small3, kvc[0], kvc[1], jnp.asarray(kconst, BF16), _overlap_t(S),
                          q_col0=c_nsq // (HG * dh), kv_col0=c_kv // NSA_KV_WIDTH,
                          gate_col0=2 * H, tq=256)

    x1 = _mix_out(x2d, h_ml.reshape(N, ML_WIDTH), h_ns.reshape(N, NSA_WIDTH), act, w_br_mlstm.astype(BF16),
                  w_br_nsa.astype(BF16), w_mix_out.astype(BF16), gate_col0=0, tm=512)

    n_mem = mem.shape[1]
    kv_mem = _norm_matmul(mem.reshape(B * n_mem, D), norm_mem, xa_wkv.astype(BF16), BF16, tm=B * n_mem, tn=512)
    x2 = _xattn(x1, norm_xattn, xa_wq.astype(BF16), kv_mem.reshape(B, n_mem, 2 * D), xa_wo.astype(BF16), seq=S, tm=1024)

    wr = jnp.zeros((ROUTER_ROWS, D), F32)
    wr = wr.at[0:MOE_GROUPS].set(router_group_w.T).at[8:8 + MOE_EXPERTS].set(router_expert_w.T)
    return _moe(x2, norm_ffn, wr, router_group_b.reshape(MOE_GROUPS, 1), router_expert_b.reshape(MOE_EXPERTS, 1),
                moe_w1.astype(BF16), moe_w3.astype(BF16), moe_w2.astype(BF16), norm_final, tm=1024, caps=(256, 304, 352)).reshape(B, S, D)


def kernel(x, mem, norm_mix, w_in, conv_qk, b_igate, b_fgate, mlstm_norm, cmp_pos_k, cmp_pos_v, cmp_k_w1, cmp_k_w2, cmp_v_w1, cmp_v_w2, w_br_mlstm, w_br_nsa, w_mix_out, norm_xattn, norm_mem, xa_wq, xa_wkv, xa_wo, norm_ffn, router_group_w, router_group_b, router_expert_w, router_expert_b, moe_w1, moe_w3, moe_w2, norm_final):
    depth = w_in.shape[0]
    assert depth == 1, "the fused final norm assumes a single layer"
    layer = 0
    return _layer(x, mem, norm_mix[layer], w_in[layer], conv_qk[layer], b_igate[layer], b_fgate[layer], mlstm_norm[layer],
                  cmp_pos_k[layer], cmp_pos_v[layer], cmp_k_w1[layer], cmp_k_w2[layer], cmp_v_w1[layer], cmp_v_w2[layer],
                  w_br_mlstm[layer], w_br_nsa[layer], w_mix_out[layer], norm_xattn[layer], norm_mem[layer], xa_wq[layer],
                  xa_wkv[layer], xa_wo[layer], norm_ffn[layer], router_group_w[layer], router_group_b[layer],
                  router_expert_w[layer], router_expert_b[layer], moe_w1[layer], moe_w3[layer], moe_w2[layer], norm_final)
```

```python
import functools

import numpy as np
import jax
import jax.numpy as jnp
from jax import lax
from jax.experimental import pallas as pl
from jax.experimental.pallas import tpu as pltpu

F32 = jnp.float32
BF16 = jnp.bfloat16

ML_HEADS = 4
ML_HEAD_DIM = 128
ML_WIDTH = ML_HEADS * ML_HEAD_DIM
ML_CONV = 4
NSA_HEADS = 8
NSA_KV_GROUPS = 2
NSA_HEAD_DIM = 64
NSA_GROUP_HEADS = NSA_HEADS // NSA_KV_GROUPS
NSA_WIDTH = NSA_HEADS * NSA_HEAD_DIM
NSA_KV_WIDTH = NSA_KV_GROUPS * NSA_HEAD_DIM
NSA_N_BRANCH = 3
CMP_BLOCK = 32
CMP_STRIDE = 16
SEL_BLOCK = 64
SEL_COUNT = 16
SEL_FORCE = 1e4
WINDOW = 512
XA_HEADS = 4
MOE_GROUPS = 4
MOE_EXPERTS_PER_GROUP = 4
MOE_EXPERTS = MOE_GROUPS * MOE_EXPERTS_PER_GROUP
RMS_EPS = 1e-6
NEG_INF = -1e30
SEL_MASK = -float(2 ** 30)
LOG2E = 1.4426950408889634

LANES = 128
VMEM_LIMIT_BYTES = 56 * 1024 * 1024
ROUTER_ROWS = 32
NSA_PEEL = 2


def _cparams(*sem):
    return pltpu.CompilerParams(dimension_semantics=sem, vmem_limit_bytes=VMEM_LIMIT_BYTES)


def _rms(x, g):
    return x * lax.rsqrt(jnp.mean(x * x, axis=-1, keepdims=True) + RMS_EPS) * g


def _dot(a, b):
    return jnp.dot(a, b, preferred_element_type=F32)


def _dot_nt(a, b):
    return lax.dot_general(a, b, (((1,), (1,)), ((), ())), preferred_element_type=F32)


def _dot_tn(a, b):
    return lax.dot_general(a, b, (((0,), (0,)), ((), ())), preferred_element_type=F32)


def _norm_matmul_kernel(x_ref, g_ref, w_ref, o_ref, h_ref):
    @pl.when(pl.program_id(1) == 0)
    def _():
        h_ref[...] = _rms(x_ref[...], g_ref[...]).astype(BF16)

    o_ref[...] = _dot(h_ref[...], w_ref[...]).astype(o_ref.dtype)


def _norm_matmul(x2d, g, w, out_dtype, tm, tn):
    m, d = x2d.shape
    n = w.shape[1]
    return pl.pallas_call(
        _norm_matmul_kernel,
        grid=(m // tm, n // tn),
        in_specs=[
            pl.BlockSpec((tm, d), lambda i, j: (i, 0)),
            pl.BlockSpec((1, d), lambda i, j: (0, 0)),
            pl.BlockSpec((d, tn), lambda i, j: (0, j)),
        ],
        out_specs=pl.BlockSpec((tm, tn), lambda i, j: (i, j)),
        out_shape=jax.ShapeDtypeStruct((m, n), out_dtype),
        scratch_shapes=[pltpu.VMEM((tm, d), BF16)],
        compiler_params=_cparams("parallel", "arbitrary"),
        name="norm_matmul",
    )(x2d, g.reshape(1, d), w)


def _in_proj_kernel(x_ref, g_ref, w_ref, ws_ref, o_ref, og_ref, oc_ref, h_ref):
    @pl.when(pl.program_id(1) == 0)
    def _():
        h_ref[...] = _rms(x_ref[...], g_ref[...]).astype(BF16)
        small = _dot(h_ref[...], ws_ref[...])
        og_ref[...] = small[:, :LANES]
        oc_ref[...] = small[:, LANES:]

    o_ref[...] = _dot(h_ref[...], w_ref[...]).astype(o_ref.dtype)


def _in_proj(x2d, g, w, w_small, tm, tn):
    m, d = x2d.shape
    n = w.shape[1]
    ns = w_small.shape[1]
    return pl.pallas_call(
        _in_proj_kernel,
        grid=(m // tm, n // tn),
        in_specs=[
            pl.BlockSpec((tm, d), lambda i, j: (i, 0)),
            pl.BlockSpec((1, d), lambda i, j: (0, 0)),
            pl.BlockSpec((d, tn), lambda i, j: (0, j)),
            pl.BlockSpec((d, ns), lambda i, j: (0, 0)),
        ],
        out_specs=[pl.BlockSpec((tm, tn), lambda i, j: (i, j)), pl.BlockSpec((tm, LANES), lambda i, j: (i, 0)),
                   pl.BlockSpec((tm, ns - LANES), lambda i, j: (i, 0))],
        out_shape=[jax.ShapeDtypeStruct((m, n), BF16), jax.ShapeDtypeStruct((m, LANES), F32),
                   jax.ShapeDtypeStruct((m, ns - LANES), F32)],
        scratch_shapes=[pltpu.VMEM((tm, d), BF16)],
        compiler_params=_cparams("parallel", "arbitrary"),
        name="in_proj",
    )(x2d, g.reshape(1, d), w, w_small)


def _mlstm_kernel(q_ref, k_ref, v_ref, o_ref, qt_ref, kt_ref, gate_ref, gbias_ref,
                  cw_ref, ng_ref, upper_ref, out_ref, c_scr, n_scr, m_scr, *, chunk):
    L, d = chunk, ML_HEAD_DIM
    c = pl.program_id(1)

    @pl.when(c == 0)
    def _():
        c_scr[...] = jnp.zeros_like(c_scr)
        n_scr[...] = jnp.zeros_like(n_scr)
        m_scr[...] = jnp.zeros_like(m_scr)

    t_idx = lax.broadcasted_iota(jnp.int32, (L, L), 0)
    s_idx = lax.broadcasted_iota(jnp.int32, (L, L), 1)
    tri = s_idx <= t_idx

    row8 = lax.broadcasted_iota(jnp.int32, (8, d), 0)

    def conv_silu(x, tail, w):
        x = x.astype(F32)
        tail = jnp.where(c > 0, tail.astype(F32), 0.0)
        y = x * w[ML_CONV - 1:ML_CONV, :]
        for j in range(ML_CONV - 1):
            s = ML_CONV - 1 - j
            r = pltpu.roll(x, s, 0)
            head = jnp.where(row8 < s, pltpu.roll(tail, s, 0), r[:8])
            y = y + jnp.concatenate([head, r[8:]], axis=0) * w[j:j + 1, :]
        return y * (0.5 * jnp.tanh(0.5 * y) + 0.5)

    g_cols = gate_ref[0] + gbias_ref[...]
    g_rows = g_cols.T
    f_rows = jax.nn.log_sigmoid(g_rows[ML_HEADS:2 * ML_HEADS, :])
    f_hi = f_rows.astype(BF16)
    f_mid = (f_rows - f_hi.astype(F32)).astype(BF16)
    f_lo = (f_rows - f_hi.astype(F32) - f_mid.astype(F32)).astype(BF16)
    pad = jnp.zeros((16 - 3 * ML_HEADS, L), BF16)
    parts = _dot(jnp.concatenate([f_hi, f_mid, f_lo, pad], axis=0), upper_ref[...])
    b_rows = parts[0:ML_HEADS] + parts[ML_HEADS:2 * ML_HEADS] + parts[2 * ML_HEADS:3 * ML_HEADS]
    b_cols = jnp.concatenate([b_rows, jnp.zeros((8 - ML_HEADS, L), F32)], axis=0).T

    outs = []
    for h in range(ML_HEADS):
        sl = slice(h * d, (h + 1) * d)
        cw = cw_ref[h]
        q = conv_silu(q_ref[0, :, sl], qt_ref[0, :, sl], cw[:, :d])
        k = conv_silu(k_ref[0, :, sl], kt_ref[0, :, sl], cw[:, d:]) * (d ** -0.5)
        v = v_ref[0, :, sl]
        qb = q.astype(BF16)
        kb = k.astype(BF16)

        i_col = g_cols[:, h:h + 1]
        i_row = g_rows[h:h + 1, :]
        f_row = f_rows[h:h + 1, :]
        b_col = b_cols[:, h:h + 1]
        b_row = b_rows[h:h + 1, :]
        dlog = jnp.where(tri, b_col + (i_row - b_row), NEG_INF)
        m_prev = m_scr[h]
        inter = b_col + m_prev
        mt = jnp.maximum(jnp.max(dlog, axis=1, keepdims=True), inter)
        w_intra = jnp.exp(dlog - mt)
        w_inter = jnp.exp(inter - mt)

        s = _dot_nt(qb, kb) * w_intra
        cmat = c_scr[h]
        nvec = n_scr[h]
        num = _dot(s.astype(BF16), v) + w_inter * _dot(qb, cmat.astype(BF16))
        den = jnp.sum(s, axis=1, keepdims=True) + w_inter * jnp.sum(q * nvec, axis=1, keepdims=True)
        hc = num / jnp.maximum(jnp.abs(den), jnp.exp(-mt))

        bl = jnp.sum(f_row, axis=1, keepdims=True)
        logw = bl - b_col + i_col
        m_new = jnp.maximum(bl + m_prev, jnp.max(logw, axis=0, keepdims=True))
        decay = jnp.exp(bl + m_prev - m_new)
        kw = k * jnp.exp(logw - m_new)
        c_scr[h] = decay * cmat + _dot_tn(kw.astype(BF16), v)
        n_scr[h] = decay * nvec + jnp.sum(kw, axis=0, keepdims=True)
        m_scr[h] = m_new

        outs.append(jax.nn.sigmoid(o_ref[0, :, sl].astype(F32)) * _rms(hc, ng_ref[h]))
    out_ref[0] = jnp.concatenate(outs, axis=1).astype(out_ref.dtype)


def _mlstm(act, gates, gate_bias, conv_w, norm_g, *, col0, chunk):
    B, S, _ = act.shape
    H, d, L = ML_HEADS, ML_HEAD_DIM, chunk
    tail_blocks = L // 8
    upper = jnp.asarray(np.triu(np.ones((L, L))), BF16)

    def blk(off):
        return pl.BlockSpec((1, L, H * d), lambda b, c: (b, c, col0 + off))

    def tail(off):
        return pl.BlockSpec((1, 8, H * d), lambda b, c: (b, jnp.maximum(c * tail_blocks - 1, 0), col0 + off))

    def const(a):
        return pl.BlockSpec(a.shape, lambda b, c: (0,) * a.ndim)

    return pl.pallas_call(
        functools.partial(_mlstm_kernel, chunk=L),
        grid=(B, S // L),
        in_specs=[
            blk(0), blk(1), blk(2), blk(3), tail(0), tail(1),
            pl.BlockSpec((1, L, LANES), lambda b, c: (b, c, 0)),
            const(gate_bias), const(conv_w), const(norm_g), const(upper),
        ],
        out_specs=pl.BlockSpec((1, L, H * d), lambda b, c: (b, c, 0)),
        out_shape=jax.ShapeDtypeStruct((B, S, H * d), BF16),
        scratch_shapes=[pltpu.VMEM((H, d, d), F32), pltpu.VMEM((H, 1, d), F32), pltpu.VMEM((H, 1, 1), F32)],
        compiler_params=_cparams("parallel", "arbitrary"),
        name="mlstm",
    )(act, act, act, act, act, act, gates, gate_bias, conv_w, norm_g, upper)


def _compress_kernel(x_ref, pos_ref, w1_ref, w2_ref, o_ref):
    n = x_ref.shape[1] // CMP_STRIDE
    groups = o_ref.shape[2]
    for g in range(groups):
        top = jnp.zeros((n, w1_ref.shape[-1]), F32)
        bot = jnp.zeros((n, w1_ref.shape[-1]), F32)
        for r in range(CMP_STRIDE):
            x = x_ref[0, pl.ds(r, n, stride=CMP_STRIDE), :]
            top = top + _dot((x + pos_ref[0, r]).astype(BF16), w1_ref[0, r, g])
            bot = bot + _dot((x + pos_ref[0, CMP_STRIDE + r]).astype(BF16), w1_ref[0, CMP_STRIDE + r, g])
        pre = top + pltpu.roll(bot, n - 1, 0)
        hid = jax.nn.gelu(pre, approximate=True)
        out = _dot(hid.astype(BF16), w2_ref[0])
        row = lax.broadcasted_iota(jnp.int32, out.shape, 0)
        out = jnp.where(row < n - 1, out, 0.0)
        j = lax.broadcasted_iota(jnp.int32, out.shape, 0)
        lane = lax.broadcasted_iota(jnp.int32, out.shape, 1)
        cols = jnp.where(lane < 3, j // 16, jnp.where(lane < 6, j % 16, jnp.where(lane < 9, 1, 0))).astype(F32)
        cols = jnp.where(pl.program_id(0) == 0, cols, jnp.where(lane == 0, 1.0, 0.0))
        o_ref[0, 0, g] = jnp.concatenate([out, cols], axis=1).astype(o_ref.dtype)


def _compress(gates, pos, w1, w2, *, col0):
    B, S, _ = gates.shape
    G, width = w1.shape[2], w1.shape[3]
    hidden = w1.shape[-1]
    dh = w2.shape[-1]
    n = S // CMP_STRIDE
    return pl.pallas_call(
        _compress_kernel,
        grid=(2, B),
        in_specs=[
            pl.BlockSpec((1, S, width), lambda a, b: (b, 0, col0 + a)),
            pl.BlockSpec((1,) + pos.shape[1:], lambda a, b: (a, 0, 0, 0)),
            pl.BlockSpec((1,) + w1.shape[1:], lambda a, b: (a, 0, 0, 0, 0)),
            pl.BlockSpec((1, hidden, dh), lambda a, b: (a, 0, 0)),
        ],
        out_specs=pl.BlockSpec((1, 1, G, n, 2 * dh), lambda a, b: (a, b, 0, 0, 0)),
        out_shape=jax.ShapeDtypeStruct((2, B, G, n, 2 * dh), BF16),
        compiler_params=_cparams("parallel", "parallel"),
        name="nsa_compress",
    )(gates, pos, w1, w2)


def _nsa_kernel(q_ref, gate_ref, kc_ref, vc_ref, ksl_ref, vsl_ref, kwn_ref, vwn_ref, kconst_ref, ovl_ref, tri_ref, cpat_ref,
                out_ref, ks_scr, vs_scr, kw_scr, vw_scr, s_scr, mrun_scr, acc_scr, *, tq, seq, gate_col0):
    HG, dh = NSA_GROUP_HEADS, NSA_HEAD_DIM
    g = pl.program_id(1)
    i = pl.program_id(2)
    t0 = i * tq
    rows = HG * tq
    n_cmp_pad = kc_ref.shape[2]
    n_slc = seq // SEL_BLOCK
    half = tq // 2
    blocks_per_chunk = tq // SEL_BLOCK
    spare = seq // tq
    PEEL = NSA_PEEL

    def assemble(lo):
        ks_scr[...] = jnp.concatenate([ksl_ref[0][:, lo:lo + dh], kconst_ref[...]], axis=1)
        kw_scr[...] = jnp.concatenate([kwn_ref[0][:, lo:lo + dh], kconst_ref[:, n_slc:n_slc + dh]], axis=1)
        ones_col = jnp.where(lax.broadcasted_iota(jnp.int32, (seq, dh), 1) == 0, 1.0, 0.0).astype(BF16)
        vs_scr[...] = jnp.concatenate([vsl_ref[0][:, lo:lo + dh], ones_col], axis=1)
        vw_scr[...] = jnp.concatenate([vwn_ref[0][:, lo:lo + dh], ones_col], axis=1)

    for gg in range(NSA_KV_GROUPS):
        @pl.when((i == 0) & (g == gg))
        def _(gg=gg):
            assemble(gg * dh)

    q_all = q_ref[0]
    q_heads = [(q_all[:, x * dh:(x + 1) * dh].astype(F32) * (dh ** -0.5 * LOG2E)).astype(BF16) for x in range(HG)]

    def head_slope(x):
        sl = jnp.float32(0.0)
        for hh in range(NSA_HEADS):
            sl = jnp.where(g * HG + x == hh, 2.0 ** (-8.0 * (hh + 1) / NSA_HEADS), sl)
        return sl

    slopes = [head_slope(x) for x in range(HG)]

    def alibi_cols(x, width, coef):
        lane = lax.broadcasted_iota(jnp.int32, (tq, width), 1)
        v = jnp.zeros((tq, width), F32)
        for c, val in enumerate(coef):
            v = jnp.where(lane // 3 == c, slopes[x] * (val * LOG2E), v)
        hi = v.astype(BF16).astype(F32)
        mid = (v - hi).astype(BF16).astype(F32)
        lo = v - hi - mid
        return jnp.where(lane % 3 == 0, hi, jnp.where(lane % 3 == 1, mid, lo)).astype(BF16)

    def stack_q(width, coef, extra=None):
        parts = []
        for x in range(HG):
            cols = [q_heads[x]] + ([extra] if extra is not None else []) + [alibi_cols(x, width, coef)]
            parts.append(jnp.concatenate(cols, axis=1))
        return jnp.concatenate(parts, axis=0)

    q_cmp = stack_q(dh, (16.0 * CMP_STRIDE, 1.0 * CMP_STRIDE, (CMP_BLOCK - 1) / 2))
    sc = _dot_nt(q_cmp, kc_ref[0, 0])
    block_done = jnp.concatenate([cpat_ref[...]] * HG, axis=0) <= t0
    sc = jnp.where(block_done, sc, NEG_INF)
    e = jnp.exp2(sc - jnp.max(sc, axis=1, keepdims=True))
    t_row = t0 + lax.broadcasted_iota(jnp.int32, (rows, 1), 0) % tq
    any_valid = jnp.where(t_row >= CMP_BLOCK - 1, 1.0, 0.0)
    p_cmp = e * (any_valid / jnp.sum(e, axis=1, keepdims=True))
    o_cmp = _dot(p_cmp.astype(BF16), vc_ref[0, 0])[:, :dh]

    p_sum = p_cmp[0:tq]
    for x in range(1, HG):
        p_sum = p_sum + p_cmp[x * tq:(x + 1) * tq]
    p_hi = p_sum.astype(BF16)
    p_lo = (p_sum - p_hi.astype(F32)).astype(BF16)
    ovl = ovl_ref[...]
    p_slc = _dot_nt(ovl, p_hi) + _dot_nt(ovl, p_lo)
    blk = lax.broadcasted_iota(jnp.int32, (n_slc, tq), 0)
    cur = (t0 + lax.broadcasted_iota(jnp.int32, (n_slc, tq), 1)) // SEL_BLOCK
    forced = (blk == 0) | (blk == cur) | (blk == cur - 1)
    score = jnp.where(forced, SEL_FORCE, jnp.where(blk > cur, -SEL_FORCE, p_slc))
    n_tiles = n_slc // 8
    tiles = [score[8 * a:8 * a + 8] for a in range(n_tiles)]
    ranks = [jnp.zeros((8, tq), F32) for _ in range(n_tiles)]
    sub = lax.broadcasted_iota(jnp.int32, (8, tq), 0)
    for kk in range(n_slc):
        sk = score[kk:kk + 1, :]
        for a in range(n_tiles):
            if a < kk // 8:
                before = sk > tiles[a]
            elif a > kk // 8:
                before = sk >= tiles[a]
            else:
                before = jnp.where(sub > kk % 8, jnp.where(sk >= tiles[a], 1.0, 0.0), jnp.where(sk > tiles[a], 1.0, 0.0)) > 0.5
            ranks[a] = ranks[a] + jnp.where(before, 1.0, 0.0)
    rank = jnp.concatenate(ranks, axis=0)
    selected = rank < min(SEL_COUNT, n_slc)
    sel_bias = jnp.where(selected, 0.0, SEL_MASK).T.astype(BF16)
    first_blk = jnp.min(jnp.where(selected & (blk >= blocks_per_chunk), blk, n_slc))
    c_lo = jnp.minimum(jnp.maximum(first_blk // blocks_per_chunk, 1), i)
    q_aug = stack_q(2 * dh, (1.0 * SEL_BLOCK, 1.0), extra=sel_bias)

    causal_bias = jnp.concatenate([tri_ref[0]] * HG, axis=0)
    upper_bias = jnp.concatenate([tri_ref[1]] * HG, axis=0)

    qw_aug = stack_q(dh, (1.0 * SEL_BLOCK, 1.0))
    start_a = pl.multiple_of(jnp.maximum(t0 - 2 * tq, 0), tq)
    start_b = pl.multiple_of(jnp.maximum(t0 - tq, 0), tq)
    start_c = pl.multiple_of(t0, tq)
    s_a = _dot_nt(qw_aug, kw_scr[pl.ds(start_a, tq), :]) + jnp.where(i >= 2, upper_bias, NEG_INF)
    s_b = _dot_nt(qw_aug, kw_scr[pl.ds(start_b, tq), :]) + jnp.where(i >= 1, 0.0, NEG_INF)
    s_c = _dot_nt(qw_aug, kw_scr[pl.ds(start_c, tq), :]) + causal_bias
    m_w = jnp.max(jnp.maximum(jnp.maximum(s_a, s_b), s_c), axis=1, keepdims=True)
    e_a = jnp.exp2(s_a - m_w)
    e_b = jnp.exp2(s_b - m_w)
    e_c = jnp.exp2(s_c - m_w)
    o_win = (_dot(e_a.astype(BF16), vw_scr[pl.ds(start_a, tq), :]) + _dot(e_b.astype(BF16), vw_scr[pl.ds(start_b, tq), :])
             + _dot(e_c.astype(BF16), vw_scr[pl.ds(start_c, tq), :]))

    mrun_scr[...] = jnp.full(mrun_scr.shape, NEG_INF, F32)
    acc_scr[...] = jnp.zeros(acc_scr.shape, F32)

    def scores(c, bias=None, slot=None):
        start = pl.multiple_of(c * tq, tq)
        s = _dot_nt(q_aug, ks_scr[pl.ds(start, tq), :])
        if bias is not None:
            s = s + bias
        s_scr[c if slot is None else slot] = s
        mrun_scr[...] = jnp.maximum(mrun_scr[...], jnp.maximum(s[:, :half], s[:, half:]))

    def weights(c, slot=None):
        start = pl.multiple_of(c * tq, tq)
        s = s_scr[c if slot is None else slot]
        m_b = mrun_scr[...]
        p0 = jnp.exp2(s[:, :half] - m_b)
        p1 = jnp.exp2(s[:, half:] - m_b)
        p = jnp.concatenate([p0, p1], axis=1).astype(BF16)
        acc_scr[...] += _dot(p, vs_scr[pl.ds(start, tq), :])

    def chunk_pairs(lo, hi, fn):
        def pair(t, carry):
            fn(lo + 2 * t)
            fn(lo + 2 * t + 1)
            return carry

        n = jnp.maximum(hi - lo, 0)
        lax.fori_loop(0, n // 2, pair, 0)

        @pl.when(n % 2 == 1)
        def _():
            fn(hi - 1)

    scores(i, bias=causal_bias)
    scores(0, bias=jnp.where(i > 0, 0.0, NEG_INF), slot=spare)
    peeled = []
    for kk in range(PEEL):
        c = c_lo + kk
        valid = c < i
        peeled.append((jnp.minimum(c, jnp.maximum(i - 1, 0)), jnp.where(valid, c, spare + 1 + kk), valid))
    for c, slot, valid in peeled:
        scores(c, bias=jnp.where(valid, 0.0, NEG_INF), slot=slot)
    chunk_pairs(c_lo + PEEL, i, scores)
    m_row = jnp.max(mrun_scr[...], axis=1, keepdims=True)
    mrun_scr[...] = jnp.broadcast_to(m_row, mrun_scr.shape)

    chunk_pairs(c_lo + PEEL, i, weights)
    for c, slot, _ in peeled:
        weights(c, slot=slot)
    weights(0, slot=spare)
    weights(i)
    gates = jax.nn.sigmoid(gate_ref[0])

    def gate_col(r):
        cols = []
        for x in range(HG):
            per_group = [gate_col0 + (gg * HG + x) * NSA_N_BRANCH + r for gg in range(NSA_KV_GROUPS)]
            col = gates[:, per_group[0]:per_group[0] + 1]
            for gg in range(1, NSA_KV_GROUPS):
                col = jnp.where(g == gg, gates[:, per_group[gg]:per_group[gg] + 1], col)
            cols.append(col)
        return jnp.concatenate(cols, axis=0)

    acc = acc_scr[...]
    o = gate_col(0) * o_cmp + acc[:, :dh] * (gate_col(1) / acc[:, dh:dh + 1])

    o = o + o_win[:, :dh] * (gate_col(2) / o_win[:, dh:dh + 1])

    out_ref[0] = jnp.concatenate([o[x * tq:(x + 1) * tq] for x in range(HG)], axis=1).astype(out_ref.dtype)


def _nsa_attention(act, gates, kc, vc, kconst, overlap_t, *, q_col0, kv_col0, gate_col0, tq):
    B, S, _ = act.shape
    G, HG, dh = NSA_KV_GROUPS, NSA_GROUP_HEADS, NSA_HEAD_DIM
    n_slc = S // SEL_BLOCK
    rows = HG * tq
    assert WINDOW == 2 * tq, "the window branch reads exactly the two chunks before the diagonal one"
    r = np.arange(tq)[:, None]
    l = np.arange(tq)[None, :]
    tri_bias = jnp.asarray(np.stack([np.where(l <= r, 0.0, NEG_INF), np.where(l > r, 0.0, NEG_INF)]), F32)
    cmp_pattern = jnp.asarray(np.arange(S // CMP_STRIDE)[None, :] * CMP_STRIDE + (CMP_BLOCK - 1) - r, jnp.int32)

    def kv_spec(off):
        return pl.BlockSpec((1, S, G * dh), lambda b, g, i: (b, 0, kv_col0 + off))

    def full(a):
        return pl.BlockSpec((1, 1) + a.shape[2:], lambda b, g, i: (b, g, 0, 0))

    return pl.pallas_call(
        functools.partial(_nsa_kernel, tq=tq, seq=S, gate_col0=gate_col0),
        grid=(B, G, S // tq),
        in_specs=[
            pl.BlockSpec((1, tq, HG * dh), lambda b, g, i: (b, i, q_col0 + g)),
            pl.BlockSpec((1, tq, LANES), lambda b, g, i: (b, i, 0)),
            full(kc), full(vc), kv_spec(0), kv_spec(1), kv_spec(2), kv_spec(3),
            pl.BlockSpec(kconst.shape, lambda b, g, i: (0, 0)),
            pl.BlockSpec(overlap_t.shape, lambda b, g, i: (0, 0)),
            pl.BlockSpec(tri_bias.shape, lambda b, g, i: (0, 0, 0)),
            pl.BlockSpec(cmp_pattern.shape, lambda b, g, i: (0, 0)),
        ],
        out_specs=pl.BlockSpec((1, tq, HG * dh), lambda b, g, i: (b, i, g)),
        out_shape=jax.ShapeDtypeStruct((B, S, NSA_WIDTH), BF16),
        scratch_shapes=[pltpu.VMEM((S, dh + n_slc + 2 * dh), BF16), pltpu.VMEM((S, 2 * dh), BF16),
                        pltpu.VMEM((S, 2 * dh), BF16), pltpu.VMEM((S, 2 * dh), BF16),
                        pltpu.VMEM((S // tq + 1 + NSA_PEEL, rows, tq), F32), pltpu.VMEM((rows, tq // 2), F32),
                        pltpu.VMEM((rows, 2 * dh), F32)],
        compiler_params=_cparams("parallel", "parallel", "arbitrary"),
        name="nsa_attn",
    )(act, gates, kc, vc, act, act, act, act, kconst, overlap_t, tri_bias, cmp_pattern)


def _mix_out_kernel(x_ref, hm_ref, hn_ref, gm_ref, gn_ref, wm_ref, wn_ref, wo_ref, o_ref):
    y_ml = _dot(hm_ref[...], wm_ref[...])
    y_ns = _dot(hn_ref[...], wn_ref[...])
    mix = jax.nn.sigmoid(gm_ref[...].astype(F32)) * y_ml + jax.nn.sigmoid(gn_ref[...].astype(F32)) * y_ns
    o_ref[...] = x_ref[...] + _dot(mix.astype(BF16), wo_ref[...])


def _mix_out(x2d, h_ml, h_ns, act2d, wm, wn, wo, *, gate_col0, tm):
    n, d = x2d.shape

    def res(w):
        return pl.BlockSpec(w.shape, lambda i: (0, 0))

    return pl.pallas_call(
        _mix_out_kernel,
        grid=(n // tm,),
        in_specs=[
            pl.BlockSpec((tm, d), lambda i: (i, 0)),
            pl.BlockSpec((tm, h_ml.shape[1]), lambda i: (i, 0)),
            pl.BlockSpec((tm, h_ns.shape[1]), lambda i: (i, 0)),
            pl.BlockSpec((tm, d), lambda i: (i, gate_col0)),
            pl.BlockSpec((tm, d), lambda i: (i, gate_col0 + 1)),
            res(wm), res(wn), res(wo),
        ],
        out_specs=pl.BlockSpec((tm, d), lambda i: (i, 0)),
        out_shape=jax.ShapeDtypeStruct((n, d), F32),
        compiler_params=_cparams("parallel"),
        name="mix_out",
    )(x2d, h_ml, h_ns, act2d, act2d, wm, wn, wo)


def _xattn_kernel(x_ref, g_ref, wq_ref, k_ref, v_ref, wo_ref, o_ref):
    x = x_ref[...]
    d = x.shape[1]
    dh = d // XA_HEADS
    h = _rms(x, g_ref[...]).astype(BF16)
    q = (_dot(h, wq_ref[...]) * (dh ** -0.5)).astype(BF16)
    k = k_ref[0]
    v = v_ref[0]
    outs = []
    for hd in range(XA_HEADS):
        sl = slice(hd * dh, (hd + 1) * dh)
        s = _dot_nt(q[:, sl], k[:, sl])
        e = jnp.exp(s - jnp.max(s, axis=1, keepdims=True))
        p = e / jnp.sum(e, axis=1, keepdims=True)
        outs.append(_dot(p.astype(BF16), v[:, sl]))
    o = jnp.concatenate(outs, axis=1).astype(BF16)
    o_ref[...] = x + _dot(o, wo_ref[...])


def _xattn(x2d, g, wq, kv, wo, *, seq, tm):
    n, d = x2d.shape
    n_mem = kv.shape[1]
    tiles_per_batch = seq // tm
    return pl.pallas_call(
        _xattn_kernel,
        grid=(n // tm,),
        in_specs=[
            pl.BlockSpec((tm, d), lambda i: (i, 0)),
            pl.BlockSpec((1, d), lambda i: (0, 0)),
            pl.BlockSpec(wq.shape, lambda i: (0, 0)),
            pl.BlockSpec((1, n_mem, d), lambda i: (i // tiles_per_batch, 0, 0)),
            pl.BlockSpec((1, n_mem, d), lambda i: (i // tiles_per_batch, 0, 1)),
            pl.BlockSpec(wo.shape, lambda i: (0, 0)),
        ],
        out_specs=pl.BlockSpec((tm, d), lambda i: (i, 0)),
        out_shape=jax.ShapeDtypeStruct((n, d), F32),
        compiler_params=_cparams("parallel"),
        name="xattn",
    )(x2d, g.reshape(1, d), wq, kv, kv, wo)


def _mix_xattn_kernel(x_ref, hm_ref, hn_ref, gm_ref, gn_ref, wm_ref, wn_ref, wo_ref, g_ref, wq_ref, k_ref, v_ref,
                      wxo_ref, o_ref):
    _mix_out_kernel(x_ref, hm_ref, hn_ref, gm_ref, gn_ref, wm_ref, wn_ref, wo_ref, o_ref)
    _xattn_kernel(o_ref, g_ref, wq_ref, k_ref, v_ref, wxo_ref, o_ref)


def _mix_xattn(x2d, h_ml, h_ns, act2d, wm, wn, wo, g, wq, kv, wxo, *, gate_col0, seq, tm):
    n, d = x2d.shape
    n_mem = kv.shape[1]
    tiles_per_batch = seq // tm

    def res(w):
        return pl.BlockSpec(w.shape, lambda i: (0, 0), pipeline_mode=pl.Buffered(1))

    return pl.pallas_call(
        _mix_xattn_kernel,
        grid=(n // tm,),
        in_specs=[
            pl.BlockSpec((tm, d), lambda i: (i, 0)),
            pl.BlockSpec((tm, h_ml.shape[1]), lambda i: (i, 0)),
            pl.BlockSpec((tm, h_ns.shape[1]), lambda i: (i, 0)),
            pl.BlockSpec((tm, d), lambda i: (i, gate_col0)),
            pl.BlockSpec((tm, d), lambda i: (i, gate_col0 + 1)),
            res(wm), res(wn), res(wo),
            pl.BlockSpec((1, d), lambda i: (0, 0)),
            res(wq),
            pl.BlockSpec((1, n_mem, d), lambda i: (i // tiles_per_batch, 0, 0)),
            pl.BlockSpec((1, n_mem, d), lambda i: (i // tiles_per_batch, 0, 1)),
            res(wxo),
        ],
        out_specs=pl.BlockSpec((tm, d), lambda i: (i, 0)),
        out_shape=jax.ShapeDtypeStruct((n, d), F32),
        compiler_params=_cparams("parallel"),
        name="mix_xattn",
    )(x2d, h_ml, h_ns, act2d, act2d, wm, wn, wo, g.reshape(1, d), wq, kv, kv, wxo)


def _route(logits, bg, be):
    G, EG = MOE_GROUPS, MOE_EXPERTS_PER_GROUP
    tm = logits.shape[1]
    lg = logits[0:G] + bg
    eg = jnp.exp(lg - jnp.max(lg, axis=0, keepdims=True))
    pg = eg / jnp.sum(eg, axis=0, keepdims=True)
    g_val = jnp.max(pg, axis=0, keepdims=True)
    g_row = lax.broadcasted_iota(jnp.int32, (G, tm), 0)
    g_idx = jnp.min(jnp.where(pg == g_val, g_row, G), axis=0, keepdims=True)
    el = logits[8:8 + G * EG] + be
    e_in = jnp.zeros((EG, tm), F32)
    for gg in range(G):
        e_in = jnp.where(g_idx == gg, el[gg * EG:(gg + 1) * EG], e_in)
    e_row = lax.broadcasted_iota(jnp.int32, (EG, tm), 0)
    v1 = jnp.max(e_in, axis=0, keepdims=True)
    i1 = jnp.min(jnp.where(e_in == v1, e_row, EG), axis=0, keepdims=True)
    rest = jnp.where(e_row == i1, -jnp.inf, e_in)
    v2 = jnp.max(rest, axis=0, keepdims=True)
    i2 = jnp.min(jnp.where(rest == v2, e_row, EG), axis=0, keepdims=True)
    e2 = jnp.exp(v2 - v1)
    c1 = g_val / (1.0 + e2)
    c2 = g_val * e2 / (1.0 + e2)
    ex = lax.broadcasted_iota(jnp.int32, (G * EG, tm), 0)
    base = g_idx * EG
    return jnp.where(ex == base + i1, c1, 0.0) + jnp.where(ex == base + i2, c2, 0.0), g_idx


def _moe_kernel(x_ref, g_ref, wr_ref, bg_ref, be_ref, before_ref, w1_ref, w3_ref, w2_ref, gf_ref, o_ref,
                h_scr, slot_scr, wt_scr, acc_scr, *, caps):
    G, EG = MOE_GROUPS, MOE_EXPERTS_PER_GROUP
    grp = pl.program_id(1)
    tm = x_ref.shape[0]

    @pl.when(grp == 0)
    def _():
        h = _rms(x_ref[...], g_ref[...])
        h_hi = h.astype(BF16)
        h_scr[...] = h_hi
        h_lo = (h - h_hi.astype(F32)).astype(BF16)
        w_hi = wr_ref[...].astype(BF16)
        w_lo = (wr_ref[...] - w_hi.astype(F32)).astype(BF16)
        logits = _dot_nt(w_hi, h_hi) + _dot_nt(w_hi, h_lo) + _dot_nt(w_lo, h_hi)
        wt, g_idx = _route(logits, bg_ref[...], be_ref[...])
        member = jnp.where(lax.broadcasted_iota(jnp.int32, (8, tm), 0) == g_idx, 1.0, 0.0)
        prefix = _dot(member.astype(BF16), before_ref[...])
        slots = jnp.where(member > 0.0, prefix, -1.0)
        zeros = jnp.zeros((8 - EG, tm), F32)
        for gg in range(G):
            slot_scr[gg] = jnp.broadcast_to(slots[gg:gg + 1], (8, tm))
            wt_scr[gg] = jnp.concatenate([wt[gg * EG:(gg + 1) * EG], zeros], axis=0)
        acc_scr[...] = jnp.zeros_like(acc_scr)

    slot = slot_scr[grp][0:1, :]
    n_rows = jnp.max(slot).astype(jnp.int32) + 1
    wts = wt_scr[grp]

    def one_pass(cap, first_row):
        row = lax.broadcasted_iota(jnp.int32, (cap, tm), 0).astype(F32)
        pick = jnp.where(row == slot - first_row.astype(F32), 1.0, 0.0)
        pick_b = pick.astype(BF16)
        hsub = _dot(pick_b, h_scr[...]).astype(BF16)
        y = jnp.zeros((cap, x_ref.shape[1]), F32)
        for e in range(EG):
            w_e = jnp.sum(pick * wts[e:e + 1, :], axis=1, keepdims=True)
            a = _dot(hsub, w1_ref[e])
            a = a * jax.nn.sigmoid(a) * _dot(hsub, w3_ref[e]) * w_e
            y = y + _dot(a.astype(BF16), w2_ref[e])
        acc_scr[...] += _dot_tn(pick_b, y.astype(BF16))

    *small_caps, cap_long = caps
    lower = 0
    for cap in small_caps:
        @pl.when((n_rows > lower) & (n_rows <= cap))
        def _(cap=cap):
            one_pass(cap, jnp.int32(0))
        lower = cap

    @pl.when(n_rows > lower)
    def _():
        def body(p, carry):
            one_pass(cap_long, p * cap_long)
            return carry

        lax.fori_loop(0, (n_rows + cap_long - 1) // cap_long, body, 0)

    @pl.when(grp == pl.num_programs(1) - 1)
    def _():
        o_ref[...] = _rms(x_ref[...] + acc_scr[...], gf_ref[...])


def _moe(x2d, g, wr, bg, be, w1, w3, w2, g_final, *, tm, caps):
    n, d = x2d.shape
    E, _, hid = w1.shape
    G, EG = MOE_GROUPS, MOE_EXPERTS_PER_GROUP
    before = jnp.asarray(np.arange(tm)[:, None] < np.arange(tm)[None, :], BF16)
    once = pl.Buffered(1)

    def const(shape):
        return pl.BlockSpec(shape, lambda i, e: (0,) * len(shape), pipeline_mode=once)

    return pl.pallas_call(
        functools.partial(_moe_kernel, caps=caps),
        grid=(n // tm, G),
        in_specs=[
            pl.BlockSpec((tm, d), lambda i, e: (i, 0), pipeline_mode=once),
            const((1, d)), const(wr.shape), const(bg.shape), const(be.shape), const((tm, tm)),
            pl.BlockSpec((EG, d, hid), lambda i, e: (e, 0, 0)),
            pl.BlockSpec((EG, d, hid), lambda i, e: (e, 0, 0)),
            pl.BlockSpec((EG, hid, d), lambda i, e: (e, 0, 0)),
            const((1, d)),
        ],
        out_specs=pl.BlockSpec((tm, d), lambda i, e: (i, 0)),
        out_shape=jax.ShapeDtypeStruct((n, d), F32),
        scratch_shapes=[pltpu.VMEM((tm, d), BF16), pltpu.VMEM((G, 8, tm), F32), pltpu.VMEM((G, 8, tm), F32),
                        pltpu.VMEM((tm, d), F32)],
        compiler_params=_cparams("parallel", "arbitrary"),
        name="moe",
    )(x2d, g.reshape(1, d), wr, bg, be, before, w1, w3, w2, g_final.reshape(1, d))


def _overlap_t(seq):
    n_cmp_pad = seq // CMP_STRIDE
    cs = np.arange(n_cmp_pad)[None, :] * CMP_STRIDE
    ss = np.arange(seq // SEL_BLOCK)[:, None] * SEL_BLOCK
    ov = np.clip(np.minimum(cs + CMP_BLOCK, ss + SEL_BLOCK) - np.maximum(cs, ss), 0, None) / CMP_BLOCK
    ov[:, n_cmp_pad - 1] = 0.0
    return jnp.asarray(ov, BF16)


def _layer(x, mem, norm_mix, w_in, conv_qk, b_igate, b_fgate, mlstm_norm, cmp_pos_k, cmp_pos_v, cmp_k_w1, cmp_k_w2,
           cmp_v_w1, cmp_v_w2, w_br_mlstm, w_br_nsa, w_mix_out, norm_xattn, norm_mem, xa_wq, xa_wkv, xa_wo, norm_ffn,
           router_group_w, router_group_b, router_expert_w, router_expert_b, moe_w1, moe_w3, moe_w2, norm_final):
    B, S, D = x.shape
    N = B * S
    H, d = ML_HEADS, ML_HEAD_DIM
    G, HG, dh = NSA_KV_GROUPS, NSA_GROUP_HEADS, NSA_HEAD_DIM
    x2d = x.reshape(N, D)

    o_mlqkvo = 0
    o_mlif = 4 * ML_WIDTH
    o_nsq = o_mlif + 2 * H
    o_kv = o_nsq + NSA_WIDTH
    o_nsg = o_kv + 6 * NSA_KV_WIDTH
    o_merge = o_nsg + NSA_N_BRANCH * NSA_HEADS
    o_kv_sel = o_kv + 2 * NSA_KV_WIDTH
    w_act = jnp.concatenate([w_in[:, o_merge:o_merge + 2 * D], w_in[:, o_mlqkvo:o_mlif], w_in[:, o_nsq:o_kv],
                             w_in[:, o_kv_sel:o_nsg]], axis=1).astype(BF16)
    n_small = 2 * H + NSA_N_BRANCH * NSA_HEADS
    w_small = jnp.concatenate([w_in[:, o_mlif:o_nsq], w_in[:, o_nsg:o_merge], jnp.zeros((D, LANES - n_small), F32),
                               w_in[:, o_kv:o_kv_sel]], axis=1).astype(BF16)
    act, small, cmp_kv = _in_proj(x2d, norm_mix, w_act, w_small, tm=1024, tn=2560)
    act3 = act.reshape(B, S, act.shape[1])
    c_ml = 2 * D
    c_nsq = c_ml + 4 * ML_WIDTH
    c_kv = c_nsq + NSA_WIDTH

    small3 = small.reshape(B, S, small.shape[1])
    gate_bias = jnp.zeros((1, LANES), F32).at[0, 0:H].set(b_igate).at[0, H:2 * H].set(b_fgate)
    conv_w = jnp.concatenate([conv_qk[:, :ML_WIDTH].reshape(ML_CONV, H, d), conv_qk[:, ML_WIDTH:].reshape(ML_CONV, H, d)],
                             axis=-1).transpose(1, 0, 2)
    h_ml = _mlstm(act3, small3, gate_bias, conv_w, mlstm_norm.reshape(H, 1, d), col0=c_ml // ML_WIDTH, chunk=512)

    pos = jnp.tile(jnp.stack([cmp_pos_k, cmp_pos_v])[:, :, None, :], (1, 1, 1, G))
    w1c = jnp.stack([cmp_k_w1, cmp_v_w1]).reshape(2, CMP_BLOCK, dh, -1)
    w1c = jnp.einsum('arld,gh->arghld', w1c, jnp.eye(G, dtype=F32))
    w1c = w1c.reshape(2, CMP_BLOCK, G, G * dh, -1).astype(BF16)
    w2c = jnp.stack([cmp_k_w2, cmp_v_w2]).astype(BF16)
    kvc = _compress(cmp_kv.reshape(B, S, cmp_kv.shape[1]), pos, w1c, w2c, col0=0)
    n_slc = S // SEL_BLOCK
    tok = np.arange(S)
    kconst = np.zeros((S, n_slc + 2 * dh), np.float32)
    kconst[tok, tok // SEL_BLOCK] = 1.0
    kconst[:, n_slc:n_slc + 3] = (tok // SEL_BLOCK)[:, None]
    kconst[:, n_slc + 3:n_slc + 6] = (tok % SEL_BLOCK)[:, None]
    h_ns = _nsa_attention(act3, small3, kvc[0], kvc[1], jnp.asarray(kconst, BF16), _overlap_t(S),
                          q_col0=c_nsq // (HG * dh), kv_col0=c_kv // NSA_KV_WIDTH,
                          gate_col0=2 * H, tq=256)

    n_mem = mem.shape[1]
    kv_mem = _norm_matmul(mem.reshape(B * n_mem, D), norm_mem, xa_wkv.astype(BF16), BF16, tm=B * n_mem, tn=512)
    x2 = _mix_xattn(x2d, h_ml.reshape(N, ML_WIDTH), h_ns.reshape(N, NSA_WIDTH), act, w_br_mlstm.astype(BF16),
                    w_br_nsa.astype(BF16), w_mix_out.astype(BF16), norm_xattn, xa_wq.astype(BF16),
                    kv_mem.reshape(B, n_mem, 2 * D), xa_wo.astype(BF16), gate_col0=0, seq=S, tm=1024)

    wr = jnp.zeros((ROUTER_ROWS, D), F32)
    wr = wr.at[0:MOE_GROUPS].set(router_group_w.T).at[8:8 + MOE_EXPERTS].set(router_expert_w.T)
    return _moe(x2, norm_ffn, wr, router_group_b.reshape(MOE_GROUPS, 1), router_expert_b.reshape(MOE_EXPERTS, 1),
                moe_w1.astype(BF16), moe_w3.astype(BF16), moe_w2.astype(BF16), norm_final, tm=1024, caps=(256, 304, 352)).reshape(B, S, D)


def kernel(x, mem, norm_mix, w_in, conv_qk, b_igate, b_fgate, mlstm_norm, cmp_pos_k, cmp_pos_v, cmp_k_w1, cmp_k_w2, cmp_v_w1, cmp_v_w2, w_br_mlstm, w_br_nsa, w_mix_out, norm_xattn, norm_mem, xa_wq, xa_wkv, xa_wo, norm_ffn, router_group_w, router_group_b, router_expert_w, router_expert_b, moe_w1, moe_w3, moe_w2, norm_final):
    depth = w_in.shape[0]
    assert depth == 1, "the fused final norm assumes a single layer"
    layer = 0
    return _layer(x, mem, norm_mix[layer], w_in[layer], conv_qk[layer], b_igate[layer], b_fgate[layer], mlstm_norm[layer],
                  cmp_pos_k[layer], cmp_pos_v[layer], cmp_k_w1[layer], cmp_k_w2[layer], cmp_v_w1[layer], cmp_v_w2[layer],
                  w_br_mlstm[layer], w_br_nsa[layer], w_mix_out[layer], norm_xattn[layer], norm_mem[layer], xa_wq[layer],
                  xa_wkv[layer], xa_wo[layer], norm_ffn[layer], router_group_w[layer], router_group_b[layer],
                  router_expert_w[layer], router_expert_b[layer], moe_w1[layer], moe_w3[layer], moe_w2[layer], norm_final)
```

```python
import functools

import numpy as np
import jax
import jax.numpy as jnp
from jax import lax
from jax.experimental import pallas as pl
from jax.experimental.pallas import tpu as pltpu

F32 = jnp.float32
BF16 = jnp.bfloat16

ML_HEADS = 4
ML_HEAD_DIM = 128
ML_WIDTH = ML_HEADS * ML_HEAD_DIM
ML_CONV = 4
NSA_HEADS = 8
NSA_KV_GROUPS = 2
NSA_HEAD_DIM = 64
NSA_GROUP_HEADS = NSA_HEADS // NSA_KV_GROUPS
NSA_WIDTH = NSA_HEADS * NSA_HEAD_DIM
NSA_KV_WIDTH = NSA_KV_GROUPS * NSA_HEAD_DIM
NSA_N_BRANCH = 3
CMP_BLOCK = 32
CMP_STRIDE = 16
SEL_BLOCK = 64
SEL_COUNT = 16
SEL_FORCE = 1e4
WINDOW = 512
XA_HEADS = 4
MOE_GROUPS = 4
MOE_EXPERTS_PER_GROUP = 4
MOE_EXPERTS = MOE_GROUPS * MOE_EXPERTS_PER_GROUP
RMS_EPS = 1e-6
NEG_INF = -1e30
SEL_MASK = -float(2 ** 30)
LOG2E = 1.4426950408889634

LANES = 128
VMEM_LIMIT_BYTES = 56 * 1024 * 1024
ROUTER_ROWS = 32
NSA_PEEL = 2


def _cparams(*sem):
    return pltpu.CompilerParams(dimension_semantics=sem, vmem_limit_bytes=VMEM_LIMIT_BYTES)


def _rms(x, g):
    return x * lax.rsqrt(jnp.mean(x * x, axis=-1, keepdims=True) + RMS_EPS) * g


def _dot(a, b):
    return jnp.dot(a, b, preferred_element_type=F32)


def _dot_nt(a, b):
    return lax.dot_general(a, b, (((1,), (1,)), ((), ())), preferred_element_type=F32)


def _dot_tn(a, b):
    return lax.dot_general(a, b, (((0,), (0,)), ((), ())), preferred_element_type=F32)


def _norm_matmul_kernel(x_ref, g_ref, w_ref, o_ref, h_ref):
    @pl.when(pl.program_id(1) == 0)
    def _():
        h_ref[...] = _rms(x_ref[...], g_ref[...]).astype(BF16)

    o_ref[...] = _dot(h_ref[...], w_ref[...]).astype(o_ref.dtype)


def _norm_matmul(x2d, g, w, out_dtype, tm, tn):
    m, d = x2d.shape
    n = w.shape[1]
    return pl.pallas_call(
        _norm_matmul_kernel,
        grid=(m // tm, n // tn),
        in_specs=[
            pl.BlockSpec((tm, d), lambda i, j: (i, 0)),
            pl.BlockSpec((1, d), lambda i, j: (0, 0)),
            pl.BlockSpec((d, tn), lambda i, j: (0, j)),
        ],
        out_specs=pl.BlockSpec((tm, tn), lambda i, j: (i, j)),
        out_shape=jax.ShapeDtypeStruct((m, n), out_dtype),
        scratch_shapes=[pltpu.VMEM((tm, d), BF16)],
        compiler_params=_cparams("parallel", "arbitrary"),
        name="norm_matmul",
    )(x2d, g.reshape(1, d), w)


def _in_proj_kernel(x_ref, g_ref, w_ref, ws_ref, o_ref, og_ref, oc_ref, h_ref):
    @pl.when(pl.program_id(1) == 0)
    def _():
        h_ref[...] = _rms(x_ref[...], g_ref[...]).astype(BF16)
        small = _dot(h_ref[...], ws_ref[...])
        og_ref[...] = small[:, :LANES]
        oc_ref[...] = small[:, LANES:]

    o_ref[...] = _dot(h_ref[...], w_ref[...]).astype(o_ref.dtype)


def _in_proj(x2d, g, w, w_small, tm, tn):
    m, d = x2d.shape
    n = w.shape[1]
    ns = w_small.shape[1]
    return pl.pallas_call(
        _in_proj_kernel,
        grid=(m // tm, n // tn),
        in_specs=[
            pl.BlockSpec((tm, d), lambda i, j: (i, 0)),
            pl.BlockSpec((1, d), lambda i, j: (0, 0)),
            pl.BlockSpec((d, tn), lambda i, j: (0, j)),
            pl.BlockSpec((d, ns), lambda i, j: (0, 0)),
        ],
        out_specs=[pl.BlockSpec((tm, tn), lambda i, j: (i, j)), pl.BlockSpec((tm, LANES), lambda i, j: (i, 0)),
                   pl.BlockSpec((tm, ns - LANES), lambda i, j: (i, 0))],
        out_shape=[jax.ShapeDtypeStruct((m, n), BF16), jax.ShapeDtypeStruct((m, LANES), F32),
                   jax.ShapeDtypeStruct((m, ns - LANES), F32)],
        scratch_shapes=[pltpu.VMEM((tm, d), BF16)],
        compiler_params=_cparams("parallel", "arbitrary"),
        name="in_proj",
    )(x2d, g.reshape(1, d), w, w_small)


def _mlstm_kernel(q_ref, k_ref, v_ref, o_ref, qt_ref, kt_ref, gate_ref, gbias_ref,
                  cw_ref, ng_ref, upper_ref, out_ref, c_scr, n_scr, m_scr, *, chunk):
    L, d = chunk, ML_HEAD_DIM
    c = pl.program_id(1)

    @pl.when(c == 0)
    def _():
        c_scr[...] = jnp.zeros_like(c_scr)
        n_scr[...] = jnp.zeros_like(n_scr)
        m_scr[...] = jnp.zeros_like(m_scr)

    t_idx = lax.broadcasted_iota(jnp.int32, (L, L), 0)
    s_idx = lax.broadcasted_iota(jnp.int32, (L, L), 1)
    tri = s_idx <= t_idx

    row8 = lax.broadcasted_iota(jnp.int32, (8, d), 0)

    def conv_silu(x, tail, w):
        x = x.astype(F32)
        tail = jnp.where(c > 0, tail.astype(F32), 0.0)
        y = x * w[ML_CONV - 1:ML_CONV, :]
        for j in range(ML_CONV - 1):
            s = ML_CONV - 1 - j
            r = pltpu.roll(x, s, 0)
            head = jnp.where(row8 < s, pltpu.roll(tail, s, 0), r[:8])
            y = y + jnp.concatenate([head, r[8:]], axis=0) * w[j:j + 1, :]
        return y * (0.5 * jnp.tanh(0.5 * y) + 0.5)

    g_cols = gate_ref[0] + gbias_ref[...]
    g_rows = g_cols.T
    f_rows = jax.nn.log_sigmoid(g_rows[ML_HEADS:2 * ML_HEADS, :])
    f_hi = f_rows.astype(BF16)
    f_mid = (f_rows - f_hi.astype(F32)).astype(BF16)
    f_lo = (f_rows - f_hi.astype(F32) - f_mid.astype(F32)).astype(BF16)
    pad = jnp.zeros((16 - 3 * ML_HEADS, L), BF16)
    parts = _dot(jnp.concatenate([f_hi, f_mid, f_lo, pad], axis=0), upper_ref[...])
    b_rows = parts[0:ML_HEADS] + parts[ML_HEADS:2 * ML_HEADS] + parts[2 * ML_HEADS:3 * ML_HEADS]
    b_cols = jnp.concatenate([b_rows, jnp.zeros((8 - ML_HEADS, L), F32)], axis=0).T

    outs = []
    for h in range(ML_HEADS):
        sl = slice(h * d, (h + 1) * d)
        cw = cw_ref[h]
        q = conv_silu(q_ref[0, :, sl], qt_ref[0, :, sl], cw[:, :d])
        k = conv_silu(k_ref[0, :, sl], kt_ref[0, :, sl], cw[:, d:]) * (d ** -0.5)
        v = v_ref[0, :, sl]
        qb = q.astype(BF16)
        kb = k.astype(BF16)

        i_col = g_cols[:, h:h + 1]
        i_row = g_rows[h:h + 1, :]
        f_row = f_rows[h:h + 1, :]
        b_col = b_cols[:, h:h + 1]
        b_row = b_rows[h:h + 1, :]
        dlog = jnp.where(tri, b_col + (i_row - b_row), NEG_INF)
        m_prev = m_scr[h]
        inter = b_col + m_prev
        mt = jnp.maximum(jnp.max(dlog, axis=1, keepdims=True), inter)
        w_intra = jnp.exp(dlog - mt)
        w_inter = jnp.exp(inter - mt)

        s = _dot_nt(qb, kb) * w_intra
        cmat = c_scr[h]
        nvec = n_scr[h]
        num = _dot(s.astype(BF16), v) + w_inter * _dot(qb, cmat.astype(BF16))
        den = jnp.sum(s, axis=1, keepdims=True) + w_inter * jnp.sum(q * nvec, axis=1, keepdims=True)
        hc = num / jnp.maximum(jnp.abs(den), jnp.exp(-mt))

        bl = jnp.sum(f_row, axis=1, keepdims=True)
        logw = bl - b_col + i_col
        m_new = jnp.maximum(bl + m_prev, jnp.max(logw, axis=0, keepdims=True))
        decay = jnp.exp(bl + m_prev - m_new)
        kw = k * jnp.exp(logw - m_new)
        c_scr[h] = decay * cmat + _dot_tn(kw.astype(BF16), v)
        n_scr[h] = decay * nvec + jnp.sum(kw, axis=0, keepdims=True)
        m_scr[h] = m_new

        outs.append(jax.nn.sigmoid(o_ref[0, :, sl].astype(F32)) * _rms(hc, ng_ref[h]))
    out_ref[0] = jnp.concatenate(outs, axis=1).astype(out_ref.dtype)


def _mlstm(act, gates, gate_bias, conv_w, norm_g, *, col0, chunk):
    B, S, _ = act.shape
    H, d, L = ML_HEADS, ML_HEAD_DIM, chunk
    tail_blocks = L // 8
    upper = jnp.asarray(np.triu(np.ones((L, L))), BF16)

    def blk(off):
        return pl.BlockSpec((1, L, H * d), lambda b, c: (b, c, col0 + off))

    def tail(off):
        return pl.BlockSpec((1, 8, H * d), lambda b, c: (b, jnp.maximum(c * tail_blocks - 1, 0), col0 + off))

    def const(a):
        return pl.BlockSpec(a.shape, lambda b, c: (0,) * a.ndim)

    return pl.pallas_call(
        functools.partial(_mlstm_kernel, chunk=L),
        grid=(B, S // L),
        in_specs=[
            blk(0), blk(1), blk(2), blk(3), tail(0), tail(1),
            pl.BlockSpec((1, L, LANES), lambda b, c: (b, c, 0)),
            const(gate_bias), const(conv_w), const(norm_g), const(upper),
        ],
        out_specs=pl.BlockSpec((1, L, H * d), lambda b, c: (b, c, 0)),
        out_shape=jax.ShapeDtypeStruct((B, S, H * d), BF16),
        scratch_shapes=[pltpu.VMEM((H, d, d), F32), pltpu.VMEM((H, 1, d), F32), pltpu.VMEM((H, 1, 1), F32)],
        compiler_params=_cparams("parallel", "arbitrary"),
        name="mlstm",
    )(act, act, act, act, act, act, gates, gate_bias, conv_w, norm_g, upper)


def _compress_kernel(x_ref, pos_ref, w1_ref, w2_ref, o_ref):
    n = x_ref.shape[1] // CMP_STRIDE
    groups = o_ref.shape[2]
    for g in range(groups):
        top = jnp.zeros((n, w1_ref.shape[-1]), F32)
        bot = jnp.zeros((n, w1_ref.shape[-1]), F32)
        for r in range(CMP_STRIDE):
            x = x_ref[0, pl.ds(r, n, stride=CMP_STRIDE), :]
            top = top + _dot((x + pos_ref[0, r]).astype(BF16), w1_ref[0, r, g])
            bot = bot + _dot((x + pos_ref[0, CMP_STRIDE + r]).astype(BF16), w1_ref[0, CMP_STRIDE + r, g])
        pre = top + pltpu.roll(bot, n - 1, 0)
        hid = jax.nn.gelu(pre, approximate=True)
        out = _dot(hid.astype(BF16), w2_ref[0])
        row = lax.broadcasted_iota(jnp.int32, out.shape, 0)
        out = jnp.where(row < n - 1, out, 0.0)
        j = lax.broadcasted_iota(jnp.int32, out.shape, 0)
        lane = lax.broadcasted_iota(jnp.int32, out.shape, 1)
        cols = jnp.where(lane < 3, j // 16, jnp.where(lane < 6, j % 16, jnp.where(lane < 9, 1, 0))).astype(F32)
        cols = jnp.where(pl.program_id(0) == 0, cols, jnp.where(lane == 0, 1.0, 0.0))
        o_ref[0, 0, g] = jnp.concatenate([out, cols], axis=1).astype(o_ref.dtype)


def _compress(gates, pos, w1, w2, *, col0):
    B, S, _ = gates.shape
    G, width = w1.shape[2], w1.shape[3]
    hidden = w1.shape[-1]
    dh = w2.shape[-1]
    n = S // CMP_STRIDE
    return pl.pallas_call(
        _compress_kernel,
        grid=(2, B),
        in_specs=[
            pl.BlockSpec((1, S, width), lambda a, b: (b, 0, col0 + a)),
            pl.BlockSpec((1,) + pos.shape[1:], lambda a, b: (a, 0, 0, 0)),
            pl.BlockSpec((1,) + w1.shape[1:], lambda a, b: (a, 0, 0, 0, 0)),
            pl.BlockSpec((1, hidden, dh), lambda a, b: (a, 0, 0)),
        ],
        out_specs=pl.BlockSpec((1, 1, G, n, 2 * dh), lambda a, b: (a, b, 0, 0, 0)),
        out_shape=jax.ShapeDtypeStruct((2, B, G, n, 2 * dh), BF16),
        compiler_params=_cparams("parallel", "parallel"),
        name="nsa_compress",
    )(gates, pos, w1, w2)


def _nsa_kernel(q_ref, gate_ref, kc_ref, vc_ref, ksl_ref, vsl_ref, kwn_ref, vwn_ref, kconst_ref, ovl_ref, tri_ref, cpat_ref,
                out_ref, ks_scr, vs_scr, kw_scr, vw_scr, s_scr, mrun_scr, acc_scr, *, tq, seq, gate_col0):
    HG, dh = NSA_GROUP_HEADS, NSA_HEAD_DIM
    g = pl.program_id(1)
    i = pl.program_id(2)
    t0 = i * tq
    rows = HG * tq
    n_cmp_pad = kc_ref.shape[2]
    n_slc = seq // SEL_BLOCK
    half = tq // 2
    blocks_per_chunk = tq // SEL_BLOCK
    spare = seq // tq
    PEEL = NSA_PEEL

    def assemble(lo):
        ks_scr[...] = jnp.concatenate([ksl_ref[0][:, lo:lo + dh], kconst_ref[...]], axis=1)
        kw_scr[...] = jnp.concatenate([kwn_ref[0][:, lo:lo + dh], kconst_ref[:, n_slc:n_slc + dh]], axis=1)
        ones_col = jnp.where(lax.broadcasted_iota(jnp.int32, (seq, dh), 1) == 0, 1.0, 0.0).astype(BF16)
        vs_scr[...] = jnp.concatenate([vsl_ref[0][:, lo:lo + dh], ones_col], axis=1)
        vw_scr[...] = jnp.concatenate([vwn_ref[0][:, lo:lo + dh], ones_col], axis=1)

    for gg in range(NSA_KV_GROUPS):
        @pl.when((i == 0) & (g == gg))
        def _(gg=gg):
            assemble(gg * dh)

    q_all = q_ref[0]
    q_heads = [(q_all[:, x * dh:(x + 1) * dh].astype(F32) * (dh ** -0.5 * LOG2E)).astype(BF16) for x in range(HG)]

    def head_slope(x):
        sl = jnp.float32(0.0)
        for hh in range(NSA_HEADS):
            sl = jnp.where(g * HG + x == hh, 2.0 ** (-8.0 * (hh + 1) / NSA_HEADS), sl)
        return sl

    slopes = [head_slope(x) for x in range(HG)]

    def alibi_cols(x, width, coef):
        lane = lax.broadcasted_iota(jnp.int32, (tq, width), 1)
        v = jnp.zeros((tq, width), F32)
        for c, val in enumerate(coef):
            v = jnp.where(lane // 3 == c, slopes[x] * (val * LOG2E), v)
        hi = v.astype(BF16).astype(F32)
        mid = (v - hi).astype(BF16).astype(F32)
        lo = v - hi - mid
        return jnp.where(lane % 3 == 0, hi, jnp.where(lane % 3 == 1, mid, lo)).astype(BF16)

    def stack_q(width, coef, extra=None):
        parts = []
        for x in range(HG):
            cols = [q_heads[x]] + ([extra] if extra is not None else []) + [alibi_cols(x, width, coef)]
            parts.append(jnp.concatenate(cols, axis=1))
        return jnp.concatenate(parts, axis=0)

    q_cmp = stack_q(dh, (16.0 * CMP_STRIDE, 1.0 * CMP_STRIDE, (CMP_BLOCK - 1) / 2))
    sc = _dot_nt(q_cmp, kc_ref[0, 0])
    block_done = jnp.concatenate([cpat_ref[...]] * HG, axis=0) <= t0
    sc = jnp.where(block_done, sc, NEG_INF)
    e = jnp.exp2(sc - jnp.max(sc, axis=1, keepdims=True))
    t_row = t0 + lax.broadcasted_iota(jnp.int32, (rows, 1), 0) % tq
    any_valid = jnp.where(t_row >= CMP_BLOCK - 1, 1.0, 0.0)
    p_cmp = e * (any_valid / jnp.sum(e, axis=1, keepdims=True))
    o_cmp = _dot(p_cmp.astype(BF16), vc_ref[0, 0])[:, :dh]

    p_sum = p_cmp[0:tq]
    for x in range(1, HG):
        p_sum = p_sum + p_cmp[x * tq:(x + 1) * tq]
    p_hi = p_sum.astype(BF16)
    p_lo = (p_sum - p_hi.astype(F32)).astype(BF16)
    ovl = ovl_ref[...]
    p_slc = _dot_nt(ovl, p_hi) + _dot_nt(ovl, p_lo)
    blk = lax.broadcasted_iota(jnp.int32, (n_slc, tq), 0)
    cur = (t0 + lax.broadcasted_iota(jnp.int32, (n_slc, tq), 1)) // SEL_BLOCK
    forced = (blk == 0) | (blk == cur) | (blk == cur - 1)
    score = jnp.where(forced, SEL_FORCE, jnp.where(blk > cur, -SEL_FORCE, p_slc))
    n_tiles = n_slc // 8
    tiles = [score[8 * a:8 * a + 8] for a in range(n_tiles)]
    ranks = [jnp.zeros((8, tq), F32) for _ in range(n_tiles)]
    sub = lax.broadcasted_iota(jnp.int32, (8, tq), 0)
    for kk in range(n_slc):
        sk = score[kk:kk + 1, :]
        for a in range(n_tiles):
            if a < kk // 8:
                before = sk > tiles[a]
            elif a > kk // 8:
                before = sk >= tiles[a]
            else:
                before = jnp.where(sub > kk % 8, jnp.where(sk >= tiles[a], 1.0, 0.0), jnp.where(sk > tiles[a], 1.0, 0.0)) > 0.5
            ranks[a] = ranks[a] + jnp.where(before, 1.0, 0.0)
    rank = jnp.concatenate(ranks, axis=0)
    selected = rank < min(SEL_COUNT, n_slc)
    sel_bias = jnp.where(selected, 0.0, SEL_MASK).T.astype(BF16)
    first_blk = jnp.min(jnp.where(selected & (blk >= blocks_per_chunk), blk, n_slc))
    c_lo = jnp.minimum(jnp.maximum(first_blk // blocks_per_chunk, 1), i)
    q_aug = stack_q(2 * dh, (1.0 * SEL_BLOCK, 1.0), extra=sel_bias)

    causal_bias = jnp.concatenate([tri_ref[0]] * HG, axis=0)
    upper_bias = jnp.concatenate([tri_ref[1]] * HG, axis=0)

    qw_aug = stack_q(dh, (1.0 * SEL_BLOCK, 1.0))
    start_a = pl.multiple_of(jnp.maximum(t0 - 2 * tq, 0), tq)
    start_b = pl.multiple_of(jnp.maximum(t0 - tq, 0), tq)
    start_c = pl.multiple_of(t0, tq)
    s_a = _dot_nt(qw_aug, kw_scr[pl.ds(start_a, tq), :]) + jnp.where(i >= 2, upper_bias, NEG_INF)
    s_b = _dot_nt(qw_aug, kw_scr[pl.ds(start_b, tq), :]) + jnp.where(i >= 1, 0.0, NEG_INF)
    s_c = _dot_nt(qw_aug, kw_scr[pl.ds(start_c, tq), :]) + causal_bias
    m_w = jnp.max(jnp.maximum(jnp.maximum(s_a, s_b), s_c), axis=1, keepdims=True)
    e_a = jnp.exp2(s_a - m_w)
    e_b = jnp.exp2(s_b - m_w)
    e_c = jnp.exp2(s_c - m_w)
    o_win = (_dot(e_a.astype(BF16), vw_scr[pl.ds(start_a, tq), :]) + _dot(e_b.astype(BF16), vw_scr[pl.ds(start_b, tq), :])
             + _dot(e_c.astype(BF16), vw_scr[pl.ds(start_c, tq), :]))

    mrun_scr[...] = jnp.full(mrun_scr.shape, NEG_INF, F32)
    acc_scr[...] = jnp.zeros(acc_scr.shape, F32)

    def scores(c, bias=None, slot=None):
        start = pl.multiple_of(c * tq, tq)
        s = _dot_nt(q_aug, ks_scr[pl.ds(start, tq), :])
        if bias is not None:
            s = s + bias
        s_scr[c if slot is None else slot] = s
        mrun_scr[...] = jnp.maximum(mrun_scr[...], jnp.maximum(s[:, :half], s[:, half:]))

    def weights(c, slot=None):
        start = pl.multiple_of(c * tq, tq)
        s = s_scr[c if slot is None else slot]
        m_b = mrun_scr[...]
        p0 = jnp.exp2(s[:, :half] - m_b)
        p1 = jnp.exp2(s[:, half:] - m_b)
        p = jnp.concatenate([p0, p1], axis=1).astype(BF16)
        acc_scr[...] += _dot(p, vs_scr[pl.ds(start, tq), :])

    def chunk_pairs(lo, hi, fn):
        def pair(t, carry):
            fn(lo + 2 * t)
            fn(lo + 2 * t + 1)
            return carry

        n = jnp.maximum(hi - lo, 0)
        lax.fori_loop(0, n // 2, pair, 0)

        @pl.when(n % 2 == 1)
        def _():
            fn(hi - 1)

    scores(i, bias=causal_bias)
    scores(0, bias=jnp.where(i > 0, 0.0, NEG_INF), slot=spare)
    peeled = []
    for kk in range(PEEL):
        c = c_lo + kk
        valid = c < i
        peeled.append((jnp.minimum(c, jnp.maximum(i - 1, 0)), jnp.where(valid, c, spare + 1 + kk), valid))
    for c, slot, valid in peeled:
        scores(c, bias=jnp.where(valid, 0.0, NEG_INF), slot=slot)
    chunk_pairs(c_lo + PEEL, i, scores)
    m_row = jnp.max(mrun_scr[...], axis=1, keepdims=True)
    mrun_scr[...] = jnp.broadcast_to(m_row, mrun_scr.shape)

    chunk_pairs(c_lo + PEEL, i, weights)
    for c, slot, _ in peeled:
        weights(c, slot=slot)
    weights(0, slot=spare)
    weights(i)
    gates = jax.nn.sigmoid(gate_ref[0])

    def gate_col(r):
        cols = []
        for x in range(HG):
            per_group = [gate_col0 + (gg * HG + x) * NSA_N_BRANCH + r for gg in range(NSA_KV_GROUPS)]
            col = gates[:, per_group[0]:per_group[0] + 1]
            for gg in range(1, NSA_KV_GROUPS):
                col = jnp.where(g == gg, gates[:, per_group[gg]:per_group[gg] + 1], col)
            cols.append(col)
        return jnp.concatenate(cols, axis=0)

    acc = acc_scr[...]
    o = gate_col(0) * o_cmp + acc[:, :dh] * (gate_col(1) / acc[:, dh:dh + 1])

    o = o + o_win[:, :dh] * (gate_col(2) / o_win[:, dh:dh + 1])

    out_ref[0] = jnp.concatenate([o[x * tq:(x + 1) * tq] for x in range(HG)], axis=1).astype(out_ref.dtype)


def _nsa_attention(act, gates, kc, vc, kconst, overlap_t, *, q_col0, kv_col0, gate_col0, tq):
    B, S, _ = act.shape
    G, HG, dh = NSA_KV_GROUPS, NSA_GROUP_HEADS, NSA_HEAD_DIM
    n_slc = S // SEL_BLOCK
    rows = HG * tq
    assert WINDOW == 2 * tq, "the window branch reads exactly the two chunks before the diagonal one"
    r = np.arange(tq)[:, None]
    l = np.arange(tq)[None, :]
    tri_bias = jnp.asarray(np.stack([np.where(l <= r, 0.0, NEG_INF), np.where(l > r, 0.0, NEG_INF)]), F32)
    cmp_pattern = jnp.asarray(np.arange(S // CMP_STRIDE)[None, :] * CMP_STRIDE + (CMP_BLOCK - 1) - r, jnp.int32)

    def kv_spec(off):
        return pl.BlockSpec((1, S, G * dh), lambda b, g, i: (b, 0, kv_col0 + off))

    def full(a):
        return pl.BlockSpec((1, 1) + a.shape[2:], lambda b, g, i: (b, g, 0, 0))

    return pl.pallas_call(
        functools.partial(_nsa_kernel, tq=tq, seq=S, gate_col0=gate_col0),
        grid=(B, G, S // tq),
        in_specs=[
            pl.BlockSpec((1, tq, HG * dh), lambda b, g, i: (b, i, q_col0 + g)),
            pl.BlockSpec((1, tq, LANES), lambda b, g, i: (b, i, 0)),
            full(kc), full(vc), kv_spec(0), kv_spec(1), kv_spec(2), kv_spec(3),
            pl.BlockSpec(kconst.shape, lambda b, g, i: (0, 0)),
            pl.BlockSpec(overlap_t.shape, lambda b, g, i: (0, 0)),
            pl.BlockSpec(tri_bias.shape, lambda b, g, i: (0, 0, 0)),
            pl.BlockSpec(cmp_pattern.shape, lambda b, g, i: (0, 0)),
        ],
        out_specs=pl.BlockSpec((1, tq, HG * dh), lambda b, g, i: (b, i, g)),
        out_shape=jax.ShapeDtypeStruct((B, S, NSA_WIDTH), BF16),
        scratch_shapes=[pltpu.VMEM((S, dh + n_slc + 2 * dh), BF16), pltpu.VMEM((S, 2 * dh), BF16),
                        pltpu.VMEM((S, 2 * dh), BF16), pltpu.VMEM((S, 2 * dh), BF16),
                        pltpu.VMEM((S // tq + 1 + NSA_PEEL, rows, tq), F32), pltpu.VMEM((rows, tq // 2), F32),
                        pltpu.VMEM((rows, 2 * dh), F32)],
        compiler_params=_cparams("parallel", "parallel", "arbitrary"),
        name="nsa_attn",
    )(act, gates, kc, vc, act, act, act, act, kconst, overlap_t, tri_bias, cmp_pattern)


def _mix_out_kernel(x_ref, hm_ref, hn_ref, gm_ref, gn_ref, wm_ref, wn_ref, wo_ref, o_ref):
    y_ml = _dot(hm_ref[...], wm_ref[...])
    y_ns = _dot(hn_ref[...], wn_ref[...])
    mix = jax.nn.sigmoid(gm_ref[...].astype(F32)) * y_ml + jax.nn.sigmoid(gn_ref[...].astype(F32)) * y_ns
    o_ref[...] = x_ref[...] + _dot(mix.astype(BF16), wo_ref[...])


def _mix_out(x2d, h_ml, h_ns, act2d, wm, wn, wo, *, gate_col0, tm):
    n, d = x2d.shape

    def res(w):
        return pl.BlockSpec(w.shape, lambda i: (0, 0))

    return pl.pallas_call(
        _mix_out_kernel,
        grid=(n // tm,),
        in_specs=[
            pl.BlockSpec((tm, d), lambda i: (i, 0)),
            pl.BlockSpec((tm, h_ml.shape[1]), lambda i: (i, 0)),
            pl.BlockSpec((tm, h_ns.shape[1]), lambda i: (i, 0)),
            pl.BlockSpec((tm, d), lambda i: (i, gate_col0)),
            pl.BlockSpec((tm, d), lambda i: (i, gate_col0 + 1)),
            res(wm), res(wn), res(wo),
        ],
        out_specs=pl.BlockSpec((tm, d), lambda i: (i, 0)),
        out_shape=jax.ShapeDtypeStruct((n, d), F32),
        compiler_params=_cparams("parallel"),
        name="mix_out",
    )(x2d, h_ml, h_ns, act2d, act2d, wm, wn, wo)


def _xattn_kernel(x_ref, g_ref, wq_ref, k_ref, v_ref, wo_ref, o_ref, kv_leading=True):
    x = x_ref[...]
    d = x.shape[1]
    dh = d // XA_HEADS
    h = _rms(x, g_ref[...]).astype(BF16)
    q = (_dot(h, wq_ref[...]) * (dh ** -0.5)).astype(BF16)
    k = k_ref[0] if kv_leading else k_ref[...]
    v = v_ref[0] if kv_leading else v_ref[...]
    outs = []
    for hd in range(XA_HEADS):
        sl = slice(hd * dh, (hd + 1) * dh)
        s = _dot_nt(q[:, sl], k[:, sl])
        e = jnp.exp(s - jnp.max(s, axis=1, keepdims=True))
        p = e / jnp.sum(e, axis=1, keepdims=True)
        outs.append(_dot(p.astype(BF16), v[:, sl]))
    o = jnp.concatenate(outs, axis=1).astype(BF16)
    o_ref[...] = x + _dot(o, wo_ref[...])


def _xattn(x2d, g, wq, kv, wo, *, seq, tm):
    n, d = x2d.shape
    n_mem = kv.shape[1]
    tiles_per_batch = seq // tm
    return pl.pallas_call(
        _xattn_kernel,
        grid=(n // tm,),
        in_specs=[
            pl.BlockSpec((tm, d), lambda i: (i, 0)),
            pl.BlockSpec((1, d), lambda i: (0, 0)),
            pl.BlockSpec(wq.shape, lambda i: (0, 0)),
            pl.BlockSpec((1, n_mem, d), lambda i: (i // tiles_per_batch, 0, 0)),
            pl.BlockSpec((1, n_mem, d), lambda i: (i // tiles_per_batch, 0, 1)),
            pl.BlockSpec(wo.shape, lambda i: (0, 0)),
        ],
        out_specs=pl.BlockSpec((tm, d), lambda i: (i, 0)),
        out_shape=jax.ShapeDtypeStruct((n, d), F32),
        compiler_params=_cparams("parallel"),
        name="xattn",
    )(x2d, g.reshape(1, d), wq, kv, kv, wo)


def _mix_xattn_kernel(x_ref, hm_ref, hn_ref, gm_ref, gn_ref, wm_ref, wn_ref, wo_ref, g_ref, wq_ref, mem_ref, gmem_ref,
                      wkv_ref, wxo_ref, o_ref, kv_scr, *, tiles_per_batch):
    @pl.when(pl.program_id(0) % tiles_per_batch == 0)
    def _():
        kv_scr[...] = _dot(_rms(mem_ref[0], gmem_ref[...]).astype(BF16), wkv_ref[...]).astype(BF16)

    d = x_ref.shape[1]
    k_ref = kv_scr.at[:, :d]
    v_ref = kv_scr.at[:, d:]
    _mix_out_kernel(x_ref, hm_ref, hn_ref, gm_ref, gn_ref, wm_ref, wn_ref, wo_ref, o_ref)
    _xattn_kernel(o_ref, g_ref, wq_ref, k_ref, v_ref, wxo_ref, o_ref, kv_leading=False)


def _mix_xattn(x2d, h_ml, h_ns, act2d, wm, wn, wo, g, wq, mem, gmem, wkv, wxo, *, gate_col0, seq, tm):
    n, d = x2d.shape
    n_mem = mem.shape[1]
    tiles_per_batch = seq // tm

    def res(w):
        return pl.BlockSpec(w.shape, lambda i: (0, 0), pipeline_mode=pl.Buffered(1))

    return pl.pallas_call(
        functools.partial(_mix_xattn_kernel, tiles_per_batch=tiles_per_batch),
        grid=(n // tm,),
        in_specs=[
            pl.BlockSpec((tm, d), lambda i: (i, 0)),
            pl.BlockSpec((tm, h_ml.shape[1]), lambda i: (i, 0)),
            pl.BlockSpec((tm, h_ns.shape[1]), lambda i: (i, 0)),
            pl.BlockSpec((tm, d), lambda i: (i, gate_col0)),
            pl.BlockSpec((tm, d), lambda i: (i, gate_col0 + 1)),
            res(wm), res(wn), res(wo),
            pl.BlockSpec((1, d), lambda i: (0, 0)),
            res(wq),
            pl.BlockSpec((1, n_mem, d), lambda i: (i // tiles_per_batch, 0, 0)),
            pl.BlockSpec((1, d), lambda i: (0, 0)),
            res(wkv), res(wxo),
        ],
        out_specs=pl.BlockSpec((tm, d), lambda i: (i, 0)),
        out_shape=jax.ShapeDtypeStruct((n, d), F32),
        scratch_shapes=[pltpu.VMEM((n_mem, 2 * d), BF16)],
        compiler_params=_cparams("arbitrary"),
        name="mix_xattn",
    )(x2d, h_ml, h_ns, act2d, act2d, wm, wn, wo, g.reshape(1, d), wq, mem, gmem.reshape(1, d), wkv, wxo)


def _route(logits, bg, be):
    G, EG = MOE_GROUPS, MOE_EXPERTS_PER_GROUP
    tm = logits.shape[1]
    lg = logits[0:G] + bg
    eg = jnp.exp(lg - jnp.max(lg, axis=0, keepdims=True))
    pg = eg / jnp.sum(eg, axis=0, keepdims=True)
    g_val = jnp.max(pg, axis=0, keepdims=True)
    g_row = lax.broadcasted_iota(jnp.int32, (G, tm), 0)
    g_idx = jnp.min(jnp.where(pg == g_val, g_row, G), axis=0, keepdims=True)
    el = logits[8:8 + G * EG] + be
    e_in = jnp.zeros((EG, tm), F32)
    for gg in range(G):
        e_in = jnp.where(g_idx == gg, el[gg * EG:(gg + 1) * EG], e_in)
    e_row = lax.broadcasted_iota(jnp.int32, (EG, tm), 0)
    v1 = jnp.max(e_in, axis=0, keepdims=True)
    i1 = jnp.min(jnp.where(e_in == v1, e_row, EG), axis=0, keepdims=True)
    rest = jnp.where(e_row == i1, -jnp.inf, e_in)
    v2 = jnp.max(rest, axis=0, keepdims=True)
    i2 = jnp.min(jnp.where(rest == v2, e_row, EG), axis=0, keepdims=True)
    e2 = jnp.exp(v2 - v1)
    c1 = g_val / (1.0 + e2)
    c2 = g_val * e2 / (1.0 + e2)
    ex = lax.broadcasted_iota(jnp.int32, (G * EG, tm), 0)
    base = g_idx * EG
    return jnp.where(ex == base + i1, c1, 0.0) + jnp.where(ex == base + i2, c2, 0.0), g_idx


def _moe_kernel(x_ref, g_ref, wr_ref, bg_ref, be_ref, before_ref, w1_ref, w3_ref, w2_ref, gf_ref, o_ref,
                h_scr, slot_scr, wt_scr, acc_scr, *, caps):
    G, EG = MOE_GROUPS, MOE_EXPERTS_PER_GROUP
    grp = pl.program_id(1)
    tm = x_ref.shape[0]

    @pl.when(grp == 0)
    def _():
        h = _rms(x_ref[...], g_ref[...])
        h_hi = h.astype(BF16)
        h_scr[...] = h_hi
        h_lo = (h - h_hi.astype(F32)).astype(BF16)
        w_hi = wr_ref[...].astype(BF16)
        w_lo = (wr_ref[...] - w_hi.astype(F32)).astype(BF16)
        logits = _dot_nt(w_hi, h_hi) + _dot_nt(w_hi, h_lo) + _dot_nt(w_lo, h_hi)
        wt, g_idx = _route(logits, bg_ref[...], be_ref[...])
        member = jnp.where(lax.broadcasted_iota(jnp.int32, (8, tm), 0) == g_idx, 1.0, 0.0)
        prefix = _dot(member.astype(BF16), before_ref[...])
        slots = jnp.where(member > 0.0, prefix, -1.0)
        zeros = jnp.zeros((8 - EG, tm), F32)
        for gg in range(G):
            slot_scr[gg] = jnp.broadcast_to(slots[gg:gg + 1], (8, tm))
            wt_scr[gg] = jnp.concatenate([wt[gg * EG:(gg + 1) * EG], zeros], axis=0)
        acc_scr[...] = jnp.zeros_like(acc_scr)

    slot = slot_scr[grp][0:1, :]
    n_rows = jnp.max(slot).astype(jnp.int32) + 1
    wts = wt_scr[grp]

    def one_pass(cap, first_row):
        row = lax.broadcasted_iota(jnp.int32, (cap, tm), 0).astype(F32)
        pick = jnp.where(row == slot - first_row.astype(F32), 1.0, 0.0)
        pick_b = pick.astype(BF16)
        hsub = _dot(pick_b, h_scr[...]).astype(BF16)
        y = jnp.zeros((cap, x_ref.shape[1]), F32)
        for e in range(EG):
            w_e = jnp.sum(pick * wts[e:e + 1, :], axis=1, keepdims=True)
            a = _dot(hsub, w1_ref[e])
            a = a * jax.nn.sigmoid(a) * _dot(hsub, w3_ref[e]) * w_e
            y = y + _dot(a.astype(BF16), w2_ref[e])
        acc_scr[...] += _dot_tn(pick_b, y.astype(BF16))

    *small_caps, cap_long = caps
    lower = 0
    for cap in small_caps:
        @pl.when((n_rows > lower) & (n_rows <= cap))
        def _(cap=cap):
            one_pass(cap, jnp.int32(0))
        lower = cap

    @pl.when(n_rows > lower)
    def _():
        def body(p, carry):
            one_pass(cap_long, p * cap_long)
            return carry

        lax.fori_loop(0, (n_rows + cap_long - 1) // cap_long, body, 0)

    @pl.when(grp == pl.num_programs(1) - 1)
    def _():
        o_ref[...] = _rms(x_ref[...] + acc_scr[...], gf_ref[...])


def _moe(x2d, g, wr, bg, be, w1, w3, w2, g_final, *, tm, caps):
    n, d = x2d.shape
    E, _, hid = w1.shape
    G, EG = MOE_GROUPS, MOE_EXPERTS_PER_GROUP
    before = jnp.asarray(np.arange(tm)[:, None] < np.arange(tm)[None, :], BF16)
    once = pl.Buffered(1)

    def const(shape):
        return pl.BlockSpec(shape, lambda i, e: (0,) * len(shape), pipeline_mode=once)

    return pl.pallas_call(
        functools.partial(_moe_kernel, caps=caps),
        grid=(n // tm, G),
        in_specs=[
            pl.BlockSpec((tm, d), lambda i, e: (i, 0), pipeline_mode=once),
            const((1, d)), const(wr.shape), const(bg.shape), const(be.shape), const((tm, tm)),
            pl.BlockSpec((EG, d, hid), lambda i, e: (e, 0, 0)),
            pl.BlockSpec((EG, d, hid), lambda i, e: (e, 0, 0)),
            pl.BlockSpec((EG, hid, d), lambda i, e: (e, 0, 0)),
            const((1, d)),
        ],
        out_specs=pl.BlockSpec((tm, d), lambda i, e: (i, 0)),
        out_shape=jax.ShapeDtypeStruct((n, d), F32),
        scratch_shapes=[pltpu.VMEM((tm, d), BF16), pltpu.VMEM((G, 8, tm), F32), pltpu.VMEM((G, 8, tm), F32),
                        pltpu.VMEM((tm, d), F32)],
        compiler_params=_cparams("parallel", "arbitrary"),
        name="moe",
    )(x2d, g.reshape(1, d), wr, bg, be, before, w1, w3, w2, g_final.reshape(1, d))


def _overlap_t(seq):
    n_cmp_pad = seq // CMP_STRIDE
    cs = np.arange(n_cmp_pad)[None, :] * CMP_STRIDE
    ss = np.arange(seq // SEL_BLOCK)[:, None] * SEL_BLOCK
    ov = np.clip(np.minimum(cs + CMP_BLOCK, ss + SEL_BLOCK) - np.maximum(cs, ss), 0, None) / CMP_BLOCK
    ov[:, n_cmp_pad - 1] = 0.0
    return jnp.asarray(ov, BF16)


def _layer(x, mem, norm_mix, w_in, conv_qk, b_igate, b_fgate, mlstm_norm, cmp_pos_k, cmp_pos_v, cmp_k_w1, cmp_k_w2,
           cmp_v_w1, cmp_v_w2, w_br_mlstm, w_br_nsa, w_mix_out, norm_xattn, norm_mem, xa_wq, xa_wkv, xa_wo, norm_ffn,
           router_group_w, router_group_b, router_expert_w, router_expert_b, moe_w1, moe_w3, moe_w2, norm_final):
    B, S, D = x.shape
    N = B * S
    H, d = ML_HEADS, ML_HEAD_DIM
    G, HG, dh = NSA_KV_GROUPS, NSA_GROUP_HEADS, NSA_HEAD_DIM
    x2d = x.reshape(N, D)

    o_mlqkvo = 0
    o_mlif = 4 * ML_WIDTH
    o_nsq = o_mlif + 2 * H
    o_kv = o_nsq + NSA_WIDTH
    o_nsg = o_kv + 6 * NSA_KV_WIDTH
    o_merge = o_nsg + NSA_N_BRANCH * NSA_HEADS
    o_kv_sel = o_kv + 2 * NSA_KV_WIDTH
    w_act = jnp.concatenate([w_in[:, o_merge:o_merge + 2 * D], w_in[:, o_mlqkvo:o_mlif], w_in[:, o_nsq:o_kv],
                             w_in[:, o_kv_sel:o_nsg]], axis=1).astype(BF16)
    n_small = 2 * H + NSA_N_BRANCH * NSA_HEADS
    w_small = jnp.concatenate([w_in[:, o_mlif:o_nsq], w_in[:, o_nsg:o_merge], jnp.zeros((D, LANES - n_small), F32),
                               w_in[:, o_kv:o_kv_sel]], axis=1).astype(BF16)
    act, small, cmp_kv = _in_proj(x2d, norm_mix, w_act, w_small, tm=1024, tn=2560)
    act3 = act.reshape(B, S, act.shape[1])
    c_ml = 2 * D
    c_nsq = c_ml + 4 * ML_WIDTH
    c_kv = c_nsq + NSA_WIDTH

    small3 = small.reshape(B, S, small.shape[1])
    gate_bias = jnp.zeros((1, LANES), F32).at[0, 0:H].set(b_igate).at[0, H:2 * H].set(b_fgate)
    conv_w = jnp.concatenate([conv_qk[:, :ML_WIDTH].reshape(ML_CONV, H, d), conv_qk[:, ML_WIDTH:].reshape(ML_CONV, H, d)],
                             axis=-1).transpose(1, 0, 2)
    h_ml = _mlstm(act3, small3, gate_bias, conv_w, mlstm_norm.reshape(H, 1, d), col0=c_ml // ML_WIDTH, chunk=512)

    pos = jnp.tile(jnp.stack([cmp_pos_k, cmp_pos_v])[:, :, None, :], (1, 1, 1, G))
    w1c = jnp.stack([cmp_k_w1, cmp_v_w1]).reshape(2, CMP_BLOCK, dh, -1)
    w1c = jnp.einsum('arld,gh->arghld', w1c, jnp.eye(G, dtype=F32))
    w1c = w1c.reshape(2, CMP_BLOCK, G, G * dh, -1).astype(BF16)
    w2c = jnp.stack([cmp_k_w2, cmp_v_w2]).astype(BF16)
    kvc = _compress(cmp_kv.reshape(B, S, cmp_kv.shape[1]), pos, w1c, w2c, col0=0)
    n_slc = S // SEL_BLOCK
    tok = np.arange(S)
    kconst = np.zeros((S, n_slc + 2 * dh), np.float32)
    kconst[tok, tok // SEL_BLOCK] = 1.0
    kconst[:, n_slc:n_slc + 3] = (tok // SEL_BLOCK)[:, None]
    kconst[:, n_slc + 3:n_slc + 6] = (tok % SEL_BLOCK)[:, None]
    h_ns = _nsa_attention(act3, small3, kvc[0], kvc[1], jnp.asarray(kconst, BF16), _overlap_t(S),
                          q_col0=c_nsq // (HG * dh), kv_col0=c_kv // NSA_KV_WIDTH,
                          gate_col0=2 * H, tq=256)

    x2 = _mix_xattn(x2d, h_ml.reshape(N, ML_WIDTH), h_ns.reshape(N, NSA_WIDTH), act, w_br_mlstm.astype(BF16),
                    w_br_nsa.astype(BF16), w_mix_out.astype(BF16), norm_xattn, xa_wq.astype(BF16), mem, norm_mem,
                    xa_wkv.astype(BF16), xa_wo.astype(BF16), gate_col0=0, seq=S, tm=1024)

    wr = jnp.zeros((ROUTER_ROWS, D), F32)
    wr = wr.at[0:MOE_GROUPS].set(router_group_w.T).at[8:8 + MOE_EXPERTS].set(router_expert_w.T)
    return _moe(x2, norm_ffn, wr, router_group_b.reshape(MOE_GROUPS, 1), router_expert_b.reshape(MOE_EXPERTS, 1),
                moe_w1.astype(BF16), moe_w3.astype(BF16), moe_w2.astype(BF16), norm_final, tm=1024, caps=(256, 304, 352)).reshape(B, S, D)


def kernel(x, mem, norm_mix, w_in, conv_qk, b_igate, b_fgate, mlstm_norm, cmp_pos_k, cmp_pos_v, cmp_k_w1, cmp_k_w2, cmp_v_w1, cmp_v_w2, w_br_mlstm, w_br_nsa, w_mix_out, norm_xattn, norm_mem, xa_wq, xa_wkv, xa_wo, norm_ffn, router_group_w, router_group_b, router_expert_w, router_expert_b, moe_w1, moe_w3, moe_w2, norm_final):
    depth = w_in.shape[0]
    assert depth == 1, "the fused final norm assumes a single layer"
    layer = 0
    return _layer(x, mem, norm_mix[layer], w_in[layer], conv_qk[layer], b_igate[layer], b_fgate[layer], mlstm_norm[layer],
                  cmp_pos_k[layer], cmp_pos_v[layer], cmp_k_w1[layer], cmp_k_w2[layer], cmp_v_w1[layer], cmp_v_w2[layer],
                  w_br_mlstm[layer], w_br_nsa[layer], w_mix_out[layer], norm_xattn[layer], norm_mem[layer], xa_wq[layer],
                  xa_wkv[layer], xa_wo[layer], norm_ffn[layer], router_group_w[layer], router_group_b[layer],
                  router_expert_w[layer], router_expert_b[layer], moe_w1[layer], moe_w3[layer], moe_w2[layer], norm_final)
```
